```python
import jax, jax.numpy as jnp
from jax import lax
import numpy as np

D_MODEL = 1024
BATCH = 8
SEQ = 8192
DEPTH = 1

CHUNK = 64
Q_BLOCK = 128
PLE_DIM = 256
CONV_DIM = D_MODEL // 2
CONV_WIDTH = 3
SB_HEADS = 8
SB_HEAD_DIM = 64
SB_DIM = SB_HEADS * SB_HEAD_DIM
MIX_WIDTH = CONV_DIM + SB_DIM
IN_PROJ_DIM = 3 * CONV_DIM + 3 * SB_DIM
D_FF = 2816
FFN_RES = 0.5
EPS = 1e-6

kernel_name = "hybrid_shortconv_stickbreaking_macaron_block"


def _rmsnorm(x, g):
    xf = x.astype(jnp.float32)
    y = xf * lax.rsqrt(jnp.mean(xf * xf, axis=-1, keepdims=True) + EPS)
    return (y * g.astype(jnp.float32)).astype(x.dtype)


def _swiglu(h, w_gate, w_up, w_down):
    return (jax.nn.silu(h @ w_gate) * (h @ w_up)) @ w_down


def _short_gated_conv(b, c, u, conv_w, conv_b):
    z = c * u
    rhs = conv_w[:, None, :].astype(z.dtype)
    y = lax.conv_general_dilated(
        z, rhs, window_strides=(1,), padding=[(CONV_WIDTH - 1, 0)],
        dimension_numbers=("NWC", "WIO", "NWC"), feature_group_count=CONV_DIM)
    return b * (y + conv_b.astype(z.dtype))


def _stick_breaking(q, k, v):
    S = q.shape[2]
    scale = SB_HEAD_DIM ** -0.5
    outs = []
    for t0 in range(0, S, Q_BLOCK):
        kl = t0 + Q_BLOCK
        qs = q[:, :, t0:t0 + Q_BLOCK]
        ks = k[:, :, :kl]
        vs = v[:, :, :kl]
        z = jnp.einsum("bhqd,bhkd->bhqk", qs, ks) * scale
        t_idx = t0 + jnp.arange(Q_BLOCK)[:, None]
        s_idx = jnp.arange(kl)[None, :]
        causal = s_idx < t_idx
        log_keep = jnp.where(causal, jax.nn.log_sigmoid(-z), 0.0)
        later = lax.cumsum(log_keep, axis=3, reverse=True) - log_keep
        a = jnp.where(causal, jnp.exp(jax.nn.log_sigmoid(z) + later), 0.0)
        outs.append(jnp.einsum("bhqk,bhkd->bhqd", a, vs))
    return jnp.concatenate(outs, axis=2)


def _fwd_setup_inputs(seed: int = 0) -> dict:
    key = jax.random.key(seed)
    ks = jax.random.split(key, 24)
    f32 = jnp.float32

    def w(k, shape, fan_in):
        return jax.random.normal(k, shape, f32) * (fan_in ** -0.5)

    def gain(k, shape):
        return 1.0 + 0.05 * jax.random.normal(k, shape, f32)

    return {
        "x": jax.random.normal(ks[0], (BATCH, SEQ, D_MODEL), f32),
        "p": jax.random.normal(ks[1], (DEPTH, BATCH, SEQ, PLE_DIM), f32),
        "ffn1_norm": gain(ks[2], (DEPTH, D_MODEL)),
        "ffn1_w_gate": w(ks[3], (DEPTH, D_MODEL, D_FF), D_MODEL),
        "ffn1_w_up": w(ks[4], (DEPTH, D_MODEL, D_FF), D_MODEL),
        "ffn1_w_down": w(ks[5], (DEPTH, D_FF, D_MODEL), D_FF),
        "mix_norm": gain(ks[6], (DEPTH, D_MODEL)),
        "w_in": w(ks[7], (DEPTH, D_MODEL, IN_PROJ_DIM), D_MODEL),
        "conv_w": w(ks[8], (DEPTH, CONV_WIDTH, CONV_DIM), CONV_WIDTH),
        "conv_b": 0.02 * jax.random.normal(ks[9], (DEPTH, CONV_DIM), f32),
        "q_norm": gain(ks[10], (DEPTH, SB_HEAD_DIM)),
        "k_norm": gain(ks[11], (DEPTH, SB_HEAD_DIM)),
        "w_out": w(ks[12], (DEPTH, MIX_WIDTH, D_MODEL), MIX_WIDTH),
        "ffn2_norm": gain(ks[13], (DEPTH, D_MODEL)),
        "ffn2_w_gate": w(ks[14], (DEPTH, D_MODEL, D_FF), D_MODEL),
        "ffn2_w_up": w(ks[15], (DEPTH, D_MODEL, D_FF), D_MODEL),
        "ffn2_w_down": w(ks[16], (DEPTH, D_FF, D_MODEL), D_FF),
        "ple_norm": gain(ks[17], (DEPTH, D_MODEL)),
        "ple_w_gate": w(ks[18], (DEPTH, D_MODEL, D_MODEL), D_MODEL),
        "ple_w_proj": w(ks[19], (DEPTH, PLE_DIM, D_MODEL), PLE_DIM),
    }


def _fwd_reference(x, p, ffn1_norm, ffn1_w_gate, ffn1_w_up, ffn1_w_down, mix_norm, w_in,
              conv_w, conv_b, q_norm, k_norm, w_out, ffn2_norm, ffn2_w_gate, ffn2_w_up,
              ffn2_w_down, ple_norm, ple_w_gate, ple_w_proj):
    B, S, _ = x.shape
    for i in range(DEPTH):
        x = x + FFN_RES * _swiglu(_rmsnorm(x, ffn1_norm[i]), ffn1_w_gate[i], ffn1_w_up[i], ffn1_w_down[i])

        h = _rmsnorm(x, mix_norm[i])
        proj = h @ w_in[i]
        b_g, c_g, u, q, k, v = jnp.split(
            proj, np.cumsum([CONV_DIM, CONV_DIM, CONV_DIM, SB_DIM, SB_DIM]).tolist(), axis=-1)

        y_conv = _short_gated_conv(b_g, c_g, u, conv_w[i], conv_b[i])

        def heads(t):
            return t.reshape(B, S, SB_HEADS, SB_HEAD_DIM).transpose(0, 2, 1, 3).astype(jnp.float32)
        qh = _rmsnorm(heads(q), q_norm[i])
        kh = _rmsnorm(heads(k), k_norm[i])
        y_sb = _stick_breaking(qh, kh, heads(v))
        y_sb = y_sb.transpose(0, 2, 1, 3).reshape(B, S, SB_DIM).astype(x.dtype)

        x = x + jnp.concatenate([y_conv, y_sb], axis=-1) @ w_out[i]

        x = x + FFN_RES * _swiglu(_rmsnorm(x, ffn2_norm[i]), ffn2_w_gate[i], ffn2_w_up[i], ffn2_w_down[i])

        gate = jax.nn.sigmoid(_rmsnorm(x, ple_norm[i]) @ ple_w_gate[i])
        x = x + gate * (p[i].astype(x.dtype) @ ple_w_proj[i])
    return x


import jax as _jax
import jax.numpy as _jnp

TWIN_FORMAT = 'train_step'
FWD_PARAMS = ['x', 'p', 'ffn1_norm', 'ffn1_w_gate', 'ffn1_w_up', 'ffn1_w_down', 'mix_norm', 'w_in', 'conv_w', 'conv_b', 'q_norm', 'k_norm', 'w_out', 'ffn2_norm', 'ffn2_w_gate', 'ffn2_w_up', 'ffn2_w_down', 'ple_norm', 'ple_w_gate', 'ple_w_proj']
TWIN_WEIGHTS = ['ffn1_norm', 'ffn1_w_gate', 'ffn1_w_up', 'ffn1_w_down', 'mix_norm', 'w_in', 'conv_w', 'conv_b', 'q_norm', 'k_norm', 'w_out', 'ffn2_norm', 'ffn2_w_gate', 'ffn2_w_up', 'ffn2_w_down', 'ple_norm', 'ple_w_gate', 'ple_w_proj']
TWIN_DIFF_INPUT = 'x'
TWIN_INPUTS = ['x', 'p', 'ffn1_norm', 'ffn1_w_gate', 'ffn1_w_up', 'ffn1_w_down', 'mix_norm', 'w_in', 'conv_w', 'conv_b', 'q_norm', 'k_norm', 'w_out', 'ffn2_norm', 'ffn2_w_gate', 'ffn2_w_up', 'ffn2_w_down', 'ple_norm', 'ple_w_gate', 'ple_w_proj', 'loss_target', 'm_ffn1_norm', 'm_ffn1_w_gate', 'm_ffn1_w_up', 'm_ffn1_w_down', 'm_mix_norm', 'm_w_in', 'm_conv_w', 'm_conv_b', 'm_q_norm', 'm_k_norm', 'm_w_out', 'm_ffn2_norm', 'm_ffn2_w_gate', 'm_ffn2_w_up', 'm_ffn2_w_down', 'm_ple_norm', 'm_ple_w_gate', 'm_ple_w_proj', 'v_ffn1_norm', 'v_ffn1_w_gate', 'v_ffn1_w_up', 'v_ffn1_w_down', 'v_mix_norm', 'v_w_in', 'v_conv_w', 'v_conv_b', 'v_q_norm', 'v_k_norm', 'v_w_out', 'v_ffn2_norm', 'v_ffn2_w_gate', 'v_ffn2_w_up', 'v_ffn2_w_down', 'v_ple_norm', 'v_ple_w_gate', 'v_ple_w_proj']
TWIN_OUTPUTS = ['loss', 'grad_x', 'grad_ffn1_norm', 'grad_ffn1_w_gate', 'grad_ffn1_w_up', 'grad_ffn1_w_down', 'grad_mix_norm', 'grad_w_in', 'grad_conv_w', 'grad_conv_b', 'grad_q_norm', 'grad_k_norm', 'grad_w_out', 'grad_ffn2_norm', 'grad_ffn2_w_gate', 'grad_ffn2_w_up', 'grad_ffn2_w_down', 'grad_ple_norm', 'grad_ple_w_gate', 'grad_ple_w_proj', 'delta_ffn1_norm', 'delta_ffn1_w_gate', 'delta_ffn1_w_up', 'delta_ffn1_w_down', 'delta_mix_norm', 'delta_w_in', 'delta_conv_w', 'delta_conv_b', 'delta_q_norm', 'delta_k_norm', 'delta_w_out', 'delta_ffn2_norm', 'delta_ffn2_w_gate', 'delta_ffn2_w_up', 'delta_ffn2_w_down', 'delta_ple_norm', 'delta_ple_w_gate', 'delta_ple_w_proj', 'new_m_ffn1_norm', 'new_m_ffn1_w_gate', 'new_m_ffn1_w_up', 'new_m_ffn1_w_down', 'new_m_mix_norm', 'new_m_w_in', 'new_m_conv_w', 'new_m_conv_b', 'new_m_q_norm', 'new_m_k_norm', 'new_m_w_out', 'new_m_ffn2_norm', 'new_m_ffn2_w_gate', 'new_m_ffn2_w_up', 'new_m_ffn2_w_down', 'new_m_ple_norm', 'new_m_ple_w_gate', 'new_m_ple_w_proj', 'new_v_ffn1_norm', 'new_v_ffn1_w_gate', 'new_v_ffn1_w_up', 'new_v_ffn1_w_down', 'new_v_mix_norm', 'new_v_w_in', 'new_v_conv_w', 'new_v_conv_b', 'new_v_q_norm', 'new_v_k_norm', 'new_v_w_out', 'new_v_ffn2_norm', 'new_v_ffn2_w_gate', 'new_v_ffn2_w_up', 'new_v_ffn2_w_down', 'new_v_ple_norm', 'new_v_ple_w_gate', 'new_v_ple_w_proj']
TWIN_LEAF_KINDS = {'loss': 'loss', 'grad_x': 'grad_x', 'grad_ffn1_norm': 'grad_w', 'grad_ffn1_w_gate': 'grad_w', 'grad_ffn1_w_up': 'grad_w', 'grad_ffn1_w_down': 'grad_w', 'grad_mix_norm': 'grad_w', 'grad_w_in': 'grad_w', 'grad_conv_w': 'grad_w', 'grad_conv_b': 'grad_w', 'grad_q_norm': 'grad_w', 'grad_k_norm': 'grad_w', 'grad_w_out': 'grad_w', 'grad_ffn2_norm': 'grad_w', 'grad_ffn2_w_gate': 'grad_w', 'grad_ffn2_w_up': 'grad_w', 'grad_ffn2_w_down': 'grad_w', 'grad_ple_norm': 'grad_w', 'grad_ple_w_gate': 'grad_w', 'grad_ple_w_proj': 'grad_w', 'delta_ffn1_norm': 'delta_w', 'delta_ffn1_w_gate': 'delta_w', 'delta_ffn1_w_up': 'delta_w', 'delta_ffn1_w_down': 'delta_w', 'delta_mix_norm': 'delta_w', 'delta_w_in': 'delta_w', 'delta_conv_w': 'delta_w', 'delta_conv_b': 'delta_w', 'delta_q_norm': 'delta_w', 'delta_k_norm': 'delta_w', 'delta_w_out': 'delta_w', 'delta_ffn2_norm': 'delta_w', 'delta_ffn2_w_gate': 'delta_w', 'delta_ffn2_w_up': 'delta_w', 'delta_ffn2_w_down': 'delta_w', 'delta_ple_norm': 'delta_w', 'delta_ple_w_gate': 'delta_w', 'delta_ple_w_proj': 'delta_w', 'new_m_ffn1_norm': 'new_m', 'new_m_ffn1_w_gate': 'new_m', 'new_m_ffn1_w_up': 'new_m', 'new_m_ffn1_w_down': 'new_m', 'new_m_mix_norm': 'new_m', 'new_m_w_in': 'new_m', 'new_m_conv_w': 'new_m', 'new_m_conv_b': 'new_m', 'new_m_q_norm': 'new_m', 'new_m_k_norm': 'new_m', 'new_m_w_out': 'new_m', 'new_m_ffn2_norm': 'new_m', 'new_m_ffn2_w_gate': 'new_m', 'new_m_ffn2_w_up': 'new_m', 'new_m_ffn2_w_down': 'new_m', 'new_m_ple_norm': 'new_m', 'new_m_ple_w_gate': 'new_m', 'new_m_ple_w_proj': 'new_m', 'new_v_ffn1_norm': 'new_v', 'new_v_ffn1_w_gate': 'new_v', 'new_v_ffn1_w_up': 'new_v', 'new_v_ffn1_w_down': 'new_v', 'new_v_mix_norm': 'new_v', 'new_v_w_in': 'new_v', 'new_v_conv_w': 'new_v', 'new_v_conv_b': 'new_v', 'new_v_q_norm': 'new_v', 'new_v_k_norm': 'new_v', 'new_v_w_out': 'new_v', 'new_v_ffn2_norm': 'new_v', 'new_v_ffn2_w_gate': 'new_v', 'new_v_ffn2_w_up': 'new_v', 'new_v_ffn2_w_down': 'new_v', 'new_v_ple_norm': 'new_v', 'new_v_ple_w_gate': 'new_v', 'new_v_ple_w_proj': 'new_v'}


def _forward(args):
    return _fwd_reference(*[args[k] for k in FWD_PARAMS])


def _output_shape():
    def fwd():
        inp = _fwd_setup_inputs(0)
        return _fwd_reference(*[inp[k] for k in FWD_PARAMS])
    out = _jax.eval_shape(fwd)
    return out.shape, out.dtype

N_MICROBATCH = 1
ADAM_LR = 0.001
ADAM_B1 = 0.9
ADAM_B2 = 0.999
ADAM_EPS = 1e-08
ADAM_WD = 0.01
ADAM_STEP = 10
PER_EXAMPLE_BATCH_AXIS = {'x': 0, 'p': 1, 'loss_target': 0}
SHARED_INPUTS = []
_WEIGHT_DTYPES = {'ffn1_norm': _jnp.float32, 'ffn1_w_gate': _jnp.float32, 'ffn1_w_up': _jnp.float32, 'ffn1_w_down': _jnp.float32, 'mix_norm': _jnp.float32, 'w_in': _jnp.float32, 'conv_w': _jnp.float32, 'conv_b': _jnp.float32, 'q_norm': _jnp.float32, 'k_norm': _jnp.float32, 'w_out': _jnp.float32, 'ffn2_norm': _jnp.float32, 'ffn2_w_gate': _jnp.float32, 'ffn2_w_up': _jnp.float32, 'ffn2_w_down': _jnp.float32, 'ple_norm': _jnp.float32, 'ple_w_gate': _jnp.float32, 'ple_w_proj': _jnp.float32}
MOMENT_SCALE = {'ffn1_norm': 1.212493e+01, 'ffn1_w_gate': 2.408635e-01, 'ffn1_w_up': 2.588664e-01, 'ffn1_w_down': 4.370776e-01, 'mix_norm': 1.046387e+02, 'w_in': 1.307102e+00, 'conv_w': 3.605450e+01, 'conv_b': 2.564366e+00, 'q_norm': 3.192793e+01, 'k_norm': 3.204586e+01, 'w_out': 1.422190e+00, 'ffn2_norm': 1.241274e+01, 'ffn2_w_gate': 1.009082e-01, 'ffn2_w_up': 1.492515e-01, 'ffn2_w_down': 2.427554e-01, 'ple_norm': 1.939035e+00, 'ple_w_gate': 1.141263e-01, 'ple_w_proj': 8.894482e-01}


def _to_microbatches(a, axis):
    t = _jnp.moveaxis(a, axis, 0)
    t = t.reshape((N_MICROBATCH, t.shape[0] // N_MICROBATCH) + t.shape[1:])
    return _jnp.moveaxis(t, 1, axis + 1)


def setup_inputs(seed: int = 0) -> dict:
    inp = _fwd_setup_inputs(seed)
    key = _jax.random.fold_in(_jax.random.key(seed), 7919)
    shape, _ = _output_shape()
    out = dict(inp)
    out["loss_target"] = _jax.random.normal(_jax.random.fold_in(key, 0), shape, _jnp.float32)
    for i, name in enumerate(TWIN_WEIGHTS):
        w = inp[name].astype(_jnp.float32)
        if MOMENT_SCALE is None:
            s = _jnp.sqrt(_jnp.mean(_jnp.square(w)) + 1e-30)
        else:
            s = MOMENT_SCALE[name]
        km, kv = _jax.random.split(_jax.random.fold_in(key, i + 1))
        out[name] = w
        out["m_" + name] = s * _jax.random.normal(km, w.shape, _jnp.float32)
        out["v_" + name] = (s * s) * _jax.random.uniform(kv, w.shape, _jnp.float32, 0.5, 1.5)
    if N_MICROBATCH > 1:
        for name, axis in PER_EXAMPLE_BATCH_AXIS.items():
            out[name] = _to_microbatches(out[name], axis)
    return {'x': out['x'], 'p': out['p'], 'ffn1_norm': out['ffn1_norm'], 'ffn1_w_gate': out['ffn1_w_gate'], 'ffn1_w_up': out['ffn1_w_up'], 'ffn1_w_down': out['ffn1_w_down'], 'mix_norm': out['mix_norm'], 'w_in': out['w_in'], 'conv_w': out['conv_w'], 'conv_b': out['conv_b'], 'q_norm': out['q_norm'], 'k_norm': out['k_norm'], 'w_out': out['w_out'], 'ffn2_norm': out['ffn2_norm'], 'ffn2_w_gate': out['ffn2_w_gate'], 'ffn2_w_up': out['ffn2_w_up'], 'ffn2_w_down': out['ffn2_w_down'], 'ple_norm': out['ple_norm'], 'ple_w_gate': out['ple_w_gate'], 'ple_w_proj': out['ple_w_proj'], 'loss_target': out['loss_target'], 'm_ffn1_norm': out['m_ffn1_norm'], 'm_ffn1_w_gate': out['m_ffn1_w_gate'], 'm_ffn1_w_up': out['m_ffn1_w_up'], 'm_ffn1_w_down': out['m_ffn1_w_down'], 'm_mix_norm': out['m_mix_norm'], 'm_w_in': out['m_w_in'], 'm_conv_w': out['m_conv_w'], 'm_conv_b': out['m_conv_b'], 'm_q_norm': out['m_q_norm'], 'm_k_norm': out['m_k_norm'], 'm_w_out': out['m_w_out'], 'm_ffn2_norm': out['m_ffn2_norm'], 'm_ffn2_w_gate': out['m_ffn2_w_gate'], 'm_ffn2_w_up': out['m_ffn2_w_up'], 'm_ffn2_w_down': out['m_ffn2_w_down'], 'm_ple_norm': out['m_ple_norm'], 'm_ple_w_gate': out['m_ple_w_gate'], 'm_ple_w_proj': out['m_ple_w_proj'], 'v_ffn1_norm': out['v_ffn1_norm'], 'v_ffn1_w_gate': out['v_ffn1_w_gate'], 'v_ffn1_w_up': out['v_ffn1_w_up'], 'v_ffn1_w_down': out['v_ffn1_w_down'], 'v_mix_norm': out['v_mix_norm'], 'v_w_in': out['v_w_in'], 'v_conv_w': out['v_conv_w'], 'v_conv_b': out['v_conv_b'], 'v_q_norm': out['v_q_norm'], 'v_k_norm': out['v_k_norm'], 'v_w_out': out['v_w_out'], 'v_ffn2_norm': out['v_ffn2_norm'], 'v_ffn2_w_gate': out['v_ffn2_w_gate'], 'v_ffn2_w_up': out['v_ffn2_w_up'], 'v_ffn2_w_down': out['v_ffn2_w_down'], 'v_ple_norm': out['v_ple_norm'], 'v_ple_w_gate': out['v_ple_w_gate'], 'v_ple_w_proj': out['v_ple_w_proj']}


def _loss(weights, diff, rest, loss_target):
    with _jax.named_scope("forward"):
        args = {**rest, TWIN_DIFF_INPUT: diff, **{k: w.astype(_WEIGHT_DTYPES[k]) for k, w in weights.items()}}
        y = _forward(args)
    with _jax.named_scope("loss_head"):
        err = _jnp.square(y.astype(_jnp.float32) - loss_target)
        return 0.5 * _jnp.sum(_jnp.mean(err, axis=-1)) if err.ndim else 0.5 * err


def _adamw(w, g, m, v):
    m = ADAM_B1 * m + (1.0 - ADAM_B1) * g
    v = ADAM_B2 * v + (1.0 - ADAM_B2) * _jnp.square(g)
    m_hat = m / (1.0 - ADAM_B1 ** ADAM_STEP)
    v_hat = v / (1.0 - ADAM_B2 ** ADAM_STEP)
    delta = -ADAM_LR * (m_hat / (_jnp.sqrt(v_hat) + ADAM_EPS) + ADAM_WD * w)
    return delta, m, v


def reference(x, p, ffn1_norm, ffn1_w_gate, ffn1_w_up, ffn1_w_down, mix_norm, w_in, conv_w, conv_b, q_norm, k_norm, w_out, ffn2_norm, ffn2_w_gate, ffn2_w_up, ffn2_w_down, ple_norm, ple_w_gate, ple_w_proj, loss_target, m_ffn1_norm, m_ffn1_w_gate, m_ffn1_w_up, m_ffn1_w_down, m_mix_norm, m_w_in, m_conv_w, m_conv_b, m_q_norm, m_k_norm, m_w_out, m_ffn2_norm, m_ffn2_w_gate, m_ffn2_w_up, m_ffn2_w_down, m_ple_norm, m_ple_w_gate, m_ple_w_proj, v_ffn1_norm, v_ffn1_w_gate, v_ffn1_w_up, v_ffn1_w_down, v_mix_norm, v_w_in, v_conv_w, v_conv_b, v_q_norm, v_k_norm, v_w_out, v_ffn2_norm, v_ffn2_w_gate, v_ffn2_w_up, v_ffn2_w_down, v_ple_norm, v_ple_w_gate, v_ple_w_proj):
    given = dict(x=x, p=p, ffn1_norm=ffn1_norm, ffn1_w_gate=ffn1_w_gate, ffn1_w_up=ffn1_w_up, ffn1_w_down=ffn1_w_down, mix_norm=mix_norm, w_in=w_in, conv_w=conv_w, conv_b=conv_b, q_norm=q_norm, k_norm=k_norm, w_out=w_out, ffn2_norm=ffn2_norm, ffn2_w_gate=ffn2_w_gate, ffn2_w_up=ffn2_w_up, ffn2_w_down=ffn2_w_down, ple_norm=ple_norm, ple_w_gate=ple_w_gate, ple_w_proj=ple_w_proj, loss_target=loss_target, m_ffn1_norm=m_ffn1_norm, m_ffn1_w_gate=m_ffn1_w_gate, m_ffn1_w_up=m_ffn1_w_up, m_ffn1_w_down=m_ffn1_w_down, m_mix_norm=m_mix_norm, m_w_in=m_w_in, m_conv_w=m_conv_w, m_conv_b=m_conv_b, m_q_norm=m_q_norm, m_k_norm=m_k_norm, m_w_out=m_w_out, m_ffn2_norm=m_ffn2_norm, m_ffn2_w_gate=m_ffn2_w_gate, m_ffn2_w_up=m_ffn2_w_up, m_ffn2_w_down=m_ffn2_w_down, m_ple_norm=m_ple_norm, m_ple_w_gate=m_ple_w_gate, m_ple_w_proj=m_ple_w_proj, v_ffn1_norm=v_ffn1_norm, v_ffn1_w_gate=v_ffn1_w_gate, v_ffn1_w_up=v_ffn1_w_up, v_ffn1_w_down=v_ffn1_w_down, v_mix_norm=v_mix_norm, v_w_in=v_w_in, v_conv_w=v_conv_w, v_conv_b=v_conv_b, v_q_norm=v_q_norm, v_k_norm=v_k_norm, v_w_out=v_w_out, v_ffn2_norm=v_ffn2_norm, v_ffn2_w_gate=v_ffn2_w_gate, v_ffn2_w_up=v_ffn2_w_up, v_ffn2_w_down=v_ffn2_w_down, v_ple_norm=v_ple_norm, v_ple_w_gate=v_ple_w_gate, v_ple_w_proj=v_ple_w_proj)
    weights = {n: given[n] for n in TWIN_WEIGHTS}
    shared = {n: given[n] for n in SHARED_INPUTS}
    per_example = {n: given[n] for n in ['x', 'p']}
    grad_fn = _jax.value_and_grad(_loss, argnums=(0, 1))

    def one_microbatch(ex, loss_target):
        ex = dict(ex)
        diff = ex.pop(TWIN_DIFF_INPUT)
        return grad_fn(weights, diff, {**shared, **ex}, loss_target)

    if N_MICROBATCH == 1:
        loss, (grad_w, grad_x) = one_microbatch(per_example, given["loss_target"])
    else:
        def body(carry, xs):
            loss_sum, grad_sum = carry
            l_k, (gw_k, gx_k) = one_microbatch(xs[0], xs[1])
            with _jax.named_scope("update"):
                return (loss_sum + l_k, _jax.tree.map(_jnp.add, grad_sum, gw_k)), gx_k

        init = (_jnp.zeros((), _jnp.float32), _jax.tree.map(_jnp.zeros_like, weights))
        (loss, grad_w), grad_x = _jax.lax.scan(body, init, (per_example, given["loss_target"]))
    with _jax.named_scope("update"):
        delta_w, new_m, new_v = {}, {}, {}
        for n in TWIN_WEIGHTS:
            delta_w[n], new_m[n], new_v[n] = _adamw(weights[n], grad_w[n], given["m_" + n], given["v_" + n])
    return (loss, grad_x, *[grad_w[n] for n in TWIN_WEIGHTS], *[delta_w[n] for n in TWIN_WEIGHTS],
            *[new_m[n] for n in TWIN_WEIGHTS], *[new_v[n] for n in TWIN_WEIGHTS])
```

```python
import jax
import jax.numpy as jnp
from jax import lax
from jax.experimental import pallas as pl
from jax.experimental.pallas import tpu as pltpu

F32 = jnp.float32
BF16 = jnp.bfloat16

EPS = 1e-6
FFN_RES = 0.5
HEAD_DIM = 64
CONV_DIM = 512
SB_DIM = 512
SLAB = 512
N_DEV = 8
MESH_AXES = ("x", "y", "c")
MESH = pl.DeviceIdType.MESH

ADAM_LR = 0.001
ADAM_B1 = 0.9
ADAM_B2 = 0.999
ADAM_EPS = 1e-08
ADAM_WD = 0.01
ADAM_STEP = 10

VMEM_LIMIT_BYTES = 56 * 1024 * 1024
SUBLANES = 8
LANES = 128


def _cparams(*semantics):
    return pltpu.CompilerParams(dimension_semantics=semantics, vmem_limit_bytes=VMEM_LIMIT_BYTES)


def _dot_nn(a, b):
    return jnp.dot(a, b, preferred_element_type=F32)


def _dot_nt(a, b):
    return lax.dot_general(a, b, (((1,), (1,)), ((), ())), preferred_element_type=F32)


def _dot_tn(a, b):
    return lax.dot_general(a, b, (((0,), (0,)), ((), ())), preferred_element_type=F32)


def _fold8(v):
    rows, cols = v.shape
    return jnp.sum(v.reshape(rows // SUBLANES, SUBLANES, cols), axis=0)


def _split2(v):
    hi = v.astype(BF16)
    lo = (v - hi.astype(F32)).astype(BF16)
    return hi, lo


def _rms_stats(x):
    return lax.rsqrt(jnp.mean(x * x, axis=-1, keepdims=True) + EPS)


def _rms_bwd(dh, x, gain):
    r = _rms_stats(x)
    u = dh * gain
    dx = r * u - x * (r * r * r) * jnp.mean(u * x, axis=-1, keepdims=True)
    return dx, dh * x * r


def _pick(n, pref):
    return pref if n % pref == 0 else n


def _rmsnorm(name, x, gain, tt):
    t, d = x.shape

    def body(x_ref, g_ref, o_ref):
        xv = x_ref[...]
        o_ref[...] = ((xv * _rms_stats(xv)) * g_ref[...]).astype(BF16)

    return pl.pallas_call(
        body, name=name, grid=(t // tt,),
        in_specs=[pl.BlockSpec((tt, d), lambda i: (i, 0)), pl.BlockSpec((1, d), lambda i: (0, 0))],
        out_specs=pl.BlockSpec((tt, d), lambda i: (i, 0)),
        out_shape=jax.ShapeDtypeStruct((t, d), BF16),
        compiler_params=_cparams("parallel"),
    )(x, gain)


def _mm_nt(name, a_list, w_list, pairs, epilogue, out_dtypes, tt, tn):
    t = a_list[0].shape[0]
    n = w_list[0].shape[0]
    na, nw = len(a_list), len(w_list)

    def body(*refs):
        a_refs, w_refs, o_refs = refs[:na], refs[na:na + nw], refs[na + nw:]
        accs = [_dot_nt(a_refs[ai][...], w_refs[wi][...]) for ai, wi in pairs]
        for o_ref, o in zip(o_refs, epilogue(accs)):
            o_ref[...] = o.astype(o_ref.dtype)

    in_specs = ([pl.BlockSpec((tt, a.shape[1]), lambda i, j: (i, 0)) for a in a_list]
                + [pl.BlockSpec((tn, w.shape[1]), lambda i, j: (j, 0)) for w in w_list])
    return pl.pallas_call(
        body, name=name, grid=(t // tt, n // tn), in_specs=in_specs,
        out_specs=[pl.BlockSpec((tt, tn), lambda i, j: (i, j)) for _ in out_dtypes],
        out_shape=[jax.ShapeDtypeStruct((t, n), dt) for dt in out_dtypes],
        compiler_params=_cparams("parallel", "arbitrary"),
    )(*a_list, *w_list)


def _mm_nn(name, pairs, rows, fulls, epilogue, out_kinds, tt, tk):
    t, k_total = pairs[0][0].shape
    n = pairs[0][1].shape[1]
    nk = k_total // tk
    nt = t // tt
    npair, nrow, nfull = len(pairs), len(rows), len(fulls)

    def body(*refs):
        a_refs = refs[:npair]
        w_refs = refs[npair:2 * npair]
        r_refs = refs[2 * npair:2 * npair + nrow]
        f_refs = refs[2 * npair + nrow:2 * npair + nrow + nfull]
        o_refs = refs[2 * npair + nrow + nfull:-1]
        acc_ref = refs[-1]
        k = pl.program_id(1)
        s = _dot_nn(a_refs[0][...], w_refs[0][...])
        for a_ref, w_ref in zip(a_refs[1:], w_refs[1:]):
            s = s + _dot_nn(a_ref[...], w_ref[...])

        @pl.when(k == 0)
        def _():
            acc_ref[...] = s

        @pl.when(k > 0)
        def _():
            acc_ref[...] += s

        @pl.when(k == nk - 1)
        def _():
            outs = epilogue(acc_ref[...], [r[...] for r in r_refs], [f[...] for f in f_refs])
            for o_ref, o in zip(o_refs, outs):
                o_ref[...] = o.astype(o_ref.dtype)

    in_specs = ([pl.BlockSpec((tt, tk), lambda i, k: (i, k)) for _ in pairs]
                + [pl.BlockSpec((tk, n), (lambda i, k, off=off: (k + off, 0))) for _, _, off in pairs]
                + [pl.BlockSpec((tt, n), lambda i, k: (i, 0)) for _ in rows]
                + [pl.BlockSpec((1, n), lambda i, k: (0, 0)) for _ in fulls])
    out_specs, out_shape = [], []
    for kind, dt in out_kinds:
        if kind == "tile":
            out_specs.append(pl.BlockSpec((tt, n), lambda i, k: (i, 0)))
            out_shape.append(jax.ShapeDtypeStruct((t, n), dt))
        else:
            out_specs.append(pl.BlockSpec((SUBLANES, n), lambda i, k: (i, 0)))
            out_shape.append(jax.ShapeDtypeStruct((nt * SUBLANES, n), dt))
    return pl.pallas_call(
        body, name=name, grid=(nt, nk), in_specs=in_specs, out_specs=out_specs, out_shape=out_shape,
        scratch_shapes=[pltpu.VMEM((tt, n), F32)],
        compiler_params=_cparams("parallel", "arbitrary"),
    )(*[a for a, _, _ in pairs], *[w for _, w, _ in pairs], *rows, *fulls)


def _mm_tn(name, a, b, scale, tm, tt):
    t, m = a.shape
    n = b.shape[1]
    nt = t // tt

    def body(a_ref, b_ref, o_ref, acc_ref):
        k = pl.program_id(1)
        s = _dot_tn(a_ref[...], b_ref[...])

        @pl.when(k == 0)
        def _():
            acc_ref[...] = s

        @pl.when(k > 0)
        def _():
            acc_ref[...] += s

        @pl.when(k == nt - 1)
        def _():
            o_ref[...] = (acc_ref[...] * scale).astype(o_ref.dtype)

    return pl.pallas_call(
        body, name=name, grid=(m // tm, nt),
        in_specs=[pl.BlockSpec((tt, tm), lambda i, k: (k, i)), pl.BlockSpec((tt, n), lambda i, k: (k, 0))],
        out_specs=pl.BlockSpec((tm, n), lambda i, k: (i, 0)),
        out_shape=jax.ShapeDtypeStruct((m, n), BF16),
        scratch_shapes=[pltpu.VMEM((tm, n), F32)],
        compiler_params=_cparams("parallel", "arbitrary"),
    )(a, b)


def _group_sum(v, bd):
    hi = v.astype(BF16)
    r1 = v - hi.astype(F32)
    mid = r1.astype(BF16)
    lo = (r1 - mid.astype(F32)).astype(BF16)
    return _dot_nn(hi, bd) + _dot_nn(mid, bd) + _dot_nn(lo, bd)


def _qknorm_fwd(proj, qg, kg, bd, tt):
    t = proj.shape[0]

    def body(q_ref, k_ref, v_ref, qg_ref, kg_ref, bd_ref, qn_ref, kn_ref, vb_ref):
        bdv = bd_ref[...]
        for x_ref, g_ref, o_ref in ((q_ref, qg_ref, qn_ref), (k_ref, kg_ref, kn_ref)):
            xv = x_ref[...]
            r = lax.rsqrt(_group_sum(xv * xv, bdv) * (1.0 / HEAD_DIM) + EPS)
            o_ref[...] = ((xv * r) * g_ref[...]).astype(BF16)
        vb_ref[...] = v_ref[...].astype(BF16)

    slab = lambda s: pl.BlockSpec((tt, SLAB), lambda i, s=s: (i, s))
    full = lambda shape: pl.BlockSpec(shape, lambda i: (0, 0))
    out = pl.BlockSpec((tt, SLAB), lambda i: (i, 0))
    return pl.pallas_call(
        body, name="qknorm_fwd", grid=(t // tt,),
        in_specs=[slab(3), slab(4), slab(5), full((1, SLAB)), full((1, SLAB)), full((SLAB, SLAB))],
        out_specs=[out, out, out],
        out_shape=[jax.ShapeDtypeStruct((t, SLAB), BF16)] * 3,
        compiler_params=_cparams("parallel"),
    )(proj, proj, proj, qg, kg, bd)


def _qknorm_bwd(proj, dqn, dkn, dv, qg, kg, bd, tt):
    t = proj.shape[0]

    def body(q_ref, k_ref, dqn_ref, dkn_ref, dv_ref, qg_ref, kg_ref, bd_ref, dq_ref, dk_ref, dvb_ref, part_ref):
        bdv = bd_ref[...]
        parts = []
        for x_ref, d_ref, g_ref, o_ref in ((q_ref, dqn_ref, qg_ref, dq_ref), (k_ref, dkn_ref, kg_ref, dk_ref)):
            xv, dn = x_ref[...], d_ref[...]
            r = lax.rsqrt(_group_sum(xv * xv, bdv) * (1.0 / HEAD_DIM) + EPS)
            u = dn * g_ref[...]
            dx = r * u - xv * (r * r * r) * (_group_sum(u * xv, bdv) * (1.0 / HEAD_DIM))
            o_ref[...] = dx.astype(BF16)
            parts.append(_fold8(dn * xv * r))
        dvb_ref[...] = dv_ref[...].astype(BF16)
        part_ref[...] = jnp.concatenate(parts, axis=1)

    slab = lambda s: pl.BlockSpec((tt, SLAB), lambda i, s=s: (i, s))
    tile = pl.BlockSpec((tt, SLAB), lambda i: (i, 0))
    full = lambda shape: pl.BlockSpec(shape, lambda i: (0, 0))
    return pl.pallas_call(
        body, name="qknorm_bwd", grid=(t // tt,),
        in_specs=[slab(3), slab(4), tile, tile, tile, full((1, SLAB)), full((1, SLAB)), full((SLAB, SLAB))],
        out_specs=[tile, tile, tile, pl.BlockSpec((SUBLANES, 2 * SLAB), lambda i: (i, 0))],
        out_shape=[jax.ShapeDtypeStruct((t, SLAB), BF16)] * 3
        + [jax.ShapeDtypeStruct((t // tt * SUBLANES, 2 * SLAB), F32)],
        compiler_params=_cparams("parallel"),
    )(proj, proj, dqn, dkn, dv, qg, kg, bd)


def _conv_taps(z, z_prev, row):
    zm1 = jnp.where(row == 0, z_prev[7:8], pltpu.roll(z, 1, 0))
    zm2 = jnp.where(row == 0, z_prev[6:7], jnp.where(row == 1, z_prev[7:8], pltpu.roll(z, 2, 0)))
    return zm1, zm2


def _conv_fwd(proj, cw, cb, tt):
    t = proj.shape[0]
    tb = tt // SUBLANES

    def body(b_ref, c_ref, u_ref, cp_ref, up_ref, cw_ref, cb_ref, o_ref):
        i = pl.program_id(0)
        z = c_ref[...] * u_ref[...]
        z_prev = jnp.where(i > 0, cp_ref[...] * up_ref[...], 0.0)
        row = lax.broadcasted_iota(jnp.int32, (tt, 1), 0)
        zm1, zm2 = _conv_taps(z, z_prev, row)
        y = cw_ref[0:1] * zm2 + cw_ref[1:2] * zm1 + cw_ref[2:3] * z + cb_ref[...]
        o_ref[...] = (b_ref[...] * y).astype(BF16)

    slab = lambda s: pl.BlockSpec((tt, SLAB), lambda i, s=s: (i, s))
    prev = lambda s: pl.BlockSpec((SUBLANES, SLAB), lambda i, s=s: (jnp.maximum(i * tb - 1, 0), s))
    return pl.pallas_call(
        body, name="conv_fwd", grid=(t // tt,),
        in_specs=[slab(0), slab(1), slab(2), prev(1), prev(2),
                  pl.BlockSpec((SUBLANES, SLAB), lambda i: (0, 0)), pl.BlockSpec((1, SLAB), lambda i: (0, 0))],
        out_specs=pl.BlockSpec((tt, SLAB), lambda i: (i, 0)),
        out_shape=jax.ShapeDtypeStruct((t, SLAB), BF16),
        compiler_params=_cparams("parallel"),
    )(proj, proj, proj, proj, proj, cw, cb)


def _conv_bwd(proj, dycat, cw, cb, tt):
    t = proj.shape[0]
    tb = tt // SUBLANES
    nblk = t // SUBLANES

    def body(b_ref, c_ref, u_ref, cp_ref, up_ref, bn_ref, dy_ref, dyn_ref, cw_ref, cb_ref,
             db_ref, dc_ref, du_ref, part_ref):
        i = pl.program_id(0)
        c, u, b, dyc = c_ref[...], u_ref[...], b_ref[...], dy_ref[...]
        z = c * u
        z_prev = jnp.where(i > 0, cp_ref[...] * up_ref[...], 0.0)
        row = lax.broadcasted_iota(jnp.int32, (tt, 1), 0)
        zm1, zm2 = _conv_taps(z, z_prev, row)
        w0, w1, w2 = cw_ref[0:1], cw_ref[1:2], cw_ref[2:3]
        y = w0 * zm2 + w1 * zm1 + w2 * z + cb_ref[...]
        db_ref[...] = (dyc * y).astype(BF16)
        g = dyc * b
        g_next = jnp.where(i < pl.num_programs(0) - 1, dyn_ref[...] * bn_ref[...], 0.0)
        gp1 = jnp.where(row == tt - 1, g_next[0:1], pltpu.roll(g, tt - 1, 0))
        gp2 = jnp.where(row == tt - 2, g_next[0:1], jnp.where(row == tt - 1, g_next[1:2], pltpu.roll(g, tt - 2, 0)))
        dz = w2 * g + w1 * gp1 + w0 * gp2
        dc_ref[...] = (dz * u).astype(BF16)
        du_ref[...] = (dz * c).astype(BF16)
        part_ref[...] = jnp.concatenate([_fold8(g * zm2), _fold8(g * zm1), _fold8(g * z), _fold8(g)], axis=1)

    slab = lambda s: pl.BlockSpec((tt, SLAB), lambda i, s=s: (i, s))
    prev = lambda s: pl.BlockSpec((SUBLANES, SLAB), lambda i, s=s: (jnp.maximum(i * tb - 1, 0), s))
    nxt = lambda s: pl.BlockSpec((SUBLANES, SLAB), lambda i, s=s: (jnp.minimum((i + 1) * tb, nblk - 1), s))
    tile = pl.BlockSpec((tt, SLAB), lambda i: (i, 0))
    return pl.pallas_call(
        body, name="conv_bwd", grid=(t // tt,),
        in_specs=[slab(0), slab(1), slab(2), prev(1), prev(2), nxt(0), slab(0), nxt(0),
                  pl.BlockSpec((SUBLANES, SLAB), lambda i: (0, 0)), pl.BlockSpec((1, SLAB), lambda i: (0, 0))],
        out_specs=[tile, tile, tile, pl.BlockSpec((SUBLANES, 4 * SLAB), lambda i: (i, 0))],
        out_shape=[jax.ShapeDtypeStruct((t, SLAB), BF16)] * 3
        + [jax.ShapeDtypeStruct((t // tt * SUBLANES, 4 * SLAB), F32)],
        compiler_params=_cparams("parallel"),
    )(proj, proj, proj, proj, proj, proj, dycat, dycat, cw, cb)


def _tri_masks(n):
    r = lax.broadcasted_iota(jnp.int32, (n, n), 0)
    c = lax.broadcasted_iota(jnp.int32, (n, n), 1)
    return (r > c).astype(BF16), (r >= c).astype(BF16)


def _sb_scores(qh, k, r_run, tri, causal):
    z = _dot_nt(qh, k)
    softplus = jnp.maximum(z, 0.0) + jnp.log(1.0 + jnp.exp(-jnp.abs(z)))
    lk = -softplus
    if causal is not None:
        lk = jnp.where(causal, lk, 0.0)
    hi, lo = _split2(lk)
    later = _dot_nn(hi, tri) + _dot_nn(lo, tri) + r_run
    ls = z + lk
    arg = ls + later
    if causal is not None:
        arg = jnp.where(causal, arg, -1e30)
    return lk, ls, jnp.exp(arg)


def _attn_fwd(qn, kn, vb, tri, bq):
    t = qn.shape[0]
    scale = HEAD_DIM ** -0.5

    def body(q_ref, k_ref, v_ref, tri_ref, o_ref, ob_ref):
        qi = pl.program_id(1)
        lane = lax.broadcasted_iota(jnp.int32, (1, LANES), 1)
        r_i = lax.broadcasted_iota(jnp.int32, (bq, bq), 0)
        c_i = lax.broadcasted_iota(jnp.int32, (bq, bq), 1)
        diag = c_i < r_i
        triv = tri_ref[...]
        out = jnp.zeros((bq, LANES), F32)
        for hh in range(2):
            hmask = (lane < HEAD_DIM) if hh == 0 else (lane >= HEAD_DIM)
            qh = jnp.where(hmask, q_ref[...], 0) * scale

            def tile(kb, carry, causal):
                r_run, acc = carry
                ks = pl.multiple_of(kb * bq, bq)
                k = k_ref[pl.ds(ks, bq), :]
                v = jnp.where(hmask, v_ref[pl.ds(ks, bq), :], 0)
                lk, _, a = _sb_scores(qh, k, r_run, triv, causal)
                acc = acc + _dot_nn(a.astype(BF16), v)
                return r_run + jnp.sum(lk, axis=1, keepdims=True), acc

            carry = tile(qi, (jnp.zeros((bq, 1), F32), jnp.zeros((bq, LANES), F32)), diag)
            carry = lax.fori_loop(0, qi, lambda i, cr: tile(qi - 1 - i, cr, None), carry)
            out = out + carry[1]
        o_ref[...] = out
        ob_ref[...] = out.astype(BF16)

    qspec = pl.BlockSpec((bq, LANES), lambda h, i: (i, h))
    kspec = pl.BlockSpec((t, LANES), lambda h, i: (0, h))
    return pl.pallas_call(
        body, name="attn_fwd", grid=(SB_DIM // LANES, t // bq),
        in_specs=[qspec, kspec, kspec, pl.BlockSpec((bq, bq), lambda h, i: (0, 0))],
        out_specs=[qspec, qspec],
        out_shape=[jax.ShapeDtypeStruct((t, SB_DIM), F32), jax.ShapeDtypeStruct((t, SB_DIM), BF16)],
        compiler_params=_cparams("parallel", "arbitrary"),
    )(qn, kn, vb, tri)


def _attn_bwd(qn, kn, vb, o, dycat, tri, tri_inc, bq):
    t = qn.shape[0]
    scale = HEAD_DIM ** -0.5

    def body(q_ref, k_ref, v_ref, o_ref, do_ref, tri_ref, tinc_ref, dq_ref, dk_ref, dv_ref):
        qi = pl.program_id(1)

        @pl.when(qi == 0)
        def _():
            dk_ref[...] = jnp.zeros_like(dk_ref)
            dv_ref[...] = jnp.zeros_like(dv_ref)

        lane = lax.broadcasted_iota(jnp.int32, (1, LANES), 1)
        r_i = lax.broadcasted_iota(jnp.int32, (bq, bq), 0)
        c_i = lax.broadcasted_iota(jnp.int32, (bq, bq), 1)
        diag = c_i < r_i
        triv, tincv = tri_ref[...], tinc_ref[...]
        dq_out = jnp.zeros((bq, LANES), F32)
        for hh in range(2):
            hmask = (lane < HEAD_DIM) if hh == 0 else (lane >= HEAD_DIM)
            qh = jnp.where(hmask, q_ref[...], 0) * scale
            dob = jnp.where(hmask, do_ref[...], 0.0).astype(BF16)
            d_row = jnp.sum(dob.astype(F32) * o_ref[...], axis=1, keepdims=True)

            def tile(kb, carry, causal):
                r_run, g_run, dq = carry
                ks = pl.multiple_of(kb * bq, bq)
                k = k_ref[pl.ds(ks, bq), :]
                v = v_ref[pl.ds(ks, bq), :]
                lk, ls, a = _sb_scores(qh, k, r_run, triv, causal)
                ab = a.astype(BF16)
                e = _dot_nt(dob, v) * ab.astype(F32)
                hi, lo = _split2(e)
                e_later = d_row - (_dot_nn(hi, tincv) + _dot_nn(lo, tincv) + g_run)
                beta = jnp.exp(ls)
                dz = e * (1.0 - beta) - e_later * beta
                if causal is not None:
                    dz = jnp.where(causal, dz, 0.0)
                dzb = dz.astype(BF16)
                dq = dq + _dot_nn(dzb, k)
                dk_ref[pl.ds(ks, bq), :] += _dot_tn(dzb, qh)
                dv_ref[pl.ds(ks, bq), :] += _dot_tn(ab, dob)
                return (r_run + jnp.sum(lk, axis=1, keepdims=True),
                        g_run + jnp.sum(e, axis=1, keepdims=True), dq)

            zero = jnp.zeros((bq, 1), F32)
            carry = tile(qi, (zero, zero, jnp.zeros((bq, LANES), F32)), diag)
            carry = lax.fori_loop(0, qi, lambda i, cr: tile(qi - 1 - i, cr, None), carry)
            dq_out = dq_out + jnp.where(hmask, carry[2] * scale, 0.0)
        dq_ref[...] = dq_out

    qspec = pl.BlockSpec((bq, LANES), lambda h, i: (i, h))
    dospec = pl.BlockSpec((bq, LANES), lambda h, i: (i, h + CONV_DIM // LANES))
    kspec = pl.BlockSpec((t, LANES), lambda h, i: (0, h))
    full = pl.BlockSpec((bq, bq), lambda h, i: (0, 0))
    return pl.pallas_call(
        body, name="attn_bwd", grid=(SB_DIM // LANES, t // bq),
        in_specs=[qspec, kspec, kspec, qspec, dospec, full, full],
        out_specs=[qspec, kspec, kspec],
        out_shape=[jax.ShapeDtypeStruct((t, SB_DIM), F32)] * 3,
        compiler_params=_cparams("parallel", "arbitrary"),
    )(qn, kn, vb, o, dycat, tri, tri_inc)


def _ple_loss(x3, p2, tgt, gain, wpg, wppt, tt):
    t, d = x3.shape
    pdim = p2.shape[1]
    nt = t // tt

    def body(x_ref, p_ref, t_ref, g_ref, wg_ref, wp_ref,
             dx_ref, dxb_ref, dwg_ref, dwp_ref, gpart_ref, lpart_ref, accg_ref, accp_ref):
        i = pl.program_id(0)
        xv, gain_v = x_ref[...], g_ref[...]
        hb = ((xv * _rms_stats(xv)) * gain_v).astype(BF16)
        gate = jax.nn.sigmoid(_dot_nn(hb, wg_ref[...]))
        pb = p_ref[...].astype(BF16)
        pe = _dot_nt(pb, wp_ref[...])
        diff = xv + gate * pe - t_ref[...]
        lsum = jnp.sum(_fold8(diff * diff), axis=1, keepdims=True) * (0.5 / d)
        lpart_ref[...] = jnp.broadcast_to(lsum, (SUBLANES, LANES))
        dy = diff * (1.0 / d)
        dgz = ((dy * pe) * gate * (1.0 - gate)).astype(BF16)
        dpe = (dy * gate).astype(BF16)
        dx_n, grow = _rms_bwd(_dot_nt(dgz, wg_ref[...]), xv, gain_v)
        dx = dy + dx_n
        dx_ref[...] = dx
        dxb_ref[...] = dx.astype(BF16)
        gpart_ref[...] = _fold8(grow)
        sg = _dot_tn(hb, dgz)
        sp = _dot_tn(dpe, pb)

        @pl.when(i == 0)
        def _():
            accg_ref[...] = sg
            accp_ref[...] = sp

        @pl.when(i > 0)
        def _():
            accg_ref[...] += sg
            accp_ref[...] += sp

        @pl.when(i == nt - 1)
        def _():
            dwg_ref[...] = accg_ref[...].astype(BF16)
            dwp_ref[...] = accp_ref[...].astype(BF16)

    tile = lambda w: pl.BlockSpec((tt, w), lambda i: (i, 0))
    full = lambda shape: pl.BlockSpec(shape, lambda i: (0, 0))
    return pl.pallas_call(
        body, name="ple_loss", grid=(nt,),
        in_specs=[tile(d), tile(pdim), tile(d), full((1, d)), full((d, d)), full((d, pdim))],
        out_specs=[tile(d), tile(d), full((d, d)), full((d, pdim)),
                   pl.BlockSpec((SUBLANES, d), lambda i: (i, 0)), pl.BlockSpec((SUBLANES, LANES), lambda i: (i, 0))],
        out_shape=[jax.ShapeDtypeStruct((t, d), F32), jax.ShapeDtypeStruct((t, d), BF16),
                   jax.ShapeDtypeStruct((d, d), BF16), jax.ShapeDtypeStruct((d, pdim), BF16),
                   jax.ShapeDtypeStruct((nt * SUBLANES, d), F32), jax.ShapeDtypeStruct((nt * SUBLANES, LANES), F32)],
        scratch_shapes=[pltpu.VMEM((d, d), F32), pltpu.VMEM((d, pdim), F32)],
        compiler_params=_cparams("arbitrary"),
    )(x3, p2, tgt, gain, wpg, wppt)


def _pack_small(parts_gain, conv_part, qk_part):
    d = parts_gain[0].shape[1]
    ng = len(parts_gain)

    def body(*refs):
        g_refs, conv_ref, qk_ref, o_ref = refs[:ng], refs[ng], refs[ng + 1], refs[ng + 2]
        rows = [jnp.sum(r[...], axis=0, keepdims=True) for r in g_refs]
        cs = jnp.sum(conv_ref[...], axis=0, keepdims=True)
        qs = jnp.sum(qk_ref[...], axis=0, keepdims=True)
        rows.append(jnp.concatenate([cs[:, 3 * SLAB:], cs[:, :SLAB]], axis=1))
        rows.append(cs[:, SLAB:3 * SLAB])
        rows.append(qs)
        rid = lax.broadcasted_iota(jnp.int32, (2 * SUBLANES, 1), 0)
        out = jnp.zeros((2 * SUBLANES, d), F32)
        for idx, r in enumerate(rows):
            out = jnp.where(rid == idx, r, out)
        o_ref[...] = out

    return pl.pallas_call(
        body, name="pack_small", out_shape=jax.ShapeDtypeStruct((2 * SUBLANES, d), F32),
    )(*parts_gain, conv_part, qk_part)


def _sum_slots(name, slots, out_dtype=F32):
    _, r, c = slots.shape

    def body(s_ref, o_ref):
        acc = s_ref[0].astype(F32)
        for d in range(1, N_DEV):
            acc = acc + s_ref[d].astype(F32)
        o_ref[...] = acc.astype(o_ref.dtype)

    return pl.pallas_call(body, name=name, out_shape=jax.ShapeDtypeStruct((r, c), out_dtype),
                          compiler_params=pltpu.CompilerParams(vmem_limit_bytes=VMEM_LIMIT_BYTES))(slots)


def _adamw(name, w, g, m, v):
    c1 = 1.0 - ADAM_B1 ** ADAM_STEP
    c2 = 1.0 - ADAM_B2 ** ADAM_STEP

    def body(w_ref, g_ref, m_ref, v_ref, d_ref, nm_ref, nv_ref):
        gv = g_ref[...]
        nm = ADAM_B1 * m_ref[...] + (1.0 - ADAM_B1) * gv
        nv = ADAM_B2 * v_ref[...] + (1.0 - ADAM_B2) * (gv * gv)
        d_ref[...] = -ADAM_LR * ((nm / c1) / (jnp.sqrt(nv / c2) + ADAM_EPS) + ADAM_WD * w_ref[...])
        nm_ref[...] = nm
        nv_ref[...] = nv

    return pl.pallas_call(body, name=name, out_shape=[jax.ShapeDtypeStruct(w.shape, F32)] * 3,
                          compiler_params=pltpu.CompilerParams(vmem_limit_bytes=VMEM_LIMIT_BYTES))(w, g, m, v)


def _any_specs(n):
    return [pl.BlockSpec(memory_space=pl.ANY)] * n


def _all_gather(name, shards):
    n = len(shards)

    def body(*refs):
        ins, outs = refs[:n], refs[n:2 * n]
        send_sems, recv_sems, local_sems = refs[2 * n:]
        x, y, c = (lax.axis_index(a) for a in MESH_AXES)
        me, sibling = (x, y, c), (x, y, 1 - c)
        chips = [(1 - x, y), (x, 1 - y), (1 - x, 1 - y)]

        def rows(a, px, py, pc):
            r = ins[a].shape[0]
            return outs[a].at[pl.ds((4 * px + 2 * py + pc) * r, r), :]

        def copy(a, k, block, to, src=None):
            return pltpu.make_async_remote_copy(
                src_ref=rows(a, *block) if src is None else src, dst_ref=rows(a, *block),
                send_sem=send_sems.at[7 * a + k], recv_sem=recv_sems.at[7 * a + k],
                device_id=to, device_id_type=MESH)

        mine = [pltpu.make_async_copy(ins[a], rows(a, *me), local_sems.at[a]) for a in range(n)]
        for cp in mine:
            cp.start()
        first = []
        for a in range(n):
            first.append(copy(a, 0, me, sibling, src=ins[a]))
            first += [copy(a, 1 + j, me, (*chip, c), src=ins[a]) for j, chip in enumerate(chips)]
        for cp in first:
            cp.start()
        passed = []
        for j, chip in enumerate(chips):
            for a in range(n):
                copy(a, 1 + j, (*chip, c), me).wait_recv()
                fwd = copy(a, 4 + j, (*chip, c), sibling)
                fwd.start()
                passed.append(fwd)
        for a in range(n):
            copy(a, 0, sibling, me).wait_recv()
            for j, chip in enumerate(chips):
                copy(a, 4 + j, (*chip, 1 - c), me).wait_recv()
        for cp in first + passed:
            cp.wait_send()
        for cp in mine:
            cp.wait()

    return pl.pallas_call(
        body, name=name, in_specs=_any_specs(n), out_specs=_any_specs(n),
        out_shape=[jax.ShapeDtypeStruct((N_DEV * s.shape[0], s.shape[1]), s.dtype) for s in shards],
        scratch_shapes=[pltpu.SemaphoreType.DMA((7 * n,)), pltpu.SemaphoreType.DMA((7 * n,)),
                        pltpu.SemaphoreType.DMA((n,))],
    )(*shards)


def _exchange_shards(name, fulls):
    n = len(fulls)
    flips = [(fx, fy, fc) for fx in (0, 1) for fy in (0, 1) for fc in (0, 1)][1:]

    def body(*refs):
        ins, outs = refs[:n], refs[n:2 * n]
        send_sems, recv_sems, local_sems = refs[2 * n:]
        x, y, c = (lax.axis_index(a) for a in MESH_AXES)
        me_idx = 4 * x + 2 * y + c

        def block(ref, a, idx):
            r = ins[a].shape[0] // N_DEV
            return ref.at[pl.ds(idx * r, r), :]

        def peer_of(flip):
            return tuple(1 - v if f else v for v, f in zip((x, y, c), flip))

        def copy(a, k):
            px, py, pc = peer_of(flips[k])
            p_idx = 4 * px + 2 * py + pc
            send = pltpu.make_async_remote_copy(
                src_ref=block(ins[a], a, p_idx), dst_ref=block(outs[a], a, me_idx),
                send_sem=send_sems.at[7 * a + k], recv_sem=recv_sems.at[7 * a + k],
                device_id=(px, py, pc), device_id_type=MESH)
            recv = pltpu.make_async_remote_copy(
                src_ref=block(ins[a], a, p_idx), dst_ref=block(outs[a], a, p_idx),
                send_sem=send_sems.at[7 * a + k], recv_sem=recv_sems.at[7 * a + k],
                device_id=(px, py, pc), device_id_type=MESH)
            return send, recv

        mine = [pltpu.make_async_copy(block(ins[a], a, me_idx), block(outs[a], a, me_idx), local_sems.at[a])
                for a in range(n)]
        for cp in mine:
            cp.start()
        copies = [copy(a, k) for a in range(n) for k in range(7)]
        for send, _ in copies:
            send.start()
        for send, recv in copies:
            recv.wait_recv()
            send.wait_send()
        for cp in mine:
            cp.wait()

    return pl.pallas_call(
        body, name=name, in_specs=_any_specs(n), out_specs=_any_specs(n),
        out_shape=[jax.ShapeDtypeStruct(f.shape, f.dtype) for f in fulls],
        scratch_shapes=[pltpu.SemaphoreType.DMA((7 * n,)), pltpu.SemaphoreType.DMA((7 * n,)),
                        pltpu.SemaphoreType.DMA((n,))],
    )(*fulls)


def _ffn_fwd(tag, x, h, wgt, wut, wd, tt_nt, tt_nn):
    f = wgt.shape[0]
    (a,) = _mm_nt(f"{tag}_gate_up", [h], [wgt, wut], [(0, 0), (0, 1)],
                  lambda accs: [jax.nn.silu(accs[0]) * accs[1]], [BF16], tt_nt, _pick(f, 256))
    (out,) = _mm_nn(f"{tag}_down", [(a, wd, 0)], [x], [],
                    lambda acc, rows, fulls: [rows[0] + FFN_RES * acc], [("tile", F32)], tt_nn, f // 2)
    return out


def _norm_bwd_epilogue(acc, rows, fulls):
    x_in, dy = rows
    dx_n, grow = _rms_bwd(acc, x_in, fulls[0])
    dx = dy + dx_n
    return [dx, dx, _fold8(grow)]


_NORM_BWD_OUTS = [("tile", F32), ("tile", BF16), ("part", F32)]


def _ffn_bwd(tag, x_in, h, dy, dyb, gain, wgt, wut, wd, tt_nt, tt_nn):
    f = wgt.shape[0]

    def epilogue(accs):
        g, u, da = accs[0], accs[1], FFN_RES * accs[2]
        sg = jax.nn.sigmoid(g)
        s = g * sg
        return [da * u * (sg * (1.0 + g * (1.0 - sg))), da * s, s * u]

    dg, du, a = _mm_nt(f"{tag}_bwd_hidden", [h, dyb], [wgt, wut, wd], [(0, 0), (0, 1), (1, 2)],
                       epilogue, [BF16, BF16, BF16], tt_nt, _pick(f, 256))
    dx, dxb, gpart = _mm_nn(f"{tag}_bwd_dx", [(dg, wgt, 0), (du, wut, 0)], [x_in, dy], [gain],
                            _norm_bwd_epilogue, _NORM_BWD_OUTS, tt_nn, f // 2)
    tt_tn = tt_nt
    dwg = _mm_tn(f"{tag}_dwg", dg, h, 1.0, f // 2, tt_tn)
    dwu = _mm_tn(f"{tag}_dwu", du, h, 1.0, f // 2, tt_tn)
    dwd = _mm_tn(f"{tag}_dwd", a, dyb, FFN_RES, f // 2, tt_tn)
    return dx, dxb, gpart, dwg, dwu, dwd


def kernel(x, p, ffn1_norm, ffn1_w_gate, ffn1_w_up, ffn1_w_down, mix_norm, w_in, conv_w, conv_b, q_norm, k_norm, w_out, ffn2_norm, ffn2_w_gate, ffn2_w_up, ffn2_w_down, ple_norm, ple_w_gate, ple_w_proj, loss_target, m_ffn1_norm, m_ffn1_w_gate, m_ffn1_w_up, m_ffn1_w_down, m_mix_norm, m_w_in, m_conv_w, m_conv_b, m_q_norm, m_k_norm, m_w_out, m_ffn2_norm, m_ffn2_w_gate, m_ffn2_w_up, m_ffn2_w_down, m_ple_norm, m_ple_w_gate, m_ple_w_proj, v_ffn1_norm, v_ffn1_w_gate, v_ffn1_w_up, v_ffn1_w_down, v_mix_norm, v_w_in, v_conv_w, v_conv_b, v_q_norm, v_k_norm, v_w_out, v_ffn2_norm, v_ffn2_w_gate, v_ffn2_w_up, v_ffn2_w_down, v_ple_norm, v_ple_w_gate, v_ple_w_proj):
    x0, p2, tgt = x[0], p[0, 0], loss_target[0]
    t, d = x0.shape
    tt_nt = _pick(t, 1024)
    tt_nn = _pick(t, 512)
    tt_ew = _pick(t, 512)
    tt_ple = _pick(t, 256)
    bq = _pick(t, 256)

    t_bf = lambda w: w[0].T.astype(BF16)
    n_bf = lambda w: w[0].astype(BF16)
    cw_tile = jnp.zeros((SUBLANES, LANES), F32).at[:conv_w.shape[1], :conv_w.shape[2]].set(conv_w[0])
    shards = [t_bf(ffn1_w_gate), t_bf(ffn1_w_up), n_bf(ffn1_w_down), t_bf(w_in), n_bf(w_out),
              t_bf(ffn2_w_gate), t_bf(ffn2_w_up), n_bf(ffn2_w_down), n_bf(ple_w_gate), t_bf(ple_w_proj), cw_tile]
    wg1t, wu1t, wd1, wint, wout, wg2t, wu2t, wd2, wpg, wppt, cw_all = _all_gather("gather_weights", shards)
    ncs = conv_w.shape[2]
    cw_full = cw_all.reshape(N_DEV, SUBLANES, LANES)[:, :, :ncs].transpose(1, 0, 2).reshape(SUBLANES, N_DEV * ncs)

    qg = jnp.tile(q_norm, (1, SB_DIM // HEAD_DIM))
    kg = jnp.tile(k_norm, (1, SB_DIM // HEAD_DIM))
    gi = lax.broadcasted_iota(jnp.int32, (SLAB, SLAB), 0) // HEAD_DIM
    gj = lax.broadcasted_iota(jnp.int32, (SLAB, SLAB), 1) // HEAD_DIM
    bd = (gi == gj).astype(BF16)
    tri, tri_inc = _tri_masks(bq)

    h1 = _rmsnorm("ffn1_norm", x0, ffn1_norm, tt_ew)
    x1 = _ffn_fwd("ffn1", x0, h1, wg1t, wu1t, wd1, tt_nt, tt_nn)
    h2 = _rmsnorm("mix_norm", x1, mix_norm, tt_ew)
    (proj,) = _mm_nt("in_proj", [h2], [wint], [(0, 0)], lambda accs: accs, [F32], tt_nt, SLAB)
    y_conv = _conv_fwd(proj, cw_full, conv_b, tt_ew)
    qn, kn, vb = _qknorm_fwd(proj, qg, kg, bd, tt_ew)
    o, ob = _attn_fwd(qn, kn, vb, tri, bq)
    (x2,) = _mm_nn("out_proj", [(y_conv, wout, 0), (ob, wout, 1)], [x1], [],
                   lambda acc, rows, fulls: [rows[0] + acc], [("tile", F32)], tt_nn, SLAB)
    h3 = _rmsnorm("ffn2_norm", x2, ffn2_norm, tt_ew)
    x3 = _ffn_fwd("ffn2", x2, h3, wg2t, wu2t, wd2, tt_nt, tt_nn)

    dx3, dx3b, dwpg, dwppt, gp_ple, lpart = _ple_loss(x3, p2, tgt, ple_norm, wpg, wppt, tt_ple)
    loss = lax.psum(jnp.sum(lpart[:, 0]), MESH_AXES)
    dx2, dx2b, gp_ffn2, dwg2, dwu2, dwd2 = _ffn_bwd("ffn2", x2, h3, dx3, dx3b, ffn2_norm, wg2t, wu2t, wd2,
                                                     tt_nt, tt_nn)
    (dycat,) = _mm_nt("out_proj_bwd", [dx2b], [wout], [(0, 0)], lambda accs: accs, [F32], tt_nt, SLAB)
    dwout = jnp.concatenate([_mm_tn("dwout_conv", y_conv, dx2b, 1.0, SLAB, tt_nt),
                             _mm_tn("dwout_attn", ob, dx2b, 1.0, SLAB, tt_nt)], axis=0)
    dqn, dkn, dv = _attn_bwd(qn, kn, vb, o, dycat, tri, tri_inc, bq)
    dq, dk, dvb, qk_part = _qknorm_bwd(proj, dqn, dkn, dv, qg, kg, bd, tt_ew)
    db, dc, du, conv_part = _conv_bwd(proj, dycat, cw_full, conv_b, tt_ew)
    dproj = [db, dc, du, dq, dk, dvb]
    dx1, dx1b, gp_mix = _mm_nn("in_proj_bwd", [(dp, wint, s) for s, dp in enumerate(dproj)], [x1, dx2], [mix_norm],
                               _norm_bwd_epilogue, _NORM_BWD_OUTS, tt_nn, SLAB)
    dwin = jnp.concatenate([_mm_tn(f"dwin_{s}", dp, h2, 1.0, SLAB, tt_nt) for s, dp in enumerate(dproj)], axis=0)
    dx0, _, gp_ffn1, dwg1, dwu1, dwd1 = _ffn_bwd("ffn1", x0, h1, dx1, dx1b, ffn1_norm, wg1t, wu1t, wd1, tt_nt, tt_nn)

    fulls = [dwg1, dwu1, dwd1, dwin, dwout, dwg2, dwu2, dwd2, dwpg, dwppt]
    slots = _exchange_shards("exchange_grads", fulls)
    sums = [_sum_slots(f"sum_grads_{i}", s.reshape(N_DEV, s.shape[0] // N_DEV, s.shape[1]))
            for i, s in enumerate(slots)]
    g_wg1, g_wu1, g_wd1, g_win, g_wout, g_wg2, g_wu2, g_wd2, g_wpg, g_wpp = sums
    small = _pack_small([gp_ffn1, gp_mix, gp_ffn2, gp_ple], conv_part, qk_part)
    (small_all,) = _all_gather("gather_small_grads", [small])
    sm = _sum_slots("sum_small_grads", small_all.reshape(N_DEV, 2 * SUBLANES, d))
    fold = lambda r: r.reshape(SB_DIM // HEAD_DIM, HEAD_DIM).sum(axis=0)[None]
    me_idx = 4 * lax.axis_index("x") + 2 * lax.axis_index("y") + lax.axis_index("c")
    cw_grad = jnp.stack([sm[4, SLAB:], sm[5, :SLAB], sm[5, SLAB:]])
    grads = {
        "ffn1_norm": sm[0:1], "ffn1_w_gate": g_wg1.T, "ffn1_w_up": g_wu1.T, "ffn1_w_down": g_wd1,
        "mix_norm": sm[1:2], "w_in": g_win.T, "conv_w": lax.dynamic_slice_in_dim(cw_grad, me_idx * ncs, ncs, axis=1),
        "conv_b": sm[4:5, :SLAB], "q_norm": fold(sm[6, :SLAB]), "k_norm": fold(sm[6, SLAB:]),
        "w_out": g_wout, "ffn2_norm": sm[2:3], "ffn2_w_gate": g_wg2.T, "ffn2_w_up": g_wu2.T, "ffn2_w_down": g_wd2,
        "ple_norm": sm[3:4], "ple_w_gate": g_wpg, "ple_w_proj": g_wpp.T,
    }

    weights = dict(ffn1_norm=ffn1_norm, ffn1_w_gate=ffn1_w_gate, ffn1_w_up=ffn1_w_up, ffn1_w_down=ffn1_w_down,
                   mix_norm=mix_norm, w_in=w_in, conv_w=conv_w, conv_b=conv_b, q_norm=q_norm, k_norm=k_norm,
                   w_out=w_out, ffn2_norm=ffn2_norm, ffn2_w_gate=ffn2_w_gate, ffn2_w_up=ffn2_w_up,
                   ffn2_w_down=ffn2_w_down, ple_norm=ple_norm, ple_w_gate=ple_w_gate, ple_w_proj=ple_w_proj)
    m_in = dict(ffn1_norm=m_ffn1_norm, ffn1_w_gate=m_ffn1_w_gate, ffn1_w_up=m_ffn1_w_up, ffn1_w_down=m_ffn1_w_down,
                mix_norm=m_mix_norm, w_in=m_w_in, conv_w=m_conv_w, conv_b=m_conv_b, q_norm=m_q_norm,
                k_norm=m_k_norm, w_out=m_w_out, ffn2_norm=m_ffn2_norm, ffn2_w_gate=m_ffn2_w_gate,
                ffn2_w_up=m_ffn2_w_up, ffn2_w_down=m_ffn2_w_down, ple_norm=m_ple_norm, ple_w_gate=m_ple_w_gate,
                ple_w_proj=m_ple_w_proj)
    v_in = dict(ffn1_norm=v_ffn1_norm, ffn1_w_gate=v_ffn1_w_gate, ffn1_w_up=v_ffn1_w_up, ffn1_w_down=v_ffn1_w_down,
                mix_norm=v_mix_norm, w_in=v_w_in, conv_w=v_conv_w, conv_b=v_conv_b, q_norm=v_q_norm,
                k_norm=v_k_norm, w_out=v_w_out, ffn2_norm=v_ffn2_norm, ffn2_w_gate=v_ffn2_w_gate,
                ffn2_w_up=v_ffn2_w_up, ffn2_w_down=v_ffn2_w_down, ple_norm=v_ple_norm, ple_w_gate=v_ple_w_gate,
                ple_w_proj=v_ple_w_proj)
    g_out, d_out, m_out, v_out = [], [], [], []
    for name, w in weights.items():
        w2 = w.reshape(w.shape[-2:])
        g2 = grads[name].reshape(w2.shape)
        dlt, nm, nv = _adamw(f"adamw_{name}", w2, g2, m_in[name].reshape(w2.shape), v_in[name].reshape(w2.shape))
        g_out.append(g2.reshape(w.shape))
        d_out.append(dlt.reshape(w.shape))
        m_out.append(nm.reshape(w.shape))
        v_out.append(nv.reshape(w.shape))
    return (loss, dx0[None], *g_out, *d_out, *m_out, *v_out)
```

```python
import jax
import jax.numpy as jnp
from jax import lax
from jax.experimental import pallas as pl
from jax.experimental.pallas import tpu as pltpu

F32 = jnp.float32
BF16 = jnp.bfloat16

EPS = 1e-6
FFN_RES = 0.5
HEAD_DIM = 64
CONV_DIM = 512
SB_DIM = 512
SLAB = 512
N_DEV = 8
MESH_AXES = ("x", "y", "c")
MESH = pl.DeviceIdType.MESH

ADAM_LR = 0.001
ADAM_B1 = 0.9
ADAM_B2 = 0.999
ADAM_EPS = 1e-08
ADAM_WD = 0.01
ADAM_STEP = 10

VMEM_LIMIT_BYTES = 56 * 1024 * 1024
SUBLANES = 8
LANES = 128


def _cparams(*semantics):
    return pltpu.CompilerParams(dimension_semantics=semantics, vmem_limit_bytes=VMEM_LIMIT_BYTES)


def _dot_nn(a, b):
    return jnp.dot(a, b, preferred_element_type=F32)


def _dot_nt(a, b):
    return lax.dot_general(a, b, (((1,), (1,)), ((), ())), preferred_element_type=F32)


def _dot_tn(a, b):
    return lax.dot_general(a, b, (((0,), (0,)), ((), ())), preferred_element_type=F32)


def _fold8(v):
    rows, cols = v.shape
    return jnp.sum(v.reshape(rows // SUBLANES, SUBLANES, cols), axis=0)


def _split2(v):
    hi = v.astype(BF16)
    lo = (v - hi.astype(F32)).astype(BF16)
    return hi, lo


def _rms_stats(x):
    return lax.rsqrt(jnp.mean(x * x, axis=-1, keepdims=True) + EPS)


def _rms_bwd(dh, x, gain):
    r = _rms_stats(x)
    u = dh * gain
    dx = r * u - x * (r * r * r) * jnp.mean(u * x, axis=-1, keepdims=True)
    return dx, dh * x * r


def _pick(n, pref):
    return pref if n % pref == 0 else n


def _rmsnorm(name, x, gain, tt):
    t, d = x.shape

    def body(x_ref, g_ref, o_ref):
        xv = x_ref[...]
        o_ref[...] = ((xv * _rms_stats(xv)) * g_ref[...]).astype(BF16)

    return pl.pallas_call(
        body, name=name, grid=(t // tt,),
        in_specs=[pl.BlockSpec((tt, d), lambda i: (i, 0)), pl.BlockSpec((1, d), lambda i: (0, 0))],
        out_specs=pl.BlockSpec((tt, d), lambda i: (i, 0)),
        out_shape=jax.ShapeDtypeStruct((t, d), BF16),
        compiler_params=_cparams("parallel"),
    )(x, gain)


def _mm_nt(name, a_list, w_list, pairs, epilogue, out_dtypes, tt, tn):
    t = a_list[0].shape[0]
    n = w_list[0].shape[0]
    na, nw = len(a_list), len(w_list)

    def body(*refs):
        a_refs, w_refs, o_refs = refs[:na], refs[na:na + nw], refs[na + nw:]
        accs = [_dot_nt(a_refs[ai][...], w_refs[wi][...]) for ai, wi in pairs]
        for o_ref, o in zip(o_refs, epilogue(accs)):
            o_ref[...] = o.astype(o_ref.dtype)

    in_specs = ([pl.BlockSpec((tt, a.shape[1]), lambda i, j: (i, 0)) for a in a_list]
                + [pl.BlockSpec((tn, w.shape[1]), lambda i, j: (j, 0)) for w in w_list])
    return pl.pallas_call(
        body, name=name, grid=(t // tt, n // tn), in_specs=in_specs,
        out_specs=[pl.BlockSpec((tt, tn), lambda i, j: (i, j)) for _ in out_dtypes],
        out_shape=[jax.ShapeDtypeStruct((t, n), dt) for dt in out_dtypes],
        compiler_params=_cparams("parallel", "arbitrary"),
    )(*a_list, *w_list)


def _mm_nn(name, pairs, rows, fulls, epilogue, out_kinds, tt, tk):
    t, k_total = pairs[0][0].shape
    n = pairs[0][1].shape[1]
    nk = k_total // tk
    nt = t // tt
    npair, nrow, nfull = len(pairs), len(rows), len(fulls)

    def body(*refs):
        a_refs = refs[:npair]
        w_refs = refs[npair:2 * npair]
        r_refs = refs[2 * npair:2 * npair + nrow]
        f_refs = refs[2 * npair + nrow:2 * npair + nrow + nfull]
        o_refs = refs[2 * npair + nrow + nfull:-1]
        acc_ref = refs[-1]
        k = pl.program_id(1)
        s = _dot_nn(a_refs[0][...], w_refs[0][...])
        for a_ref, w_ref in zip(a_refs[1:], w_refs[1:]):
            s = s + _dot_nn(a_ref[...], w_ref[...])

        @pl.when(k == 0)
        def _():
            acc_ref[...] = s

        @pl.when(k > 0)
        def _():
            acc_ref[...] += s

        @pl.when(k == nk - 1)
        def _():
            outs = epilogue(acc_ref[...], [r[...] for r in r_refs], [f[...] for f in f_refs])
            for o_ref, o in zip(o_refs, outs):
                o_ref[...] = o.astype(o_ref.dtype)

    in_specs = ([pl.BlockSpec((tt, tk), lambda i, k: (i, k)) for _ in pairs]
                + [pl.BlockSpec((tk, n), (lambda i, k, off=off: (k + off, 0))) for _, _, off in pairs]
                + [pl.BlockSpec((tt, n), lambda i, k: (i, 0)) for _ in rows]
                + [pl.BlockSpec((1, n), lambda i, k: (0, 0)) for _ in fulls])
    out_specs, out_shape = [], []
    for kind, dt in out_kinds:
        if kind == "tile":
            out_specs.append(pl.BlockSpec((tt, n), lambda i, k: (i, 0)))
            out_shape.append(jax.ShapeDtypeStruct((t, n), dt))
        else:
            out_specs.append(pl.BlockSpec((SUBLANES, n), lambda i, k: (i, 0)))
            out_shape.append(jax.ShapeDtypeStruct((nt * SUBLANES, n), dt))
    return pl.pallas_call(
        body, name=name, grid=(nt, nk), in_specs=in_specs, out_specs=out_specs, out_shape=out_shape,
        scratch_shapes=[pltpu.VMEM((tt, n), F32)],
        compiler_params=_cparams("parallel", "arbitrary"),
    )(*[a for a, _, _ in pairs], *[w for _, w, _ in pairs], *rows, *fulls)


def _mm_tn(name, a, b, scale, tm, tt):
    t, m = a.shape
    n = b.shape[1]
    nt = t // tt

    def body(a_ref, b_ref, o_ref, acc_ref):
        k = pl.program_id(1)
        s = _dot_tn(a_ref[...], b_ref[...])

        @pl.when(k == 0)
        def _():
            acc_ref[...] = s

        @pl.when(k > 0)
        def _():
            acc_ref[...] += s

        @pl.when(k == nt - 1)
        def _():
            o_ref[...] = (acc_ref[...] * scale).astype(o_ref.dtype)

    return pl.pallas_call(
        body, name=name, grid=(m // tm, nt),
        in_specs=[pl.BlockSpec((tt, tm), lambda i, k: (k, i)), pl.BlockSpec((tt, n), lambda i, k: (k, 0))],
        out_specs=pl.BlockSpec((tm, n), lambda i, k: (i, 0)),
        out_shape=jax.ShapeDtypeStruct((m, n), BF16),
        scratch_shapes=[pltpu.VMEM((tm, n), F32)],
        compiler_params=_cparams("parallel", "arbitrary"),
    )(a, b)


def _group_sum(v, bd):
    hi = v.astype(BF16)
    r1 = v - hi.astype(F32)
    mid = r1.astype(BF16)
    lo = (r1 - mid.astype(F32)).astype(BF16)
    return _dot_nn(hi, bd) + _dot_nn(mid, bd) + _dot_nn(lo, bd)


def _qknorm_fwd(proj, qg, kg, bd, tt):
    t = proj.shape[0]

    def body(q_ref, k_ref, v_ref, qg_ref, kg_ref, bd_ref, qn_ref, kn_ref, vb_ref):
        bdv = bd_ref[...]
        for x_ref, g_ref, o_ref in ((q_ref, qg_ref, qn_ref), (k_ref, kg_ref, kn_ref)):
            xv = x_ref[...]
            r = lax.rsqrt(_group_sum(xv * xv, bdv) * (1.0 / HEAD_DIM) + EPS)
            o_ref[...] = ((xv * r) * g_ref[...]).astype(BF16)
        vb_ref[...] = v_ref[...].astype(BF16)

    slab = lambda s: pl.BlockSpec((tt, SLAB), lambda i, s=s: (i, s))
    full = lambda shape: pl.BlockSpec(shape, lambda i: (0, 0))
    out = pl.BlockSpec((tt, SLAB), lambda i: (i, 0))
    return pl.pallas_call(
        body, name="qknorm_fwd", grid=(t // tt,),
        in_specs=[slab(3), slab(4), slab(5), full((1, SLAB)), full((1, SLAB)), full((SLAB, SLAB))],
        out_specs=[out, out, out],
        out_shape=[jax.ShapeDtypeStruct((t, SLAB), BF16)] * 3,
        compiler_params=_cparams("parallel"),
    )(proj, proj, proj, qg, kg, bd)


def _qknorm_bwd(proj, dqn, dkn, dv, qg, kg, bd, tt):
    t = proj.shape[0]

    def body(q_ref, k_ref, dqn_ref, dkn_ref, dv_ref, qg_ref, kg_ref, bd_ref, dq_ref, dk_ref, dvb_ref, part_ref):
        bdv = bd_ref[...]
        parts = []
        for x_ref, d_ref, g_ref, o_ref in ((q_ref, dqn_ref, qg_ref, dq_ref), (k_ref, dkn_ref, kg_ref, dk_ref)):
            xv, dn = x_ref[...], d_ref[...]
            r = lax.rsqrt(_group_sum(xv * xv, bdv) * (1.0 / HEAD_DIM) + EPS)
            u = dn * g_ref[...]
            dx = r * u - xv * (r * r * r) * (_group_sum(u * xv, bdv) * (1.0 / HEAD_DIM))
            o_ref[...] = dx.astype(BF16)
            parts.append(_fold8(dn * xv * r))
        dvb_ref[...] = dv_ref[...].astype(BF16)
        part_ref[...] = jnp.concatenate(parts, axis=1)

    slab = lambda s: pl.BlockSpec((tt, SLAB), lambda i, s=s: (i, s))
    tile = pl.BlockSpec((tt, SLAB), lambda i: (i, 0))
    full = lambda shape: pl.BlockSpec(shape, lambda i: (0, 0))
    return pl.pallas_call(
        body, name="qknorm_bwd", grid=(t // tt,),
        in_specs=[slab(3), slab(4), tile, tile, tile, full((1, SLAB)), full((1, SLAB)), full((SLAB, SLAB))],
        out_specs=[tile, tile, tile, pl.BlockSpec((SUBLANES, 2 * SLAB), lambda i: (i, 0))],
        out_shape=[jax.ShapeDtypeStruct((t, SLAB), BF16)] * 3
        + [jax.ShapeDtypeStruct((t // tt * SUBLANES, 2 * SLAB), F32)],
        compiler_params=_cparams("parallel"),
    )(proj, proj, dqn, dkn, dv, qg, kg, bd)


def _conv_taps(z, z_prev, row):
    zm1 = jnp.where(row == 0, z_prev[7:8], pltpu.roll(z, 1, 0))
    zm2 = jnp.where(row == 0, z_prev[6:7], jnp.where(row == 1, z_prev[7:8], pltpu.roll(z, 2, 0)))
    return zm1, zm2


def _conv_fwd(proj, cw, cb, tt):
    t = proj.shape[0]
    tb = tt // SUBLANES

    def body(b_ref, c_ref, u_ref, cp_ref, up_ref, cw_ref, cb_ref, o_ref):
        i = pl.program_id(0)
        z = c_ref[...] * u_ref[...]
        z_prev = jnp.where(i > 0, cp_ref[...] * up_ref[...], 0.0)
        row = lax.broadcasted_iota(jnp.int32, (tt, 1), 0)
        zm1, zm2 = _conv_taps(z, z_prev, row)
        y = cw_ref[0:1] * zm2 + cw_ref[1:2] * zm1 + cw_ref[2:3] * z + cb_ref[...]
        o_ref[...] = (b_ref[...] * y).astype(BF16)

    slab = lambda s: pl.BlockSpec((tt, SLAB), lambda i, s=s: (i, s))
    prev = lambda s: pl.BlockSpec((SUBLANES, SLAB), lambda i, s=s: (jnp.maximum(i * tb - 1, 0), s))
    return pl.pallas_call(
        body, name="conv_fwd", grid=(t // tt,),
        in_specs=[slab(0), slab(1), slab(2), prev(1), prev(2),
                  pl.BlockSpec((SUBLANES, SLAB), lambda i: (0, 0)), pl.BlockSpec((1, SLAB), lambda i: (0, 0))],
        out_specs=pl.BlockSpec((tt, SLAB), lambda i: (i, 0)),
        out_shape=jax.ShapeDtypeStruct((t, SLAB), BF16),
        compiler_params=_cparams("parallel"),
    )(proj, proj, proj, proj, proj, cw, cb)


def _conv_bwd(proj, dycat, cw, cb, tt):
    t = proj.shape[0]
    tb = tt // SUBLANES
    nblk = t // SUBLANES

    def body(b_ref, c_ref, u_ref, cp_ref, up_ref, bn_ref, dy_ref, dyn_ref, cw_ref, cb_ref,
             db_ref, dc_ref, du_ref, part_ref):
        i = pl.program_id(0)
        c, u, b, dyc = c_ref[...], u_ref[...], b_ref[...], dy_ref[...]
        z = c * u
        z_prev = jnp.where(i > 0, cp_ref[...] * up_ref[...], 0.0)
        row = lax.broadcasted_iota(jnp.int32, (tt, 1), 0)
        zm1, zm2 = _conv_taps(z, z_prev, row)
        w0, w1, w2 = cw_ref[0:1], cw_ref[1:2], cw_ref[2:3]
        y = w0 * zm2 + w1 * zm1 + w2 * z + cb_ref[...]
        db_ref[...] = (dyc * y).astype(BF16)
        g = dyc * b
        g_next = jnp.where(i < pl.num_programs(0) - 1, dyn_ref[...] * bn_ref[...], 0.0)
        gp1 = jnp.where(row == tt - 1, g_next[0:1], pltpu.roll(g, tt - 1, 0))
        gp2 = jnp.where(row == tt - 2, g_next[0:1], jnp.where(row == tt - 1, g_next[1:2], pltpu.roll(g, tt - 2, 0)))
        dz = w2 * g + w1 * gp1 + w0 * gp2
        dc_ref[...] = (dz * u).astype(BF16)
        du_ref[...] = (dz * c).astype(BF16)
        part_ref[...] = jnp.concatenate([_fold8(g * zm2), _fold8(g * zm1), _fold8(g * z), _fold8(g)], axis=1)

    slab = lambda s: pl.BlockSpec((tt, SLAB), lambda i, s=s: (i, s))
    prev = lambda s: pl.BlockSpec((SUBLANES, SLAB), lambda i, s=s: (jnp.maximum(i * tb - 1, 0), s))
    nxt = lambda s: pl.BlockSpec((SUBLANES, SLAB), lambda i, s=s: (jnp.minimum((i + 1) * tb, nblk - 1), s))
    tile = pl.BlockSpec((tt, SLAB), lambda i: (i, 0))
    return pl.pallas_call(
        body, name="conv_bwd", grid=(t // tt,),
        in_specs=[slab(0), slab(1), slab(2), prev(1), prev(2), nxt(0), slab(0), nxt(0),
                  pl.BlockSpec((SUBLANES, SLAB), lambda i: (0, 0)), pl.BlockSpec((1, SLAB), lambda i: (0, 0))],
        out_specs=[tile, tile, tile, pl.BlockSpec((SUBLANES, 4 * SLAB), lambda i: (i, 0))],
        out_shape=[jax.ShapeDtypeStruct((t, SLAB), BF16)] * 3
        + [jax.ShapeDtypeStruct((t // tt * SUBLANES, 4 * SLAB), F32)],
        compiler_params=_cparams("parallel"),
    )(proj, proj, proj, proj, proj, proj, dycat, dycat, cw, cb)


def _tri_masks(n):
    r = lax.broadcasted_iota(jnp.int32, (n, n), 0)
    c = lax.broadcasted_iota(jnp.int32, (n, n), 1)
    return (r > c).astype(BF16), (r >= c).astype(BF16)


def _sb_scores(qh, k, r_run, tri, causal):
    z = _dot_nt(qh, k)
    softplus = jnp.maximum(z, 0.0) + jnp.log(1.0 + jnp.exp(-jnp.abs(z)))
    lk = -softplus
    if causal is not None:
        lk = jnp.where(causal, lk, 0.0)
    hi, lo = _split2(lk)
    later = _dot_nn(hi, tri) + _dot_nn(lo, tri) + r_run
    ls = z + lk
    arg = ls + later
    if causal is not None:
        arg = jnp.where(causal, arg, -1e30)
    return lk, ls, jnp.exp(arg), later[:, 0:1] + lk[:, 0:1]


SB_DEAD_LOG = -111.0
CHAINS = ((0, 0), (0, 1), (1, 0), (1, 1))


def _all_dead(r_runs):
    m = r_runs[0]
    for r in r_runs[1:]:
        m = jnp.maximum(m, r)
    return (jnp.max(m) < SB_DEAD_LOG).astype(jnp.int32)


def _attn_fwd(qn, kn, vb, tri, sb):
    t = qn.shape[0]
    bq = 2 * sb
    scale = HEAD_DIM ** -0.5

    def body(q_ref, k_ref, v_ref, tri_ref, o_ref, ob_ref, acc_ref):
        qi = pl.program_id(1)
        lane = lax.broadcasted_iota(jnp.int32, (1, LANES), 1)
        hmasks = (lane < HEAD_DIM, lane >= HEAD_DIM)
        diag = lax.broadcasted_iota(jnp.int32, (sb, sb), 1) < lax.broadcasted_iota(jnp.int32, (sb, sb), 0)
        triv = tri_ref[...]
        qs = [[jnp.where(hm, q_ref[pl.ds(s * sb, sb), :], 0) * scale for hm in hmasks] for s in range(2)]
        acc_ref[...] = jnp.zeros_like(acc_ref)

        def load_kv(kb):
            ks = pl.multiple_of(kb * sb, sb)
            vraw = v_ref[pl.ds(ks, sb), :]
            return k_ref[pl.ds(ks, sb), :], [jnp.where(hm, vraw, 0) for hm in hmasks]

        def tile(s, hh, kv, r_run, causal):
            _, _, a, r_new = _sb_scores(qs[s][hh], kv[0], r_run, triv, causal)
            acc_ref[2 * s + hh] += _dot_nn(a.astype(BF16), kv[1][hh])
            return r_new

        zero = jnp.zeros((sb, 1), F32)
        kv_hi, kv_lo = load_kv(2 * qi + 1), load_kv(2 * qi)
        r_runs = [None] * 4
        for hh in range(2):
            r_runs[hh] = tile(0, hh, kv_lo, zero, diag)
            r_runs[2 + hh] = tile(1, hh, kv_lo, tile(1, hh, kv_hi, zero, diag), None)

        def step(carry):
            i, _, *rs = carry
            kv = load_kv(2 * qi - 1 - i)
            rs = [tile(s, hh, kv, rs[2 * s + hh], None) for s, hh in CHAINS]
            return (i + 1, _all_dead(rs), *rs)

        lax.while_loop(lambda c: jnp.logical_and(c[0] < 2 * qi, c[1] == 0), step,
                       (jnp.int32(0), _all_dead(r_runs), *r_runs))
        for s in range(2):
            out = acc_ref[2 * s] + acc_ref[2 * s + 1]
            o_ref[pl.ds(s * sb, sb), :] = out
            ob_ref[pl.ds(s * sb, sb), :] = out.astype(BF16)

    qspec = pl.BlockSpec((bq, LANES), lambda h, i: (i, h))
    kspec = pl.BlockSpec((t, LANES), lambda h, i: (0, h))
    return pl.pallas_call(
        body, name="attn_fwd", grid=(SB_DIM // LANES, t // bq),
        in_specs=[qspec, kspec, kspec, pl.BlockSpec((sb, sb), lambda h, i: (0, 0))],
        out_specs=[qspec, qspec],
        out_shape=[jax.ShapeDtypeStruct((t, SB_DIM), F32), jax.ShapeDtypeStruct((t, SB_DIM), BF16)],
        scratch_shapes=[pltpu.VMEM((4, sb, LANES), F32)],
        compiler_params=_cparams("parallel", "arbitrary"),
    )(qn, kn, vb, tri)


def _attn_bwd(qn, kn, vb, o, dycat, tri, tri_inc, sb):
    t = qn.shape[0]
    bq = 2 * sb
    scale = HEAD_DIM ** -0.5

    def body(q_ref, k_ref, v_ref, o_ref, do_ref, tri_ref, tinc_ref, dq_ref, dk_ref, dv_ref, dq_acc):
        qi = pl.program_id(1)

        @pl.when(qi == 0)
        def _():
            dk_ref[...] = jnp.zeros_like(dk_ref)
            dv_ref[...] = jnp.zeros_like(dv_ref)

        lane = lax.broadcasted_iota(jnp.int32, (1, LANES), 1)
        hmasks = (lane < HEAD_DIM, lane >= HEAD_DIM)
        diag = lax.broadcasted_iota(jnp.int32, (sb, sb), 1) < lax.broadcasted_iota(jnp.int32, (sb, sb), 0)
        triv, tincv = tri_ref[...], tinc_ref[...]
        qs, dobs, d_rows = [], [], []
        for s in range(2):
            rows = pl.ds(s * sb, sb)
            qs.append([jnp.where(hm, q_ref[rows, :], 0) * scale for hm in hmasks])
            dobs.append([jnp.where(hm, do_ref[rows, :], 0.0).astype(BF16) for hm in hmasks])
            d_rows.append([jnp.sum(d.astype(F32) * o_ref[rows, :], axis=1, keepdims=True) for d in dobs[s]])
        dq_acc[...] = jnp.zeros_like(dq_acc)

        def load_kv(kb):
            ks = pl.multiple_of(kb * sb, sb)
            return k_ref[pl.ds(ks, sb), :], v_ref[pl.ds(ks, sb), :], ks

        def tile(s, hh, kv, r_run, g_run, causal):
            k, v, _ = kv
            qh, dob = qs[s][hh], dobs[s][hh]
            _, ls, a, r_new = _sb_scores(qh, k, r_run, triv, causal)
            ab = a.astype(BF16)
            e = _dot_nt(dob, v) * ab.astype(F32)
            hi, lo = _split2(e)
            e_from = _dot_nn(hi, tincv) + _dot_nn(lo, tincv) + g_run
            beta = jnp.exp(ls)
            dz = e - beta * (e + (d_rows[s][hh] - e_from))
            if causal is not None:
                dz = jnp.where(causal, dz, 0.0)
            dzb = dz.astype(BF16)
            dq_acc[2 * s + hh] += _dot_nn(dzb, k)
            return r_new, e_from[:, 0:1], _dot_tn(dzb, qh), _dot_tn(ab, dob)

        def scatter(kv, parts):
            rows = pl.ds(kv[2], sb)
            dk_ref[rows, :] += sum(p[2] for p in parts[1:]) + parts[0][2]
            dv_ref[rows, :] += sum(p[3] for p in parts[1:]) + parts[0][3]

        zero = jnp.zeros((sb, 1), F32)
        kv_hi, kv_lo = load_kv(2 * qi + 1), load_kv(2 * qi)
        top = [tile(1, hh, kv_hi, zero, zero, diag) for hh in range(2)]
        scatter(kv_hi, top)
        low = [tile(0, hh, kv_lo, zero, zero, diag) for hh in range(2)]
        low += [tile(1, hh, kv_lo, top[hh][0], top[hh][1], None) for hh in range(2)]
        scatter(kv_lo, low)
        r_runs, g_runs = [p[0] for p in low], [p[1] for p in low]

        def step(carry):
            i, _, *rg = carry
            kv = load_kv(2 * qi - 1 - i)
            parts = [tile(s, hh, kv, rg[2 * s + hh], rg[4 + 2 * s + hh], None) for s, hh in CHAINS]
            scatter(kv, parts)
            rs = [p[0] for p in parts]
            return (i + 1, _all_dead(rs), *rs, *[p[1] for p in parts])

        lax.while_loop(lambda c: jnp.logical_and(c[0] < 2 * qi, c[1] == 0), step,
                       (jnp.int32(0), _all_dead(r_runs), *r_runs, *g_runs))
        for s in range(2):
            dq_ref[pl.ds(s * sb, sb), :] = jnp.where(hmasks[0], dq_acc[2 * s], dq_acc[2 * s + 1]) * scale

    qspec = pl.BlockSpec((bq, LANES), lambda h, i: (i, h))
    dospec = pl.BlockSpec((bq, LANES), lambda h, i: (i, h + CONV_DIM // LANES))
    kspec = pl.BlockSpec((t, LANES), lambda h, i: (0, h))
    full = pl.BlockSpec((sb, sb), lambda h, i: (0, 0))
    return pl.pallas_call(
        body, name="attn_bwd", grid=(SB_DIM // LANES, t // bq),
        in_specs=[qspec, kspec, kspec, qspec, dospec, full, full],
        out_specs=[qspec, kspec, kspec],
        out_shape=[jax.ShapeDtypeStruct((t, SB_DIM), F32)] * 3,
        scratch_shapes=[pltpu.VMEM((4, sb, LANES), F32)],
        compiler_params=_cparams("parallel", "arbitrary"),
    )(qn, kn, vb, o, dycat, tri, tri_inc)


def _ple_loss(x3, p2, tgt, gain, wpg, wppt, tt):
    t, d = x3.shape
    pdim = p2.shape[1]
    nt = t // tt

    def body(x_ref, p_ref, t_ref, g_ref, wg_ref, wp_ref,
             dx_ref, dxb_ref, dwg_ref, dwp_ref, gpart_ref, lpart_ref, accg_ref, accp_ref):
        i = pl.program_id(0)
        xv, gain_v = x_ref[...], g_ref[...]
        hb = ((xv * _rms_stats(xv)) * gain_v).astype(BF16)
        gate = jax.nn.sigmoid(_dot_nn(hb, wg_ref[...]))
        pb = p_ref[...].astype(BF16)
        pe = _dot_nt(pb, wp_ref[...])
        diff = xv + gate * pe - t_ref[...]
        lsum = jnp.sum(_fold8(diff * diff), axis=1, keepdims=True) * (0.5 / d)
        lpart_ref[...] = jnp.broadcast_to(lsum, (SUBLANES, LANES))
        dy = diff * (1.0 / d)
        dgz = ((dy * pe) * gate * (1.0 - gate)).astype(BF16)
        dpe = (dy * gate).astype(BF16)
        dx_n, grow = _rms_bwd(_dot_nt(dgz, wg_ref[...]), xv, gain_v)
        dx = dy + dx_n
        dx_ref[...] = dx
        dxb_ref[...] = dx.astype(BF16)
        gpart_ref[...] = _fold8(grow)
        sg = _dot_tn(hb, dgz)
        sp = _dot_tn(dpe, pb)

        @pl.when(i == 0)
        def _():
            accg_ref[...] = sg
            accp_ref[...] = sp

        @pl.when(i > 0)
        def _():
            accg_ref[...] += sg
            accp_ref[...] += sp

        @pl.when(i == nt - 1)
        def _():
            dwg_ref[...] = accg_ref[...].astype(BF16)
            dwp_ref[...] = accp_ref[...].astype(BF16)

    tile = lambda w: pl.BlockSpec((tt, w), lambda i: (i, 0))
    full = lambda shape: pl.BlockSpec(shape, lambda i: (0, 0))
    return pl.pallas_call(
        body, name="ple_loss", grid=(nt,),
        in_specs=[tile(d), tile(pdim), tile(d), full((1, d)), full((d, d)), full((d, pdim))],
        out_specs=[tile(d), tile(d), full((d, d)), full((d, pdim)),
                   pl.BlockSpec((SUBLANES, d), lambda i: (i, 0)), pl.BlockSpec((SUBLANES, LANES), lambda i: (i, 0))],
        out_shape=[jax.ShapeDtypeStruct((t, d), F32), jax.ShapeDtypeStruct((t, d), BF16),
                   jax.ShapeDtypeStruct((d, d), BF16), jax.ShapeDtypeStruct((d, pdim), BF16),
                   jax.ShapeDtypeStruct((nt * SUBLANES, d), F32), jax.ShapeDtypeStruct((nt * SUBLANES, LANES), F32)],
        scratch_shapes=[pltpu.VMEM((d, d), F32), pltpu.VMEM((d, pdim), F32)],
        compiler_params=_cparams("arbitrary"),
    )(x3, p2, tgt, gain, wpg, wppt)


def _pack_small(parts_gain, conv_part, qk_part):
    d = parts_gain[0].shape[1]
    ng = len(parts_gain)

    def body(*refs):
        g_refs, conv_ref, qk_ref, o_ref = refs[:ng], refs[ng], refs[ng + 1], refs[ng + 2]
        rows = [jnp.sum(r[...], axis=0, keepdims=True) for r in g_refs]
        cs = jnp.sum(conv_ref[...], axis=0, keepdims=True)
        qs = jnp.sum(qk_ref[...], axis=0, keepdims=True)
        rows.append(jnp.concatenate([cs[:, 3 * SLAB:], cs[:, :SLAB]], axis=1))
        rows.append(cs[:, SLAB:3 * SLAB])
        rows.append(qs)
        rid = lax.broadcasted_iota(jnp.int32, (2 * SUBLANES, 1), 0)
        out = jnp.zeros((2 * SUBLANES, d), F32)
        for idx, r in enumerate(rows):
            out = jnp.where(rid == idx, r, out)
        o_ref[...] = out

    return pl.pallas_call(
        body, name="pack_small", out_shape=jax.ShapeDtypeStruct((2 * SUBLANES, d), F32),
    )(*parts_gain, conv_part, qk_part)


def _sum_slots(name, slots, out_dtype=F32):
    _, r, c = slots.shape

    def body(s_ref, o_ref):
        acc = s_ref[0].astype(F32)
        for d in range(1, N_DEV):
            acc = acc + s_ref[d].astype(F32)
        o_ref[...] = acc.astype(o_ref.dtype)

    return pl.pallas_call(body, name=name, out_shape=jax.ShapeDtypeStruct((r, c), out_dtype),
                          compiler_params=pltpu.CompilerParams(vmem_limit_bytes=VMEM_LIMIT_BYTES))(slots)


def _adamw(name, w, g, m, v):
    c1 = 1.0 - ADAM_B1 ** ADAM_STEP
    c2 = 1.0 - ADAM_B2 ** ADAM_STEP

    def body(w_ref, g_ref, m_ref, v_ref, d_ref, nm_ref, nv_ref):
        gv = g_ref[...]
        nm = ADAM_B1 * m_ref[...] + (1.0 - ADAM_B1) * gv
        nv = ADAM_B2 * v_ref[...] + (1.0 - ADAM_B2) * (gv * gv)
        d_ref[...] = -ADAM_LR * ((nm / c1) / (jnp.sqrt(nv / c2) + ADAM_EPS) + ADAM_WD * w_ref[...])
        nm_ref[...] = nm
        nv_ref[...] = nv

    return pl.pallas_call(body, name=name, out_shape=[jax.ShapeDtypeStruct(w.shape, F32)] * 3,
                          compiler_params=pltpu.CompilerParams(vmem_limit_bytes=VMEM_LIMIT_BYTES))(w, g, m, v)


def _any_specs(n):
    return [pl.BlockSpec(memory_space=pl.ANY)] * n


def _all_gather(name, shards):
    n = len(shards)

    def body(*refs):
        ins, outs = refs[:n], refs[n:2 * n]
        send_sems, recv_sems, local_sems = refs[2 * n:]
        x, y, c = (lax.axis_index(a) for a in MESH_AXES)
        me, sibling = (x, y, c), (x, y, 1 - c)
        chips = [(1 - x, y), (x, 1 - y), (1 - x, 1 - y)]

        def rows(a, px, py, pc):
            r = ins[a].shape[0]
            return outs[a].at[pl.ds((4 * px + 2 * py + pc) * r, r), :]

        def copy(a, k, block, to, src=None):
            return pltpu.make_async_remote_copy(
                src_ref=rows(a, *block) if src is None else src, dst_ref=rows(a, *block),
                send_sem=send_sems.at[7 * a + k], recv_sem=recv_sems.at[7 * a + k],
                device_id=to, device_id_type=MESH)

        mine = [pltpu.make_async_copy(ins[a], rows(a, *me), local_sems.at[a]) for a in range(n)]
        for cp in mine:
            cp.start()
        first = []
        for a in range(n):
            first.append(copy(a, 0, me, sibling, src=ins[a]))
            first += [copy(a, 1 + j, me, (*chip, c), src=ins[a]) for j, chip in enumerate(chips)]
        for cp in first:
            cp.start()
        passed = []
        for j, chip in enumerate(chips):
            for a in range(n):
                copy(a, 1 + j, (*chip, c), me).wait_recv()
                fwd = copy(a, 4 + j, (*chip, c), sibling)
                fwd.start()
                passed.append(fwd)
        for a in range(n):
            copy(a, 0, sibling, me).wait_recv()
            for j, chip in enumerate(chips):
                copy(a, 4 + j, (*chip, 1 - c), me).wait_recv()
        for cp in first + passed:
            cp.wait_send()
        for cp in mine:
            cp.wait()

    return pl.pallas_call(
        body, name=name, in_specs=_any_specs(n), out_specs=_any_specs(n),
        out_shape=[jax.ShapeDtypeStruct((N_DEV * s.shape[0], s.shape[1]), s.dtype) for s in shards],
        scratch_shapes=[pltpu.SemaphoreType.DMA((7 * n,)), pltpu.SemaphoreType.DMA((7 * n,)),
                        pltpu.SemaphoreType.DMA((n,))],
    )(*shards)


def _exchange_shards(name, fulls):
    n = len(fulls)
    flips = [(fx, fy, fc) for fx in (0, 1) for fy in (0, 1) for fc in (0, 1)][1:]

    def body(*refs):
        ins, outs = refs[:n], refs[n:2 * n]
        send_sems, recv_sems, local_sems = refs[2 * n:]
        x, y, c = (lax.axis_index(a) for a in MESH_AXES)
        me_idx = 4 * x + 2 * y + c

        def block(ref, a, idx):
            r = ins[a].shape[0] // N_DEV
            return ref.at[pl.ds(idx * r, r), :]

        def peer_of(flip):
            return tuple(1 - v if f else v for v, f in zip((x, y, c), flip))

        def copy(a, k):
            px, py, pc = peer_of(flips[k])
            p_idx = 4 * px + 2 * py + pc
            send = pltpu.make_async_remote_copy(
                src_ref=block(ins[a], a, p_idx), dst_ref=block(outs[a], a, me_idx),
                send_sem=send_sems.at[7 * a + k], recv_sem=recv_sems.at[7 * a + k],
                device_id=(px, py, pc), device_id_type=MESH)
            recv = pltpu.make_async_remote_copy(
                src_ref=block(ins[a], a, p_idx), dst_ref=block(outs[a], a, p_idx),
                send_sem=send_sems.at[7 * a + k], recv_sem=recv_sems.at[7 * a + k],
                device_id=(px, py, pc), device_id_type=MESH)
            return send, recv

        mine = [pltpu.make_async_copy(block(ins[a], a, me_idx), block(outs[a], a, me_idx), local_sems.at[a])
                for a in range(n)]
        for cp in mine:
            cp.start()
        copies = [copy(a, k) for a in range(n) for k in range(7)]
        for send, _ in copies:
            send.start()
        for send, recv in copies:
            recv.wait_recv()
            send.wait_send()
        for cp in mine:
            cp.wait()

    return pl.pallas_call(
        body, name=name, in_specs=_any_specs(n), out_specs=_any_specs(n),
        out_shape=[jax.ShapeDtypeStruct(f.shape, f.dtype) for f in fulls],
        scratch_shapes=[pltpu.SemaphoreType.DMA((7 * n,)), pltpu.SemaphoreType.DMA((7 * n,)),
                        pltpu.SemaphoreType.DMA((n,))],
    )(*fulls)


def _ffn_fwd(tag, x, h, wgt, wut, wd, tt_nt, tt_nn):
    f = wgt.shape[0]
    (a,) = _mm_nt(f"{tag}_gate_up", [h], [wgt, wut], [(0, 0), (0, 1)],
                  lambda accs: [jax.nn.silu(accs[0]) * accs[1]], [BF16], tt_nt, _pick(f, 256))
    (out,) = _mm_nn(f"{tag}_down", [(a, wd, 0)], [x], [],
                    lambda acc, rows, fulls: [rows[0] + FFN_RES * acc], [("tile", F32)], tt_nn, f // 2)
    return out


def _norm_bwd_epilogue(acc, rows, fulls):
    x_in, dy = rows
    dx_n, grow = _rms_bwd(acc, x_in, fulls[0])
    dx = dy + dx_n
    return [dx, dx, _fold8(grow)]


_NORM_BWD_OUTS = [("tile", F32), ("tile", BF16), ("part", F32)]


def _ffn_bwd(tag, x_in, h, dy, dyb, gain, wgt, wut, wd, tt_nt, tt_nn):
    f = wgt.shape[0]

    def epilogue(accs):
        g, u, da = accs[0], accs[1], FFN_RES * accs[2]
        sg = jax.nn.sigmoid(g)
        s = g * sg
        return [da * u * (sg * (1.0 + g * (1.0 - sg))), da * s, s * u]

    dg, du, a = _mm_nt(f"{tag}_bwd_hidden", [h, dyb], [wgt, wut, wd], [(0, 0), (0, 1), (1, 2)],
                       epilogue, [BF16, BF16, BF16], tt_nt, _pick(f, 256))
    dx, dxb, gpart = _mm_nn(f"{tag}_bwd_dx", [(dg, wgt, 0), (du, wut, 0)], [x_in, dy], [gain],
                            _norm_bwd_epilogue, _NORM_BWD_OUTS, tt_nn, f // 2)
    tt_tn = tt_nt
    dwg = _mm_tn(f"{tag}_dwg", dg, h, 1.0, f // 2, tt_tn)
    dwu = _mm_tn(f"{tag}_dwu", du, h, 1.0, f // 2, tt_tn)
    dwd = _mm_tn(f"{tag}_dwd", a, dyb, FFN_RES, f // 2, tt_tn)
    return dx, dxb, gpart, dwg, dwu, dwd


def kernel(x, p, ffn1_norm, ffn1_w_gate, ffn1_w_up, ffn1_w_down, mix_norm, w_in, conv_w, conv_b, q_norm, k_norm, w_out, ffn2_norm, ffn2_w_gate, ffn2_w_up, ffn2_w_down, ple_norm, ple_w_gate, ple_w_proj, loss_target, m_ffn1_norm, m_ffn1_w_gate, m_ffn1_w_up, m_ffn1_w_down, m_mix_norm, m_w_in, m_conv_w, m_conv_b, m_q_norm, m_k_norm, m_w_out, m_ffn2_norm, m_ffn2_w_gate, m_ffn2_w_up, m_ffn2_w_down, m_ple_norm, m_ple_w_gate, m_ple_w_proj, v_ffn1_norm, v_ffn1_w_gate, v_ffn1_w_up, v_ffn1_w_down, v_mix_norm, v_w_in, v_conv_w, v_conv_b, v_q_norm, v_k_norm, v_w_out, v_ffn2_norm, v_ffn2_w_gate, v_ffn2_w_up, v_ffn2_w_down, v_ple_norm, v_ple_w_gate, v_ple_w_proj):
    x0, p2, tgt = x[0], p[0, 0], loss_target[0]
    t, d = x0.shape
    tt_nt = _pick(t, 1024)
    tt_nn = _pick(t, 512)
    tt_ew = _pick(t, 512)
    tt_ple = _pick(t, 256)
    sb = _pick(t // 2, 256)

    t_bf = lambda w: w[0].T.astype(BF16)
    n_bf = lambda w: w[0].astype(BF16)
    cw_tile = jnp.zeros((SUBLANES, LANES), F32).at[:conv_w.shape[1], :conv_w.shape[2]].set(conv_w[0])
    shards = [t_bf(ffn1_w_gate), t_bf(ffn1_w_up), n_bf(ffn1_w_down), t_bf(w_in), n_bf(w_out),
              t_bf(ffn2_w_gate), t_bf(ffn2_w_up), n_bf(ffn2_w_down), n_bf(ple_w_gate), t_bf(ple_w_proj), cw_tile]
    wg1t, wu1t, wd1, wint, wout, wg2t, wu2t, wd2, wpg, wppt, cw_all = _all_gather("gather_weights", shards)
    ncs = conv_w.shape[2]
    cw_full = cw_all.reshape(N_DEV, SUBLANES, LANES)[:, :, :ncs].transpose(1, 0, 2).reshape(SUBLANES, N_DEV * ncs)

    qg = jnp.tile(q_norm, (1, SB_DIM // HEAD_DIM))
    kg = jnp.tile(k_norm, (1, SB_DIM // HEAD_DIM))
    gi = lax.broadcasted_iota(jnp.int32, (SLAB, SLAB), 0) // HEAD_DIM
    gj = lax.broadcasted_iota(jnp.int32, (SLAB, SLAB), 1) // HEAD_DIM
    bd = (gi == gj).astype(BF16)
    tri, tri_inc = _tri_masks(sb)

    h1 = _rmsnorm("ffn1_norm", x0, ffn1_norm, tt_ew)
    x1 = _ffn_fwd("ffn1", x0, h1, wg1t, wu1t, wd1, tt_nt, tt_nn)
    h2 = _rmsnorm("mix_norm", x1, mix_norm, tt_ew)
    (proj,) = _mm_nt("in_proj", [h2], [wint], [(0, 0)], lambda accs: accs, [F32], tt_nt, SLAB)
    y_conv = _conv_fwd(proj, cw_full, conv_b, tt_ew)
    qn, kn, vb = _qknorm_fwd(proj, qg, kg, bd, tt_ew)
    o, ob = _attn_fwd(qn, kn, vb, tri, sb)
    (x2,) = _mm_nn("out_proj", [(y_conv, wout, 0), (ob, wout, 1)], [x1], [],
                   lambda acc, rows, fulls: [rows[0] + acc], [("tile", F32)], tt_nn, SLAB)
    h3 = _rmsnorm("ffn2_norm", x2, ffn2_norm, tt_ew)
    x3 = _ffn_fwd("ffn2", x2, h3, wg2t, wu2t, wd2, tt_nt, tt_nn)

    dx3, dx3b, dwpg, dwppt, gp_ple, lpart = _ple_loss(x3, p2, tgt, ple_norm, wpg, wppt, tt_ple)
    loss = lax.psum(jnp.sum(lpart[:, 0]), MESH_AXES)
    dx2, dx2b, gp_ffn2, dwg2, dwu2, dwd2 = _ffn_bwd("ffn2", x2, h3, dx3, dx3b, ffn2_norm, wg2t, wu2t, wd2,
                                                     tt_nt, tt_nn)
    (dycat,) = _mm_nt("out_proj_bwd", [dx2b], [wout], [(0, 0)], lambda accs: accs, [F32], tt_nt, SLAB)
    dwout = jnp.concatenate([_mm_tn("dwout_conv", y_conv, dx2b, 1.0, SLAB, tt_nt),
                             _mm_tn("dwout_attn", ob, dx2b, 1.0, SLAB, tt_nt)], axis=0)
    dqn, dkn, dv = _attn_bwd(qn, kn, vb, o, dycat, tri, tri_inc, sb)
    dq, dk, dvb, qk_part = _qknorm_bwd(proj, dqn, dkn, dv, qg, kg, bd, tt_ew)
    db, dc, du, conv_part = _conv_bwd(proj, dycat, cw_full, conv_b, tt_ew)
    dproj = [db, dc, du, dq, dk, dvb]
    dx1, dx1b, gp_mix = _mm_nn("in_proj_bwd", [(dp, wint, s) for s, dp in enumerate(dproj)], [x1, dx2], [mix_norm],
                               _norm_bwd_epilogue, _NORM_BWD_OUTS, tt_nn, SLAB)
    dwin = jnp.concatenate([_mm_tn(f"dwin_{s}", dp, h2, 1.0, SLAB, tt_nt) for s, dp in enumerate(dproj)], axis=0)
    dx0, _, gp_ffn1, dwg1, dwu1, dwd1 = _ffn_bwd("ffn1", x0, h1, dx1, dx1b, ffn1_norm, wg1t, wu1t, wd1, tt_nt, tt_nn)

    fulls = [dwg1, dwu1, dwd1, dwin, dwout, dwg2, dwu2, dwd2, dwpg, dwppt]
    slots = _exchange_shards("exchange_grads", fulls)
    sums = [_sum_slots(f"sum_grads_{i}", s.reshape(N_DEV, s.shape[0] // N_DEV, s.shape[1]))
            for i, s in enumerate(slots)]
    g_wg1, g_wu1, g_wd1, g_win, g_wout, g_wg2, g_wu2, g_wd2, g_wpg, g_wpp = sums
    small = _pack_small([gp_ffn1, gp_mix, gp_ffn2, gp_ple], conv_part, qk_part)
    (small_all,) = _all_gather("gather_small_grads", [small])
    sm = _sum_slots("sum_small_grads", small_all.reshape(N_DEV, 2 * SUBLANES, d))
    fold = lambda r: r.reshape(SB_DIM // HEAD_DIM, HEAD_DIM).sum(axis=0)[None]
    me_idx = 4 * lax.axis_index("x") + 2 * lax.axis_index("y") + lax.axis_index("c")
    cw_grad = jnp.stack([sm[4, SLAB:], sm[5, :SLAB], sm[5, SLAB:]])
    grads = {
        "ffn1_norm": sm[0:1], "ffn1_w_gate": g_wg1.T, "ffn1_w_up": g_wu1.T, "ffn1_w_down": g_wd1,
        "mix_norm": sm[1:2], "w_in": g_win.T, "conv_w": lax.dynamic_slice_in_dim(cw_grad, me_idx * ncs, ncs, axis=1),
        "conv_b": sm[4:5, :SLAB], "q_norm": fold(sm[6, :SLAB]), "k_norm": fold(sm[6, SLAB:]),
        "w_out": g_wout, "ffn2_norm": sm[2:3], "ffn2_w_gate": g_wg2.T, "ffn2_w_up": g_wu2.T, "ffn2_w_down": g_wd2,
        "ple_norm": sm[3:4], "ple_w_gate": g_wpg, "ple_w_proj": g_wpp.T,
    }

    weights = dict(ffn1_norm=ffn1_norm, ffn1_w_gate=ffn1_w_gate, ffn1_w_up=ffn1_w_up, ffn1_w_down=ffn1_w_down,
                   mix_norm=mix_norm, w_in=w_in, conv_w=conv_w, conv_b=conv_b, q_norm=q_norm, k_norm=k_norm,
                   w_out=w_out, ffn2_norm=ffn2_norm, ffn2_w_gate=ffn2_w_gate, ffn2_w_up=ffn2_w_up,
                   ffn2_w_down=ffn2_w_down, ple_norm=ple_norm, ple_w_gate=ple_w_gate, ple_w_proj=ple_w_proj)
    m_in = dict(ffn1_norm=m_ffn1_norm, ffn1_w_gate=m_ffn1_w_gate, ffn1_w_up=m_ffn1_w_up, ffn1_w_down=m_ffn1_w_down,
                mix_norm=m_mix_norm, w_in=m_w_in, conv_w=m_conv_w, conv_b=m_conv_b, q_norm=m_q_norm,
                k_norm=m_k_norm, w_out=m_w_out, ffn2_norm=m_ffn2_norm, ffn2_w_gate=m_ffn2_w_gate,
                ffn2_w_up=m_ffn2_w_up, ffn2_w_down=m_ffn2_w_down, ple_norm=m_ple_norm, ple_w_gate=m_ple_w_gate,
                ple_w_proj=m_ple_w_proj)
    v_in = dict(ffn1_norm=v_ffn1_norm, ffn1_w_gate=v_ffn1_w_gate, ffn1_w_up=v_ffn1_w_up, ffn1_w_down=v_ffn1_w_down,
                mix_norm=v_mix_norm, w_in=v_w_in, conv_w=v_conv_w, conv_b=v_conv_b, q_norm=v_q_norm,
                k_norm=v_k_norm, w_out=v_w_out, ffn2_norm=v_ffn2_norm, ffn2_w_gate=v_ffn2_w_gate,
                ffn2_w_up=v_ffn2_w_up, ffn2_w_down=v_ffn2_w_down, ple_norm=v_ple_norm, ple_w_gate=v_ple_w_gate,
                ple_w_proj=v_ple_w_proj)
    g_out, d_out, m_out, v_out = [], [], [], []
    for name, w in weights.items():
        w2 = w.reshape(w.shape[-2:])
        g2 = grads[name].reshape(w2.shape)
        dlt, nm, nv = _adamw(f"adamw_{name}", w2, g2, m_in[name].reshape(w2.shape), v_in[name].reshape(w2.shape))
        g_out.append(g2.reshape(w.shape))
        d_out.append(dlt.reshape(w.shape))
        m_out.append(nm.reshape(w.shape))
        v_out.append(nv.reshape(w.shape))
    return (loss, dx0[None], *g_out, *d_out, *m_out, *v_out)
```

```python
import jax
import jax.numpy as jnp
from jax import lax
from jax.experimental import pallas as pl
from jax.experimental.pallas import tpu as pltpu

F32 = jnp.float32
BF16 = jnp.bfloat16

EPS = 1e-6
FFN_RES = 0.5
HEAD_DIM = 64
CONV_DIM = 512
SB_DIM = 512
SLAB = 512
N_DEV = 8
MESH_AXES = ("x", "y", "c")
MESH = pl.DeviceIdType.MESH

ADAM_LR = 0.001
ADAM_B1 = 0.9
ADAM_B2 = 0.999
ADAM_EPS = 1e-08
ADAM_WD = 0.01
ADAM_STEP = 10

VMEM_LIMIT_BYTES = 56 * 1024 * 1024
SUBLANES = 8
LANES = 128


def _cparams(*semantics):
    return pltpu.CompilerParams(dimension_semantics=semantics, vmem_limit_bytes=VMEM_LIMIT_BYTES)


def _dot_nn(a, b):
    return jnp.dot(a, b, preferred_element_type=F32)


def _dot_nt(a, b):
    return lax.dot_general(a, b, (((1,), (1,)), ((), ())), preferred_element_type=F32)


def _dot_tn(a, b):
    return lax.dot_general(a, b, (((0,), (0,)), ((), ())), preferred_element_type=F32)


def _fold8(v):
    rows, cols = v.shape
    return jnp.sum(v.reshape(rows // SUBLANES, SUBLANES, cols), axis=0)


def _split2(v):
    hi = v.astype(BF16)
    lo = (v - hi.astype(F32)).astype(BF16)
    return hi, lo


def _rms_stats(x):
    return lax.rsqrt(jnp.mean(x * x, axis=-1, keepdims=True) + EPS)


def _rms_bwd(dh, x, gain):
    r = _rms_stats(x)
    u = dh * gain
    dx = r * u - x * (r * r * r) * jnp.mean(u * x, axis=-1, keepdims=True)
    return dx, dh * x * r


def _pick(n, pref):
    return pref if n % pref == 0 else n


def _rmsnorm(name, x, gain, tt):
    t, d = x.shape

    def body(x_ref, g_ref, o_ref):
        xv = x_ref[...]
        o_ref[...] = ((xv * _rms_stats(xv)) * g_ref[...]).astype(BF16)

    return pl.pallas_call(
        body, name=name, grid=(t // tt,),
        in_specs=[pl.BlockSpec((tt, d), lambda i: (i, 0)), pl.BlockSpec((1, d), lambda i: (0, 0))],
        out_specs=pl.BlockSpec((tt, d), lambda i: (i, 0)),
        out_shape=jax.ShapeDtypeStruct((t, d), BF16),
        compiler_params=_cparams("parallel"),
    )(x, gain)


class _Exchange:
    FLIPS = [(fx, fy, fc) for fx in (0, 1) for fy in (0, 1) for fc in (0, 1)][1:]

    def __init__(self, arrays, gather):
        self.arrays = list(arrays)
        self.gather = gather
        self.n = len(self.arrays)
        self.rows = [a.shape[0] if gather else a.shape[0] // N_DEV for a in self.arrays]
        self.out_shape = [jax.ShapeDtypeStruct((N_DEV * r, a.shape[1]), a.dtype)
                          for r, a in zip(self.rows, self.arrays)]
        self.scratch = [pltpu.SemaphoreType.DMA((7 * self.n,)), pltpu.SemaphoreType.DMA((7 * self.n,)),
                        pltpu.SemaphoreType.DMA((self.n,))]

    def _copies(self, ins, outs, sems):
        send_sems, recv_sems, local_sems = sems
        x, y, c = (lax.axis_index(a) for a in MESH_AXES)
        me_idx = 4 * x + 2 * y + c
        local, send, recv = [], [], []
        for a in range(self.n):
            r = self.rows[a]

            def blk(ref, idx, r=r):
                return ref.at[pl.ds(idx * r, r), :]

            def src(idx, a=a, blk=blk):
                return ins[a] if self.gather else blk(ins[a], idx)

            local.append(pltpu.make_async_copy(src(me_idx), blk(outs[a], me_idx), local_sems.at[a]))
            for k, flip in enumerate(self.FLIPS):
                px, py, pc = (1 - v if f else v for v, f in zip((x, y, c), flip))
                p_idx = 4 * px + 2 * py + pc
                for dst_idx, group in ((me_idx, send), (p_idx, recv)):
                    group.append(pltpu.make_async_remote_copy(
                        src_ref=src(p_idx), dst_ref=blk(outs[a], dst_idx),
                        send_sem=send_sems.at[7 * a + k], recv_sem=recv_sems.at[7 * a + k],
                        device_id=(px, py, pc), device_id_type=MESH))
        return local, send, recv

    def start(self, ins, outs, sems):
        local, send, _ = self._copies(ins, outs, sems)
        for cp in local + send:
            cp.start()

    def wait(self, ins, outs, sems):
        local, send, recv = self._copies(ins, outs, sems)
        for s, r in zip(send, recv):
            r.wait_recv()
            s.wait_send()
        for cp in local:
            cp.wait()

    def attach(self, refs, first, last):
        pl.when(first)(lambda: self.start(*refs))
        pl.when(last)(lambda: self.wait(*refs))


def _split_refs(refs, n_in, n_out, n_scratch, comm):
    nc = comm.n if comm else 0
    ins, rest = refs[:n_in], refs[n_in:]
    c_in, rest = rest[:nc], rest[nc:]
    outs, rest = rest[:n_out], rest[n_out:]
    c_out, rest = rest[:nc], rest[nc:]
    scratch, c_sems = rest[:n_scratch], rest[n_scratch:]
    return ins, outs, scratch, ((c_in, c_out, c_sems) if comm else None)


def _with_comm(comm, in_specs, out_specs, out_shape, scratch):
    if comm is None:
        return in_specs, out_specs, out_shape, scratch, []
    return (in_specs + _any_specs(comm.n), out_specs + _any_specs(comm.n), out_shape + comm.out_shape,
            scratch + comm.scratch, comm.arrays)


def _mm_nt(name, a_list, w_list, pairs, epilogue, out_dtypes, tt, tn, comm=None):
    t = a_list[0].shape[0]
    n = w_list[0].shape[0]
    na, nw = len(a_list), len(w_list)
    ni, nj = t // tt, n // tn

    def body(*refs):
        ins, o_refs, _, c_refs = _split_refs(refs, na + nw, len(out_dtypes), 0, comm)
        a_refs, w_refs = ins[:na], ins[na:]
        if comm:
            i, j = pl.program_id(0), pl.program_id(1)
            comm.attach(c_refs, jnp.logical_and(i == 0, j == 0), jnp.logical_and(i == ni - 1, j == nj - 1))
        accs = [_dot_nt(a_refs[ai][...], w_refs[wi][...]) for ai, wi in pairs]
        for o_ref, o in zip(o_refs, epilogue(accs)):
            o_ref[...] = o.astype(o_ref.dtype)

    in_specs = ([pl.BlockSpec((tt, a.shape[1]), lambda i, j: (i, 0)) for a in a_list]
                + [pl.BlockSpec((tn, w.shape[1]), lambda i, j: (j, 0)) for w in w_list])
    in_specs, out_specs, out_shape, scratch, extra = _with_comm(
        comm, in_specs, [pl.BlockSpec((tt, tn), lambda i, j: (i, j)) for _ in out_dtypes],
        [jax.ShapeDtypeStruct((t, n), dt) for dt in out_dtypes], [])
    return pl.pallas_call(
        body, name=name, grid=(ni, nj), in_specs=in_specs, out_specs=out_specs, out_shape=out_shape,
        scratch_shapes=scratch,
        compiler_params=_cparams("arbitrary" if comm else "parallel", "arbitrary"),
    )(*a_list, *w_list, *extra)


def _mm_nn(name, pairs, rows, fulls, epilogue, out_kinds, tt, tk):
    t, k_total = pairs[0][0].shape
    n = pairs[0][1].shape[1]
    nk = k_total // tk
    nt = t // tt
    npair, nrow, nfull = len(pairs), len(rows), len(fulls)

    def body(*refs):
        a_refs = refs[:npair]
        w_refs = refs[npair:2 * npair]
        r_refs = refs[2 * npair:2 * npair + nrow]
        f_refs = refs[2 * npair + nrow:2 * npair + nrow + nfull]
        o_refs = refs[2 * npair + nrow + nfull:-1]
        acc_ref = refs[-1]
        k = pl.program_id(1)
        s = _dot_nn(a_refs[0][...], w_refs[0][...])
        for a_ref, w_ref in zip(a_refs[1:], w_refs[1:]):
            s = s + _dot_nn(a_ref[...], w_ref[...])

        @pl.when(k == 0)
        def _():
            acc_ref[...] = s

        @pl.when(k > 0)
        def _():
            acc_ref[...] += s

        @pl.when(k == nk - 1)
        def _():
            outs = epilogue(acc_ref[...], [r[...] for r in r_refs], [f[...] for f in f_refs])
            for o_ref, o in zip(o_refs, outs):
                o_ref[...] = o.astype(o_ref.dtype)

    in_specs = ([pl.BlockSpec((tt, tk), lambda i, k: (i, k)) for _ in pairs]
                + [pl.BlockSpec((tk, n), (lambda i, k, off=off: (k + off, 0))) for _, _, off in pairs]
                + [pl.BlockSpec((tt, n), lambda i, k: (i, 0)) for _ in rows]
                + [pl.BlockSpec((1, n), lambda i, k: (0, 0)) for _ in fulls])
    out_specs, out_shape = [], []
    for kind, dt in out_kinds:
        if kind == "tile":
            out_specs.append(pl.BlockSpec((tt, n), lambda i, k: (i, 0)))
            out_shape.append(jax.ShapeDtypeStruct((t, n), dt))
        else:
            out_specs.append(pl.BlockSpec((SUBLANES, n), lambda i, k: (i, 0)))
            out_shape.append(jax.ShapeDtypeStruct((nt * SUBLANES, n), dt))
    return pl.pallas_call(
        body, name=name, grid=(nt, nk), in_specs=in_specs, out_specs=out_specs, out_shape=out_shape,
        scratch_shapes=[pltpu.VMEM((tt, n), F32)],
        compiler_params=_cparams("parallel", "arbitrary"),
    )(*[a for a, _, _ in pairs], *[w for _, w, _ in pairs], *rows, *fulls)


def _mm_tn(name, a, b, scale, tm, tt):
    t, m = a.shape
    n = b.shape[1]
    nt = t // tt

    def body(a_ref, b_ref, o_ref, acc_ref):
        k = pl.program_id(1)
        s = _dot_tn(a_ref[...], b_ref[...])

        @pl.when(k == 0)
        def _():
            acc_ref[...] = s

        @pl.when(k > 0)
        def _():
            acc_ref[...] += s

        @pl.when(k == nt - 1)
        def _():
            o_ref[...] = (acc_ref[...] * scale).astype(o_ref.dtype)

    return pl.pallas_call(
        body, name=name, grid=(m // tm, nt),
        in_specs=[pl.BlockSpec((tt, tm), lambda i, k: (k, i)), pl.BlockSpec((tt, n), lambda i, k: (k, 0))],
        out_specs=pl.BlockSpec((tm, n), lambda i, k: (i, 0)),
        out_shape=jax.ShapeDtypeStruct((m, n), BF16),
        scratch_shapes=[pltpu.VMEM((tm, n), F32)],
        compiler_params=_cparams("parallel", "arbitrary"),
    )(a, b)


def _group_sum(v, bd):
    hi = v.astype(BF16)
    r1 = v - hi.astype(F32)
    mid = r1.astype(BF16)
    lo = (r1 - mid.astype(F32)).astype(BF16)
    return _dot_nn(hi, bd) + _dot_nn(mid, bd) + _dot_nn(lo, bd)


def _qknorm_fwd(proj, qg, kg, bd, tt):
    t = proj.shape[0]

    def body(q_ref, k_ref, v_ref, qg_ref, kg_ref, bd_ref, qn_ref, kn_ref, vb_ref):
        bdv = bd_ref[...]
        for x_ref, g_ref, o_ref in ((q_ref, qg_ref, qn_ref), (k_ref, kg_ref, kn_ref)):
            xv = x_ref[...]
            r = lax.rsqrt(_group_sum(xv * xv, bdv) * (1.0 / HEAD_DIM) + EPS)
            o_ref[...] = ((xv * r) * g_ref[...]).astype(BF16)
        vb_ref[...] = v_ref[...].astype(BF16)

    slab = lambda s: pl.BlockSpec((tt, SLAB), lambda i, s=s: (i, s))
    full = lambda shape: pl.BlockSpec(shape, lambda i: (0, 0))
    out = pl.BlockSpec((tt, SLAB), lambda i: (i, 0))
    return pl.pallas_call(
        body, name="qknorm_fwd", grid=(t // tt,),
        in_specs=[slab(3), slab(4), slab(5), full((1, SLAB)), full((1, SLAB)), full((SLAB, SLAB))],
        out_specs=[out, out, out],
        out_shape=[jax.ShapeDtypeStruct((t, SLAB), BF16)] * 3,
        compiler_params=_cparams("parallel"),
    )(proj, proj, proj, qg, kg, bd)


def _qknorm_bwd(proj, dqn, dkn, dv, qg, kg, bd, tt):
    t = proj.shape[0]

    def body(q_ref, k_ref, dqn_ref, dkn_ref, dv_ref, qg_ref, kg_ref, bd_ref, dq_ref, dk_ref, dvb_ref, part_ref):
        bdv = bd_ref[...]
        parts = []
        for x_ref, d_ref, g_ref, o_ref in ((q_ref, dqn_ref, qg_ref, dq_ref), (k_ref, dkn_ref, kg_ref, dk_ref)):
            xv, dn = x_ref[...], d_ref[...]
            r = lax.rsqrt(_group_sum(xv * xv, bdv) * (1.0 / HEAD_DIM) + EPS)
            u = dn * g_ref[...]
            dx = r * u - xv * (r * r * r) * (_group_sum(u * xv, bdv) * (1.0 / HEAD_DIM))
            o_ref[...] = dx.astype(BF16)
            parts.append(_fold8(dn * xv * r))
        dvb_ref[...] = dv_ref[...].astype(BF16)
        part_ref[...] = jnp.concatenate(parts, axis=1)

    slab = lambda s: pl.BlockSpec((tt, SLAB), lambda i, s=s: (i, s))
    tile = pl.BlockSpec((tt, SLAB), lambda i: (i, 0))
    full = lambda shape: pl.BlockSpec(shape, lambda i: (0, 0))
    return pl.pallas_call(
        body, name="qknorm_bwd", grid=(t // tt,),
        in_specs=[slab(3), slab(4), tile, tile, tile, full((1, SLAB)), full((1, SLAB)), full((SLAB, SLAB))],
        out_specs=[tile, tile, tile, pl.BlockSpec((SUBLANES, 2 * SLAB), lambda i: (i, 0))],
        out_shape=[jax.ShapeDtypeStruct((t, SLAB), BF16)] * 3
        + [jax.ShapeDtypeStruct((t // tt * SUBLANES, 2 * SLAB), F32)],
        compiler_params=_cparams("parallel"),
    )(proj, proj, dqn, dkn, dv, qg, kg, bd)


def _conv_taps(z, z_prev, row):
    zm1 = jnp.where(row == 0, z_prev[7:8], pltpu.roll(z, 1, 0))
    zm2 = jnp.where(row == 0, z_prev[6:7], jnp.where(row == 1, z_prev[7:8], pltpu.roll(z, 2, 0)))
    return zm1, zm2


def _conv_fwd(proj, cw, cb, tt):
    t = proj.shape[0]
    tb = tt // SUBLANES

    def body(b_ref, c_ref, u_ref, cp_ref, up_ref, cw_ref, cb_ref, o_ref):
        i = pl.program_id(0)
        z = c_ref[...] * u_ref[...]
        z_prev = jnp.where(i > 0, cp_ref[...] * up_ref[...], 0.0)
        row = lax.broadcasted_iota(jnp.int32, (tt, 1), 0)
        zm1, zm2 = _conv_taps(z, z_prev, row)
        y = cw_ref[0:1] * zm2 + cw_ref[1:2] * zm1 + cw_ref[2:3] * z + cb_ref[...]
        o_ref[...] = (b_ref[...] * y).astype(BF16)

    slab = lambda s: pl.BlockSpec((tt, SLAB), lambda i, s=s: (i, s))
    prev = lambda s: pl.BlockSpec((SUBLANES, SLAB), lambda i, s=s: (jnp.maximum(i * tb - 1, 0), s))
    return pl.pallas_call(
        body, name="conv_fwd", grid=(t // tt,),
        in_specs=[slab(0), slab(1), slab(2), prev(1), prev(2),
                  pl.BlockSpec((SUBLANES, SLAB), lambda i: (0, 0)), pl.BlockSpec((1, SLAB), lambda i: (0, 0))],
        out_specs=pl.BlockSpec((tt, SLAB), lambda i: (i, 0)),
        out_shape=jax.ShapeDtypeStruct((t, SLAB), BF16),
        compiler_params=_cparams("parallel"),
    )(proj, proj, proj, proj, proj, cw, cb)


def _conv_bwd(proj, dycat, cw, cb, tt):
    t = proj.shape[0]
    tb = tt // SUBLANES
    nblk = t // SUBLANES

    def body(b_ref, c_ref, u_ref, cp_ref, up_ref, bn_ref, dy_ref, dyn_ref, cw_ref, cb_ref,
             db_ref, dc_ref, du_ref, part_ref):
        i = pl.program_id(0)
        c, u, b, dyc = c_ref[...], u_ref[...], b_ref[...], dy_ref[...]
        z = c * u
        z_prev = jnp.where(i > 0, cp_ref[...] * up_ref[...], 0.0)
        row = lax.broadcasted_iota(jnp.int32, (tt, 1), 0)
        zm1, zm2 = _conv_taps(z, z_prev, row)
        w0, w1, w2 = cw_ref[0:1], cw_ref[1:2], cw_ref[2:3]
        y = w0 * zm2 + w1 * zm1 + w2 * z + cb_ref[...]
        db_ref[...] = (dyc * y).astype(BF16)
        g = dyc * b
        g_next = jnp.where(i < pl.num_programs(0) - 1, dyn_ref[...] * bn_ref[...], 0.0)
        gp1 = jnp.where(row == tt - 1, g_next[0:1], pltpu.roll(g, tt - 1, 0))
        gp2 = jnp.where(row == tt - 2, g_next[0:1], jnp.where(row == tt - 1, g_next[1:2], pltpu.roll(g, tt - 2, 0)))
        dz = w2 * g + w1 * gp1 + w0 * gp2
        dc_ref[...] = (dz * u).astype(BF16)
        du_ref[...] = (dz * c).astype(BF16)
        part_ref[...] = jnp.concatenate([_fold8(g * zm2), _fold8(g * zm1), _fold8(g * z), _fold8(g)], axis=1)

    slab = lambda s: pl.BlockSpec((tt, SLAB), lambda i, s=s: (i, s))
    prev = lambda s: pl.BlockSpec((SUBLANES, SLAB), lambda i, s=s: (jnp.maximum(i * tb - 1, 0), s))
    nxt = lambda s: pl.BlockSpec((SUBLANES, SLAB), lambda i, s=s: (jnp.minimum((i + 1) * tb, nblk - 1), s))
    tile = pl.BlockSpec((tt, SLAB), lambda i: (i, 0))
    return pl.pallas_call(
        body, name="conv_bwd", grid=(t // tt,),
        in_specs=[slab(0), slab(1), slab(2), prev(1), prev(2), nxt(0), slab(0), nxt(0),
                  pl.BlockSpec((SUBLANES, SLAB), lambda i: (0, 0)), pl.BlockSpec((1, SLAB), lambda i: (0, 0))],
        out_specs=[tile, tile, tile, pl.BlockSpec((SUBLANES, 4 * SLAB), lambda i: (i, 0))],
        out_shape=[jax.ShapeDtypeStruct((t, SLAB), BF16)] * 3
        + [jax.ShapeDtypeStruct((t // tt * SUBLANES, 4 * SLAB), F32)],
        compiler_params=_cparams("parallel"),
    )(proj, proj, proj, proj, proj, proj, dycat, dycat, cw, cb)


def _tri_masks(n):
    r = lax.broadcasted_iota(jnp.int32, (n, n), 0)
    c = lax.broadcasted_iota(jnp.int32, (n, n), 1)
    return (r > c).astype(BF16), (r >= c).astype(BF16)


def _sb_scores(qh, k, r_run, tri, causal):
    z = _dot_nt(qh, k)
    softplus = jnp.maximum(z, 0.0) + jnp.log(1.0 + jnp.exp(-jnp.abs(z)))
    lk = -softplus
    if causal is not None:
        lk = jnp.where(causal, lk, 0.0)
    hi, lo = _split2(lk)
    later = _dot_nn(hi, tri) + _dot_nn(lo, tri) + r_run
    ls = z + lk
    arg = ls + later
    if causal is not None:
        arg = jnp.where(causal, arg, -1e30)
    return lk, ls, jnp.exp(arg), later[:, 0:1] + lk[:, 0:1]


SB_DEAD_LOG = -111.0
CHAINS = ((0, 0), (0, 1), (1, 0), (1, 1))


def _all_dead(r_runs):
    m = r_runs[0]
    for r in r_runs[1:]:
        m = jnp.maximum(m, r)
    return (jnp.max(m) < SB_DEAD_LOG).astype(jnp.int32)


def _attn_fwd(qn, kn, vb, tri, sb, comm=None):
    t = qn.shape[0]
    bq = 2 * sb
    scale = HEAD_DIM ** -0.5

    def body(*refs):
        (q_ref, k_ref, v_ref, tri_ref), (o_ref, ob_ref), (acc_ref,), c_refs = _split_refs(refs, 4, 2, 1, comm)
        qi = pl.program_id(1)
        if comm:
            hp = pl.program_id(0)
            comm.attach(c_refs, jnp.logical_and(hp == 0, qi == 0),
                        jnp.logical_and(hp == pl.num_programs(0) - 1, qi == pl.num_programs(1) - 1))
        lane = lax.broadcasted_iota(jnp.int32, (1, LANES), 1)
        hmasks = (lane < HEAD_DIM, lane >= HEAD_DIM)
        diag = lax.broadcasted_iota(jnp.int32, (sb, sb), 1) < lax.broadcasted_iota(jnp.int32, (sb, sb), 0)
        triv = tri_ref[...]
        qs = [[jnp.where(hm, q_ref[pl.ds(s * sb, sb), :], 0) * scale for hm in hmasks] for s in range(2)]
        acc_ref[...] = jnp.zeros_like(acc_ref)

        def load_kv(kb):
            ks = pl.multiple_of(kb * sb, sb)
            vraw = v_ref[pl.ds(ks, sb), :]
            return k_ref[pl.ds(ks, sb), :], [jnp.where(hm, vraw, 0) for hm in hmasks]

        def tile(s, hh, kv, r_run, causal):
            _, _, a, r_new = _sb_scores(qs[s][hh], kv[0], r_run, triv, causal)
            acc_ref[2 * s + hh] += _dot_nn(a.astype(BF16), kv[1][hh])
            return r_new

        zero = jnp.zeros((sb, 1), F32)
        kv_hi, kv_lo = load_kv(2 * qi + 1), load_kv(2 * qi)
        r_runs = [None] * 4
        for hh in range(2):
            r_runs[hh] = tile(0, hh, kv_lo, zero, diag)
            r_runs[2 + hh] = tile(1, hh, kv_lo, tile(1, hh, kv_hi, zero, diag), None)

        def step(carry):
            i, _, *rs = carry
            kv = load_kv(2 * qi - 1 - i)
            rs = [tile(s, hh, kv, rs[2 * s + hh], None) for s, hh in CHAINS]
            return (i + 1, _all_dead(rs), *rs)

        lax.while_loop(lambda c: jnp.logical_and(c[0] < 2 * qi, c[1] == 0), step,
                       (jnp.int32(0), _all_dead(r_runs), *r_runs))
        for s in range(2):
            out = acc_ref[2 * s] + acc_ref[2 * s + 1]
            o_ref[pl.ds(s * sb, sb), :] = out
            ob_ref[pl.ds(s * sb, sb), :] = out.astype(BF16)

    qspec = pl.BlockSpec((bq, LANES), lambda h, i: (i, h))
    kspec = pl.BlockSpec((t, LANES), lambda h, i: (0, h))
    in_specs, out_specs, out_shape, scratch, extra = _with_comm(
        comm, [qspec, kspec, kspec, pl.BlockSpec((sb, sb), lambda h, i: (0, 0))], [qspec, qspec],
        [jax.ShapeDtypeStruct((t, SB_DIM), F32), jax.ShapeDtypeStruct((t, SB_DIM), BF16)],
        [pltpu.VMEM((4, sb, LANES), F32)])
    return pl.pallas_call(
        body, name="attn_fwd", grid=(SB_DIM // LANES, t // bq),
        in_specs=in_specs, out_specs=out_specs, out_shape=out_shape, scratch_shapes=scratch,
        compiler_params=_cparams("arbitrary" if comm else "parallel", "arbitrary"),
    )(qn, kn, vb, tri, *extra)


def _attn_bwd(qn, kn, vb, o, dycat, tri, tri_inc, sb, comm=None):
    t = qn.shape[0]
    bq = 2 * sb
    scale = HEAD_DIM ** -0.5

    def body(*refs):
        ins, (dq_ref, dk_ref, dv_ref), (dq_acc,), c_refs = _split_refs(refs, 7, 3, 1, comm)
        q_ref, k_ref, v_ref, o_ref, do_ref, tri_ref, tinc_ref = ins
        qi = pl.program_id(1)
        if comm:
            hp = pl.program_id(0)
            comm.attach(c_refs, jnp.logical_and(hp == 0, qi == 0),
                        jnp.logical_and(hp == pl.num_programs(0) - 1, qi == pl.num_programs(1) - 1))

        @pl.when(qi == 0)
        def _():
            dk_ref[...] = jnp.zeros_like(dk_ref)
            dv_ref[...] = jnp.zeros_like(dv_ref)

        lane = lax.broadcasted_iota(jnp.int32, (1, LANES), 1)
        hmasks = (lane < HEAD_DIM, lane >= HEAD_DIM)
        diag = lax.broadcasted_iota(jnp.int32, (sb, sb), 1) < lax.broadcasted_iota(jnp.int32, (sb, sb), 0)
        triv, tincv = tri_ref[...], tinc_ref[...]
        qs, dobs, d_rows = [], [], []
        for s in range(2):
            rows = pl.ds(s * sb, sb)
            qs.append([jnp.where(hm, q_ref[rows, :], 0) * scale for hm in hmasks])
            dobs.append([jnp.where(hm, do_ref[rows, :], 0.0).astype(BF16) for hm in hmasks])
            d_rows.append([jnp.sum(d.astype(F32) * o_ref[rows, :], axis=1, keepdims=True) for d in dobs[s]])
        dq_acc[...] = jnp.zeros_like(dq_acc)

        def load_kv(kb):
            ks = pl.multiple_of(kb * sb, sb)
            return k_ref[pl.ds(ks, sb), :], v_ref[pl.ds(ks, sb), :], ks

        def tile(s, hh, kv, r_run, g_run, causal):
            k, v, _ = kv
            qh, dob = qs[s][hh], dobs[s][hh]
            _, ls, a, r_new = _sb_scores(qh, k, r_run, triv, causal)
            ab = a.astype(BF16)
            e = _dot_nt(dob, v) * ab.astype(F32)
            hi, lo = _split2(e)
            e_from = _dot_nn(hi, tincv) + _dot_nn(lo, tincv) + g_run
            beta = jnp.exp(ls)
            dz = e - beta * (e + (d_rows[s][hh] - e_from))
            if causal is not None:
                dz = jnp.where(causal, dz, 0.0)
            dzb = dz.astype(BF16)
            dq_acc[2 * s + hh] += _dot_nn(dzb, k)
            return r_new, e_from[:, 0:1], _dot_tn(dzb, qh), _dot_tn(ab, dob)

        def scatter(kv, parts):
            rows = pl.ds(kv[2], sb)
            dk_ref[rows, :] += sum(p[2] for p in parts[1:]) + parts[0][2]
            dv_ref[rows, :] += sum(p[3] for p in parts[1:]) + parts[0][3]

        zero = jnp.zeros((sb, 1), F32)
        kv_hi, kv_lo = load_kv(2 * qi + 1), load_kv(2 * qi)
        top = [tile(1, hh, kv_hi, zero, zero, diag) for hh in range(2)]
        scatter(kv_hi, top)
        low = [tile(0, hh, kv_lo, zero, zero, diag) for hh in range(2)]
        low += [tile(1, hh, kv_lo, top[hh][0], top[hh][1], None) for hh in range(2)]
        scatter(kv_lo, low)
        r_runs, g_runs = [p[0] for p in low], [p[1] for p in low]

        def step(carry):
            i, _, *rg = carry
            kv = load_kv(2 * qi - 1 - i)
            parts = [tile(s, hh, kv, rg[2 * s + hh], rg[4 + 2 * s + hh], None) for s, hh in CHAINS]
            scatter(kv, parts)
            rs = [p[0] for p in parts]
            return (i + 1, _all_dead(rs), *rs, *[p[1] for p in parts])

        lax.while_loop(lambda c: jnp.logical_and(c[0] < 2 * qi, c[1] == 0), step,
                       (jnp.int32(0), _all_dead(r_runs), *r_runs, *g_runs))
        for s in range(2):
            dq_ref[pl.ds(s * sb, sb), :] = jnp.where(hmasks[0], dq_acc[2 * s], dq_acc[2 * s + 1]) * scale

    qspec = pl.BlockSpec((bq, LANES), lambda h, i: (i, h))
    dospec = pl.BlockSpec((bq, LANES), lambda h, i: (i, h + CONV_DIM // LANES))
    kspec = pl.BlockSpec((t, LANES), lambda h, i: (0, h))
    full = pl.BlockSpec((sb, sb), lambda h, i: (0, 0))
    in_specs, out_specs, out_shape, scratch, extra = _with_comm(
        comm, [qspec, kspec, kspec, qspec, dospec, full, full], [qspec, kspec, kspec],
        [jax.ShapeDtypeStruct((t, SB_DIM), F32)] * 3, [pltpu.VMEM((4, sb, LANES), F32)])
    return pl.pallas_call(
        body, name="attn_bwd", grid=(SB_DIM // LANES, t // bq),
        in_specs=in_specs, out_specs=out_specs, out_shape=out_shape, scratch_shapes=scratch,
        compiler_params=_cparams("arbitrary" if comm else "parallel", "arbitrary"),
    )(qn, kn, vb, o, dycat, tri, tri_inc, *extra)


def _ple_loss(x3, p2, tgt, gain, wpg, wppt, tt):
    t, d = x3.shape
    pdim = p2.shape[1]
    nt = t // tt

    def body(x_ref, p_ref, t_ref, g_ref, wg_ref, wp_ref,
             dx_ref, dxb_ref, dwg_ref, dwp_ref, gpart_ref, lpart_ref, accg_ref, accp_ref):
        i = pl.program_id(0)
        xv, gain_v = x_ref[...], g_ref[...]
        hb = ((xv * _rms_stats(xv)) * gain_v).astype(BF16)
        gate = jax.nn.sigmoid(_dot_nn(hb, wg_ref[...]))
        pb = p_ref[...].astype(BF16)
        pe = _dot_nt(pb, wp_ref[...])
        diff = xv + gate * pe - t_ref[...]
        lsum = jnp.sum(_fold8(diff * diff), axis=1, keepdims=True) * (0.5 / d)
        lpart_ref[...] = jnp.broadcast_to(lsum, (SUBLANES, LANES))
        dy = diff * (1.0 / d)
        dgz = ((dy * pe) * gate * (1.0 - gate)).astype(BF16)
        dpe = (dy * gate).astype(BF16)
        dx_n, grow = _rms_bwd(_dot_nt(dgz, wg_ref[...]), xv, gain_v)
        dx = dy + dx_n
        dx_ref[...] = dx
        dxb_ref[...] = dx.astype(BF16)
        gpart_ref[...] = _fold8(grow)
        sg = _dot_tn(hb, dgz)
        sp = _dot_tn(dpe, pb)

        @pl.when(i == 0)
        def _():
            accg_ref[...] = sg
            accp_ref[...] = sp

        @pl.when(i > 0)
        def _():
            accg_ref[...] += sg
            accp_ref[...] += sp

        @pl.when(i == nt - 1)
        def _():
            dwg_ref[...] = accg_ref[...].astype(BF16)
            dwp_ref[...] = accp_ref[...].astype(BF16)

    tile = lambda w: pl.BlockSpec((tt, w), lambda i: (i, 0))
    full = lambda shape: pl.BlockSpec(shape, lambda i: (0, 0))
    return pl.pallas_call(
        body, name="ple_loss", grid=(nt,),
        in_specs=[tile(d), tile(pdim), tile(d), full((1, d)), full((d, d)), full((d, pdim))],
        out_specs=[tile(d), tile(d), full((d, d)), full((d, pdim)),
                   pl.BlockSpec((SUBLANES, d), lambda i: (i, 0)), pl.BlockSpec((SUBLANES, LANES), lambda i: (i, 0))],
        out_shape=[jax.ShapeDtypeStruct((t, d), F32), jax.ShapeDtypeStruct((t, d), BF16),
                   jax.ShapeDtypeStruct((d, d), BF16), jax.ShapeDtypeStruct((d, pdim), BF16),
                   jax.ShapeDtypeStruct((nt * SUBLANES, d), F32), jax.ShapeDtypeStruct((nt * SUBLANES, LANES), F32)],
        scratch_shapes=[pltpu.VMEM((d, d), F32), pltpu.VMEM((d, pdim), F32)],
        compiler_params=_cparams("arbitrary"),
    )(x3, p2, tgt, gain, wpg, wppt)


def _pack_small(parts_gain, conv_part, qk_part):
    d = parts_gain[0].shape[1]
    ng = len(parts_gain)

    def body(*refs):
        g_refs, conv_ref, qk_ref, o_ref = refs[:ng], refs[ng], refs[ng + 1], refs[ng + 2]
        rows = [jnp.sum(r[...], axis=0, keepdims=True) for r in g_refs]
        cs = jnp.sum(conv_ref[...], axis=0, keepdims=True)
        qs = jnp.sum(qk_ref[...], axis=0, keepdims=True)
        rows.append(jnp.concatenate([cs[:, 3 * SLAB:], cs[:, :SLAB]], axis=1))
        rows.append(cs[:, SLAB:3 * SLAB])
        rows.append(qs)
        rid = lax.broadcasted_iota(jnp.int32, (2 * SUBLANES, 1), 0)
        out = jnp.zeros((2 * SUBLANES, d), F32)
        for idx, r in enumerate(rows):
            out = jnp.where(rid == idx, r, out)
        o_ref[...] = out

    return pl.pallas_call(
        body, name="pack_small", out_shape=jax.ShapeDtypeStruct((2 * SUBLANES, d), F32),
    )(*parts_gain, conv_part, qk_part)


def _sum_slots(name, slots, out_dtype=F32):
    _, r, c = slots.shape

    def body(s_ref, o_ref):
        acc = s_ref[0].astype(F32)
        for d in range(1, N_DEV):
            acc = acc + s_ref[d].astype(F32)
        o_ref[...] = acc.astype(o_ref.dtype)

    return pl.pallas_call(body, name=name, out_shape=jax.ShapeDtypeStruct((r, c), out_dtype),
                          compiler_params=pltpu.CompilerParams(vmem_limit_bytes=VMEM_LIMIT_BYTES))(slots)


def _adamw(name, w, g, m, v):
    c1 = 1.0 - ADAM_B1 ** ADAM_STEP
    c2 = 1.0 - ADAM_B2 ** ADAM_STEP

    def body(w_ref, g_ref, m_ref, v_ref, d_ref, nm_ref, nv_ref):
        gv = g_ref[...]
        nm = ADAM_B1 * m_ref[...] + (1.0 - ADAM_B1) * gv
        nv = ADAM_B2 * v_ref[...] + (1.0 - ADAM_B2) * (gv * gv)
        d_ref[...] = -ADAM_LR * ((nm / c1) / (jnp.sqrt(nv / c2) + ADAM_EPS) + ADAM_WD * w_ref[...])
        nm_ref[...] = nm
        nv_ref[...] = nv

    return pl.pallas_call(body, name=name, out_shape=[jax.ShapeDtypeStruct(w.shape, F32)] * 3,
                          compiler_params=pltpu.CompilerParams(vmem_limit_bytes=VMEM_LIMIT_BYTES))(w, g, m, v)


def _any_specs(n):
    return [pl.BlockSpec(memory_space=pl.ANY)] * n


def _all_gather(name, shards):
    n = len(shards)

    def body(*refs):
        ins, outs = refs[:n], refs[n:2 * n]
        send_sems, recv_sems, local_sems = refs[2 * n:]
        x, y, c = (lax.axis_index(a) for a in MESH_AXES)
        me, sibling = (x, y, c), (x, y, 1 - c)
        chips = [(1 - x, y), (x, 1 - y), (1 - x, 1 - y)]

        def rows(a, px, py, pc):
            r = ins[a].shape[0]
            return outs[a].at[pl.ds((4 * px + 2 * py + pc) * r, r), :]

        def copy(a, k, block, to, src=None):
            return pltpu.make_async_remote_copy(
                src_ref=rows(a, *block) if src is None else src, dst_ref=rows(a, *block),
                send_sem=send_sems.at[7 * a + k], recv_sem=recv_sems.at[7 * a + k],
                device_id=to, device_id_type=MESH)

        mine = [pltpu.make_async_copy(ins[a], rows(a, *me), local_sems.at[a]) for a in range(n)]
        for cp in mine:
            cp.start()
        first = []
        for a in range(n):
            first.append(copy(a, 0, me, sibling, src=ins[a]))
            first += [copy(a, 1 + j, me, (*chip, c), src=ins[a]) for j, chip in enumerate(chips)]
        for cp in first:
            cp.start()
        passed = []
        for j, chip in enumerate(chips):
            for a in range(n):
                copy(a, 1 + j, (*chip, c), me).wait_recv()
                fwd = copy(a, 4 + j, (*chip, c), sibling)
                fwd.start()
                passed.append(fwd)
        for a in range(n):
            copy(a, 0, sibling, me).wait_recv()
            for j, chip in enumerate(chips):
                copy(a, 4 + j, (*chip, 1 - c), me).wait_recv()
        for cp in first + passed:
            cp.wait_send()
        for cp in mine:
            cp.wait()

    return pl.pallas_call(
        body, name=name, in_specs=_any_specs(n), out_specs=_any_specs(n),
        out_shape=[jax.ShapeDtypeStruct((N_DEV * s.shape[0], s.shape[1]), s.dtype) for s in shards],
        scratch_shapes=[pltpu.SemaphoreType.DMA((7 * n,)), pltpu.SemaphoreType.DMA((7 * n,)),
                        pltpu.SemaphoreType.DMA((n,))],
    )(*shards)


def _exchange_shards(name, fulls):
    n = len(fulls)
    flips = [(fx, fy, fc) for fx in (0, 1) for fy in (0, 1) for fc in (0, 1)][1:]

    def body(*refs):
        ins, outs = refs[:n], refs[n:2 * n]
        send_sems, recv_sems, local_sems = refs[2 * n:]
        x, y, c = (lax.axis_index(a) for a in MESH_AXES)
        me_idx = 4 * x + 2 * y + c

        def block(ref, a, idx):
            r = ins[a].shape[0] // N_DEV
            return ref.at[pl.ds(idx * r, r), :]

        def peer_of(flip):
            return tuple(1 - v if f else v for v, f in zip((x, y, c), flip))

        def copy(a, k):
            px, py, pc = peer_of(flips[k])
            p_idx = 4 * px + 2 * py + pc
            send = pltpu.make_async_remote_copy(
                src_ref=block(ins[a], a, p_idx), dst_ref=block(outs[a], a, me_idx),
                send_sem=send_sems.at[7 * a + k], recv_sem=recv_sems.at[7 * a + k],
                device_id=(px, py, pc), device_id_type=MESH)
            recv = pltpu.make_async_remote_copy(
                src_ref=block(ins[a], a, p_idx), dst_ref=block(outs[a], a, p_idx),
                send_sem=send_sems.at[7 * a + k], recv_sem=recv_sems.at[7 * a + k],
                device_id=(px, py, pc), device_id_type=MESH)
            return send, recv

        mine = [pltpu.make_async_copy(block(ins[a], a, me_idx), block(outs[a], a, me_idx), local_sems.at[a])
                for a in range(n)]
        for cp in mine:
            cp.start()
        copies = [copy(a, k) for a in range(n) for k in range(7)]
        for send, _ in copies:
            send.start()
        for send, recv in copies:
            recv.wait_recv()
            send.wait_send()
        for cp in mine:
            cp.wait()

    return pl.pallas_call(
        body, name=name, in_specs=_any_specs(n), out_specs=_any_specs(n),
        out_shape=[jax.ShapeDtypeStruct(f.shape, f.dtype) for f in fulls],
        scratch_shapes=[pltpu.SemaphoreType.DMA((7 * n,)), pltpu.SemaphoreType.DMA((7 * n,)),
                        pltpu.SemaphoreType.DMA((n,))],
    )(*fulls)


def _ffn_fwd(tag, x, h, wgt, wut, wd, tt_nt, tt_nn, comm=None):
    f = wgt.shape[0]
    a, *received = _mm_nt(f"{tag}_gate_up", [h], [wgt, wut], [(0, 0), (0, 1)],
                          lambda accs: [jax.nn.silu(accs[0]) * accs[1]], [BF16], tt_nt, _pick(f, 256), comm)
    (out,) = _mm_nn(f"{tag}_down", [(a, wd, 0)], [x], [],
                    lambda acc, rows, fulls: [rows[0] + FFN_RES * acc], [("tile", F32)], tt_nn, f // 2)
    return out, received


def _norm_bwd_epilogue(acc, rows, fulls):
    x_in, dy = rows
    dx_n, grow = _rms_bwd(acc, x_in, fulls[0])
    dx = dy + dx_n
    return [dx, dx, _fold8(grow)]


_NORM_BWD_OUTS = [("tile", F32), ("tile", BF16), ("part", F32)]


def _ffn_bwd(tag, x_in, h, dy, dyb, gain, wgt, wut, wd, tt_nt, tt_nn, comm=None):
    f = wgt.shape[0]

    def epilogue(accs):
        g, u, da = accs[0], accs[1], FFN_RES * accs[2]
        sg = jax.nn.sigmoid(g)
        s = g * sg
        return [da * u * (sg * (1.0 + g * (1.0 - sg))), da * s, s * u]

    dg, du, a, *received = _mm_nt(f"{tag}_bwd_hidden", [h, dyb], [wgt, wut, wd], [(0, 0), (0, 1), (1, 2)],
                                  epilogue, [BF16, BF16, BF16], tt_nt, _pick(f, 256), comm)
    dx, dxb, gpart = _mm_nn(f"{tag}_bwd_dx", [(dg, wgt, 0), (du, wut, 0)], [x_in, dy], [gain],
                            _norm_bwd_epilogue, _NORM_BWD_OUTS, tt_nn, f // 2)
    tt_tn = tt_nt
    dwg = _mm_tn(f"{tag}_dwg", dg, h, 1.0, f // 2, tt_tn)
    dwu = _mm_tn(f"{tag}_dwu", du, h, 1.0, f // 2, tt_tn)
    dwd = _mm_tn(f"{tag}_dwd", a, dyb, FFN_RES, f // 2, tt_tn)
    return dx, dxb, gpart, dwg, dwu, dwd, received


def kernel(x, p, ffn1_norm, ffn1_w_gate, ffn1_w_up, ffn1_w_down, mix_norm, w_in, conv_w, conv_b, q_norm, k_norm, w_out, ffn2_norm, ffn2_w_gate, ffn2_w_up, ffn2_w_down, ple_norm, ple_w_gate, ple_w_proj, loss_target, m_ffn1_norm, m_ffn1_w_gate, m_ffn1_w_up, m_ffn1_w_down, m_mix_norm, m_w_in, m_conv_w, m_conv_b, m_q_norm, m_k_norm, m_w_out, m_ffn2_norm, m_ffn2_w_gate, m_ffn2_w_up, m_ffn2_w_down, m_ple_norm, m_ple_w_gate, m_ple_w_proj, v_ffn1_norm, v_ffn1_w_gate, v_ffn1_w_up, v_ffn1_w_down, v_mix_norm, v_w_in, v_conv_w, v_conv_b, v_q_norm, v_k_norm, v_w_out, v_ffn2_norm, v_ffn2_w_gate, v_ffn2_w_up, v_ffn2_w_down, v_ple_norm, v_ple_w_gate, v_ple_w_proj):
    x0, p2, tgt = x[0], p[0, 0], loss_target[0]
    t, d = x0.shape
    tt_nt = _pick(t, 1024)
    tt_nn = _pick(t, 512)
    tt_ew = _pick(t, 512)
    tt_ple = _pick(t, 256)
    sb = _pick(t // 2, 256)

    t_bf = lambda w: w[0].T.astype(BF16)
    n_bf = lambda w: w[0].astype(BF16)
    cw_tile = jnp.zeros((SUBLANES, LANES), F32).at[:conv_w.shape[1], :conv_w.shape[2]].set(conv_w[0])
    wg1t, wu1t, wd1 = _all_gather("gather_ffn1_weights", [t_bf(ffn1_w_gate), t_bf(ffn1_w_up), n_bf(ffn1_w_down)])
    gather_mix = _Exchange([t_bf(w_in), n_bf(w_out), cw_tile], gather=True)
    gather_late = _Exchange([t_bf(ffn2_w_gate), t_bf(ffn2_w_up), n_bf(ffn2_w_down), n_bf(ple_w_gate),
                             t_bf(ple_w_proj)], gather=True)
    ncs = conv_w.shape[2]

    qg = jnp.tile(q_norm, (1, SB_DIM // HEAD_DIM))
    kg = jnp.tile(k_norm, (1, SB_DIM // HEAD_DIM))
    gi = lax.broadcasted_iota(jnp.int32, (SLAB, SLAB), 0) // HEAD_DIM
    gj = lax.broadcasted_iota(jnp.int32, (SLAB, SLAB), 1) // HEAD_DIM
    bd = (gi == gj).astype(BF16)
    tri, tri_inc = _tri_masks(sb)

    h1 = _rmsnorm("ffn1_norm", x0, ffn1_norm, tt_ew)
    x1, (wint, wout, cw_all) = _ffn_fwd("ffn1", x0, h1, wg1t, wu1t, wd1, tt_nt, tt_nn, gather_mix)
    cw_full = cw_all.reshape(N_DEV, SUBLANES, LANES)[:, :, :ncs].transpose(1, 0, 2).reshape(SUBLANES, N_DEV * ncs)
    h2 = _rmsnorm("mix_norm", x1, mix_norm, tt_ew)
    (proj,) = _mm_nt("in_proj", [h2], [wint], [(0, 0)], lambda accs: accs, [F32], tt_nt, SLAB)
    y_conv = _conv_fwd(proj, cw_full, conv_b, tt_ew)
    qn, kn, vb = _qknorm_fwd(proj, qg, kg, bd, tt_ew)
    o, ob, wg2t, wu2t, wd2, wpg, wppt = _attn_fwd(qn, kn, vb, tri, sb, gather_late)
    (x2,) = _mm_nn("out_proj", [(y_conv, wout, 0), (ob, wout, 1)], [x1], [],
                   lambda acc, rows, fulls: [rows[0] + acc], [("tile", F32)], tt_nn, SLAB)
    h3 = _rmsnorm("ffn2_norm", x2, ffn2_norm, tt_ew)
    x3, _ = _ffn_fwd("ffn2", x2, h3, wg2t, wu2t, wd2, tt_nt, tt_nn)

    dx3, dx3b, dwpg, dwppt, gp_ple, lpart = _ple_loss(x3, p2, tgt, ple_norm, wpg, wppt, tt_ple)
    loss = lax.psum(jnp.sum(lpart[:, 0]), MESH_AXES)
    dx2, dx2b, gp_ffn2, dwg2, dwu2, dwd2, _ = _ffn_bwd("ffn2", x2, h3, dx3, dx3b, ffn2_norm, wg2t, wu2t, wd2,
                                                        tt_nt, tt_nn)
    (dycat,) = _mm_nt("out_proj_bwd", [dx2b], [wout], [(0, 0)], lambda accs: accs, [F32], tt_nt, SLAB)
    dwout = jnp.concatenate([_mm_tn("dwout_conv", y_conv, dx2b, 1.0, SLAB, tt_nt),
                             _mm_tn("dwout_attn", ob, dx2b, 1.0, SLAB, tt_nt)], axis=0)
    dqn, dkn, dv, *slots_late = _attn_bwd(qn, kn, vb, o, dycat, tri, tri_inc, sb,
                                          _Exchange([dwg2, dwu2, dwd2, dwpg, dwppt], gather=False))
    dq, dk, dvb, qk_part = _qknorm_bwd(proj, dqn, dkn, dv, qg, kg, bd, tt_ew)
    db, dc, du, conv_part = _conv_bwd(proj, dycat, cw_full, conv_b, tt_ew)
    dproj = [db, dc, du, dq, dk, dvb]
    dx1, dx1b, gp_mix = _mm_nn("in_proj_bwd", [(dp, wint, s) for s, dp in enumerate(dproj)], [x1, dx2], [mix_norm],
                               _norm_bwd_epilogue, _NORM_BWD_OUTS, tt_nn, SLAB)
    dwin = jnp.concatenate([_mm_tn(f"dwin_{s}", dp, h2, 1.0, SLAB, tt_nt) for s, dp in enumerate(dproj)], axis=0)
    dx0, _, gp_ffn1, dwg1, dwu1, dwd1, slots_mix = _ffn_bwd(
        "ffn1", x0, h1, dx1, dx1b, ffn1_norm, wg1t, wu1t, wd1, tt_nt, tt_nn, _Exchange([dwin, dwout], gather=False))

    slots = [*_exchange_shards("exchange_ffn1_grads", [dwg1, dwu1, dwd1]), *slots_mix, *slots_late]
    sums = [_sum_slots(f"sum_grads_{i}", s.reshape(N_DEV, s.shape[0] // N_DEV, s.shape[1]))
            for i, s in enumerate(slots)]
    g_wg1, g_wu1, g_wd1, g_win, g_wout, g_wg2, g_wu2, g_wd2, g_wpg, g_wpp = sums
    small = _pack_small([gp_ffn1, gp_mix, gp_ffn2, gp_ple], conv_part, qk_part)
    (small_all,) = _all_gather("gather_small_grads", [small])
    sm = _sum_slots("sum_small_grads", small_all.reshape(N_DEV, 2 * SUBLANES, d))
    fold = lambda r: r.reshape(SB_DIM // HEAD_DIM, HEAD_DIM).sum(axis=0)[None]
    me_idx = 4 * lax.axis_index("x") + 2 * lax.axis_index("y") + lax.axis_index("c")
    cw_grad = jnp.stack([sm[4, SLAB:], sm[5, :SLAB], sm[5, SLAB:]])
    grads = {
        "ffn1_norm": sm[0:1], "ffn1_w_gate": g_wg1.T, "ffn1_w_up": g_wu1.T, "ffn1_w_down": g_wd1,
        "mix_norm": sm[1:2], "w_in": g_win.T, "conv_w": lax.dynamic_slice_in_dim(cw_grad, me_idx * ncs, ncs, axis=1),
        "conv_b": sm[4:5, :SLAB], "q_norm": fold(sm[6, :SLAB]), "k_norm": fold(sm[6, SLAB:]),
        "w_out": g_wout, "ffn2_norm": sm[2:3], "ffn2_w_gate": g_wg2.T, "ffn2_w_up": g_wu2.T, "ffn2_w_down": g_wd2,
        "ple_norm": sm[3:4], "ple_w_gate": g_wpg, "ple_w_proj": g_wpp.T,
    }

    weights = dict(ffn1_norm=ffn1_norm, ffn1_w_gate=ffn1_w_gate, ffn1_w_up=ffn1_w_up, ffn1_w_down=ffn1_w_down,
                   mix_norm=mix_norm, w_in=w_in, conv_w=conv_w, conv_b=conv_b, q_norm=q_norm, k_norm=k_norm,
                   w_out=w_out, ffn2_norm=ffn2_norm, ffn2_w_gate=ffn2_w_gate, ffn2_w_up=ffn2_w_up,
                   ffn2_w_down=ffn2_w_down, ple_norm=ple_norm, ple_w_gate=ple_w_gate, ple_w_proj=ple_w_proj)
    m_in = dict(ffn1_norm=m_ffn1_norm, ffn1_w_gate=m_ffn1_w_gate, ffn1_w_up=m_ffn1_w_up, ffn1_w_down=m_ffn1_w_down,
                mix_norm=m_mix_norm, w_in=m_w_in, conv_w=m_conv_w, conv_b=m_conv_b, q_norm=m_q_norm,
                k_norm=m_k_norm, w_out=m_w_out, ffn2_norm=m_ffn2_norm, ffn2_w_gate=m_ffn2_w_gate,
                ffn2_w_up=m_ffn2_w_up, ffn2_w_down=m_ffn2_w_down, ple_norm=m_ple_norm, ple_w_gate=m_ple_w_gate,
                ple_w_proj=m_ple_w_proj)
    v_in = dict(ffn1_norm=v_ffn1_norm, ffn1_w_gate=v_ffn1_w_gate, ffn1_w_up=v_ffn1_w_up, ffn1_w_down=v_ffn1_w_down,
                mix_norm=v_mix_norm, w_in=v_w_in, conv_w=v_conv_w, conv_b=v_conv_b, q_norm=v_q_norm,
                k_norm=v_k_norm, w_out=v_w_out, ffn2_norm=v_ffn2_norm, ffn2_w_gate=v_ffn2_w_gate,
                ffn2_w_up=v_ffn2_w_up, ffn2_w_down=v_ffn2_w_down, ple_norm=v_ple_norm, ple_w_gate=v_ple_w_gate,
                ple_w_proj=v_ple_w_proj)
    g_out, d_out, m_out, v_out = [], [], [], []
    for name, w in weights.items():
        w2 = w.reshape(w.shape[-2:])
        g2 = grads[name].reshape(w2.shape)
        dlt, nm, nv = _adamw(f"adamw_{name}", w2, g2, m_in[name].reshape(w2.shape), v_in[name].reshape(w2.shape))
        g_out.append(g2.reshape(w.shape))
        d_out.append(dlt.reshape(w.shape))
        m_out.append(nm.reshape(w.shape))
        v_out.append(nv.reshape(w.shape))
    return (loss, dx0[None], *g_out, *d_out, *m_out, *v_out)
```

```python
import jax
import jax.numpy as jnp
from jax import lax
from jax.experimental import pallas as pl
from jax.experimental.pallas import tpu as pltpu

F32 = jnp.float32
BF16 = jnp.bfloat16

EPS = 1e-6
FFN_RES = 0.5
HEAD_DIM = 64
CONV_DIM = 512
SB_DIM = 512
SLAB = 512
N_DEV = 8
MESH_AXES = ("x", "y", "c")
MESH = pl.DeviceIdType.MESH

ADAM_LR = 0.001
ADAM_B1 = 0.9
ADAM_B2 = 0.999
ADAM_EPS = 1e-08
ADAM_WD = 0.01
ADAM_STEP = 10

VMEM_LIMIT_BYTES = 56 * 1024 * 1024
SUBLANES = 8
LANES = 128


def _cparams(*semantics):
    return pltpu.CompilerParams(dimension_semantics=semantics, vmem_limit_bytes=VMEM_LIMIT_BYTES)


def _dot_nn(a, b):
    return jnp.dot(a, b, preferred_element_type=F32)


def _dot_nt(a, b):
    return lax.dot_general(a, b, (((1,), (1,)), ((), ())), preferred_element_type=F32)


def _dot_tn(a, b):
    return lax.dot_general(a, b, (((0,), (0,)), ((), ())), preferred_element_type=F32)


def _fold8(v):
    rows, cols = v.shape
    return jnp.sum(v.reshape(rows // SUBLANES, SUBLANES, cols), axis=0)


def _split2(v):
    hi = v.astype(BF16)
    lo = (v - hi.astype(F32)).astype(BF16)
    return hi, lo


def _rms_stats(x):
    return lax.rsqrt(jnp.mean(x * x, axis=-1, keepdims=True) + EPS)


def _rms_bwd(dh, x, gain):
    r = _rms_stats(x)
    u = dh * gain
    dx = r * u - x * (r * r * r) * jnp.mean(u * x, axis=-1, keepdims=True)
    return dx, dh * x * r


def _pick(n, pref):
    return pref if n % pref == 0 else n


def _rmsnorm(name, x, gain, tt):
    t, d = x.shape

    def body(x_ref, g_ref, o_ref):
        xv = x_ref[...]
        o_ref[...] = ((xv * _rms_stats(xv)) * g_ref[...]).astype(BF16)

    return pl.pallas_call(
        body, name=name, grid=(t // tt,),
        in_specs=[pl.BlockSpec((tt, d), lambda i: (i, 0)), pl.BlockSpec((1, d), lambda i: (0, 0))],
        out_specs=pl.BlockSpec((tt, d), lambda i: (i, 0)),
        out_shape=jax.ShapeDtypeStruct((t, d), BF16),
        compiler_params=_cparams("parallel"),
    )(x, gain)


class _Exchange:
    FLIPS = [(fx, fy, fc) for fx in (0, 1) for fy in (0, 1) for fc in (0, 1)][1:]

    def __init__(self, arrays, gather):
        self.arrays = list(arrays)
        self.gather = gather
        self.n = len(self.arrays)
        self.rows = [a.shape[0] if gather else a.shape[0] // N_DEV for a in self.arrays]
        self.out_shape = [jax.ShapeDtypeStruct((N_DEV * r, a.shape[1]), a.dtype)
                          for r, a in zip(self.rows, self.arrays)]
        self.scratch = [pltpu.SemaphoreType.DMA((7 * self.n,)), pltpu.SemaphoreType.DMA((7 * self.n,)),
                        pltpu.SemaphoreType.DMA((self.n,))]

    def _copies(self, ins, outs, sems, arrivals):
        send_sems, recv_sems, local_sems = sems
        x, y, c = (lax.axis_index(a) for a in MESH_AXES)
        me_idx = 4 * x + 2 * y + c
        local, send, recv = [], [], []
        for a in range(self.n):
            r = self.rows[a]

            def blk(ref, idx, r=r):
                return ref.at[pl.ds(idx * r, r), :]

            def src(idx, a=a, blk=blk):
                return ins[a] if self.gather else blk(ins[a], idx)

            local.append(pltpu.make_async_copy(src(me_idx), blk(outs[a], me_idx), local_sems.at[a]))
            for k, flip in enumerate(self.FLIPS):
                px, py, pc = (1 - v if f else v for v, f in zip((x, y, c), flip))
                p_idx = 4 * px + 2 * py + pc
                for dst_idx, group in ((me_idx, send), (p_idx, recv))[:2 if arrivals else 1]:
                    group.append(pltpu.make_async_remote_copy(
                        src_ref=src(p_idx), dst_ref=blk(outs[a], dst_idx),
                        send_sem=send_sems.at[7 * a + k], recv_sem=recv_sems.at[7 * a + k],
                        device_id=(px, py, pc), device_id_type=MESH))
        return local, send, recv

    def start(self, ins, outs, sems):
        local, send, _ = self._copies(ins, outs, sems, arrivals=False)
        for cp in local + send:
            cp.start()

    def wait(self, ins, outs, sems):
        local, send, recv = self._copies(ins, outs, sems, arrivals=True)
        for s, r in zip(send, recv):
            r.wait_recv()
            s.wait_send()
        for cp in local:
            cp.wait()

    def attach(self, refs, first, last):
        pl.when(first)(lambda: self.start(*refs))
        pl.when(last)(lambda: self.wait(*refs))


def _split_refs(refs, n_in, n_out, n_scratch, comm):
    nc = comm.n if comm else 0
    ins, rest = refs[:n_in], refs[n_in:]
    c_in, rest = rest[:nc], rest[nc:]
    outs, rest = rest[:n_out], rest[n_out:]
    c_out, rest = rest[:nc], rest[nc:]
    scratch, c_sems = rest[:n_scratch], rest[n_scratch:]
    return ins, outs, scratch, ((c_in, c_out, c_sems) if comm else None)


def _with_comm(comm, in_specs, out_specs, out_shape, scratch):
    if comm is None:
        return in_specs, out_specs, out_shape, scratch, []
    return (in_specs + _any_specs(comm.n), out_specs + _any_specs(comm.n), out_shape + comm.out_shape,
            scratch + comm.scratch, comm.arrays)


def _mm_nt(name, a_list, w_list, pairs, epilogue, out_dtypes, tt, tn, comm=None):
    t = a_list[0].shape[0]
    n = w_list[0].shape[0]
    na, nw = len(a_list), len(w_list)
    ni, nj = t // tt, n // tn

    def body(*refs):
        ins, o_refs, _, c_refs = _split_refs(refs, na + nw, len(out_dtypes), 0, comm)
        a_refs, w_refs = ins[:na], ins[na:]
        if comm:
            i = pl.program_id(0)
            comm.attach(c_refs, i == 0, i == ni - 1)
        a_vals = [a_ref[...] for a_ref in a_refs]
        for j in range(nj):
            cols = pl.ds(j * tn, tn)
            accs = [_dot_nt(a_vals[ai], w_refs[wi][cols, :]) for ai, wi in pairs]
            for o_ref, o in zip(o_refs, epilogue(accs)):
                o_ref[:, cols] = o.astype(o_ref.dtype)

    in_specs = ([pl.BlockSpec((tt, a.shape[1]), lambda i: (i, 0)) for a in a_list]
                + [pl.BlockSpec(w.shape, lambda i: (0, 0), pipeline_mode=pl.Buffered(1)) for w in w_list])
    in_specs, out_specs, out_shape, scratch, extra = _with_comm(
        comm, in_specs, [pl.BlockSpec((tt, n), lambda i: (i, 0)) for _ in out_dtypes],
        [jax.ShapeDtypeStruct((t, n), dt) for dt in out_dtypes], [])
    return pl.pallas_call(
        body, name=name, grid=(ni,), in_specs=in_specs, out_specs=out_specs, out_shape=out_shape,
        scratch_shapes=scratch,
        compiler_params=_cparams("arbitrary" if comm else "parallel"),
    )(*a_list, *w_list, *extra)


def _mm_nn(name, pairs, rows, fulls, epilogue, out_kinds, tt, tk, comm=None):
    t, k_total = pairs[0][0].shape
    n = pairs[0][1].shape[1]
    nk = k_total // tk
    nt = t // tt
    npair, nrow, nfull = len(pairs), len(rows), len(fulls)

    def body(*refs):
        ins, o_refs, scratch, c_refs = _split_refs(refs, 2 * npair + nrow + nfull, len(out_kinds), min(nk - 1, 1), comm)
        a_refs, w_refs = ins[:npair], ins[npair:2 * npair]
        r_refs, f_refs = ins[2 * npair:2 * npair + nrow], ins[2 * npair + nrow:]
        i, k = pl.program_id(0), pl.program_id(1)
        if comm:
            comm.attach(c_refs, jnp.logical_and(i == 0, k == 0), jnp.logical_and(i == nt - 1, k == nk - 1))
        s = _dot_nn(a_refs[0][...], w_refs[0][...])
        for a_ref, w_ref in zip(a_refs[1:], w_refs[1:]):
            s = s + _dot_nn(a_ref[...], w_ref[...])

        def finish(acc):
            outs = epilogue(acc, [r[...] for r in r_refs], [f[...] for f in f_refs])
            for o_ref, o in zip(o_refs, outs):
                o_ref[...] = o.astype(o_ref.dtype)

        if nk == 1:
            finish(s)
        else:
            acc_ref = scratch[0]

            @pl.when(k == 0)
            def _():
                acc_ref[...] = s

            @pl.when(k > 0)
            def _():
                acc_ref[...] += s

            @pl.when(k == nk - 1)
            def _():
                finish(acc_ref[...])

    once = dict(pipeline_mode=pl.Buffered(1)) if nk == 1 else {}
    in_specs = ([pl.BlockSpec((tt, tk), lambda i, k: (i, k)) for _ in pairs]
                + [pl.BlockSpec((tk, n), (lambda i, k, off=off: (k + off, 0)), **once) for _, _, off in pairs]
                + [pl.BlockSpec((tt, n), lambda i, k: (i, 0)) for _ in rows]
                + [pl.BlockSpec((1, n), lambda i, k: (0, 0)) for _ in fulls])
    out_specs, out_shape = [], []
    for kind, dt in out_kinds:
        if kind == "tile":
            out_specs.append(pl.BlockSpec((tt, n), lambda i, k: (i, 0)))
            out_shape.append(jax.ShapeDtypeStruct((t, n), dt))
        else:
            out_specs.append(pl.BlockSpec((SUBLANES, n), lambda i, k: (i, 0)))
            out_shape.append(jax.ShapeDtypeStruct((nt * SUBLANES, n), dt))
    in_specs, out_specs, out_shape, scratch, extra = _with_comm(
        comm, in_specs, out_specs, out_shape, [] if nk == 1 else [pltpu.VMEM((tt, n), F32)])
    return pl.pallas_call(
        body, name=name, grid=(nt, nk), in_specs=in_specs, out_specs=out_specs, out_shape=out_shape,
        scratch_shapes=scratch,
        compiler_params=_cparams("arbitrary" if comm else "parallel", "arbitrary"),
    )(*[a for a, _, _ in pairs], *[w for _, w, _ in pairs], *rows, *fulls, *extra)


def _mm_tn(name, a, b, scale, tm, tt):
    t, m = a.shape
    n = b.shape[1]
    nt = t // tt

    def body(a_ref, b_ref, o_ref, acc_ref):
        k = pl.program_id(1)
        s = _dot_tn(a_ref[...], b_ref[...])

        @pl.when(k == 0)
        def _():
            acc_ref[...] = s

        @pl.when(k > 0)
        def _():
            acc_ref[...] += s

        @pl.when(k == nt - 1)
        def _():
            o_ref[...] = (acc_ref[...] * scale).astype(o_ref.dtype)

    return pl.pallas_call(
        body, name=name, grid=(m // tm, nt),
        in_specs=[pl.BlockSpec((tt, tm), lambda i, k: (k, i)), pl.BlockSpec((tt, n), lambda i, k: (k, 0))],
        out_specs=pl.BlockSpec((tm, n), lambda i, k: (i, 0)),
        out_shape=jax.ShapeDtypeStruct((m, n), BF16),
        scratch_shapes=[pltpu.VMEM((tm, n), F32)],
        compiler_params=_cparams("parallel", "arbitrary"),
    )(a, b)


def _group_sum(v, bd):
    hi = v.astype(BF16)
    r1 = v - hi.astype(F32)
    mid = r1.astype(BF16)
    lo = (r1 - mid.astype(F32)).astype(BF16)
    return _dot_nn(hi, bd) + _dot_nn(mid, bd) + _dot_nn(lo, bd)


def _qknorm_fwd(proj, qg, kg, bd, tt):
    t = proj.shape[0]

    def body(q_ref, k_ref, v_ref, qg_ref, kg_ref, bd_ref, qn_ref, kn_ref, vb_ref):
        bdv = bd_ref[...]
        for x_ref, g_ref, o_ref in ((q_ref, qg_ref, qn_ref), (k_ref, kg_ref, kn_ref)):
            xv = x_ref[...]
            r = lax.rsqrt(_group_sum(xv * xv, bdv) * (1.0 / HEAD_DIM) + EPS)
            o_ref[...] = ((xv * r) * g_ref[...]).astype(BF16)
        vb_ref[...] = v_ref[...].astype(BF16)

    slab = lambda s: pl.BlockSpec((tt, SLAB), lambda i, s=s: (i, s))
    full = lambda shape: pl.BlockSpec(shape, lambda i: (0, 0))
    out = pl.BlockSpec((tt, SLAB), lambda i: (i, 0))
    return pl.pallas_call(
        body, name="qknorm_fwd", grid=(t // tt,),
        in_specs=[slab(3), slab(4), slab(5), full((1, SLAB)), full((1, SLAB)), full((SLAB, SLAB))],
        out_specs=[out, out, out],
        out_shape=[jax.ShapeDtypeStruct((t, SLAB), BF16)] * 3,
        compiler_params=_cparams("parallel"),
    )(proj, proj, proj, qg, kg, bd)


def _qknorm_bwd(proj, dqn, dkn, dv, qg, kg, bd, tt):
    t = proj.shape[0]

    def body(q_ref, k_ref, dqn_ref, dkn_ref, dv_ref, qg_ref, kg_ref, bd_ref, dq_ref, dk_ref, dvb_ref, part_ref):
        bdv = bd_ref[...]
        parts = []
        for x_ref, d_ref, g_ref, o_ref in ((q_ref, dqn_ref, qg_ref, dq_ref), (k_ref, dkn_ref, kg_ref, dk_ref)):
            xv, dn = x_ref[...], d_ref[...]
            r = lax.rsqrt(_group_sum(xv * xv, bdv) * (1.0 / HEAD_DIM) + EPS)
            u = dn * g_ref[...]
            dx = r * u - xv * (r * r * r) * (_group_sum(u * xv, bdv) * (1.0 / HEAD_DIM))
            o_ref[...] = dx.astype(BF16)
            parts.append(_fold8(dn * xv * r))
        dvb_ref[...] = dv_ref[...].astype(BF16)
        part_ref[...] = jnp.concatenate(parts, axis=1)

    slab = lambda s: pl.BlockSpec((tt, SLAB), lambda i, s=s: (i, s))
    tile = pl.BlockSpec((tt, SLAB), lambda i: (i, 0))
    full = lambda shape: pl.BlockSpec(shape, lambda i: (0, 0))
    return pl.pallas_call(
        body, name="qknorm_bwd", grid=(t // tt,),
        in_specs=[slab(3), slab(4), tile, tile, tile, full((1, SLAB)), full((1, SLAB)), full((SLAB, SLAB))],
        out_specs=[tile, tile, tile, pl.BlockSpec((SUBLANES, 2 * SLAB), lambda i: (i, 0))],
        out_shape=[jax.ShapeDtypeStruct((t, SLAB), BF16)] * 3
        + [jax.ShapeDtypeStruct((t // tt * SUBLANES, 2 * SLAB), F32)],
        compiler_params=_cparams("parallel"),
    )(proj, proj, dqn, dkn, dv, qg, kg, bd)


def _conv_taps(z, z_prev, row):
    zm1 = jnp.where(row == 0, z_prev[7:8], pltpu.roll(z, 1, 0))
    zm2 = jnp.where(row == 0, z_prev[6:7], jnp.where(row == 1, z_prev[7:8], pltpu.roll(z, 2, 0)))
    return zm1, zm2


def _conv_fwd(proj, cw, cb, tt):
    t = proj.shape[0]
    tb = tt // SUBLANES

    def body(b_ref, c_ref, u_ref, cp_ref, up_ref, cw_ref, cb_ref, o_ref):
        i = pl.program_id(0)
        z = c_ref[...] * u_ref[...]
        z_prev = jnp.where(i > 0, cp_ref[...] * up_ref[...], 0.0)
        row = lax.broadcasted_iota(jnp.int32, (tt, 1), 0)
        zm1, zm2 = _conv_taps(z, z_prev, row)
        y = cw_ref[0:1] * zm2 + cw_ref[1:2] * zm1 + cw_ref[2:3] * z + cb_ref[...]
        o_ref[...] = (b_ref[...] * y).astype(BF16)

    slab = lambda s: pl.BlockSpec((tt, SLAB), lambda i, s=s: (i, s))
    prev = lambda s: pl.BlockSpec((SUBLANES, SLAB), lambda i, s=s: (jnp.maximum(i * tb - 1, 0), s))
    return pl.pallas_call(
        body, name="conv_fwd", grid=(t // tt,),
        in_specs=[slab(0), slab(1), slab(2), prev(1), prev(2),
                  pl.BlockSpec((SUBLANES, SLAB), lambda i: (0, 0)), pl.BlockSpec((1, SLAB), lambda i: (0, 0))],
        out_specs=pl.BlockSpec((tt, SLAB), lambda i: (i, 0)),
        out_shape=jax.ShapeDtypeStruct((t, SLAB), BF16),
        compiler_params=_cparams("parallel"),
    )(proj, proj, proj, proj, proj, cw, cb)


def _conv_bwd(proj, dycat, cw, cb, tt):
    t = proj.shape[0]
    tb = tt // SUBLANES
    nblk = t // SUBLANES

    def body(b_ref, c_ref, u_ref, cp_ref, up_ref, bn_ref, dy_ref, dyn_ref, cw_ref, cb_ref,
             db_ref, dc_ref, du_ref, part_ref):
        i = pl.program_id(0)
        c, u, b, dyc = c_ref[...], u_ref[...], b_ref[...], dy_ref[...]
        z = c * u
        z_prev = jnp.where(i > 0, cp_ref[...] * up_ref[...], 0.0)
        row = lax.broadcasted_iota(jnp.int32, (tt, 1), 0)
        zm1, zm2 = _conv_taps(z, z_prev, row)
        w0, w1, w2 = cw_ref[0:1], cw_ref[1:2], cw_ref[2:3]
        y = w0 * zm2 + w1 * zm1 + w2 * z + cb_ref[...]
        db_ref[...] = (dyc * y).astype(BF16)
        g = dyc * b
        g_next = jnp.where(i < pl.num_programs(0) - 1, dyn_ref[...] * bn_ref[...], 0.0)
        gp1 = jnp.where(row == tt - 1, g_next[0:1], pltpu.roll(g, tt - 1, 0))
        gp2 = jnp.where(row == tt - 2, g_next[0:1], jnp.where(row == tt - 1, g_next[1:2], pltpu.roll(g, tt - 2, 0)))
        dz = w2 * g + w1 * gp1 + w0 * gp2
        dc_ref[...] = (dz * u).astype(BF16)
        du_ref[...] = (dz * c).astype(BF16)
        part_ref[...] = jnp.concatenate([_fold8(g * zm2), _fold8(g * zm1), _fold8(g * z), _fold8(g)], axis=1)

    slab = lambda s: pl.BlockSpec((tt, SLAB), lambda i, s=s: (i, s))
    prev = lambda s: pl.BlockSpec((SUBLANES, SLAB), lambda i, s=s: (jnp.maximum(i * tb - 1, 0), s))
    nxt = lambda s: pl.BlockSpec((SUBLANES, SLAB), lambda i, s=s: (jnp.minimum((i + 1) * tb, nblk - 1), s))
    tile = pl.BlockSpec((tt, SLAB), lambda i: (i, 0))
    return pl.pallas_call(
        body, name="conv_bwd", grid=(t // tt,),
        in_specs=[slab(0), slab(1), slab(2), prev(1), prev(2), nxt(0), slab(0), nxt(0),
                  pl.BlockSpec((SUBLANES, SLAB), lambda i: (0, 0)), pl.BlockSpec((1, SLAB), lambda i: (0, 0))],
        out_specs=[tile, tile, tile, pl.BlockSpec((SUBLANES, 4 * SLAB), lambda i: (i, 0))],
        out_shape=[jax.ShapeDtypeStruct((t, SLAB), BF16)] * 3
        + [jax.ShapeDtypeStruct((t // tt * SUBLANES, 4 * SLAB), F32)],
        compiler_params=_cparams("parallel"),
    )(proj, proj, proj, proj, proj, proj, dycat, dycat, cw, cb)


def _tri_masks(n):
    r = lax.broadcasted_iota(jnp.int32, (n, n), 0)
    c = lax.broadcasted_iota(jnp.int32, (n, n), 1)
    return (r > c).astype(BF16), (r >= c).astype(BF16)


def _sb_scores(qh, k, r_run, tri, causal):
    z = _dot_nt(qh, k)
    softplus = jnp.maximum(z, 0.0) + jnp.log(1.0 + jnp.exp(-jnp.abs(z)))
    lk = -softplus
    if causal is not None:
        lk = jnp.where(causal, lk, 0.0)
    hi, lo = _split2(lk)
    later = _dot_nn(hi, tri) + _dot_nn(lo, tri) + r_run
    ls = z + lk
    arg = ls + later
    if causal is not None:
        arg = jnp.where(causal, arg, -1e30)
    return lk, ls, jnp.exp(arg), later[:, 0:1] + lk[:, 0:1]


SB_DEAD_LOG = -111.0
CHAINS = ((0, 0), (0, 1), (1, 0), (1, 1))


def _all_dead(r_runs):
    m = r_runs[0]
    for r in r_runs[1:]:
        m = jnp.maximum(m, r)
    return (jnp.max(m) < SB_DEAD_LOG).astype(jnp.int32)


def _attn_fwd(qn, kn, vb, tri, sb, comm=None):
    t = qn.shape[0]
    bq = 2 * sb
    scale = HEAD_DIM ** -0.5

    def body(*refs):
        (q_ref, k_ref, v_ref, tri_ref), (o_ref, ob_ref), (acc_ref,), c_refs = _split_refs(refs, 4, 2, 1, comm)
        qi = pl.program_id(1)
        if comm:
            hp = pl.program_id(0)
            comm.attach(c_refs, jnp.logical_and(hp == 0, qi == 0),
                        jnp.logical_and(hp == pl.num_programs(0) - 1, qi == pl.num_programs(1) - 1))
        lane = lax.broadcasted_iota(jnp.int32, (1, LANES), 1)
        hmasks = (lane < HEAD_DIM, lane >= HEAD_DIM)
        diag = lax.broadcasted_iota(jnp.int32, (sb, sb), 1) < lax.broadcasted_iota(jnp.int32, (sb, sb), 0)
        triv = tri_ref[...]
        qs = [[jnp.where(hm, q_ref[pl.ds(s * sb, sb), :], 0) * scale for hm in hmasks] for s in range(2)]
        acc_ref[...] = jnp.zeros_like(acc_ref)

        def load_kv(kb):
            ks = pl.multiple_of(kb * sb, sb)
            vraw = v_ref[pl.ds(ks, sb), :]
            return k_ref[pl.ds(ks, sb), :], [jnp.where(hm, vraw, 0) for hm in hmasks]

        def tile(s, hh, kv, r_run, causal):
            _, _, a, r_new = _sb_scores(qs[s][hh], kv[0], r_run, triv, causal)
            acc_ref[2 * s + hh] += _dot_nn(a.astype(BF16), kv[1][hh])
            return r_new

        zero = jnp.zeros((sb, 1), F32)
        kv_hi, kv_lo = load_kv(2 * qi + 1), load_kv(2 * qi)
        r_runs = [None] * 4
        for hh in range(2):
            r_runs[hh] = tile(0, hh, kv_lo, zero, diag)
            r_runs[2 + hh] = tile(1, hh, kv_lo, tile(1, hh, kv_hi, zero, diag), None)

        def step(carry):
            i, _, *rs = carry
            kv = load_kv(2 * qi - 1 - i)
            rs = [tile(s, hh, kv, rs[2 * s + hh], None) for s, hh in CHAINS]
            return (i + 1, _all_dead(rs), *rs)

        lax.while_loop(lambda c: jnp.logical_and(c[0] < 2 * qi, c[1] == 0), step,
                       (jnp.int32(0), _all_dead(r_runs), *r_runs))
        for s in range(2):
            out = acc_ref[2 * s] + acc_ref[2 * s + 1]
            o_ref[pl.ds(s * sb, sb), :] = out
            ob_ref[pl.ds(s * sb, sb), :] = out.astype(BF16)

    qspec = pl.BlockSpec((bq, LANES), lambda h, i: (i, h))
    kspec = pl.BlockSpec((t, LANES), lambda h, i: (0, h))
    in_specs, out_specs, out_shape, scratch, extra = _with_comm(
        comm, [qspec, kspec, kspec, pl.BlockSpec((sb, sb), lambda h, i: (0, 0))], [qspec, qspec],
        [jax.ShapeDtypeStruct((t, SB_DIM), F32), jax.ShapeDtypeStruct((t, SB_DIM), BF16)],
        [pltpu.VMEM((4, sb, LANES), F32)])
    return pl.pallas_call(
        body, name="attn_fwd", grid=(SB_DIM // LANES, t // bq),
        in_specs=in_specs, out_specs=out_specs, out_shape=out_shape, scratch_shapes=scratch,
        compiler_params=_cparams("arbitrary" if comm else "parallel", "arbitrary"),
    )(qn, kn, vb, tri, *extra)


def _attn_bwd(qn, kn, vb, o, dycat, tri, tri_inc, sb, comm=None):
    t = qn.shape[0]
    bq = 2 * sb
    scale = HEAD_DIM ** -0.5

    def body(*refs):
        ins, (dq_ref, dk_ref, dv_ref), (dq_acc,), c_refs = _split_refs(refs, 7, 3, 1, comm)
        q_ref, k_ref, v_ref, o_ref, do_ref, tri_ref, tinc_ref = ins
        qi = pl.program_id(1)
        if comm:
            hp = pl.program_id(0)
            comm.attach(c_refs, jnp.logical_and(hp == 0, qi == 0),
                        jnp.logical_and(hp == pl.num_programs(0) - 1, qi == pl.num_programs(1) - 1))

        @pl.when(qi == 0)
        def _():
            dk_ref[...] = jnp.zeros_like(dk_ref)
            dv_ref[...] = jnp.zeros_like(dv_ref)

        lane = lax.broadcasted_iota(jnp.int32, (1, LANES), 1)
        hmasks = (lane < HEAD_DIM, lane >= HEAD_DIM)
        diag = lax.broadcasted_iota(jnp.int32, (sb, sb), 1) < lax.broadcasted_iota(jnp.int32, (sb, sb), 0)
        triv, tincv = tri_ref[...], tinc_ref[...]
        qs, dobs, d_rows = [], [], []
        for s in range(2):
            rows = pl.ds(s * sb, sb)
            qs.append([jnp.where(hm, q_ref[rows, :], 0) * scale for hm in hmasks])
            dobs.append([jnp.where(hm, do_ref[rows, :], 0.0).astype(BF16) for hm in hmasks])
            d_rows.append([jnp.sum(d.astype(F32) * o_ref[rows, :], axis=1, keepdims=True) for d in dobs[s]])
        dq_acc[...] = jnp.zeros_like(dq_acc)

        def load_kv(kb):
            ks = pl.multiple_of(kb * sb, sb)
            return k_ref[pl.ds(ks, sb), :], v_ref[pl.ds(ks, sb), :], ks

        def tile(s, hh, kv, r_run, g_run, causal):
            k, v, _ = kv
            qh, dob = qs[s][hh], dobs[s][hh]
            _, ls, a, r_new = _sb_scores(qh, k, r_run, triv, causal)
            ab = a.astype(BF16)
            e = _dot_nt(dob, v) * ab.astype(F32)
            hi, lo = _split2(e)
            e_from = _dot_nn(hi, tincv) + _dot_nn(lo, tincv) + g_run
            beta = jnp.exp(ls)
            dz = e - beta * (e + (d_rows[s][hh] - e_from))
            if causal is not None:
                dz = jnp.where(causal, dz, 0.0)
            dzb = dz.astype(BF16)
            dq_acc[2 * s + hh] += _dot_nn(dzb, k)
            return r_new, e_from[:, 0:1], _dot_tn(dzb, qh), _dot_tn(ab, dob)

        def scatter(kv, parts):
            rows = pl.ds(kv[2], sb)
            dk_ref[rows, :] += sum(p[2] for p in parts[1:]) + parts[0][2]
            dv_ref[rows, :] += sum(p[3] for p in parts[1:]) + parts[0][3]

        zero = jnp.zeros((sb, 1), F32)
        kv_hi, kv_lo = load_kv(2 * qi + 1), load_kv(2 * qi)
        top = [tile(1, hh, kv_hi, zero, zero, diag) for hh in range(2)]
        scatter(kv_hi, top)
        low = [tile(0, hh, kv_lo, zero, zero, diag) for hh in range(2)]
        low += [tile(1, hh, kv_lo, top[hh][0], top[hh][1], None) for hh in range(2)]
        scatter(kv_lo, low)
        r_runs, g_runs = [p[0] for p in low], [p[1] for p in low]

        def step(carry):
            i, _, *rg = carry
            kv = load_kv(2 * qi - 1 - i)
            parts = [tile(s, hh, kv, rg[2 * s + hh], rg[4 + 2 * s + hh], None) for s, hh in CHAINS]
            scatter(kv, parts)
            rs = [p[0] for p in parts]
            return (i + 1, _all_dead(rs), *rs, *[p[1] for p in parts])

        lax.while_loop(lambda c: jnp.logical_and(c[0] < 2 * qi, c[1] == 0), step,
                       (jnp.int32(0), _all_dead(r_runs), *r_runs, *g_runs))
        for s in range(2):
            dq_ref[pl.ds(s * sb, sb), :] = jnp.where(hmasks[0], dq_acc[2 * s], dq_acc[2 * s + 1]) * scale

    qspec = pl.BlockSpec((bq, LANES), lambda h, i: (i, h))
    dospec = pl.BlockSpec((bq, LANES), lambda h, i: (i, h + CONV_DIM // LANES))
    kspec = pl.BlockSpec((t, LANES), lambda h, i: (0, h))
    full = pl.BlockSpec((sb, sb), lambda h, i: (0, 0))
    in_specs, out_specs, out_shape, scratch, extra = _with_comm(
        comm, [qspec, kspec, kspec, qspec, dospec, full, full], [qspec, kspec, kspec],
        [jax.ShapeDtypeStruct((t, SB_DIM), F32)] * 3, [pltpu.VMEM((4, sb, LANES), F32)])
    return pl.pallas_call(
        body, name="attn_bwd", grid=(SB_DIM // LANES, t // bq),
        in_specs=in_specs, out_specs=out_specs, out_shape=out_shape, scratch_shapes=scratch,
        compiler_params=_cparams("arbitrary" if comm else "parallel", "arbitrary"),
    )(qn, kn, vb, o, dycat, tri, tri_inc, *extra)


def _ple_loss(x3, p2, tgt, gain, wpg, wppt, tt):
    t, d = x3.shape
    pdim = p2.shape[1]
    nt = t // tt

    def body(x_ref, p_ref, t_ref, g_ref, wg_ref, wp_ref,
             dx_ref, dxb_ref, dwg_ref, dwp_ref, gpart_ref, lpart_ref, accg_ref, accp_ref):
        i = pl.program_id(0)
        xv, gain_v = x_ref[...], g_ref[...]
        hb = ((xv * _rms_stats(xv)) * gain_v).astype(BF16)
        gate = jax.nn.sigmoid(_dot_nn(hb, wg_ref[...]))
        pb = p_ref[...].astype(BF16)
        pe = _dot_nt(pb, wp_ref[...])
        diff = xv + gate * pe - t_ref[...]
        lsum = jnp.sum(_fold8(diff * diff), axis=1, keepdims=True) * (0.5 / d)
        lpart_ref[...] = jnp.broadcast_to(lsum, (SUBLANES, LANES))
        dy = diff * (1.0 / d)
        dgz = ((dy * pe) * gate * (1.0 - gate)).astype(BF16)
        dpe = (dy * gate).astype(BF16)
        dx_n, grow = _rms_bwd(_dot_nt(dgz, wg_ref[...]), xv, gain_v)
        dx = dy + dx_n
        dx_ref[...] = dx
        dxb_ref[...] = dx.astype(BF16)
        gpart_ref[...] = _fold8(grow)
        sg = _dot_tn(hb, dgz)
        sp = _dot_tn(dpe, pb)

        @pl.when(i == 0)
        def _():
            accg_ref[...] = sg
            accp_ref[...] = sp

        @pl.when(i > 0)
        def _():
            accg_ref[...] += sg
            accp_ref[...] += sp

        @pl.when(i == nt - 1)
        def _():
            dwg_ref[...] = accg_ref[...].astype(BF16)
            dwp_ref[...] = accp_ref[...].astype(BF16)

    tile = lambda w: pl.BlockSpec((tt, w), lambda i: (i, 0))
    full = lambda shape: pl.BlockSpec(shape, lambda i: (0, 0))
    return pl.pallas_call(
        body, name="ple_loss", grid=(nt,),
        in_specs=[tile(d), tile(pdim), tile(d), full((1, d)),
                  pl.BlockSpec((d, d), lambda i: (0, 0), pipeline_mode=pl.Buffered(1)),
                  pl.BlockSpec((d, pdim), lambda i: (0, 0), pipeline_mode=pl.Buffered(1))],
        out_specs=[tile(d), tile(d), full((d, d)), full((d, pdim)),
                   pl.BlockSpec((SUBLANES, d), lambda i: (i, 0)), pl.BlockSpec((SUBLANES, LANES), lambda i: (i, 0))],
        out_shape=[jax.ShapeDtypeStruct((t, d), F32), jax.ShapeDtypeStruct((t, d), BF16),
                   jax.ShapeDtypeStruct((d, d), BF16), jax.ShapeDtypeStruct((d, pdim), BF16),
                   jax.ShapeDtypeStruct((nt * SUBLANES, d), F32), jax.ShapeDtypeStruct((nt * SUBLANES, LANES), F32)],
        scratch_shapes=[pltpu.VMEM((d, d), F32), pltpu.VMEM((d, pdim), F32)],
        compiler_params=_cparams("arbitrary"),
    )(x3, p2, tgt, gain, wpg, wppt)


def _pack_small(parts_gain, conv_part, qk_part):
    d = parts_gain[0].shape[1]
    ng = len(parts_gain)

    def body(*refs):
        g_refs, conv_ref, qk_ref, o_ref = refs[:ng], refs[ng], refs[ng + 1], refs[ng + 2]
        rows = [jnp.sum(r[...], axis=0, keepdims=True) for r in g_refs]
        cs = jnp.sum(conv_ref[...], axis=0, keepdims=True)
        qs = jnp.sum(qk_ref[...], axis=0, keepdims=True)
        rows.append(jnp.concatenate([cs[:, 3 * SLAB:], cs[:, :SLAB]], axis=1))
        rows.append(cs[:, SLAB:3 * SLAB])
        rows.append(qs)
        rid = lax.broadcasted_iota(jnp.int32, (2 * SUBLANES, 1), 0)
        out = jnp.zeros((2 * SUBLANES, d), F32)
        for idx, r in enumerate(rows):
            out = jnp.where(rid == idx, r, out)
        o_ref[...] = out

    return pl.pallas_call(
        body, name="pack_small", out_shape=jax.ShapeDtypeStruct((2 * SUBLANES, d), F32),
    )(*parts_gain, conv_part, qk_part)


def _sum_slots(name, slots, out_dtype=F32):
    _, r, c = slots.shape

    def body(s_ref, o_ref):
        acc = s_ref[0].astype(F32)
        for d in range(1, N_DEV):
            acc = acc + s_ref[d].astype(F32)
        o_ref[...] = acc.astype(o_ref.dtype)

    return pl.pallas_call(body, name=name, out_shape=jax.ShapeDtypeStruct((r, c), out_dtype),
                          compiler_params=pltpu.CompilerParams(vmem_limit_bytes=VMEM_LIMIT_BYTES))(slots)


def _adamw(name, w, g, m, v):
    c1 = 1.0 - ADAM_B1 ** ADAM_STEP
    c2 = 1.0 - ADAM_B2 ** ADAM_STEP

    def body(w_ref, g_ref, m_ref, v_ref, d_ref, nm_ref, nv_ref):
        gv = g_ref[...]
        nm = ADAM_B1 * m_ref[...] + (1.0 - ADAM_B1) * gv
        nv = ADAM_B2 * v_ref[...] + (1.0 - ADAM_B2) * (gv * gv)
        d_ref[...] = -ADAM_LR * ((nm / c1) / (jnp.sqrt(nv / c2) + ADAM_EPS) + ADAM_WD * w_ref[...])
        nm_ref[...] = nm
        nv_ref[...] = nv

    return pl.pallas_call(body, name=name, out_shape=[jax.ShapeDtypeStruct(w.shape, F32)] * 3,
                          compiler_params=pltpu.CompilerParams(vmem_limit_bytes=VMEM_LIMIT_BYTES))(w, g, m, v)


def _any_specs(n):
    return [pl.BlockSpec(memory_space=pl.ANY)] * n


def _all_gather(name, shards):
    n = len(shards)

    def body(*refs):
        ins, outs = refs[:n], refs[n:2 * n]
        send_sems, recv_sems, local_sems = refs[2 * n:]
        x, y, c = (lax.axis_index(a) for a in MESH_AXES)
        me, sibling = (x, y, c), (x, y, 1 - c)
        chips = [(1 - x, y), (x, 1 - y), (1 - x, 1 - y)]

        def rows(a, px, py, pc):
            r = ins[a].shape[0]
            return outs[a].at[pl.ds((4 * px + 2 * py + pc) * r, r), :]

        def copy(a, k, block, to, src=None):
            return pltpu.make_async_remote_copy(
                src_ref=rows(a, *block) if src is None else src, dst_ref=rows(a, *block),
                send_sem=send_sems.at[7 * a + k], recv_sem=recv_sems.at[7 * a + k],
                device_id=to, device_id_type=MESH)

        mine = [pltpu.make_async_copy(ins[a], rows(a, *me), local_sems.at[a]) for a in range(n)]
        for cp in mine:
            cp.start()
        first = []
        for a in range(n):
            first.append(copy(a, 0, me, sibling, src=ins[a]))
            first += [copy(a, 1 + j, me, (*chip, c), src=ins[a]) for j, chip in enumerate(chips)]
        for cp in first:
            cp.start()
        passed = []
        for j, chip in enumerate(chips):
            for a in range(n):
                copy(a, 1 + j, (*chip, c), me).wait_recv()
                fwd = copy(a, 4 + j, (*chip, c), sibling)
                fwd.start()
                passed.append(fwd)
        for a in range(n):
            copy(a, 0, sibling, me).wait_recv()
            for j, chip in enumerate(chips):
                copy(a, 4 + j, (*chip, 1 - c), me).wait_recv()
        for cp in first + passed:
            cp.wait_send()
        for cp in mine:
            cp.wait()

    return pl.pallas_call(
        body, name=name, in_specs=_any_specs(n), out_specs=_any_specs(n),
        out_shape=[jax.ShapeDtypeStruct((N_DEV * s.shape[0], s.shape[1]), s.dtype) for s in shards],
        scratch_shapes=[pltpu.SemaphoreType.DMA((7 * n,)), pltpu.SemaphoreType.DMA((7 * n,)),
                        pltpu.SemaphoreType.DMA((n,))],
    )(*shards)


def _ffn_fwd(tag, x, h, wgt, wut, wd, tt_nt, tt_nn, comm=None):
    f = wgt.shape[0]
    a, *received = _mm_nt(f"{tag}_gate_up", [h], [wgt, wut], [(0, 0), (0, 1)],
                          lambda accs: [jax.nn.silu(accs[0]) * accs[1]], [BF16], tt_nn, _pick(f, 256), comm)
    if wd is None:
        wd = received[0]
    (out,) = _mm_nn(f"{tag}_down", [(a, wd, 0)], [x], [],
                    lambda acc, rows, fulls: [rows[0] + FFN_RES * acc], [("tile", F32)], tt_nt, f)
    return out, received


def _norm_bwd_epilogue(acc, rows, fulls):
    x_in, dy = rows
    dx_n, grow = _rms_bwd(acc, x_in, fulls[0])
    dx = dy + dx_n
    return [dx, dx, _fold8(grow)]


_NORM_BWD_OUTS = [("tile", F32), ("tile", BF16), ("part", F32)]


def _ffn_bwd(tag, x_in, h, dy, dyb, gain, wgt, wut, wd, tt_nt, tt_nn, comm=None, exchange_own=False):
    f = wgt.shape[0]

    def epilogue(accs):
        g, u, da = accs[0], accs[1], FFN_RES * accs[2]
        sg = jax.nn.sigmoid(g)
        s = g * sg
        return [da * u * (sg * (1.0 + g * (1.0 - sg))), da * s, s * u]

    dg, du, a, *received = _mm_nt(f"{tag}_bwd_hidden", [h, dyb], [wgt, wut, wd], [(0, 0), (0, 1), (1, 2)],
                                  epilogue, [BF16, BF16, BF16], tt_nn, _pick(f, 256), comm)
    tt_tn = _pick(h.shape[0], 2 * tt_nt)
    dwg = _mm_tn(f"{tag}_dwg", dg, h, 1.0, f // 2, tt_tn)
    dwu = _mm_tn(f"{tag}_dwu", du, h, 1.0, f // 2, tt_tn)
    dwd = _mm_tn(f"{tag}_dwd", a, dyb, FFN_RES, f // 2, tt_tn)
    own = _Exchange([dwg, dwu, dwd], gather=False) if exchange_own else None
    dx, dxb, gpart, *own_slots = _mm_nn(f"{tag}_bwd_dx", [(dg, wgt, 0), (du, wut, 0)], [x_in, dy], [gain],
                                        _norm_bwd_epilogue, _NORM_BWD_OUTS, tt_nn, f, own)
    return dx, dxb, gpart, dwg, dwu, dwd, received, own_slots


def kernel(x, p, ffn1_norm, ffn1_w_gate, ffn1_w_up, ffn1_w_down, mix_norm, w_in, conv_w, conv_b, q_norm, k_norm, w_out, ffn2_norm, ffn2_w_gate, ffn2_w_up, ffn2_w_down, ple_norm, ple_w_gate, ple_w_proj, loss_target, m_ffn1_norm, m_ffn1_w_gate, m_ffn1_w_up, m_ffn1_w_down, m_mix_norm, m_w_in, m_conv_w, m_conv_b, m_q_norm, m_k_norm, m_w_out, m_ffn2_norm, m_ffn2_w_gate, m_ffn2_w_up, m_ffn2_w_down, m_ple_norm, m_ple_w_gate, m_ple_w_proj, v_ffn1_norm, v_ffn1_w_gate, v_ffn1_w_up, v_ffn1_w_down, v_mix_norm, v_w_in, v_conv_w, v_conv_b, v_q_norm, v_k_norm, v_w_out, v_ffn2_norm, v_ffn2_w_gate, v_ffn2_w_up, v_ffn2_w_down, v_ple_norm, v_ple_w_gate, v_ple_w_proj):
    x0, p2, tgt = x[0], p[0, 0], loss_target[0]
    t, d = x0.shape
    tt_nt = _pick(t, 1024)
    tt_nn = _pick(t, 512)
    tt_ew = _pick(t, 512)
    tt_ple = _pick(t, 512)
    tt_tn = _pick(t, 2048)
    sb = _pick(t // 2, 256)

    t_bf = lambda w: w[0].T.astype(BF16)
    n_bf = lambda w: w[0].astype(BF16)
    cw_tile = jnp.zeros((SUBLANES, LANES), F32).at[:conv_w.shape[1], :conv_w.shape[2]].set(conv_w[0])
    wg1t, wu1t = _all_gather("gather_ffn1_weights", [t_bf(ffn1_w_gate), t_bf(ffn1_w_up)])
    gather_mix = _Exchange([n_bf(ffn1_w_down), t_bf(w_in), n_bf(w_out), cw_tile], gather=True)
    gather_late = _Exchange([t_bf(ffn2_w_gate), t_bf(ffn2_w_up), n_bf(ffn2_w_down), n_bf(ple_w_gate),
                             t_bf(ple_w_proj)], gather=True)
    ncs = conv_w.shape[2]

    qg = jnp.tile(q_norm, (1, SB_DIM // HEAD_DIM))
    kg = jnp.tile(k_norm, (1, SB_DIM // HEAD_DIM))
    gi = lax.broadcasted_iota(jnp.int32, (SLAB, SLAB), 0) // HEAD_DIM
    gj = lax.broadcasted_iota(jnp.int32, (SLAB, SLAB), 1) // HEAD_DIM
    bd = (gi == gj).astype(BF16)
    tri, tri_inc = _tri_masks(sb)

    h1 = _rmsnorm("ffn1_norm", x0, ffn1_norm, tt_ew)
    x1, (wd1, wint, wout, cw_all) = _ffn_fwd("ffn1", x0, h1, wg1t, wu1t, None, tt_nt, tt_nn, gather_mix)
    cw_full = cw_all.reshape(N_DEV, SUBLANES, LANES)[:, :, :ncs].transpose(1, 0, 2).reshape(SUBLANES, N_DEV * ncs)
    h2 = _rmsnorm("mix_norm", x1, mix_norm, tt_ew)
    (proj,) = _mm_nt("in_proj", [h2], [wint], [(0, 0)], lambda accs: accs, [F32], tt_nn, SLAB)
    y_conv = _conv_fwd(proj, cw_full, conv_b, tt_ew)
    qn, kn, vb = _qknorm_fwd(proj, qg, kg, bd, tt_ew)
    o, ob, wg2t, wu2t, wd2, wpg, wppt = _attn_fwd(qn, kn, vb, tri, sb, gather_late)
    (x2,) = _mm_nn("out_proj", [(y_conv, wout, 0), (ob, wout, 1)], [x1], [],
                   lambda acc, rows, fulls: [rows[0] + acc], [("tile", F32)], tt_nn, SLAB)
    h3 = _rmsnorm("ffn2_norm", x2, ffn2_norm, tt_ew)
    x3, _ = _ffn_fwd("ffn2", x2, h3, wg2t, wu2t, wd2, tt_nt, tt_nn)

    dx3, dx3b, dwpg, dwppt, gp_ple, lpart = _ple_loss(x3, p2, tgt, ple_norm, wpg, wppt, tt_ple)
    loss = lax.psum(jnp.sum(lpart[:, 0]), MESH_AXES)
    dx2, dx2b, gp_ffn2, dwg2, dwu2, dwd2, _, _ = _ffn_bwd("ffn2", x2, h3, dx3, dx3b, ffn2_norm, wg2t, wu2t, wd2,
                                                           tt_nt, tt_nn)
    (dycat,) = _mm_nt("out_proj_bwd", [dx2b], [wout], [(0, 0)], lambda accs: accs, [F32], tt_nn, SLAB)
    dwout = jnp.concatenate([_mm_tn("dwout_conv", y_conv, dx2b, 1.0, SLAB, tt_tn),
                             _mm_tn("dwout_attn", ob, dx2b, 1.0, SLAB, tt_tn)], axis=0)
    dqn, dkn, dv, *slots_late = _attn_bwd(qn, kn, vb, o, dycat, tri, tri_inc, sb,
                                          _Exchange([dwg2, dwu2, dwd2, dwpg, dwppt], gather=False))
    dq, dk, dvb, qk_part = _qknorm_bwd(proj, dqn, dkn, dv, qg, kg, bd, tt_ew)
    db, dc, du, conv_part = _conv_bwd(proj, dycat, cw_full, conv_b, tt_ew)
    dproj = [db, dc, du, dq, dk, dvb]
    dx1, dx1b, gp_mix = _mm_nn("in_proj_bwd", [(dp, wint, s) for s, dp in enumerate(dproj)], [x1, dx2], [mix_norm],
                               _norm_bwd_epilogue, _NORM_BWD_OUTS, tt_nn, SLAB)
    dwin = jnp.concatenate([_mm_tn(f"dwin_{s}", dp, h2, 1.0, SLAB, tt_tn) for s, dp in enumerate(dproj)], axis=0)
    dx0, _, gp_ffn1, _, _, _, slots_mix, slots_ffn1 = _ffn_bwd(
        "ffn1", x0, h1, dx1, dx1b, ffn1_norm, wg1t, wu1t, wd1, tt_nt, tt_nn,
        _Exchange([dwin, dwout], gather=False), exchange_own=True)

    slots = [*slots_ffn1, *slots_mix, *slots_late]
    sums = [_sum_slots(f"sum_grads_{i}", s.reshape(N_DEV, s.shape[0] // N_DEV, s.shape[1]))
            for i, s in enumerate(slots)]
    g_wg1, g_wu1, g_wd1, g_win, g_wout, g_wg2, g_wu2, g_wd2, g_wpg, g_wpp = sums
    small = _pack_small([gp_ffn1, gp_mix, gp_ffn2, gp_ple], conv_part, qk_part)
    (small_all,) = _all_gather("gather_small_grads", [small])
    sm = _sum_slots("sum_small_grads", small_all.reshape(N_DEV, 2 * SUBLANES, d))
    fold = lambda r: r.reshape(SB_DIM // HEAD_DIM, HEAD_DIM).sum(axis=0)[None]
    me_idx = 4 * lax.axis_index("x") + 2 * lax.axis_index("y") + lax.axis_index("c")
    cw_grad = jnp.stack([sm[4, SLAB:], sm[5, :SLAB], sm[5, SLAB:]])
    grads = {
        "ffn1_norm": sm[0:1], "ffn1_w_gate": g_wg1.T, "ffn1_w_up": g_wu1.T, "ffn1_w_down": g_wd1,
        "mix_norm": sm[1:2], "w_in": g_win.T, "conv_w": lax.dynamic_slice_in_dim(cw_grad, me_idx * ncs, ncs, axis=1),
        "conv_b": sm[4:5, :SLAB], "q_norm": fold(sm[6, :SLAB]), "k_norm": fold(sm[6, SLAB:]),
        "w_out": g_wout, "ffn2_norm": sm[2:3], "ffn2_w_gate": g_wg2.T, "ffn2_w_up": g_wu2.T, "ffn2_w_down": g_wd2,
        "ple_norm": sm[3:4], "ple_w_gate": g_wpg, "ple_w_proj": g_wpp.T,
    }

    weights = dict(ffn1_norm=ffn1_norm, ffn1_w_gate=ffn1_w_gate, ffn1_w_up=ffn1_w_up, ffn1_w_down=ffn1_w_down,
                   mix_norm=mix_norm, w_in=w_in, conv_w=conv_w, conv_b=conv_b, q_norm=q_norm, k_norm=k_norm,
                   w_out=w_out, ffn2_norm=ffn2_norm, ffn2_w_gate=ffn2_w_gate, ffn2_w_up=ffn2_w_up,
                   ffn2_w_down=ffn2_w_down, ple_norm=ple_norm, ple_w_gate=ple_w_gate, ple_w_proj=ple_w_proj)
    m_in = dict(ffn1_norm=m_ffn1_norm, ffn1_w_gate=m_ffn1_w_gate, ffn1_w_up=m_ffn1_w_up, ffn1_w_down=m_ffn1_w_down,
                mix_norm=m_mix_norm, w_in=m_w_in, conv_w=m_conv_w, conv_b=m_conv_b, q_norm=m_q_norm,
                k_norm=m_k_norm, w_out=m_w_out, ffn2_norm=m_ffn2_norm, ffn2_w_gate=m_ffn2_w_gate,
                ffn2_w_up=m_ffn2_w_up, ffn2_w_down=m_ffn2_w_down, ple_norm=m_ple_norm, ple_w_gate=m_ple_w_gate,
                ple_w_proj=m_ple_w_proj)
    v_in = dict(ffn1_norm=v_ffn1_norm, ffn1_w_gate=v_ffn1_w_gate, ffn1_w_up=v_ffn1_w_up, ffn1_w_down=v_ffn1_w_down,
                mix_norm=v_mix_norm, w_in=v_w_in, conv_w=v_conv_w, conv_b=v_conv_b, q_norm=v_q_norm,
                k_norm=v_k_norm, w_out=v_w_out, ffn2_norm=v_ffn2_norm, ffn2_w_gate=v_ffn2_w_gate,
                ffn2_w_up=v_ffn2_w_up, ffn2_w_down=v_ffn2_w_down, ple_norm=v_ple_norm, ple_w_gate=v_ple_w_gate,
                ple_w_proj=v_ple_w_proj)
    g_out, d_out, m_out, v_out = [], [], [], []
    for name, w in weights.items():
        w2 = w.reshape(w.shape[-2:])
        g2 = grads[name].reshape(w2.shape)
        dlt, nm, nv = _adamw(f"adamw_{name}", w2, g2, m_in[name].reshape(w2.shape), v_in[name].reshape(w2.shape))
        g_out.append(g2.reshape(w.shape))
        d_out.append(dlt.reshape(w.shape))
        m_out.append(nm.reshape(w.shape))
        v_out.append(nv.reshape(w.shape))
    return (loss, dx0[None], *g_out, *d_out, *m_out, *v_out)
```

```python
import jax
import jax.numpy as jnp
from jax import lax
from jax.experimental import pallas as pl
from jax.experimental.pallas import tpu as pltpu

F32 = jnp.float32
BF16 = jnp.bfloat16

EPS = 1e-6
FFN_RES = 0.5
HEAD_DIM = 64
CONV_DIM = 512
SB_DIM = 512
SLAB = 512
N_DEV = 8
MESH_AXES = ("x", "y", "c")
MESH = pl.DeviceIdType.MESH

ADAM_LR = 0.001
ADAM_B1 = 0.9
ADAM_B2 = 0.999
ADAM_EPS = 1e-08
ADAM_WD = 0.01
ADAM_STEP = 10

VMEM_LIMIT_BYTES = 56 * 1024 * 1024
SUBLANES = 8
LANES = 128


def _cparams(*semantics):
    return pltpu.CompilerParams(dimension_semantics=semantics, vmem_limit_bytes=VMEM_LIMIT_BYTES)


def _dot_nn(a, b):
    return jnp.dot(a, b, preferred_element_type=F32)


def _dot_nt(a, b):
    return lax.dot_general(a, b, (((1,), (1,)), ((), ())), preferred_element_type=F32)


def _dot_tn(a, b):
    return lax.dot_general(a, b, (((0,), (0,)), ((), ())), preferred_element_type=F32)


def _fold8(v):
    rows, cols = v.shape
    return jnp.sum(v.reshape(rows // SUBLANES, SUBLANES, cols), axis=0)


def _split2(v):
    hi = v.astype(BF16)
    lo = (v - hi.astype(F32)).astype(BF16)
    return hi, lo


def _rms_stats(x):
    return lax.rsqrt(jnp.mean(x * x, axis=-1, keepdims=True) + EPS)


def _rms_bwd(dh, x, gain):
    r = _rms_stats(x)
    u = dh * gain
    dx = r * u - x * (r * r * r) * jnp.mean(u * x, axis=-1, keepdims=True)
    return dx, dh * x * r


def _pick(n, pref):
    return pref if n % pref == 0 else n


def _rmsnorm(name, x, gain, tt):
    t, d = x.shape

    def body(x_ref, g_ref, o_ref):
        xv = x_ref[...]
        o_ref[...] = ((xv * _rms_stats(xv)) * g_ref[...]).astype(BF16)

    return pl.pallas_call(
        body, name=name, grid=(t // tt,),
        in_specs=[pl.BlockSpec((tt, d), lambda i: (i, 0)), pl.BlockSpec((1, d), lambda i: (0, 0))],
        out_specs=pl.BlockSpec((tt, d), lambda i: (i, 0)),
        out_shape=jax.ShapeDtypeStruct((t, d), BF16),
        compiler_params=_cparams("parallel"),
    )(x, gain)


class _Exchange:
    FLIPS = [(fx, fy, fc) for fx in (0, 1) for fy in (0, 1) for fc in (0, 1)][1:]

    def __init__(self, arrays, gather):
        self.arrays = list(arrays)
        self.gather = gather
        self.n = len(self.arrays)
        self.rows = [a.shape[0] if gather else a.shape[0] // N_DEV for a in self.arrays]
        self.out_shape = [jax.ShapeDtypeStruct((N_DEV * r, a.shape[1]), a.dtype)
                          for r, a in zip(self.rows, self.arrays)]
        self.scratch = [pltpu.SemaphoreType.DMA((7 * self.n,)), pltpu.SemaphoreType.DMA((7 * self.n,)),
                        pltpu.SemaphoreType.DMA((self.n,))]

    def _copies(self, ins, outs, sems, arrivals):
        send_sems, recv_sems, local_sems = sems
        x, y, c = (lax.axis_index(a) for a in MESH_AXES)
        me_idx = 4 * x + 2 * y + c
        local, send, recv = [], [], []
        for a in range(self.n):
            r = self.rows[a]

            def blk(ref, idx, r=r):
                return ref.at[pl.ds(idx * r, r), :]

            def src(idx, a=a, blk=blk):
                return ins[a] if self.gather else blk(ins[a], idx)

            local.append(pltpu.make_async_copy(src(me_idx), blk(outs[a], me_idx), local_sems.at[a]))
            for k, flip in enumerate(self.FLIPS):
                px, py, pc = (1 - v if f else v for v, f in zip((x, y, c), flip))
                p_idx = 4 * px + 2 * py + pc
                for dst_idx, group in ((me_idx, send), (p_idx, recv))[:2 if arrivals else 1]:
                    group.append(pltpu.make_async_remote_copy(
                        src_ref=src(p_idx), dst_ref=blk(outs[a], dst_idx),
                        send_sem=send_sems.at[7 * a + k], recv_sem=recv_sems.at[7 * a + k],
                        device_id=(px, py, pc), device_id_type=MESH))
        return local, send, recv

    def start(self, ins, outs, sems):
        local, send, _ = self._copies(ins, outs, sems, arrivals=False)
        for cp in local + send:
            cp.start()

    def wait(self, ins, outs, sems):
        local, send, recv = self._copies(ins, outs, sems, arrivals=True)
        for s, r in zip(send, recv):
            r.wait_recv()
            s.wait_send()
        for cp in local:
            cp.wait()

    def attach(self, refs, first, last):
        pl.when(first)(lambda: self.start(*refs))
        pl.when(last)(lambda: self.wait(*refs))


def _split_refs(refs, n_in, n_out, n_scratch, comm):
    nc = comm.n if comm else 0
    ins, rest = refs[:n_in], refs[n_in:]
    c_in, rest = rest[:nc], rest[nc:]
    outs, rest = rest[:n_out], rest[n_out:]
    c_out, rest = rest[:nc], rest[nc:]
    scratch, c_sems = rest[:n_scratch], rest[n_scratch:]
    return ins, outs, scratch, ((c_in, c_out, c_sems) if comm else None)


def _with_comm(comm, in_specs, out_specs, out_shape, scratch):
    if comm is None:
        return in_specs, out_specs, out_shape, scratch, []
    return (in_specs + _any_specs(comm.n), out_specs + _any_specs(comm.n), out_shape + comm.out_shape,
            scratch + comm.scratch, comm.arrays)


def _mm_nt(name, a_list, w_list, pairs, epilogue, out_dtypes, tt, tn, comm=None, tiles=()):
    t = a_list[0].shape[0]
    n = w_list[0].shape[0]
    na, nw, ntile = len(a_list), len(w_list), len(tiles)
    ni, nj = t // tt, n // tn

    def body(*refs):
        ins, o_refs, _, c_refs = _split_refs(refs, na + nw + ntile, len(out_dtypes), 0, comm)
        a_refs, w_refs, t_refs = ins[:na], ins[na:na + nw], ins[na + nw:]
        if comm:
            i = pl.program_id(0)
            comm.attach(c_refs, i == 0, i == ni - 1)
        a_vals = [a_ref[...] for a_ref in a_refs]
        for j in range(nj):
            cols = pl.ds(j * tn, tn)
            accs = [_dot_nt(a_vals[ai], w_refs[wi][cols, :]) for ai, wi in pairs]
            for o_ref, o in zip(o_refs, epilogue(accs, [t_ref[:, cols] for t_ref in t_refs])):
                o_ref[:, cols] = o.astype(o_ref.dtype)

    in_specs = ([pl.BlockSpec((tt, a.shape[1]), lambda i: (i, 0)) for a in a_list]
                + [pl.BlockSpec(w.shape, lambda i: (0, 0), pipeline_mode=pl.Buffered(1)) for w in w_list]
                + [pl.BlockSpec((tt, n), lambda i: (i, 0)) for _ in tiles])
    in_specs, out_specs, out_shape, scratch, extra = _with_comm(
        comm, in_specs, [pl.BlockSpec((tt, n), lambda i: (i, 0)) for _ in out_dtypes],
        [jax.ShapeDtypeStruct((t, n), dt) for dt in out_dtypes], [])
    return pl.pallas_call(
        body, name=name, grid=(ni,), in_specs=in_specs, out_specs=out_specs, out_shape=out_shape,
        scratch_shapes=scratch,
        compiler_params=_cparams("arbitrary" if comm else "parallel"),
    )(*a_list, *w_list, *tiles, *extra)


def _mm_nn(name, pairs, rows, fulls, epilogue, out_kinds, tt, tk, comm=None):
    t, k_total = pairs[0][0].shape
    n = pairs[0][1].shape[1]
    nk = k_total // tk
    nt = t // tt
    npair, nrow, nfull = len(pairs), len(rows), len(fulls)

    def body(*refs):
        ins, o_refs, scratch, c_refs = _split_refs(refs, 2 * npair + nrow + nfull, len(out_kinds), min(nk - 1, 1), comm)
        a_refs, w_refs = ins[:npair], ins[npair:2 * npair]
        r_refs, f_refs = ins[2 * npair:2 * npair + nrow], ins[2 * npair + nrow:]
        i, k = pl.program_id(0), pl.program_id(1)
        if comm:
            comm.attach(c_refs, jnp.logical_and(i == 0, k == 0), jnp.logical_and(i == nt - 1, k == nk - 1))
        s = _dot_nn(a_refs[0][...], w_refs[0][...])
        for a_ref, w_ref in zip(a_refs[1:], w_refs[1:]):
            s = s + _dot_nn(a_ref[...], w_ref[...])

        def finish(acc):
            outs = epilogue(acc, [r[...] for r in r_refs], [f[...] for f in f_refs])
            for o_ref, o in zip(o_refs, outs):
                o_ref[...] = o.astype(o_ref.dtype)

        if nk == 1:
            finish(s)
        else:
            acc_ref = scratch[0]

            @pl.when(k == 0)
            def _():
                acc_ref[...] = s

            @pl.when(k > 0)
            def _():
                acc_ref[...] += s

            @pl.when(k == nk - 1)
            def _():
                finish(acc_ref[...])

    once = dict(pipeline_mode=pl.Buffered(1)) if nk == 1 else {}
    in_specs = ([pl.BlockSpec((tt, tk), lambda i, k: (i, k)) for _ in pairs]
                + [pl.BlockSpec((tk, n), (lambda i, k, off=off: (k + off, 0)), **once) for _, _, off in pairs]
                + [pl.BlockSpec((tt, n), lambda i, k: (i, 0)) for _ in rows]
                + [pl.BlockSpec((1, n), lambda i, k: (0, 0)) for _ in fulls])
    out_specs, out_shape = [], []
    for kind, dt in out_kinds:
        if kind == "tile":
            out_specs.append(pl.BlockSpec((tt, n), lambda i, k: (i, 0)))
            out_shape.append(jax.ShapeDtypeStruct((t, n), dt))
        else:
            out_specs.append(pl.BlockSpec((SUBLANES, n), lambda i, k: (i, 0)))
            out_shape.append(jax.ShapeDtypeStruct((nt * SUBLANES, n), dt))
    in_specs, out_specs, out_shape, scratch, extra = _with_comm(
        comm, in_specs, out_specs, out_shape, [] if nk == 1 else [pltpu.VMEM((tt, n), F32)])
    return pl.pallas_call(
        body, name=name, grid=(nt, nk), in_specs=in_specs, out_specs=out_specs, out_shape=out_shape,
        scratch_shapes=scratch,
        compiler_params=_cparams("arbitrary" if comm else "parallel", "arbitrary"),
    )(*[a for a, _, _ in pairs], *[w for _, w, _ in pairs], *rows, *fulls, *extra)


def _mm_tn(name, a, b, scale, tm, tt, comm=None):
    t, m = a.shape
    n = b.shape[1]
    nt = t // tt
    nm = m // tm

    def body(*refs):
        (a_ref, b_ref), (o_ref,), (acc_ref,), c_refs = _split_refs(refs, 2, 1, 1, comm)
        k = pl.program_id(1)
        if comm:
            i = pl.program_id(0)
            comm.attach(c_refs, jnp.logical_and(i == 0, k == 0), jnp.logical_and(i == nm - 1, k == nt - 1))
        s = _dot_tn(a_ref[...], b_ref[...])

        @pl.when(k == 0)
        def _():
            acc_ref[...] = s

        @pl.when(k > 0)
        def _():
            acc_ref[...] += s

        @pl.when(k == nt - 1)
        def _():
            o_ref[...] = (acc_ref[...] * scale).astype(o_ref.dtype)

    in_specs, out_specs, out_shape, scratch, extra = _with_comm(
        comm, [pl.BlockSpec((tt, tm), lambda i, k: (k, i)), pl.BlockSpec((tt, n), lambda i, k: (k, 0))],
        [pl.BlockSpec((tm, n), lambda i, k: (i, 0))], [jax.ShapeDtypeStruct((m, n), BF16)],
        [pltpu.VMEM((tm, n), F32)])
    out = pl.pallas_call(
        body, name=name, grid=(nm, nt), in_specs=in_specs, out_specs=out_specs, out_shape=out_shape,
        scratch_shapes=scratch,
        compiler_params=_cparams("arbitrary" if comm else "parallel", "arbitrary"),
    )(a, b, *extra)
    return out if comm else out[0]


def _group_sum(v, bd):
    hi = v.astype(BF16)
    r1 = v - hi.astype(F32)
    mid = r1.astype(BF16)
    lo = (r1 - mid.astype(F32)).astype(BF16)
    return _dot_nn(hi, bd) + _dot_nn(mid, bd) + _dot_nn(lo, bd)


def _qknorm_fwd(proj, qg, kg, bd, tt):
    t = proj.shape[0]

    def body(q_ref, k_ref, v_ref, qg_ref, kg_ref, bd_ref, qn_ref, kn_ref, vb_ref):
        bdv = bd_ref[...]
        for x_ref, g_ref, o_ref in ((q_ref, qg_ref, qn_ref), (k_ref, kg_ref, kn_ref)):
            xv = x_ref[...]
            r = lax.rsqrt(_group_sum(xv * xv, bdv) * (1.0 / HEAD_DIM) + EPS)
            o_ref[...] = ((xv * r) * g_ref[...]).astype(BF16)
        vb_ref[...] = v_ref[...].astype(BF16)

    slab = lambda s: pl.BlockSpec((tt, SLAB), lambda i, s=s: (i, s))
    full = lambda shape: pl.BlockSpec(shape, lambda i: (0, 0))
    out = pl.BlockSpec((tt, SLAB), lambda i: (i, 0))
    return pl.pallas_call(
        body, name="qknorm_fwd", grid=(t // tt,),
        in_specs=[slab(3), slab(4), slab(5), full((1, SLAB)), full((1, SLAB)), full((SLAB, SLAB))],
        out_specs=[out, out, out],
        out_shape=[jax.ShapeDtypeStruct((t, SLAB), BF16)] * 3,
        compiler_params=_cparams("parallel"),
    )(proj, proj, proj, qg, kg, bd)


def _qknorm_bwd(proj, dqn, dkn, dv, qg, kg, bd, tt):
    t = proj.shape[0]

    def body(q_ref, k_ref, dqn_ref, dkn_ref, dv_ref, qg_ref, kg_ref, bd_ref, dq_ref, dk_ref, dvb_ref, part_ref):
        bdv = bd_ref[...]
        parts = []
        for x_ref, d_ref, g_ref, o_ref in ((q_ref, dqn_ref, qg_ref, dq_ref), (k_ref, dkn_ref, kg_ref, dk_ref)):
            xv, dn = x_ref[...], d_ref[...]
            r = lax.rsqrt(_group_sum(xv * xv, bdv) * (1.0 / HEAD_DIM) + EPS)
            u = dn * g_ref[...]
            dx = r * u - xv * (r * r * r) * (_group_sum(u * xv, bdv) * (1.0 / HEAD_DIM))
            o_ref[...] = dx.astype(BF16)
            parts.append(_fold8(dn * xv * r))
        dvb_ref[...] = dv_ref[...].astype(BF16)
        part_ref[...] = jnp.concatenate(parts, axis=1)

    slab = lambda s: pl.BlockSpec((tt, SLAB), lambda i, s=s: (i, s))
    tile = pl.BlockSpec((tt, SLAB), lambda i: (i, 0))
    full = lambda shape: pl.BlockSpec(shape, lambda i: (0, 0))
    return pl.pallas_call(
        body, name="qknorm_bwd", grid=(t // tt,),
        in_specs=[slab(3), slab(4), tile, tile, tile, full((1, SLAB)), full((1, SLAB)), full((SLAB, SLAB))],
        out_specs=[tile, tile, tile, pl.BlockSpec((SUBLANES, 2 * SLAB), lambda i: (i, 0))],
        out_shape=[jax.ShapeDtypeStruct((t, SLAB), BF16)] * 3
        + [jax.ShapeDtypeStruct((t // tt * SUBLANES, 2 * SLAB), F32)],
        compiler_params=_cparams("parallel"),
    )(proj, proj, dqn, dkn, dv, qg, kg, bd)


def _conv_taps(z, z_prev, row):
    zm1 = jnp.where(row == 0, z_prev[7:8], pltpu.roll(z, 1, 0))
    zm2 = jnp.where(row == 0, z_prev[6:7], jnp.where(row == 1, z_prev[7:8], pltpu.roll(z, 2, 0)))
    return zm1, zm2


def _conv_fwd(proj, cw, cb, tt):
    t = proj.shape[0]
    tb = tt // SUBLANES

    def body(b_ref, c_ref, u_ref, cp_ref, up_ref, cw_ref, cb_ref, o_ref):
        i = pl.program_id(0)
        z = c_ref[...] * u_ref[...]
        z_prev = jnp.where(i > 0, cp_ref[...] * up_ref[...], 0.0)
        row = lax.broadcasted_iota(jnp.int32, (tt, 1), 0)
        zm1, zm2 = _conv_taps(z, z_prev, row)
        y = cw_ref[0:1] * zm2 + cw_ref[1:2] * zm1 + cw_ref[2:3] * z + cb_ref[...]
        o_ref[...] = (b_ref[...] * y).astype(BF16)

    slab = lambda s: pl.BlockSpec((tt, SLAB), lambda i, s=s: (i, s))
    prev = lambda s: pl.BlockSpec((SUBLANES, SLAB), lambda i, s=s: (jnp.maximum(i * tb - 1, 0), s))
    return pl.pallas_call(
        body, name="conv_fwd", grid=(t // tt,),
        in_specs=[slab(0), slab(1), slab(2), prev(1), prev(2),
                  pl.BlockSpec((SUBLANES, SLAB), lambda i: (0, 0)), pl.BlockSpec((1, SLAB), lambda i: (0, 0))],
        out_specs=pl.BlockSpec((tt, SLAB), lambda i: (i, 0)),
        out_shape=jax.ShapeDtypeStruct((t, SLAB), BF16),
        compiler_params=_cparams("parallel"),
    )(proj, proj, proj, proj, proj, cw, cb)


def _conv_bwd(proj, dycat, cw, cb, tt):
    t = proj.shape[0]
    tb = tt // SUBLANES
    nblk = t // SUBLANES

    def body(b_ref, c_ref, u_ref, cp_ref, up_ref, bn_ref, dy_ref, dyn_ref, cw_ref, cb_ref,
             db_ref, dc_ref, du_ref, part_ref):
        i = pl.program_id(0)
        c, u, b, dyc = c_ref[...], u_ref[...], b_ref[...], dy_ref[...]
        z = c * u
        z_prev = jnp.where(i > 0, cp_ref[...] * up_ref[...], 0.0)
        row = lax.broadcasted_iota(jnp.int32, (tt, 1), 0)
        zm1, zm2 = _conv_taps(z, z_prev, row)
        w0, w1, w2 = cw_ref[0:1], cw_ref[1:2], cw_ref[2:3]
        y = w0 * zm2 + w1 * zm1 + w2 * z + cb_ref[...]
        db_ref[...] = (dyc * y).astype(BF16)
        g = dyc * b
        g_next = jnp.where(i < pl.num_programs(0) - 1, dyn_ref[...] * bn_ref[...], 0.0)
        gp1 = jnp.where(row == tt - 1, g_next[0:1], pltpu.roll(g, tt - 1, 0))
        gp2 = jnp.where(row == tt - 2, g_next[0:1], jnp.where(row == tt - 1, g_next[1:2], pltpu.roll(g, tt - 2, 0)))
        dz = w2 * g + w1 * gp1 + w0 * gp2
        dc_ref[...] = (dz * u).astype(BF16)
        du_ref[...] = (dz * c).astype(BF16)
        part_ref[...] = jnp.concatenate([_fold8(g * zm2), _fold8(g * zm1), _fold8(g * z), _fold8(g)], axis=1)

    slab = lambda s: pl.BlockSpec((tt, SLAB), lambda i, s=s: (i, s))
    prev = lambda s: pl.BlockSpec((SUBLANES, SLAB), lambda i, s=s: (jnp.maximum(i * tb - 1, 0), s))
    nxt = lambda s: pl.BlockSpec((SUBLANES, SLAB), lambda i, s=s: (jnp.minimum((i + 1) * tb, nblk - 1), s))
    tile = pl.BlockSpec((tt, SLAB), lambda i: (i, 0))
    return pl.pallas_call(
        body, name="conv_bwd", grid=(t // tt,),
        in_specs=[slab(0), slab(1), slab(2), prev(1), prev(2), nxt(0), slab(0), nxt(0),
                  pl.BlockSpec((SUBLANES, SLAB), lambda i: (0, 0)), pl.BlockSpec((1, SLAB), lambda i: (0, 0))],
        out_specs=[tile, tile, tile, pl.BlockSpec((SUBLANES, 4 * SLAB), lambda i: (i, 0))],
        out_shape=[jax.ShapeDtypeStruct((t, SLAB), BF16)] * 3
        + [jax.ShapeDtypeStruct((t // tt * SUBLANES, 4 * SLAB), F32)],
        compiler_params=_cparams("parallel"),
    )(proj, proj, proj, proj, proj, proj, dycat, dycat, cw, cb)


def _tri_masks(n):
    r = lax.broadcasted_iota(jnp.int32, (n, n), 0)
    c = lax.broadcasted_iota(jnp.int32, (n, n), 1)
    return (r > c).astype(BF16), (r >= c).astype(BF16)


def _sb_scores(qh, k, r_run, tri, causal):
    z = _dot_nt(qh, k)
    softplus = jnp.maximum(z, 0.0) + jnp.log(1.0 + jnp.exp(-jnp.abs(z)))
    lk = -softplus
    if causal is not None:
        lk = jnp.where(causal, lk, 0.0)
    hi, lo = _split2(lk)
    later = _dot_nn(hi, tri) + _dot_nn(lo, tri) + r_run
    ls = z + lk
    arg = ls + later
    if causal is not None:
        arg = jnp.where(causal, arg, -1e30)
    return lk, ls, jnp.exp(arg), later[:, 0:1] + lk[:, 0:1]


SB_DEAD_LOG = -111.0
CHAINS = ((0, 0), (0, 1), (1, 0), (1, 1))


def _all_dead(r_runs):
    m = r_runs[0]
    for r in r_runs[1:]:
        m = jnp.maximum(m, r)
    return (jnp.max(m) < SB_DEAD_LOG).astype(jnp.int32)


def _attn_fwd(qn, kn, vb, tri, sb, comm=None):
    t = qn.shape[0]
    bq = 2 * sb
    scale = HEAD_DIM ** -0.5

    def body(*refs):
        (q_ref, k_ref, v_ref, tri_ref), (o_ref, ob_ref), (acc_ref,), c_refs = _split_refs(refs, 4, 2, 1, comm)
        qi = pl.program_id(1)
        if comm:
            hp = pl.program_id(0)
            comm.attach(c_refs, jnp.logical_and(hp == 0, qi == 0),
                        jnp.logical_and(hp == pl.num_programs(0) - 1, qi == pl.num_programs(1) - 1))
        lane = lax.broadcasted_iota(jnp.int32, (1, LANES), 1)
        hmasks = (lane < HEAD_DIM, lane >= HEAD_DIM)
        diag = lax.broadcasted_iota(jnp.int32, (sb, sb), 1) < lax.broadcasted_iota(jnp.int32, (sb, sb), 0)
        triv = tri_ref[...]
        qs = [[jnp.where(hm, q_ref[pl.ds(s * sb, sb), :], 0) * scale for hm in hmasks] for s in range(2)]
        acc_ref[...] = jnp.zeros_like(acc_ref)

        def load_kv(kb):
            ks = pl.multiple_of(kb * sb, sb)
            vraw = v_ref[pl.ds(ks, sb), :]
            return k_ref[pl.ds(ks, sb), :], [jnp.where(hm, vraw, 0) for hm in hmasks]

        def tile(s, hh, kv, r_run, causal):
            _, _, a, r_new = _sb_scores(qs[s][hh], kv[0], r_run, triv, causal)
            acc_ref[2 * s + hh] += _dot_nn(a.astype(BF16), kv[1][hh])
            return r_new

        zero = jnp.zeros((sb, 1), F32)
        kv_hi, kv_lo = load_kv(2 * qi + 1), load_kv(2 * qi)
        r_runs = [None] * 4
        for hh in range(2):
            r_runs[hh] = tile(0, hh, kv_lo, zero, diag)
            r_runs[2 + hh] = tile(1, hh, kv_lo, tile(1, hh, kv_hi, zero, diag), None)

        def step(carry):
            i, _, *rs = carry
            kv = load_kv(2 * qi - 1 - i)
            rs = [tile(s, hh, kv, rs[2 * s + hh], None) for s, hh in CHAINS]
            return (i + 1, _all_dead(rs), *rs)

        lax.while_loop(lambda c: jnp.logical_and(c[0] < 2 * qi, c[1] == 0), step,
                       (jnp.int32(0), _all_dead(r_runs), *r_runs))
        for s in range(2):
            out = acc_ref[2 * s] + acc_ref[2 * s + 1]
            o_ref[pl.ds(s * sb, sb), :] = out
            ob_ref[pl.ds(s * sb, sb), :] = out.astype(BF16)

    qspec = pl.BlockSpec((bq, LANES), lambda h, i: (i, h))
    kspec = pl.BlockSpec((t, LANES), lambda h, i: (0, h))
    in_specs, out_specs, out_shape, scratch, extra = _with_comm(
        comm, [qspec, kspec, kspec, pl.BlockSpec((sb, sb), lambda h, i: (0, 0))], [qspec, qspec],
        [jax.ShapeDtypeStruct((t, SB_DIM), F32), jax.ShapeDtypeStruct((t, SB_DIM), BF16)],
        [pltpu.VMEM((4, sb, LANES), F32)])
    return pl.pallas_call(
        body, name="attn_fwd", grid=(SB_DIM // LANES, t // bq),
        in_specs=in_specs, out_specs=out_specs, out_shape=out_shape, scratch_shapes=scratch,
        compiler_params=_cparams("arbitrary" if comm else "parallel", "arbitrary"),
    )(qn, kn, vb, tri, *extra)


def _attn_bwd(qn, kn, vb, o, dycat, tri, tri_inc, sb, comm=None):
    t = qn.shape[0]
    bq = 2 * sb
    scale = HEAD_DIM ** -0.5

    def body(*refs):
        ins, (dq_ref, dk_ref, dv_ref), (dq_acc,), c_refs = _split_refs(refs, 7, 3, 1, comm)
        q_ref, k_ref, v_ref, o_ref, do_ref, tri_ref, tinc_ref = ins
        qi = pl.program_id(1)
        if comm:
            hp = pl.program_id(0)
            comm.attach(c_refs, jnp.logical_and(hp == 0, qi == 0),
                        jnp.logical_and(hp == pl.num_programs(0) - 1, qi == pl.num_programs(1) - 1))

        @pl.when(qi == 0)
        def _():
            dk_ref[...] = jnp.zeros_like(dk_ref)
            dv_ref[...] = jnp.zeros_like(dv_ref)

        lane = lax.broadcasted_iota(jnp.int32, (1, LANES), 1)
        hmasks = (lane < HEAD_DIM, lane >= HEAD_DIM)
        diag = lax.broadcasted_iota(jnp.int32, (sb, sb), 1) < lax.broadcasted_iota(jnp.int32, (sb, sb), 0)
        triv, tincv = tri_ref[...], tinc_ref[...]
        qs, dobs, d_rows = [], [], []
        for s in range(2):
            rows = pl.ds(s * sb, sb)
            qs.append([jnp.where(hm, q_ref[rows, :], 0) * scale for hm in hmasks])
            dobs.append([jnp.where(hm, do_ref[rows, :], 0.0).astype(BF16) for hm in hmasks])
            d_rows.append([jnp.sum(d.astype(F32) * o_ref[rows, :], axis=1, keepdims=True) for d in dobs[s]])
        dq_acc[...] = jnp.zeros_like(dq_acc)

        def load_kv(kb):
            ks = pl.multiple_of(kb * sb, sb)
            return k_ref[pl.ds(ks, sb), :], v_ref[pl.ds(ks, sb), :], ks

        def tile(s, hh, kv, r_run, g_run, causal):
            k, v, _ = kv
            qh, dob = qs[s][hh], dobs[s][hh]
            _, ls, a, r_new = _sb_scores(qh, k, r_run, triv, causal)
            ab = a.astype(BF16)
            e = _dot_nt(dob, v) * ab.astype(F32)
            hi, lo = _split2(e)
            e_from = _dot_nn(hi, tincv) + _dot_nn(lo, tincv) + g_run
            beta = jnp.exp(ls)
            dz = e - beta * (e + (d_rows[s][hh] - e_from))
            if causal is not None:
                dz = jnp.where(causal, dz, 0.0)
            dzb = dz.astype(BF16)
            dq_acc[2 * s + hh] += _dot_nn(dzb, k)
            return r_new, e_from[:, 0:1], _dot_tn(dzb, qh), _dot_tn(ab, dob)

        def scatter(kv, parts):
            rows = pl.ds(kv[2], sb)
            dk_ref[rows, :] += sum(p[2] for p in parts[1:]) + parts[0][2]
            dv_ref[rows, :] += sum(p[3] for p in parts[1:]) + parts[0][3]

        zero = jnp.zeros((sb, 1), F32)
        kv_hi, kv_lo = load_kv(2 * qi + 1), load_kv(2 * qi)
        top = [tile(1, hh, kv_hi, zero, zero, diag) for hh in range(2)]
        scatter(kv_hi, top)
        low = [tile(0, hh, kv_lo, zero, zero, diag) for hh in range(2)]
        low += [tile(1, hh, kv_lo, top[hh][0], top[hh][1], None) for hh in range(2)]
        scatter(kv_lo, low)
        r_runs, g_runs = [p[0] for p in low], [p[1] for p in low]

        def step(carry):
            i, _, *rg = carry
            kv = load_kv(2 * qi - 1 - i)
            parts = [tile(s, hh, kv, rg[2 * s + hh], rg[4 + 2 * s + hh], None) for s, hh in CHAINS]
            scatter(kv, parts)
            rs = [p[0] for p in parts]
            return (i + 1, _all_dead(rs), *rs, *[p[1] for p in parts])

        lax.while_loop(lambda c: jnp.logical_and(c[0] < 2 * qi, c[1] == 0), step,
                       (jnp.int32(0), _all_dead(r_runs), *r_runs, *g_runs))
        for s in range(2):
            dq_ref[pl.ds(s * sb, sb), :] = jnp.where(hmasks[0], dq_acc[2 * s], dq_acc[2 * s + 1]) * scale

    qspec = pl.BlockSpec((bq, LANES), lambda h, i: (i, h))
    dospec = pl.BlockSpec((bq, LANES), lambda h, i: (i, h + CONV_DIM // LANES))
    kspec = pl.BlockSpec((t, LANES), lambda h, i: (0, h))
    full = pl.BlockSpec((sb, sb), lambda h, i: (0, 0))
    in_specs, out_specs, out_shape, scratch, extra = _with_comm(
        comm, [qspec, kspec, kspec, qspec, dospec, full, full], [qspec, kspec, kspec],
        [jax.ShapeDtypeStruct((t, SB_DIM), F32)] * 3, [pltpu.VMEM((4, sb, LANES), F32)])
    return pl.pallas_call(
        body, name="attn_bwd", grid=(SB_DIM // LANES, t // bq),
        in_specs=in_specs, out_specs=out_specs, out_shape=out_shape, scratch_shapes=scratch,
        compiler_params=_cparams("arbitrary" if comm else "parallel", "arbitrary"),
    )(qn, kn, vb, o, dycat, tri, tri_inc, *extra)


def _ple_loss(x3, p2, tgt, gain, wpg, wppt, tt):
    t, d = x3.shape
    pdim = p2.shape[1]
    nt = t // tt

    def body(x_ref, p_ref, t_ref, g_ref, wg_ref, wp_ref,
             dx_ref, dxb_ref, dwg_ref, dwp_ref, gpart_ref, lpart_ref, accg_ref, accp_ref):
        i = pl.program_id(0)
        xv, gain_v = x_ref[...], g_ref[...]
        hb = ((xv * _rms_stats(xv)) * gain_v).astype(BF16)
        gate = jax.nn.sigmoid(_dot_nn(hb, wg_ref[...]))
        pb = p_ref[...].astype(BF16)
        pe = _dot_nt(pb, wp_ref[...])
        diff = xv + gate * pe - t_ref[...]
        lsum = jnp.sum(_fold8(diff * diff), axis=1, keepdims=True) * (0.5 / d)
        lpart_ref[...] = jnp.broadcast_to(lsum, (SUBLANES, LANES))
        dy = diff * (1.0 / d)
        dgz = ((dy * pe) * gate * (1.0 - gate)).astype(BF16)
        dpe = (dy * gate).astype(BF16)
        dx_n, grow = _rms_bwd(_dot_nt(dgz, wg_ref[...]), xv, gain_v)
        dx = dy + dx_n
        dx_ref[...] = dx
        dxb_ref[...] = dx.astype(BF16)
        gpart_ref[...] = _fold8(grow)
        sg = _dot_tn(hb, dgz)
        sp = _dot_tn(dpe, pb)

        @pl.when(i == 0)
        def _():
            accg_ref[...] = sg
            accp_ref[...] = sp

        @pl.when(i > 0)
        def _():
            accg_ref[...] += sg
            accp_ref[...] += sp

        @pl.when(i == nt - 1)
        def _():
            dwg_ref[...] = accg_ref[...].astype(BF16)
            dwp_ref[...] = accp_ref[...].astype(BF16)

    tile = lambda w: pl.BlockSpec((tt, w), lambda i: (i, 0))
    full = lambda shape: pl.BlockSpec(shape, lambda i: (0, 0))
    return pl.pallas_call(
        body, name="ple_loss", grid=(nt,),
        in_specs=[tile(d), tile(pdim), tile(d), full((1, d)),
                  pl.BlockSpec((d, d), lambda i: (0, 0), pipeline_mode=pl.Buffered(1)),
                  pl.BlockSpec((d, pdim), lambda i: (0, 0), pipeline_mode=pl.Buffered(1))],
        out_specs=[tile(d), tile(d), full((d, d)), full((d, pdim)),
                   pl.BlockSpec((SUBLANES, d), lambda i: (i, 0)), pl.BlockSpec((SUBLANES, LANES), lambda i: (i, 0))],
        out_shape=[jax.ShapeDtypeStruct((t, d), F32), jax.ShapeDtypeStruct((t, d), BF16),
                   jax.ShapeDtypeStruct((d, d), BF16), jax.ShapeDtypeStruct((d, pdim), BF16),
                   jax.ShapeDtypeStruct((nt * SUBLANES, d), F32), jax.ShapeDtypeStruct((nt * SUBLANES, LANES), F32)],
        scratch_shapes=[pltpu.VMEM((d, d), F32), pltpu.VMEM((d, pdim), F32)],
        compiler_params=_cparams("arbitrary"),
    )(x3, p2, tgt, gain, wpg, wppt)


def _pack_small(parts_gain, conv_part, qk_part):
    d = parts_gain[0].shape[1]
    ng = len(parts_gain)

    def body(*refs):
        g_refs, conv_ref, qk_ref, o_ref = refs[:ng], refs[ng], refs[ng + 1], refs[ng + 2]
        rows = [jnp.sum(r[...], axis=0, keepdims=True) for r in g_refs]
        cs = jnp.sum(conv_ref[...], axis=0, keepdims=True)
        qs = jnp.sum(qk_ref[...], axis=0, keepdims=True)
        rows.append(jnp.concatenate([cs[:, 3 * SLAB:], cs[:, :SLAB]], axis=1))
        rows.append(cs[:, SLAB:3 * SLAB])
        rows.append(qs)
        rid = lax.broadcasted_iota(jnp.int32, (2 * SUBLANES, 1), 0)
        out = jnp.zeros((2 * SUBLANES, d), F32)
        for idx, r in enumerate(rows):
            out = jnp.where(rid == idx, r, out)
        o_ref[...] = out

    return pl.pallas_call(
        body, name="pack_small", out_shape=jax.ShapeDtypeStruct((2 * SUBLANES, d), F32),
    )(*parts_gain, conv_part, qk_part)


def _sum_slots(name, slots, out_dtype=F32):
    _, r, c = slots.shape

    def body(s_ref, o_ref):
        acc = s_ref[0].astype(F32)
        for d in range(1, N_DEV):
            acc = acc + s_ref[d].astype(F32)
        o_ref[...] = acc.astype(o_ref.dtype)

    return pl.pallas_call(body, name=name, out_shape=jax.ShapeDtypeStruct((r, c), out_dtype),
                          compiler_params=pltpu.CompilerParams(vmem_limit_bytes=VMEM_LIMIT_BYTES))(slots)


def _adamw(name, w, g, m, v):
    c1 = 1.0 - ADAM_B1 ** ADAM_STEP
    c2 = 1.0 - ADAM_B2 ** ADAM_STEP

    def body(w_ref, g_ref, m_ref, v_ref, d_ref, nm_ref, nv_ref):
        gv = g_ref[...]
        nm = ADAM_B1 * m_ref[...] + (1.0 - ADAM_B1) * gv
        nv = ADAM_B2 * v_ref[...] + (1.0 - ADAM_B2) * (gv * gv)
        d_ref[...] = -ADAM_LR * ((nm / c1) / (jnp.sqrt(nv / c2) + ADAM_EPS) + ADAM_WD * w_ref[...])
        nm_ref[...] = nm
        nv_ref[...] = nv

    return pl.pallas_call(body, name=name, out_shape=[jax.ShapeDtypeStruct(w.shape, F32)] * 3,
                          compiler_params=pltpu.CompilerParams(vmem_limit_bytes=VMEM_LIMIT_BYTES))(w, g, m, v)


def _any_specs(n):
    return [pl.BlockSpec(memory_space=pl.ANY)] * n


def _all_gather(name, shards):
    n = len(shards)

    def body(*refs):
        ins, outs = refs[:n], refs[n:2 * n]
        send_sems, recv_sems, local_sems = refs[2 * n:]
        x, y, c = (lax.axis_index(a) for a in MESH_AXES)
        me, sibling = (x, y, c), (x, y, 1 - c)
        chips = [(1 - x, y), (x, 1 - y), (1 - x, 1 - y)]

        def rows(a, px, py, pc):
            r = ins[a].shape[0]
            return outs[a].at[pl.ds((4 * px + 2 * py + pc) * r, r), :]

        def copy(a, k, block, to, src=None):
            return pltpu.make_async_remote_copy(
                src_ref=rows(a, *block) if src is None else src, dst_ref=rows(a, *block),
                send_sem=send_sems.at[7 * a + k], recv_sem=recv_sems.at[7 * a + k],
                device_id=to, device_id_type=MESH)

        mine = [pltpu.make_async_copy(ins[a], rows(a, *me), local_sems.at[a]) for a in range(n)]
        for cp in mine:
            cp.start()
        first = []
        for a in range(n):
            first.append(copy(a, 0, me, sibling, src=ins[a]))
            first += [copy(a, 1 + j, me, (*chip, c), src=ins[a]) for j, chip in enumerate(chips)]
        for cp in first:
            cp.start()
        passed = []
        for j, chip in enumerate(chips):
            for a in range(n):
                copy(a, 1 + j, (*chip, c), me).wait_recv()
                fwd = copy(a, 4 + j, (*chip, c), sibling)
                fwd.start()
                passed.append(fwd)
        for a in range(n):
            copy(a, 0, sibling, me).wait_recv()
            for j, chip in enumerate(chips):
                copy(a, 4 + j, (*chip, 1 - c), me).wait_recv()
        for cp in first + passed:
            cp.wait_send()
        for cp in mine:
            cp.wait()

    return pl.pallas_call(
        body, name=name, in_specs=_any_specs(n), out_specs=_any_specs(n),
        out_shape=[jax.ShapeDtypeStruct((N_DEV * s.shape[0], s.shape[1]), s.dtype) for s in shards],
        scratch_shapes=[pltpu.SemaphoreType.DMA((7 * n,)), pltpu.SemaphoreType.DMA((7 * n,)),
                        pltpu.SemaphoreType.DMA((n,))],
    )(*shards)


def _residual_and_norm(res_scale):
    def epilogue(acc, rows, fulls):
        out = rows[0] + res_scale * acc
        return [out] + [(out * _rms_stats(out)) * gain for gain in fulls]
    return epilogue


def _ffn_fwd(tag, x, h, wgt, wut, wd, next_gain, tt_nt, tt_nn, comm_gate=None, comm_down=None):
    f = wgt.shape[0]

    def gate_up(accs, _):
        return [accs[0], accs[1], jax.nn.silu(accs[0]) * accs[1]]

    g, u, a, *got_gate = _mm_nt(f"{tag}_gate_up", [h], [wgt, wut], [(0, 0), (0, 1)], gate_up, [BF16] * 3,
                                tt_nn, _pick(f, 256), comm_gate)
    if wd is None:
        wd = got_gate[0]
    gains = [] if next_gain is None else [next_gain]
    out, *rest = _mm_nn(f"{tag}_down", [(a, wd, 0)], [x], gains, _residual_and_norm(FFN_RES),
                        [("tile", F32)] + [("tile", BF16)] * len(gains), tt_nt, f, comm_down)
    h_next = rest.pop(0) if gains else None
    return out, h_next, (g, u, a), got_gate, rest


def _norm_bwd_epilogue(acc, rows, fulls):
    x_in, dy = rows
    dx_n, grow = _rms_bwd(acc, x_in, fulls[0])
    dx = dy + dx_n
    return [dx, dx, _fold8(grow)]


_NORM_BWD_OUTS = [("tile", F32), ("tile", BF16), ("part", F32)]


def _ffn_bwd(tag, x_in, h, hidden, dy, dyb, gain, wgt, wut, wd, tt_nt, tt_nn, riders=(None, None),
             exchange_own=False):
    g, u, a = hidden
    f = wgt.shape[0]
    tt_tn = _pick(h.shape[0], 2 * tt_nt)
    own = (lambda arr: _Exchange([arr], gather=False)) if exchange_own else (lambda arr: None)

    def carried(result, rider):
        return (result[0], result[1:]) if rider else (result, [])

    def hidden_grads(accs, tiles):
        da = FFN_RES * accs[0]
        gv, uv = tiles[0].astype(F32), tiles[1].astype(F32)
        sg = jax.nn.sigmoid(gv)
        s = gv * sg
        return [da * uv * (sg * (1.0 + gv * (1.0 - sg))), da * s]

    dwd, got_dwd = carried(_mm_tn(f"{tag}_dwd", a, dyb, FFN_RES, f // 2, tt_tn, riders[0]), riders[0])
    dg, du, *x_dwd = _mm_nt(f"{tag}_bwd_hidden", [dyb], [wd], [(0, 0)], hidden_grads, [BF16, BF16],
                            tt_nn, _pick(f, 256), own(dwd), tiles=[g, u])
    dwg, got_dwg = carried(_mm_tn(f"{tag}_dwg", dg, h, 1.0, f // 2, tt_tn, riders[1]), riders[1])
    dwu, x_dwg = carried(_mm_tn(f"{tag}_dwu", du, h, 1.0, f // 2, tt_tn, own(dwg)), exchange_own)
    dx, dxb, gpart, *x_dwu = _mm_nn(f"{tag}_bwd_dx", [(dg, wgt, 0), (du, wut, 0)], [x_in, dy], [gain],
                                    _norm_bwd_epilogue, _NORM_BWD_OUTS, tt_nn, f, own(dwu))
    grads = [*x_dwg, *x_dwu, *x_dwd] if exchange_own else [dwg, dwu, dwd]
    return dx, dxb, gpart, grads, [*got_dwd, *got_dwg]


def kernel(x, p, ffn1_norm, ffn1_w_gate, ffn1_w_up, ffn1_w_down, mix_norm, w_in, conv_w, conv_b, q_norm, k_norm, w_out, ffn2_norm, ffn2_w_gate, ffn2_w_up, ffn2_w_down, ple_norm, ple_w_gate, ple_w_proj, loss_target, m_ffn1_norm, m_ffn1_w_gate, m_ffn1_w_up, m_ffn1_w_down, m_mix_norm, m_w_in, m_conv_w, m_conv_b, m_q_norm, m_k_norm, m_w_out, m_ffn2_norm, m_ffn2_w_gate, m_ffn2_w_up, m_ffn2_w_down, m_ple_norm, m_ple_w_gate, m_ple_w_proj, v_ffn1_norm, v_ffn1_w_gate, v_ffn1_w_up, v_ffn1_w_down, v_mix_norm, v_w_in, v_conv_w, v_conv_b, v_q_norm, v_k_norm, v_w_out, v_ffn2_norm, v_ffn2_w_gate, v_ffn2_w_up, v_ffn2_w_down, v_ple_norm, v_ple_w_gate, v_ple_w_proj):
    x0, p2, tgt = x[0], p[0, 0], loss_target[0]
    t, d = x0.shape
    tt_nt = _pick(t, 1024)
    tt_nn = _pick(t, 512)
    tt_ew = _pick(t, 512)
    tt_ple = _pick(t, 512)
    tt_tn = _pick(t, 2048)
    sb = _pick(t // 2, 256)

    t_bf = lambda w: w[0].T.astype(BF16)
    n_bf = lambda w: w[0].astype(BF16)
    cw_tile = jnp.zeros((SUBLANES, LANES), F32).at[:conv_w.shape[1], :conv_w.shape[2]].set(conv_w[0])
    wg1t, wu1t = _all_gather("gather_ffn1_weights", [t_bf(ffn1_w_gate), t_bf(ffn1_w_up)])
    gather_down = _Exchange([n_bf(ffn1_w_down), n_bf(w_out), cw_tile], gather=True)
    gather_in = _Exchange([t_bf(w_in)], gather=True)
    gather_late = _Exchange([t_bf(ffn2_w_gate), t_bf(ffn2_w_up), n_bf(ffn2_w_down), n_bf(ple_w_gate),
                             t_bf(ple_w_proj)], gather=True)
    ncs = conv_w.shape[2]

    qg = jnp.tile(q_norm, (1, SB_DIM // HEAD_DIM))
    kg = jnp.tile(k_norm, (1, SB_DIM // HEAD_DIM))
    gi = lax.broadcasted_iota(jnp.int32, (SLAB, SLAB), 0) // HEAD_DIM
    gj = lax.broadcasted_iota(jnp.int32, (SLAB, SLAB), 1) // HEAD_DIM
    bd = (gi == gj).astype(BF16)
    tri, tri_inc = _tri_masks(sb)

    h1 = _rmsnorm("ffn1_norm", x0, ffn1_norm, tt_ew)
    x1, h2, hidden1, (wd1, wout, cw_all), (wint,) = _ffn_fwd(
        "ffn1", x0, h1, wg1t, wu1t, None, mix_norm, tt_nt, tt_nn, gather_down, gather_in)
    cw_full = cw_all.reshape(N_DEV, SUBLANES, LANES)[:, :, :ncs].transpose(1, 0, 2).reshape(SUBLANES, N_DEV * ncs)
    (proj,) = _mm_nt("in_proj", [h2], [wint], [(0, 0)], lambda accs, _: accs, [F32], tt_nn, SLAB)
    y_conv = _conv_fwd(proj, cw_full, conv_b, tt_ew)
    qn, kn, vb = _qknorm_fwd(proj, qg, kg, bd, tt_ew)
    o, ob, wg2t, wu2t, wd2, wpg, wppt = _attn_fwd(qn, kn, vb, tri, sb, gather_late)
    x2, h3 = _mm_nn("out_proj", [(y_conv, wout, 0), (ob, wout, 1)], [x1], [ffn2_norm], _residual_and_norm(1.0),
                    [("tile", F32), ("tile", BF16)], tt_nn, SLAB)
    x3, _, hidden2, _, _ = _ffn_fwd("ffn2", x2, h3, wg2t, wu2t, wd2, None, tt_nt, tt_nn)

    dx3, dx3b, dwpg, dwppt, gp_ple, lpart = _ple_loss(x3, p2, tgt, ple_norm, wpg, wppt, tt_ple)
    loss = lax.psum(jnp.sum(lpart[:, 0]), MESH_AXES)
    dx2, dx2b, gp_ffn2, (dwg2, dwu2, dwd2), _ = _ffn_bwd("ffn2", x2, h3, hidden2, dx3, dx3b, ffn2_norm,
                                                         wg2t, wu2t, wd2, tt_nt, tt_nn)
    (dycat,) = _mm_nt("out_proj_bwd", [dx2b], [wout], [(0, 0)], lambda accs, _: accs, [F32], tt_nn, SLAB)
    dwout = jnp.concatenate([_mm_tn("dwout_conv", y_conv, dx2b, 1.0, SLAB, tt_tn),
                             _mm_tn("dwout_attn", ob, dx2b, 1.0, SLAB, tt_tn)], axis=0)
    dqn, dkn, dv, *slots_late = _attn_bwd(qn, kn, vb, o, dycat, tri, tri_inc, sb,
                                          _Exchange([dwg2, dwu2, dwd2, dwpg, dwppt], gather=False))
    dq, dk, dvb, qk_part = _qknorm_bwd(proj, dqn, dkn, dv, qg, kg, bd, tt_ew)
    db, dc, du, conv_part = _conv_bwd(proj, dycat, cw_full, conv_b, tt_ew)
    dproj = [db, dc, du, dq, dk, dvb]
    dx1, dx1b, gp_mix = _mm_nn("in_proj_bwd", [(dp, wint, s) for s, dp in enumerate(dproj)], [x1, dx2], [mix_norm],
                               _norm_bwd_epilogue, _NORM_BWD_OUTS, tt_nn, SLAB)
    dwin = jnp.concatenate([_mm_tn(f"dwin_{s}", dp, h2, 1.0, SLAB, tt_tn) for s, dp in enumerate(dproj)], axis=0)
    dx0, _, gp_ffn1, slots_ffn1, (slot_wout, slot_win) = _ffn_bwd(
        "ffn1", x0, h1, hidden1, dx1, dx1b, ffn1_norm, wg1t, wu1t, wd1, tt_nt, tt_nn,
        (_Exchange([dwout], gather=False), _Exchange([dwin], gather=False)), exchange_own=True)

    slots = [*slots_ffn1, slot_win, slot_wout, *slots_late]
    sums = [_sum_slots(f"sum_grads_{i}", s.reshape(N_DEV, s.shape[0] // N_DEV, s.shape[1]))
            for i, s in enumerate(slots)]
    g_wg1, g_wu1, g_wd1, g_win, g_wout, g_wg2, g_wu2, g_wd2, g_wpg, g_wpp = sums
    small = _pack_small([gp_ffn1, gp_mix, gp_ffn2, gp_ple], conv_part, qk_part)
    (small_all,) = _all_gather("gather_small_grads", [small])
    sm = _sum_slots("sum_small_grads", small_all.reshape(N_DEV, 2 * SUBLANES, d))
    fold = lambda r: r.reshape(SB_DIM // HEAD_DIM, HEAD_DIM).sum(axis=0)[None]
    me_idx = 4 * lax.axis_index("x") + 2 * lax.axis_index("y") + lax.axis_index("c")
    cw_grad = jnp.stack([sm[4, SLAB:], sm[5, :SLAB], sm[5, SLAB:]])
    grads = {
        "ffn1_norm": sm[0:1], "ffn1_w_gate": g_wg1.T, "ffn1_w_up": g_wu1.T, "ffn1_w_down": g_wd1,
        "mix_norm": sm[1:2], "w_in": g_win.T, "conv_w": lax.dynamic_slice_in_dim(cw_grad, me_idx * ncs, ncs, axis=1),
        "conv_b": sm[4:5, :SLAB], "q_norm": fold(sm[6, :SLAB]), "k_norm": fold(sm[6, SLAB:]),
        "w_out": g_wout, "ffn2_norm": sm[2:3], "ffn2_w_gate": g_wg2.T, "ffn2_w_up": g_wu2.T, "ffn2_w_down": g_wd2,
        "ple_norm": sm[3:4], "ple_w_gate": g_wpg, "ple_w_proj": g_wpp.T,
    }

    weights = dict(ffn1_norm=ffn1_norm, ffn1_w_gate=ffn1_w_gate, ffn1_w_up=ffn1_w_up, ffn1_w_down=ffn1_w_down,
                   mix_norm=mix_norm, w_in=w_in, conv_w=conv_w, conv_b=conv_b, q_norm=q_norm, k_norm=k_norm,
                   w_out=w_out, ffn2_norm=ffn2_norm, ffn2_w_gate=ffn2_w_gate, ffn2_w_up=ffn2_w_up,
                   ffn2_w_down=ffn2_w_down, ple_norm=ple_norm, ple_w_gate=ple_w_gate, ple_w_proj=ple_w_proj)
    m_in = dict(ffn1_norm=m_ffn1_norm, ffn1_w_gate=m_ffn1_w_gate, ffn1_w_up=m_ffn1_w_up, ffn1_w_down=m_ffn1_w_down,
                mix_norm=m_mix_norm, w_in=m_w_in, conv_w=m_conv_w, conv_b=m_conv_b, q_norm=m_q_norm,
                k_norm=m_k_norm, w_out=m_w_out, ffn2_norm=m_ffn2_norm, ffn2_w_gate=m_ffn2_w_gate,
                ffn2_w_up=m_ffn2_w_up, ffn2_w_down=m_ffn2_w_down, ple_norm=m_ple_norm, ple_w_gate=m_ple_w_gate,
                ple_w_proj=m_ple_w_proj)
    v_in = dict(ffn1_norm=v_ffn1_norm, ffn1_w_gate=v_ffn1_w_gate, ffn1_w_up=v_ffn1_w_up, ffn1_w_down=v_ffn1_w_down,
                mix_norm=v_mix_norm, w_in=v_w_in, conv_w=v_conv_w, conv_b=v_conv_b, q_norm=v_q_norm,
                k_norm=v_k_norm, w_out=v_w_out, ffn2_norm=v_ffn2_norm, ffn2_w_gate=v_ffn2_w_gate,
                ffn2_w_up=v_ffn2_w_up, ffn2_w_down=v_ffn2_w_down, ple_norm=v_ple_norm, ple_w_gate=v_ple_w_gate,
                ple_w_proj=v_ple_w_proj)
    g_out, d_out, m_out, v_out = [], [], [], []
    for name, w in weights.items():
        w2 = w.reshape(w.shape[-2:])
        g2 = grads[name].reshape(w2.shape)
        dlt, nm, nv = _adamw(f"adamw_{name}", w2, g2, m_in[name].reshape(w2.shape), v_in[name].reshape(w2.shape))
        g_out.append(g2.reshape(w.shape))
        d_out.append(dlt.reshape(w.shape))
        m_out.append(nm.reshape(w.shape))
        v_out.append(nv.reshape(w.shape))
    return (loss, dx0[None], *g_out, *d_out, *m_out, *v_out)
```

```python
import jax
import jax.numpy as jnp
from jax import lax
from jax.experimental import pallas as pl
from jax.experimental.pallas import tpu as pltpu

F32 = jnp.float32
BF16 = jnp.bfloat16

EPS = 1e-6
FFN_RES = 0.5
HEAD_DIM = 64
CONV_DIM = 512
SB_DIM = 512
SLAB = 512
N_DEV = 8
MESH_AXES = ("x", "y", "c")
MESH = pl.DeviceIdType.MESH

ADAM_LR = 0.001
ADAM_B1 = 0.9
ADAM_B2 = 0.999
ADAM_EPS = 1e-08
ADAM_WD = 0.01
ADAM_STEP = 10

VMEM_LIMIT_BYTES = 56 * 1024 * 1024
SUBLANES = 8
LANES = 128


def _cparams(*semantics):
    return pltpu.CompilerParams(dimension_semantics=semantics, vmem_limit_bytes=VMEM_LIMIT_BYTES)


def _dot_nn(a, b):
    return jnp.dot(a, b, preferred_element_type=F32)


def _dot_nt(a, b):
    return lax.dot_general(a, b, (((1,), (1,)), ((), ())), preferred_element_type=F32)


def _dot_tn(a, b):
    return lax.dot_general(a, b, (((0,), (0,)), ((), ())), preferred_element_type=F32)


def _fold8(v):
    rows, cols = v.shape
    return jnp.sum(v.reshape(rows // SUBLANES, SUBLANES, cols), axis=0)


def _split2(v):
    hi = v.astype(BF16)
    lo = (v - hi.astype(F32)).astype(BF16)
    return hi, lo


def _rms_stats(x):
    return lax.rsqrt(jnp.mean(x * x, axis=-1, keepdims=True) + EPS)


def _rms_bwd(dh, x, gain):
    r = _rms_stats(x)
    u = dh * gain
    dx = r * u - x * (r * r * r) * jnp.mean(u * x, axis=-1, keepdims=True)
    return dx, dh * x * r


def _pick(n, pref):
    return pref if n % pref == 0 else n


def _rmsnorm(name, x, gain, tt):
    t, d = x.shape

    def body(x_ref, g_ref, o_ref):
        xv = x_ref[...]
        o_ref[...] = ((xv * _rms_stats(xv)) * g_ref[...]).astype(BF16)

    return pl.pallas_call(
        body, name=name, grid=(t // tt,),
        in_specs=[pl.BlockSpec((tt, d), lambda i: (i, 0)), pl.BlockSpec((1, d), lambda i: (0, 0))],
        out_specs=pl.BlockSpec((tt, d), lambda i: (i, 0)),
        out_shape=jax.ShapeDtypeStruct((t, d), BF16),
        compiler_params=_cparams("parallel"),
    )(x, gain)


class _Exchange:
    FLIPS = [(fx, fy, fc) for fx in (0, 1) for fy in (0, 1) for fc in (0, 1)][1:]

    def __init__(self, arrays, gather):
        self.arrays = list(arrays)
        self.gather = gather
        self.n = len(self.arrays)
        self.rows = [a.shape[0] if gather else a.shape[0] // N_DEV for a in self.arrays]
        self.out_shape = [jax.ShapeDtypeStruct((N_DEV * r, a.shape[1]), a.dtype)
                          for r, a in zip(self.rows, self.arrays)]
        self.scratch = [pltpu.SemaphoreType.DMA((7 * self.n,)), pltpu.SemaphoreType.DMA((7 * self.n,)),
                        pltpu.SemaphoreType.DMA((self.n,))]

    def _copies(self, ins, outs, sems, arrivals):
        send_sems, recv_sems, local_sems = sems
        x, y, c = (lax.axis_index(a) for a in MESH_AXES)
        me_idx = 4 * x + 2 * y + c
        local, send, recv = [], [], []
        for a in range(self.n):
            r = self.rows[a]

            def blk(ref, idx, r=r):
                return ref.at[pl.ds(idx * r, r), :]

            def src(idx, a=a, blk=blk):
                return ins[a] if self.gather else blk(ins[a], idx)

            local.append(pltpu.make_async_copy(src(me_idx), blk(outs[a], me_idx), local_sems.at[a]))
            for k, flip in enumerate(self.FLIPS):
                px, py, pc = (1 - v if f else v for v, f in zip((x, y, c), flip))
                p_idx = 4 * px + 2 * py + pc
                for dst_idx, group in ((me_idx, send), (p_idx, recv))[:2 if arrivals else 1]:
                    group.append(pltpu.make_async_remote_copy(
                        src_ref=src(p_idx), dst_ref=blk(outs[a], dst_idx),
                        send_sem=send_sems.at[7 * a + k], recv_sem=recv_sems.at[7 * a + k],
                        device_id=(px, py, pc), device_id_type=MESH))
        return local, send, recv

    def start(self, ins, outs, sems):
        local, send, _ = self._copies(ins, outs, sems, arrivals=False)
        for cp in local + send:
            cp.start()

    def wait(self, ins, outs, sems):
        local, send, recv = self._copies(ins, outs, sems, arrivals=True)
        for s, r in zip(send, recv):
            r.wait_recv()
            s.wait_send()
        for cp in local:
            cp.wait()

    def attach(self, refs, first, last):
        pl.when(first)(lambda: self.start(*refs))
        pl.when(last)(lambda: self.wait(*refs))


def _split_refs(refs, n_in, n_out, n_scratch, comm):
    nc = comm.n if comm else 0
    ins, rest = refs[:n_in], refs[n_in:]
    c_in, rest = rest[:nc], rest[nc:]
    outs, rest = rest[:n_out], rest[n_out:]
    c_out, rest = rest[:nc], rest[nc:]
    scratch, c_sems = rest[:n_scratch], rest[n_scratch:]
    return ins, outs, scratch, ((c_in, c_out, c_sems) if comm else None)


def _with_comm(comm, in_specs, out_specs, out_shape, scratch):
    if comm is None:
        return in_specs, out_specs, out_shape, scratch, []
    return (in_specs + _any_specs(comm.n), out_specs + _any_specs(comm.n), out_shape + comm.out_shape,
            scratch + comm.scratch, comm.arrays)


def _mm_nt(name, a_list, w_list, pairs, epilogue, out_dtypes, tt, tn, comm=None, tiles=()):
    t = a_list[0].shape[0]
    n = w_list[0].shape[0]
    na, nw, ntile = len(a_list), len(w_list), len(tiles)
    ni, nj = t // tt, n // tn

    def body(*refs):
        ins, o_refs, _, c_refs = _split_refs(refs, na + nw + ntile, len(out_dtypes), 0, comm)
        a_refs, w_refs, t_refs = ins[:na], ins[na:na + nw], ins[na + nw:]
        if comm:
            i = pl.program_id(0)
            comm.attach(c_refs, i == 0, i == ni - 1)
        a_vals = [a_ref[...] for a_ref in a_refs]
        for j in range(nj):
            cols = pl.ds(j * tn, tn)
            accs = [_dot_nt(a_vals[ai], w_refs[wi][cols, :]) for ai, wi in pairs]
            for o_ref, o in zip(o_refs, epilogue(accs, [t_ref[:, cols] for t_ref in t_refs])):
                o_ref[:, cols] = o.astype(o_ref.dtype)

    in_specs = ([pl.BlockSpec((tt, a.shape[1]), lambda i: (i, 0)) for a in a_list]
                + [pl.BlockSpec(w.shape, lambda i: (0, 0), pipeline_mode=pl.Buffered(1)) for w in w_list]
                + [pl.BlockSpec((tt, n), lambda i: (i, 0)) for _ in tiles])
    in_specs, out_specs, out_shape, scratch, extra = _with_comm(
        comm, in_specs, [pl.BlockSpec((tt, n), lambda i: (i, 0)) for _ in out_dtypes],
        [jax.ShapeDtypeStruct((t, n), dt) for dt in out_dtypes], [])
    return pl.pallas_call(
        body, name=name, grid=(ni,), in_specs=in_specs, out_specs=out_specs, out_shape=out_shape,
        scratch_shapes=scratch,
        compiler_params=_cparams("arbitrary" if comm else "parallel"),
    )(*a_list, *w_list, *tiles, *extra)


def _mm_nn(name, pairs, rows, fulls, epilogue, out_kinds, tt, tk, comm=None):
    t, k_total = pairs[0][0].shape
    n = pairs[0][1].shape[1]
    nk = k_total // tk
    nt = t // tt
    npair, nrow, nfull = len(pairs), len(rows), len(fulls)

    def body(*refs):
        ins, o_refs, scratch, c_refs = _split_refs(refs, 2 * npair + nrow + nfull, len(out_kinds), min(nk - 1, 1), comm)
        a_refs, w_refs = ins[:npair], ins[npair:2 * npair]
        r_refs, f_refs = ins[2 * npair:2 * npair + nrow], ins[2 * npair + nrow:]
        i, k = pl.program_id(0), pl.program_id(1)
        if comm:
            comm.attach(c_refs, jnp.logical_and(i == 0, k == 0), jnp.logical_and(i == nt - 1, k == nk - 1))
        s = _dot_nn(a_refs[0][...], w_refs[0][...])
        for a_ref, w_ref in zip(a_refs[1:], w_refs[1:]):
            s = s + _dot_nn(a_ref[...], w_ref[...])

        def finish(acc):
            outs = epilogue(acc, [r[...] for r in r_refs], [f[...] for f in f_refs])
            for o_ref, o in zip(o_refs, outs):
                o_ref[...] = o.astype(o_ref.dtype)

        if nk == 1:
            finish(s)
        else:
            acc_ref = scratch[0]

            @pl.when(k == 0)
            def _():
                acc_ref[...] = s

            @pl.when(k > 0)
            def _():
                acc_ref[...] += s

            @pl.when(k == nk - 1)
            def _():
                finish(acc_ref[...])

    once = dict(pipeline_mode=pl.Buffered(1)) if nk == 1 else {}
    in_specs = ([pl.BlockSpec((tt, tk), lambda i, k: (i, k)) for _ in pairs]
                + [pl.BlockSpec((tk, n), (lambda i, k, off=off: (k + off, 0)), **once) for _, _, off in pairs]
                + [pl.BlockSpec((tt, n), lambda i, k: (i, 0)) for _ in rows]
                + [pl.BlockSpec((1, n), lambda i, k: (0, 0)) for _ in fulls])
    out_specs, out_shape = [], []
    for kind, dt in out_kinds:
        if kind == "tile":
            out_specs.append(pl.BlockSpec((tt, n), lambda i, k: (i, 0)))
            out_shape.append(jax.ShapeDtypeStruct((t, n), dt))
        else:
            out_specs.append(pl.BlockSpec((SUBLANES, n), lambda i, k: (i, 0)))
            out_shape.append(jax.ShapeDtypeStruct((nt * SUBLANES, n), dt))
    in_specs, out_specs, out_shape, scratch, extra = _with_comm(
        comm, in_specs, out_specs, out_shape, [] if nk == 1 else [pltpu.VMEM((tt, n), F32)])
    return pl.pallas_call(
        body, name=name, grid=(nt, nk), in_specs=in_specs, out_specs=out_specs, out_shape=out_shape,
        scratch_shapes=scratch,
        compiler_params=_cparams("arbitrary" if comm else "parallel", "arbitrary"),
    )(*[a for a, _, _ in pairs], *[w for _, w, _ in pairs], *rows, *fulls, *extra)


def _mm_tn(name, a, b, scale, tm, tt, comm=None):
    t, m = a.shape
    n = b.shape[1]
    nt = t // tt
    nm = m // tm

    def body(*refs):
        (a_ref, b_ref), (o_ref,), (acc_ref,), c_refs = _split_refs(refs, 2, 1, 1, comm)
        k = pl.program_id(1)
        if comm:
            i = pl.program_id(0)
            comm.attach(c_refs, jnp.logical_and(i == 0, k == 0), jnp.logical_and(i == nm - 1, k == nt - 1))
        s = _dot_tn(a_ref[...], b_ref[...])

        @pl.when(k == 0)
        def _():
            acc_ref[...] = s

        @pl.when(k > 0)
        def _():
            acc_ref[...] += s

        @pl.when(k == nt - 1)
        def _():
            o_ref[...] = (acc_ref[...] * scale).astype(o_ref.dtype)

    in_specs, out_specs, out_shape, scratch, extra = _with_comm(
        comm, [pl.BlockSpec((tt, tm), lambda i, k: (k, i)), pl.BlockSpec((tt, n), lambda i, k: (k, 0))],
        [pl.BlockSpec((tm, n), lambda i, k: (i, 0))], [jax.ShapeDtypeStruct((m, n), BF16)],
        [pltpu.VMEM((tm, n), F32)])
    out = pl.pallas_call(
        body, name=name, grid=(nm, nt), in_specs=in_specs, out_specs=out_specs, out_shape=out_shape,
        scratch_shapes=scratch,
        compiler_params=_cparams("arbitrary" if comm else "parallel", "arbitrary"),
    )(a, b, *extra)
    return out if comm else out[0]


def _mm_tn_slabs(name, a_list, b, tt, comm=None):
    t, m = a_list[0].shape
    n = b.shape[1]
    na = len(a_list)
    nt = t // tt

    def body(*refs):
        ins, (o_ref,), (acc_ref,), c_refs = _split_refs(refs, na + 1, 1, 1, comm)
        k = pl.program_id(0)
        if comm:
            comm.attach(c_refs, k == 0, k == nt - 1)
        bv = ins[na][...]
        parts = [_dot_tn(a_ref[...], bv) for a_ref in ins[:na]]

        @pl.when(k == 0)
        def _():
            for j, part in enumerate(parts):
                acc_ref[pl.ds(j * m, m), :] = part

        @pl.when(k > 0)
        def _():
            for j, part in enumerate(parts):
                acc_ref[pl.ds(j * m, m), :] += part

        @pl.when(k == nt - 1)
        def _():
            o_ref[...] = acc_ref[...].astype(o_ref.dtype)

    in_specs, out_specs, out_shape, scratch, extra = _with_comm(
        comm, [pl.BlockSpec((tt, m), lambda k: (k, 0))] * na + [pl.BlockSpec((tt, n), lambda k: (k, 0))],
        [pl.BlockSpec((na * m, n), lambda k: (0, 0))], [jax.ShapeDtypeStruct((na * m, n), BF16)],
        [pltpu.VMEM((na * m, n), F32)])
    out = pl.pallas_call(
        body, name=name, grid=(nt,), in_specs=in_specs, out_specs=out_specs, out_shape=out_shape,
        scratch_shapes=scratch, compiler_params=_cparams("arbitrary"),
    )(*a_list, b, *extra)
    return out if comm else out[0]


def _group_sum(v, bd):
    hi = v.astype(BF16)
    r1 = v - hi.astype(F32)
    mid = r1.astype(BF16)
    lo = (r1 - mid.astype(F32)).astype(BF16)
    return _dot_nn(hi, bd) + _dot_nn(mid, bd) + _dot_nn(lo, bd)


def _qknorm_fwd(proj, qg, kg, bd, tt):
    t = proj.shape[0]

    def body(q_ref, k_ref, v_ref, qg_ref, kg_ref, bd_ref, qn_ref, kn_ref, vb_ref):
        bdv = bd_ref[...]
        for x_ref, g_ref, o_ref in ((q_ref, qg_ref, qn_ref), (k_ref, kg_ref, kn_ref)):
            xv = x_ref[...]
            r = lax.rsqrt(_group_sum(xv * xv, bdv) * (1.0 / HEAD_DIM) + EPS)
            o_ref[...] = ((xv * r) * g_ref[...]).astype(BF16)
        vb_ref[...] = v_ref[...].astype(BF16)

    slab = lambda s: pl.BlockSpec((tt, SLAB), lambda i, s=s: (i, s))
    full = lambda shape: pl.BlockSpec(shape, lambda i: (0, 0))
    out = pl.BlockSpec((tt, SLAB), lambda i: (i, 0))
    return pl.pallas_call(
        body, name="qknorm_fwd", grid=(t // tt,),
        in_specs=[slab(3), slab(4), slab(5), full((1, SLAB)), full((1, SLAB)), full((SLAB, SLAB))],
        out_specs=[out, out, out],
        out_shape=[jax.ShapeDtypeStruct((t, SLAB), BF16)] * 3,
        compiler_params=_cparams("parallel"),
    )(proj, proj, proj, qg, kg, bd)


def _qknorm_bwd(proj, dqn, dkn, dv, qg, kg, bd, tt):
    t = proj.shape[0]

    def body(q_ref, k_ref, dqn_ref, dkn_ref, dv_ref, qg_ref, kg_ref, bd_ref, dq_ref, dk_ref, dvb_ref, part_ref):
        bdv = bd_ref[...]
        parts = []
        for x_ref, d_ref, g_ref, o_ref in ((q_ref, dqn_ref, qg_ref, dq_ref), (k_ref, dkn_ref, kg_ref, dk_ref)):
            xv, dn = x_ref[...], d_ref[...]
            r = lax.rsqrt(_group_sum(xv * xv, bdv) * (1.0 / HEAD_DIM) + EPS)
            u = dn * g_ref[...]
            dx = r * u - xv * (r * r * r) * (_group_sum(u * xv, bdv) * (1.0 / HEAD_DIM))
            o_ref[...] = dx.astype(BF16)
            parts.append(_fold8(dn * xv * r))
        dvb_ref[...] = dv_ref[...].astype(BF16)
        part_ref[...] = jnp.concatenate(parts, axis=1)

    slab = lambda s: pl.BlockSpec((tt, SLAB), lambda i, s=s: (i, s))
    tile = pl.BlockSpec((tt, SLAB), lambda i: (i, 0))
    full = lambda shape: pl.BlockSpec(shape, lambda i: (0, 0))
    return pl.pallas_call(
        body, name="qknorm_bwd", grid=(t // tt,),
        in_specs=[slab(3), slab(4), tile, tile, tile, full((1, SLAB)), full((1, SLAB)), full((SLAB, SLAB))],
        out_specs=[tile, tile, tile, pl.BlockSpec((SUBLANES, 2 * SLAB), lambda i: (i, 0))],
        out_shape=[jax.ShapeDtypeStruct((t, SLAB), BF16)] * 3
        + [jax.ShapeDtypeStruct((t // tt * SUBLANES, 2 * SLAB), F32)],
        compiler_params=_cparams("parallel"),
    )(proj, proj, dqn, dkn, dv, qg, kg, bd)


def _conv_taps(z, z_prev, row):
    zm1 = jnp.where(row == 0, z_prev[7:8], pltpu.roll(z, 1, 0))
    zm2 = jnp.where(row == 0, z_prev[6:7], jnp.where(row == 1, z_prev[7:8], pltpu.roll(z, 2, 0)))
    return zm1, zm2


def _conv_fwd(proj, cw, cb, tt):
    t = proj.shape[0]
    tb = tt // SUBLANES

    def body(b_ref, c_ref, u_ref, cp_ref, up_ref, cw_ref, cb_ref, o_ref):
        i = pl.program_id(0)
        z = c_ref[...] * u_ref[...]
        z_prev = jnp.where(i > 0, cp_ref[...] * up_ref[...], 0.0)
        row = lax.broadcasted_iota(jnp.int32, (tt, 1), 0)
        zm1, zm2 = _conv_taps(z, z_prev, row)
        y = cw_ref[0:1] * zm2 + cw_ref[1:2] * zm1 + cw_ref[2:3] * z + cb_ref[...]
        o_ref[...] = (b_ref[...] * y).astype(BF16)

    slab = lambda s: pl.BlockSpec((tt, SLAB), lambda i, s=s: (i, s))
    prev = lambda s: pl.BlockSpec((SUBLANES, SLAB), lambda i, s=s: (jnp.maximum(i * tb - 1, 0), s))
    return pl.pallas_call(
        body, name="conv_fwd", grid=(t // tt,),
        in_specs=[slab(0), slab(1), slab(2), prev(1), prev(2),
                  pl.BlockSpec((SUBLANES, SLAB), lambda i: (0, 0)), pl.BlockSpec((1, SLAB), lambda i: (0, 0))],
        out_specs=pl.BlockSpec((tt, SLAB), lambda i: (i, 0)),
        out_shape=jax.ShapeDtypeStruct((t, SLAB), BF16),
        compiler_params=_cparams("parallel"),
    )(proj, proj, proj, proj, proj, cw, cb)


def _conv_bwd(proj, dycat, cw, cb, tt):
    t = proj.shape[0]
    tb = tt // SUBLANES
    nblk = t // SUBLANES

    def body(b_ref, c_ref, u_ref, cp_ref, up_ref, bn_ref, dy_ref, dyn_ref, cw_ref, cb_ref,
             db_ref, dc_ref, du_ref, part_ref):
        i = pl.program_id(0)
        c, u, b, dyc = c_ref[...], u_ref[...], b_ref[...], dy_ref[...]
        z = c * u
        z_prev = jnp.where(i > 0, cp_ref[...] * up_ref[...], 0.0)
        row = lax.broadcasted_iota(jnp.int32, (tt, 1), 0)
        zm1, zm2 = _conv_taps(z, z_prev, row)
        w0, w1, w2 = cw_ref[0:1], cw_ref[1:2], cw_ref[2:3]
        y = w0 * zm2 + w1 * zm1 + w2 * z + cb_ref[...]
        db_ref[...] = (dyc * y).astype(BF16)
        g = dyc * b
        g_next = jnp.where(i < pl.num_programs(0) - 1, dyn_ref[...] * bn_ref[...], 0.0)
        gp1 = jnp.where(row == tt - 1, g_next[0:1], pltpu.roll(g, tt - 1, 0))
        gp2 = jnp.where(row == tt - 2, g_next[0:1], jnp.where(row == tt - 1, g_next[1:2], pltpu.roll(g, tt - 2, 0)))
        dz = w2 * g + w1 * gp1 + w0 * gp2
        dc_ref[...] = (dz * u).astype(BF16)
        du_ref[...] = (dz * c).astype(BF16)
        part_ref[...] = jnp.concatenate([_fold8(g * zm2), _fold8(g * zm1), _fold8(g * z), _fold8(g)], axis=1)

    slab = lambda s: pl.BlockSpec((tt, SLAB), lambda i, s=s: (i, s))
    prev = lambda s: pl.BlockSpec((SUBLANES, SLAB), lambda i, s=s: (jnp.maximum(i * tb - 1, 0), s))
    nxt = lambda s: pl.BlockSpec((SUBLANES, SLAB), lambda i, s=s: (jnp.minimum((i + 1) * tb, nblk - 1), s))
    tile = pl.BlockSpec((tt, SLAB), lambda i: (i, 0))
    return pl.pallas_call(
        body, name="conv_bwd", grid=(t // tt,),
        in_specs=[slab(0), slab(1), slab(2), prev(1), prev(2), nxt(0), slab(0), nxt(0),
                  pl.BlockSpec((SUBLANES, SLAB), lambda i: (0, 0)), pl.BlockSpec((1, SLAB), lambda i: (0, 0))],
        out_specs=[tile, tile, tile, pl.BlockSpec((SUBLANES, 4 * SLAB), lambda i: (i, 0))],
        out_shape=[jax.ShapeDtypeStruct((t, SLAB), BF16)] * 3
        + [jax.ShapeDtypeStruct((t // tt * SUBLANES, 4 * SLAB), F32)],
        compiler_params=_cparams("parallel"),
    )(proj, proj, proj, proj, proj, proj, dycat, dycat, cw, cb)


def _tri_masks(n):
    r = lax.broadcasted_iota(jnp.int32, (n, n), 0)
    c = lax.broadcasted_iota(jnp.int32, (n, n), 1)
    return (r > c).astype(BF16), (r >= c).astype(BF16)


def _sb_logits(z, causal):
    softplus = jnp.maximum(z, 0.0) + jnp.log(1.0 + jnp.exp(-jnp.abs(z)))
    lk = -softplus
    if causal is not None:
        lk = jnp.where(causal, lk, 0.0)
    return (z, lk, *_split2(lk))


def _sb_finish(z, lk, hi, lo, r_run, tri, causal, later):
    later = later + r_run
    ls = z + lk
    arg = ls + later
    if causal is not None:
        arg = jnp.where(causal, arg, -1e30)
    return lk, ls, jnp.exp(arg), later[:, 0:1] + lk[:, 0:1]


SB_DEAD_LOG = -111.0
CHAINS = ((0, 0), (0, 1), (1, 0), (1, 1))


def _all_dead(r_runs):
    m = r_runs[0]
    for r in r_runs[1:]:
        m = jnp.maximum(m, r)
    return (jnp.max(m) < SB_DEAD_LOG).astype(jnp.int32)


def _attn_fwd(qn, kn, vb, tri, sb, comm=None):
    t = qn.shape[0]
    bq = 2 * sb
    scale = HEAD_DIM ** -0.5

    def body(*refs):
        (q_ref, k_ref, v_ref, tri_ref), (o_ref, ob_ref), (acc_ref,), c_refs = _split_refs(refs, 4, 2, 1, comm)
        qi = pl.program_id(1)
        if comm:
            hp = pl.program_id(0)
            comm.attach(c_refs, jnp.logical_and(hp == 0, qi == 0),
                        jnp.logical_and(hp == pl.num_programs(0) - 1, qi == pl.num_programs(1) - 1))
        lane = lax.broadcasted_iota(jnp.int32, (1, LANES), 1)
        hmasks = (lane < HEAD_DIM, lane >= HEAD_DIM)
        diag = lax.broadcasted_iota(jnp.int32, (sb, sb), 1) < lax.broadcasted_iota(jnp.int32, (sb, sb), 0)
        triv = tri_ref[...]
        qs = [jnp.where(hmasks[hh], q_ref[pl.ds(s * sb, sb), :], 0) * scale for s, hh in CHAINS]
        acc_ref[...] = jnp.zeros_like(acc_ref)

        def load_kv(kb):
            ks = pl.multiple_of(kb * sb, sb)
            vraw = v_ref[pl.ds(ks, sb), :]
            return k_ref[pl.ds(ks, sb), :], [jnp.where(hm, vraw, 0) for hm in hmasks]

        def run_tiles(tiles, r_in):
            zs = [_dot_nt(qs[c], kv[0]) for c, kv, _, _ in tiles]
            mids = [_sb_logits(z, causal) for z, (_, _, causal, _) in zip(zs, tiles)]
            laters = [_dot_nn(m[2], triv) + _dot_nn(m[3], triv) for m in mids]
            outs = []
            for m, later, (c, _, causal, dep) in zip(mids, laters, tiles):
                outs.append(_sb_finish(*m, r_in[c] if dep is None else outs[dep][3], triv, causal, later))
            for o, (c, kv, _, _) in zip(outs, tiles):
                acc_ref[c] += _dot_nn(o[2].astype(BF16), kv[1][c % 2])
            return [o[3] for o in outs]

        zero = jnp.zeros((sb, 1), F32)
        kv_hi, kv_lo = load_kv(2 * qi + 1), load_kv(2 * qi)
        first = ([(2 + hh, kv_hi, diag, None) for hh in range(2)]
                 + [(2 + hh, kv_lo, None, hh) for hh in range(2)] + [(hh, kv_lo, diag, None) for hh in range(2)])
        r_first = run_tiles(first, [zero] * 4)
        r_runs = [r_first[4], r_first[5], r_first[2], r_first[3]]

        def step(carry):
            i, _, *rs = carry
            kv = load_kv(2 * qi - 1 - i)
            rs = run_tiles([(c, kv, None, None) for c in range(4)], rs)
            return (i + 1, _all_dead(rs), *rs)

        lax.while_loop(lambda c: jnp.logical_and(c[0] < 2 * qi, c[1] == 0), step,
                       (jnp.int32(0), _all_dead(r_runs), *r_runs))
        for s in range(2):
            out = acc_ref[2 * s] + acc_ref[2 * s + 1]
            o_ref[pl.ds(s * sb, sb), :] = out
            ob_ref[pl.ds(s * sb, sb), :] = out.astype(BF16)

    qspec = pl.BlockSpec((bq, LANES), lambda h, i: (i, h))
    kspec = pl.BlockSpec((t, LANES), lambda h, i: (0, h))
    in_specs, out_specs, out_shape, scratch, extra = _with_comm(
        comm, [qspec, kspec, kspec, pl.BlockSpec((sb, sb), lambda h, i: (0, 0))], [qspec, qspec],
        [jax.ShapeDtypeStruct((t, SB_DIM), F32), jax.ShapeDtypeStruct((t, SB_DIM), BF16)],
        [pltpu.VMEM((4, sb, LANES), F32)])
    return pl.pallas_call(
        body, name="attn_fwd", grid=(SB_DIM // LANES, t // bq),
        in_specs=in_specs, out_specs=out_specs, out_shape=out_shape, scratch_shapes=scratch,
        compiler_params=_cparams("arbitrary" if comm else "parallel", "arbitrary"),
    )(qn, kn, vb, tri, *extra)


def _attn_bwd(qn, kn, vb, o, dycat, tri, tri_inc, sb, comm=None):
    t = qn.shape[0]
    bq = 2 * sb
    scale = HEAD_DIM ** -0.5

    def body(*refs):
        ins, (dq_ref, dk_ref, dv_ref), (dq_acc,), c_refs = _split_refs(refs, 7, 3, 1, comm)
        q_ref, k_ref, v_ref, o_ref, do_ref, tri_ref, tinc_ref = ins
        qi = pl.program_id(1)
        if comm:
            hp = pl.program_id(0)
            comm.attach(c_refs, jnp.logical_and(hp == 0, qi == 0),
                        jnp.logical_and(hp == pl.num_programs(0) - 1, qi == pl.num_programs(1) - 1))

        @pl.when(qi == 0)
        def _():
            dk_ref[...] = jnp.zeros_like(dk_ref)
            dv_ref[...] = jnp.zeros_like(dv_ref)

        lane = lax.broadcasted_iota(jnp.int32, (1, LANES), 1)
        hmasks = (lane < HEAD_DIM, lane >= HEAD_DIM)
        diag = lax.broadcasted_iota(jnp.int32, (sb, sb), 1) < lax.broadcasted_iota(jnp.int32, (sb, sb), 0)
        triv, tincv = tri_ref[...], tinc_ref[...]
        qs, dobs, d_rows = [], [], []
        for s, hh in CHAINS:
            rows = pl.ds(s * sb, sb)
            qs.append(jnp.where(hmasks[hh], q_ref[rows, :], 0) * scale)
            dobs.append(jnp.where(hmasks[hh], do_ref[rows, :], 0.0).astype(BF16))
            d_rows.append(jnp.sum(dobs[-1].astype(F32) * o_ref[rows, :], axis=1, keepdims=True))
        dq_acc[...] = jnp.zeros_like(dq_acc)

        def load_kv(kb):
            ks = pl.multiple_of(kb * sb, sb)
            return k_ref[pl.ds(ks, sb), :], v_ref[pl.ds(ks, sb), :], ks

        def run_tiles(tiles, r_in, g_in):
            zs = [_dot_nt(qs[c], kv[0]) for c, kv, _, _ in tiles]
            das = [_dot_nt(dobs[c], kv[1]) for c, kv, _, _ in tiles]
            mids = [_sb_logits(z, causal) for z, (_, _, causal, _) in zip(zs, tiles)]
            laters = [_dot_nn(m[2], triv) + _dot_nn(m[3], triv) for m in mids]
            fins, abs_, es = [], [], []
            for m, later, da, (c, _, causal, dep) in zip(mids, laters, das, tiles):
                fins.append(_sb_finish(*m, r_in[c] if dep is None else fins[dep][3], triv, causal, later))
                abs_.append(fins[-1][2].astype(BF16))
                es.append(da * abs_[-1].astype(F32))
            splits = [_split2(e) for e in es]
            e_sums = [_dot_nn(hi, tincv) + _dot_nn(lo, tincv) for hi, lo in splits]
            e_froms, dzbs = [], []
            for e, e_sum, fin, (c, _, causal, dep) in zip(es, e_sums, fins, tiles):
                e_froms.append(e_sum + (g_in[c] if dep is None else e_froms[dep][:, 0:1]))
                dz = e - jnp.exp(fin[1]) * (e + (d_rows[c] - e_froms[-1]))
                if causal is not None:
                    dz = jnp.where(causal, dz, 0.0)
                dzbs.append(dz.astype(BF16))
            for dzb, (c, kv, _, _) in zip(dzbs, tiles):
                dq_acc[c] += _dot_nn(dzb, kv[0])
            by_rows = {}
            for dzb, ab, (c, kv, _, _) in zip(dzbs, abs_, tiles):
                by_rows.setdefault(id(kv), (kv[2], []))[1].append((_dot_tn(dzb, qs[c]), _dot_tn(ab, dobs[c])))
            for first_row, parts in by_rows.values():
                rows = pl.ds(first_row, sb)
                dk_ref[rows, :] += sum(p[0] for p in parts[1:]) + parts[0][0]
                dv_ref[rows, :] += sum(p[1] for p in parts[1:]) + parts[0][1]
            return [f[3] for f in fins], [ef[:, 0:1] for ef in e_froms]

        zero = jnp.zeros((sb, 1), F32)
        kv_hi, kv_lo = load_kv(2 * qi + 1), load_kv(2 * qi)
        first = ([(2 + hh, kv_hi, diag, None) for hh in range(2)]
                 + [(2 + hh, kv_lo, None, hh) for hh in range(2)] + [(hh, kv_lo, diag, None) for hh in range(2)])
        r_first, g_first = run_tiles(first, [zero] * 4, [zero] * 4)
        order = (4, 5, 2, 3)
        r_runs, g_runs = [r_first[j] for j in order], [g_first[j] for j in order]

        def step(carry):
            i, _, *rg = carry
            kv = load_kv(2 * qi - 1 - i)
            rs, gs = run_tiles([(c, kv, None, None) for c in range(4)], rg[:4], rg[4:])
            return (i + 1, _all_dead(rs), *rs, *gs)

        lax.while_loop(lambda c: jnp.logical_and(c[0] < 2 * qi, c[1] == 0), step,
                       (jnp.int32(0), _all_dead(r_runs), *r_runs, *g_runs))
        for s in range(2):
            dq_ref[pl.ds(s * sb, sb), :] = jnp.where(hmasks[0], dq_acc[2 * s], dq_acc[2 * s + 1]) * scale

    qspec = pl.BlockSpec((bq, LANES), lambda h, i: (i, h))
    dospec = pl.BlockSpec((bq, LANES), lambda h, i: (i, h + CONV_DIM // LANES))
    kspec = pl.BlockSpec((t, LANES), lambda h, i: (0, h))
    full = pl.BlockSpec((sb, sb), lambda h, i: (0, 0))
    in_specs, out_specs, out_shape, scratch, extra = _with_comm(
        comm, [qspec, kspec, kspec, qspec, dospec, full, full], [qspec, kspec, kspec],
        [jax.ShapeDtypeStruct((t, SB_DIM), F32)] * 3, [pltpu.VMEM((4, sb, LANES), F32)])
    return pl.pallas_call(
        body, name="attn_bwd", grid=(SB_DIM // LANES, t // bq),
        in_specs=in_specs, out_specs=out_specs, out_shape=out_shape, scratch_shapes=scratch,
        compiler_params=_cparams("arbitrary" if comm else "parallel", "arbitrary"),
    )(qn, kn, vb, o, dycat, tri, tri_inc, *extra)


def _ple_loss(x3, p2, tgt, gain, wpg, wppt, tt):
    t, d = x3.shape
    pdim = p2.shape[1]
    nt = t // tt

    def body(x_ref, p_ref, t_ref, g_ref, wg_ref, wp_ref,
             dx_ref, dxb_ref, dwg_ref, dwp_ref, gpart_ref, lpart_ref, accg_ref, accp_ref):
        i = pl.program_id(0)
        xv, gain_v = x_ref[...], g_ref[...]
        hb = ((xv * _rms_stats(xv)) * gain_v).astype(BF16)
        gate = jax.nn.sigmoid(_dot_nn(hb, wg_ref[...]))
        pb = p_ref[...].astype(BF16)
        pe = _dot_nt(pb, wp_ref[...])
        diff = xv + gate * pe - t_ref[...]
        lsum = jnp.sum(_fold8(diff * diff), axis=1, keepdims=True) * (0.5 / d)
        lpart_ref[...] = jnp.broadcast_to(lsum, (SUBLANES, LANES))
        dy = diff * (1.0 / d)
        dgz = ((dy * pe) * gate * (1.0 - gate)).astype(BF16)
        dpe = (dy * gate).astype(BF16)
        dx_n, grow = _rms_bwd(_dot_nt(dgz, wg_ref[...]), xv, gain_v)
        dx = dy + dx_n
        dx_ref[...] = dx
        dxb_ref[...] = dx.astype(BF16)
        gpart_ref[...] = _fold8(grow)
        sg = _dot_tn(hb, dgz)
        sp = _dot_tn(dpe, pb)

        @pl.when(i == 0)
        def _():
            accg_ref[...] = sg
            accp_ref[...] = sp

        @pl.when(i > 0)
        def _():
            accg_ref[...] += sg
            accp_ref[...] += sp

        @pl.when(i == nt - 1)
        def _():
            dwg_ref[...] = accg_ref[...].astype(BF16)
            dwp_ref[...] = accp_ref[...].astype(BF16)

    tile = lambda w: pl.BlockSpec((tt, w), lambda i: (i, 0))
    full = lambda shape: pl.BlockSpec(shape, lambda i: (0, 0))
    return pl.pallas_call(
        body, name="ple_loss", grid=(nt,),
        in_specs=[tile(d), tile(pdim), tile(d), full((1, d)),
                  pl.BlockSpec((d, d), lambda i: (0, 0), pipeline_mode=pl.Buffered(1)),
                  pl.BlockSpec((d, pdim), lambda i: (0, 0), pipeline_mode=pl.Buffered(1))],
        out_specs=[tile(d), tile(d), full((d, d)), full((d, pdim)),
                   pl.BlockSpec((SUBLANES, d), lambda i: (i, 0)), pl.BlockSpec((SUBLANES, LANES), lambda i: (i, 0))],
        out_shape=[jax.ShapeDtypeStruct((t, d), F32), jax.ShapeDtypeStruct((t, d), BF16),
                   jax.ShapeDtypeStruct((d, d), BF16), jax.ShapeDtypeStruct((d, pdim), BF16),
                   jax.ShapeDtypeStruct((nt * SUBLANES, d), F32), jax.ShapeDtypeStruct((nt * SUBLANES, LANES), F32)],
        scratch_shapes=[pltpu.VMEM((d, d), F32), pltpu.VMEM((d, pdim), F32)],
        compiler_params=_cparams("arbitrary"),
    )(x3, p2, tgt, gain, wpg, wppt)


def _pack_small(parts_gain, conv_part, qk_part):
    d = parts_gain[0].shape[1]
    ng = len(parts_gain)

    def body(*refs):
        g_refs, conv_ref, qk_ref, o_ref = refs[:ng], refs[ng], refs[ng + 1], refs[ng + 2]
        rows = [jnp.sum(r[...], axis=0, keepdims=True) for r in g_refs]
        cs = jnp.sum(conv_ref[...], axis=0, keepdims=True)
        qs = jnp.sum(qk_ref[...], axis=0, keepdims=True)
        rows.append(jnp.concatenate([cs[:, 3 * SLAB:], cs[:, :SLAB]], axis=1))
        rows.append(cs[:, SLAB:3 * SLAB])
        rows.append(qs)
        rid = lax.broadcasted_iota(jnp.int32, (2 * SUBLANES, 1), 0)
        out = jnp.zeros((2 * SUBLANES, d), F32)
        for idx, r in enumerate(rows):
            out = jnp.where(rid == idx, r, out)
        o_ref[...] = out

    return pl.pallas_call(
        body, name="pack_small", out_shape=jax.ShapeDtypeStruct((2 * SUBLANES, d), F32),
    )(*parts_gain, conv_part, qk_part)


def _sum_slots(name, slots, out_dtype=F32):
    _, r, c = slots.shape

    def body(s_ref, o_ref):
        acc = s_ref[0].astype(F32)
        for d in range(1, N_DEV):
            acc = acc + s_ref[d].astype(F32)
        o_ref[...] = acc.astype(o_ref.dtype)

    return pl.pallas_call(body, name=name, out_shape=jax.ShapeDtypeStruct((r, c), out_dtype),
                          compiler_params=pltpu.CompilerParams(vmem_limit_bytes=VMEM_LIMIT_BYTES))(slots)


def _adamw(name, w, g, m, v):
    c1 = 1.0 - ADAM_B1 ** ADAM_STEP
    c2 = 1.0 - ADAM_B2 ** ADAM_STEP

    def body(w_ref, g_ref, m_ref, v_ref, d_ref, nm_ref, nv_ref):
        gv = g_ref[...]
        nm = ADAM_B1 * m_ref[...] + (1.0 - ADAM_B1) * gv
        nv = ADAM_B2 * v_ref[...] + (1.0 - ADAM_B2) * (gv * gv)
        d_ref[...] = -ADAM_LR * ((nm / c1) / (jnp.sqrt(nv / c2) + ADAM_EPS) + ADAM_WD * w_ref[...])
        nm_ref[...] = nm
        nv_ref[...] = nv

    return pl.pallas_call(body, name=name, out_shape=[jax.ShapeDtypeStruct(w.shape, F32)] * 3,
                          compiler_params=pltpu.CompilerParams(vmem_limit_bytes=VMEM_LIMIT_BYTES))(w, g, m, v)


def _any_specs(n):
    return [pl.BlockSpec(memory_space=pl.ANY)] * n


def _all_gather(name, shards):
    n = len(shards)

    def body(*refs):
        ins, outs = refs[:n], refs[n:2 * n]
        send_sems, recv_sems, local_sems = refs[2 * n:]
        x, y, c = (lax.axis_index(a) for a in MESH_AXES)
        me, sibling = (x, y, c), (x, y, 1 - c)
        chips = [(1 - x, y), (x, 1 - y), (1 - x, 1 - y)]

        def rows(a, px, py, pc):
            r = ins[a].shape[0]
            return outs[a].at[pl.ds((4 * px + 2 * py + pc) * r, r), :]

        def copy(a, k, block, to, src=None):
            return pltpu.make_async_remote_copy(
                src_ref=rows(a, *block) if src is None else src, dst_ref=rows(a, *block),
                send_sem=send_sems.at[7 * a + k], recv_sem=recv_sems.at[7 * a + k],
                device_id=to, device_id_type=MESH)

        mine = [pltpu.make_async_copy(ins[a], rows(a, *me), local_sems.at[a]) for a in range(n)]
        for cp in mine:
            cp.start()
        first = []
        for a in range(n):
            first.append(copy(a, 0, me, sibling, src=ins[a]))
            first += [copy(a, 1 + j, me, (*chip, c), src=ins[a]) for j, chip in enumerate(chips)]
        for cp in first:
            cp.start()
        passed = []
        for j, chip in enumerate(chips):
            for a in range(n):
                copy(a, 1 + j, (*chip, c), me).wait_recv()
                fwd = copy(a, 4 + j, (*chip, c), sibling)
                fwd.start()
                passed.append(fwd)
        for a in range(n):
            copy(a, 0, sibling, me).wait_recv()
            for j, chip in enumerate(chips):
                copy(a, 4 + j, (*chip, 1 - c), me).wait_recv()
        for cp in first + passed:
            cp.wait_send()
        for cp in mine:
            cp.wait()

    return pl.pallas_call(
        body, name=name, in_specs=_any_specs(n), out_specs=_any_specs(n),
        out_shape=[jax.ShapeDtypeStruct((N_DEV * s.shape[0], s.shape[1]), s.dtype) for s in shards],
        scratch_shapes=[pltpu.SemaphoreType.DMA((7 * n,)), pltpu.SemaphoreType.DMA((7 * n,)),
                        pltpu.SemaphoreType.DMA((n,))],
    )(*shards)


def _residual_and_norm(res_scale):
    def epilogue(acc, rows, fulls):
        out = rows[0] + res_scale * acc
        return [out] + [(out * _rms_stats(out)) * gain for gain in fulls]
    return epilogue


def _ffn_fwd(tag, x, h, wgt, wut, wd, next_gain, tt_nt, tt_nn, comm_gate=None, comm_down=None):
    f = wgt.shape[0]

    def gate_up(accs, _):
        return [accs[0], accs[1], jax.nn.silu(accs[0]) * accs[1]]

    g, u, a, *got_gate = _mm_nt(f"{tag}_gate_up", [h], [wgt, wut], [(0, 0), (0, 1)], gate_up, [BF16] * 3,
                                tt_nn, _pick(f, 256), comm_gate)
    if wd is None:
        wd = got_gate[0]
    gains = [] if next_gain is None else [next_gain]
    out, *rest = _mm_nn(f"{tag}_down", [(a, wd, 0)], [x], gains, _residual_and_norm(FFN_RES),
                        [("tile", F32)] + [("tile", BF16)] * len(gains), tt_nt, f, comm_down)
    h_next = rest.pop(0) if gains else None
    return out, h_next, (g, u, a), got_gate, rest


def _norm_bwd_epilogue(acc, rows, fulls):
    x_in, dy = rows
    dx_n, grow = _rms_bwd(acc, x_in, fulls[0])
    dx = dy + dx_n
    return [dx, dx, _fold8(grow)]


_NORM_BWD_OUTS = [("tile", F32), ("tile", BF16), ("part", F32)]


def _ffn_bwd(tag, x_in, h, hidden, dy, dyb, gain, wgt, wut, wd, tt_nt, tt_nn, riders=(None, None),
             exchange_own=False):
    g, u, a = hidden
    f = wgt.shape[0]
    tt_tn = _pick(h.shape[0], 2 * tt_nt)
    own = (lambda arr: _Exchange([arr], gather=False)) if exchange_own else (lambda arr: None)

    def carried(result, rider):
        return (result[0], result[1:]) if rider else (result, [])

    def hidden_grads(accs, tiles):
        da = FFN_RES * accs[0]
        gv, uv = tiles[0].astype(F32), tiles[1].astype(F32)
        sg = jax.nn.sigmoid(gv)
        s = gv * sg
        return [da * uv * (sg * (1.0 + gv * (1.0 - sg))), da * s]

    dwd, got_dwd = carried(_mm_tn(f"{tag}_dwd", a, dyb, FFN_RES, f // 2, tt_tn, riders[0]), riders[0])
    dg, du, *x_dwd = _mm_nt(f"{tag}_bwd_hidden", [dyb], [wd], [(0, 0)], hidden_grads, [BF16, BF16],
                            tt_nn, _pick(f, 256), own(dwd), tiles=[g, u])
    dwg, got_dwg = carried(_mm_tn(f"{tag}_dwg", dg, h, 1.0, f // 2, tt_tn, riders[1]), riders[1])
    dwu, x_dwg = carried(_mm_tn(f"{tag}_dwu", du, h, 1.0, f // 2, tt_tn, own(dwg)), exchange_own)
    dx, dxb, gpart, *x_dwu = _mm_nn(f"{tag}_bwd_dx", [(dg, wgt, 0), (du, wut, 0)], [x_in, dy], [gain],
                                    _norm_bwd_epilogue, _NORM_BWD_OUTS, tt_nn, f, own(dwu))
    grads = [*x_dwg, *x_dwu, *x_dwd] if exchange_own else [dwg, dwu, dwd]
    return dx, dxb, gpart, grads, [*got_dwd, *got_dwg]


def kernel(x, p, ffn1_norm, ffn1_w_gate, ffn1_w_up, ffn1_w_down, mix_norm, w_in, conv_w, conv_b, q_norm, k_norm, w_out, ffn2_norm, ffn2_w_gate, ffn2_w_up, ffn2_w_down, ple_norm, ple_w_gate, ple_w_proj, loss_target, m_ffn1_norm, m_ffn1_w_gate, m_ffn1_w_up, m_ffn1_w_down, m_mix_norm, m_w_in, m_conv_w, m_conv_b, m_q_norm, m_k_norm, m_w_out, m_ffn2_norm, m_ffn2_w_gate, m_ffn2_w_up, m_ffn2_w_down, m_ple_norm, m_ple_w_gate, m_ple_w_proj, v_ffn1_norm, v_ffn1_w_gate, v_ffn1_w_up, v_ffn1_w_down, v_mix_norm, v_w_in, v_conv_w, v_conv_b, v_q_norm, v_k_norm, v_w_out, v_ffn2_norm, v_ffn2_w_gate, v_ffn2_w_up, v_ffn2_w_down, v_ple_norm, v_ple_w_gate, v_ple_w_proj):
    x0, p2, tgt = x[0], p[0, 0], loss_target[0]
    t, d = x0.shape
    tt_nt = _pick(t, 1024)
    tt_nn = _pick(t, 512)
    tt_ew = _pick(t, 512)
    tt_ple = _pick(t, 512)
    sb = _pick(t // 2, 256)

    t_bf = lambda w: w[0].T.astype(BF16)
    n_bf = lambda w: w[0].astype(BF16)
    cw_tile = jnp.zeros((SUBLANES, LANES), F32).at[:conv_w.shape[1], :conv_w.shape[2]].set(conv_w[0])
    wg1t, wu1t = _all_gather("gather_ffn1_weights", [t_bf(ffn1_w_gate), t_bf(ffn1_w_up)])
    gather_down = _Exchange([n_bf(ffn1_w_down), n_bf(w_out), cw_tile], gather=True)
    gather_in = _Exchange([t_bf(w_in)], gather=True)
    gather_late = _Exchange([t_bf(ffn2_w_gate), t_bf(ffn2_w_up), n_bf(ffn2_w_down), n_bf(ple_w_gate),
                             t_bf(ple_w_proj)], gather=True)
    ncs = conv_w.shape[2]

    qg = jnp.tile(q_norm, (1, SB_DIM // HEAD_DIM))
    kg = jnp.tile(k_norm, (1, SB_DIM // HEAD_DIM))
    gi = lax.broadcasted_iota(jnp.int32, (SLAB, SLAB), 0) // HEAD_DIM
    gj = lax.broadcasted_iota(jnp.int32, (SLAB, SLAB), 1) // HEAD_DIM
    bd = (gi == gj).astype(BF16)
    tri, tri_inc = _tri_masks(sb)

    h1 = _rmsnorm("ffn1_norm", x0, ffn1_norm, tt_ew)
    x1, h2, hidden1, (wd1, wout, cw_all), (wint,) = _ffn_fwd(
        "ffn1", x0, h1, wg1t, wu1t, None, mix_norm, tt_nt, tt_nn, gather_down, gather_in)
    cw_full = cw_all.reshape(N_DEV, SUBLANES, LANES)[:, :, :ncs].transpose(1, 0, 2).reshape(SUBLANES, N_DEV * ncs)
    (proj,) = _mm_nt("in_proj", [h2], [wint], [(0, 0)], lambda accs, _: accs, [F32], tt_nn, SLAB)
    y_conv = _conv_fwd(proj, cw_full, conv_b, tt_ew)
    qn, kn, vb = _qknorm_fwd(proj, qg, kg, bd, tt_ew)
    o, ob, wg2t, wu2t, wd2, wpg, wppt = _attn_fwd(qn, kn, vb, tri, sb, gather_late)
    x2, h3 = _mm_nn("out_proj", [(y_conv, wout, 0), (ob, wout, 1)], [x1], [ffn2_norm], _residual_and_norm(1.0),
                    [("tile", F32), ("tile", BF16)], tt_nn, SLAB)
    x3, _, hidden2, _, _ = _ffn_fwd("ffn2", x2, h3, wg2t, wu2t, wd2, None, tt_nt, tt_nn)

    dx3, dx3b, dwpg, dwppt, gp_ple, lpart = _ple_loss(x3, p2, tgt, ple_norm, wpg, wppt, tt_ple)
    loss = lax.psum(jnp.sum(lpart[:, 0]), MESH_AXES)
    dx2, dx2b, gp_ffn2, (dwg2, dwu2, dwd2), _ = _ffn_bwd("ffn2", x2, h3, hidden2, dx3, dx3b, ffn2_norm,
                                                         wg2t, wu2t, wd2, tt_nt, tt_nn)
    (dycat,) = _mm_nt("out_proj_bwd", [dx2b], [wout], [(0, 0)], lambda accs, _: accs, [F32], tt_nn, SLAB)
    dwout = _mm_tn_slabs("dwout", [y_conv, ob], dx2b, tt_nt)
    dqn, dkn, dv, *slots_late = _attn_bwd(qn, kn, vb, o, dycat, tri, tri_inc, sb,
                                          _Exchange([dwg2, dwu2, dwd2, dwpg, dwppt], gather=False))
    dq, dk, dvb, qk_part = _qknorm_bwd(proj, dqn, dkn, dv, qg, kg, bd, tt_ew)
    db, dc, du, conv_part = _conv_bwd(proj, dycat, cw_full, conv_b, tt_ew)
    dproj = [db, dc, du, dq, dk, dvb]
    dwin, slot_wout = _mm_tn_slabs("dwin", dproj, h2, tt_nt, _Exchange([dwout], gather=False))
    dx1, dx1b, gp_mix, slot_win = _mm_nn(
        "in_proj_bwd", [(dp, wint, s) for s, dp in enumerate(dproj)], [x1, dx2], [mix_norm],
        _norm_bwd_epilogue, _NORM_BWD_OUTS, tt_nn, SLAB, _Exchange([dwin], gather=False))
    dx0, _, gp_ffn1, slots_ffn1, _ = _ffn_bwd(
        "ffn1", x0, h1, hidden1, dx1, dx1b, ffn1_norm, wg1t, wu1t, wd1, tt_nt, tt_nn, exchange_own=True)

    slots = [*slots_ffn1, slot_win, slot_wout, *slots_late]
    sums = [_sum_slots(f"sum_grads_{i}", s.reshape(N_DEV, s.shape[0] // N_DEV, s.shape[1]))
            for i, s in enumerate(slots)]
    g_wg1, g_wu1, g_wd1, g_win, g_wout, g_wg2, g_wu2, g_wd2, g_wpg, g_wpp = sums
    small = _pack_small([gp_ffn1, gp_mix, gp_ffn2, gp_ple], conv_part, qk_part)
    (small_all,) = _all_gather("gather_small_grads", [small])
    sm = _sum_slots("sum_small_grads", small_all.reshape(N_DEV, 2 * SUBLANES, d))
    fold = lambda r: r.reshape(SB_DIM // HEAD_DIM, HEAD_DIM).sum(axis=0)[None]
    me_idx = 4 * lax.axis_index("x") + 2 * lax.axis_index("y") + lax.axis_index("c")
    cw_grad = jnp.stack([sm[4, SLAB:], sm[5, :SLAB], sm[5, SLAB:]])
    grads = {
        "ffn1_norm": sm[0:1], "ffn1_w_gate": g_wg1.T, "ffn1_w_up": g_wu1.T, "ffn1_w_down": g_wd1,
        "mix_norm": sm[1:2], "w_in": g_win.T, "conv_w": lax.dynamic_slice_in_dim(cw_grad, me_idx * ncs, ncs, axis=1),
        "conv_b": sm[4:5, :SLAB], "q_norm": fold(sm[6, :SLAB]), "k_norm": fold(sm[6, SLAB:]),
        "w_out": g_wout, "ffn2_norm": sm[2:3], "ffn2_w_gate": g_wg2.T, "ffn2_w_up": g_wu2.T, "ffn2_w_down": g_wd2,
        "ple_norm": sm[3:4], "ple_w_gate": g_wpg, "ple_w_proj": g_wpp.T,
    }

    weights = dict(ffn1_norm=ffn1_norm, ffn1_w_gate=ffn1_w_gate, ffn1_w_up=ffn1_w_up, ffn1_w_down=ffn1_w_down,
                   mix_norm=mix_norm, w_in=w_in, conv_w=conv_w, conv_b=conv_b, q_norm=q_norm, k_norm=k_norm,
                   w_out=w_out, ffn2_norm=ffn2_norm, ffn2_w_gate=ffn2_w_gate, ffn2_w_up=ffn2_w_up,
                   ffn2_w_down=ffn2_w_down, ple_norm=ple_norm, ple_w_gate=ple_w_gate, ple_w_proj=ple_w_proj)
    m_in = dict(ffn1_norm=m_ffn1_norm, ffn1_w_gate=m_ffn1_w_gate, ffn1_w_up=m_ffn1_w_up, ffn1_w_down=m_ffn1_w_down,
                mix_norm=m_mix_norm, w_in=m_w_in, conv_w=m_conv_w, conv_b=m_conv_b, q_norm=m_q_norm,
                k_norm=m_k_norm, w_out=m_w_out, ffn2_norm=m_ffn2_norm, ffn2_w_gate=m_ffn2_w_gate,
                ffn2_w_up=m_ffn2_w_up, ffn2_w_down=m_ffn2_w_down, ple_norm=m_ple_norm, ple_w_gate=m_ple_w_gate,
                ple_w_proj=m_ple_w_proj)
    v_in = dict(ffn1_norm=v_ffn1_norm, ffn1_w_gate=v_ffn1_w_gate, ffn1_w_up=v_ffn1_w_up, ffn1_w_down=v_ffn1_w_down,
                mix_norm=v_mix_norm, w_in=v_w_in, conv_w=v_conv_w, conv_b=v_conv_b, q_norm=v_q_norm,
                k_norm=v_k_norm, w_out=v_w_out, ffn2_norm=v_ffn2_norm, ffn2_w_gate=v_ffn2_w_gate,
                ffn2_w_up=v_ffn2_w_up, ffn2_w_down=v_ffn2_w_down, ple_norm=v_ple_norm, ple_w_gate=v_ple_w_gate,
                ple_w_proj=v_ple_w_proj)
    g_out, d_out, m_out, v_out = [], [], [], []
    for name, w in weights.items():
        w2 = w.reshape(w.shape[-2:])
        g2 = grads[name].reshape(w2.shape)
        dlt, nm, nv = _adamw(f"adamw_{name}", w2, g2, m_in[name].reshape(w2.shape), v_in[name].reshape(w2.shape))
        g_out.append(g2.reshape(w.shape))
        d_out.append(dlt.reshape(w.shape))
        m_out.append(nm.reshape(w.shape))
        v_out.append(nv.reshape(w.shape))
    return (loss, dx0[None], *g_out, *d_out, *m_out, *v_out)
```

```python
import jax
import jax.numpy as jnp
from jax import lax
from jax.experimental import pallas as pl
from jax.experimental.pallas import tpu as pltpu

F32 = jnp.float32
BF16 = jnp.bfloat16

EPS = 1e-6
FFN_RES = 0.5
HEAD_DIM = 64
CONV_DIM = 512
SB_DIM = 512
SLAB = 512
N_DEV = 8
MESH_AXES = ("x", "y", "c")
MESH = pl.DeviceIdType.MESH

ADAM_LR = 0.001
ADAM_B1 = 0.9
ADAM_B2 = 0.999
ADAM_EPS = 1e-08
ADAM_WD = 0.01
ADAM_STEP = 10

VMEM_LIMIT_BYTES = 56 * 1024 * 1024
SUBLANES = 8
LANES = 128


def _cparams(*semantics):
    return pltpu.CompilerParams(dimension_semantics=semantics, vmem_limit_bytes=VMEM_LIMIT_BYTES)


def _dot_nn(a, b):
    return jnp.dot(a, b, preferred_element_type=F32)


def _dot_nt(a, b):
    return lax.dot_general(a, b, (((1,), (1,)), ((), ())), preferred_element_type=F32)


def _dot_tn(a, b):
    return lax.dot_general(a, b, (((0,), (0,)), ((), ())), preferred_element_type=F32)


def _fold8(v):
    rows, cols = v.shape
    return jnp.sum(v.reshape(rows // SUBLANES, SUBLANES, cols), axis=0)


def _split2(v):
    hi = v.astype(BF16)
    lo = (v - hi.astype(F32)).astype(BF16)
    return hi, lo


def _rms_stats(x):
    return lax.rsqrt(jnp.mean(x * x, axis=-1, keepdims=True) + EPS)


def _rms_bwd(dh, x, gain):
    r = _rms_stats(x)
    u = dh * gain
    dx = r * u - x * (r * r * r) * jnp.mean(u * x, axis=-1, keepdims=True)
    return dx, dh * x * r


def _pick(n, pref):
    return pref if n % pref == 0 else n


def _rmsnorm(name, x, gain, tt):
    t, d = x.shape

    def body(x_ref, g_ref, o_ref):
        xv = x_ref[...]
        o_ref[...] = ((xv * _rms_stats(xv)) * g_ref[...]).astype(BF16)

    return pl.pallas_call(
        body, name=name, grid=(t // tt,),
        in_specs=[pl.BlockSpec((tt, d), lambda i: (i, 0)), pl.BlockSpec((1, d), lambda i: (0, 0))],
        out_specs=pl.BlockSpec((tt, d), lambda i: (i, 0)),
        out_shape=jax.ShapeDtypeStruct((t, d), BF16),
        compiler_params=_cparams("parallel"),
    )(x, gain)


class _Exchange:
    FLIPS = [(fx, fy, fc) for fx in (0, 1) for fy in (0, 1) for fc in (0, 1)][1:]

    def __init__(self, arrays, gather):
        self.arrays = list(arrays)
        self.gather = gather
        self.n = len(self.arrays)
        self.rows = [a.shape[0] if gather else a.shape[0] // N_DEV for a in self.arrays]
        self.out_shape = [jax.ShapeDtypeStruct((N_DEV * r, a.shape[1]), a.dtype)
                          for r, a in zip(self.rows, self.arrays)]
        self.scratch = [pltpu.SemaphoreType.DMA((7 * self.n,)), pltpu.SemaphoreType.DMA((7 * self.n,)),
                        pltpu.SemaphoreType.DMA((self.n,))]

    def _copies(self, ins, outs, sems, arrivals):
        send_sems, recv_sems, local_sems = sems
        x, y, c = (lax.axis_index(a) for a in MESH_AXES)
        me_idx = 4 * x + 2 * y + c
        local, send, recv = [], [], []
        for a in range(self.n):
            r = self.rows[a]

            def blk(ref, idx, r=r):
                return ref.at[pl.ds(idx * r, r), :]

            def src(idx, a=a, blk=blk):
                return ins[a] if self.gather else blk(ins[a], idx)

            local.append(pltpu.make_async_copy(src(me_idx), blk(outs[a], me_idx), local_sems.at[a]))
            for k, flip in enumerate(self.FLIPS):
                px, py, pc = (1 - v if f else v for v, f in zip((x, y, c), flip))
                p_idx = 4 * px + 2 * py + pc
                for dst_idx, group in ((me_idx, send), (p_idx, recv))[:2 if arrivals else 1]:
                    group.append(pltpu.make_async_remote_copy(
                        src_ref=src(p_idx), dst_ref=blk(outs[a], dst_idx),
                        send_sem=send_sems.at[7 * a + k], recv_sem=recv_sems.at[7 * a + k],
                        device_id=(px, py, pc), device_id_type=MESH))
        return local, send, recv

    def start(self, ins, outs, sems):
        local, send, _ = self._copies(ins, outs, sems, arrivals=False)
        for cp in local + send:
            cp.start()

    def wait(self, ins, outs, sems):
        local, send, recv = self._copies(ins, outs, sems, arrivals=True)
        for s, r in zip(send, recv):
            r.wait_recv()
            s.wait_send()
        for cp in local:
            cp.wait()

    def attach(self, refs, first, last):
        pl.when(first)(lambda: self.start(*refs))
        pl.when(last)(lambda: self.wait(*refs))


def _split_refs(refs, n_in, n_out, n_scratch, comm):
    nc = comm.n if comm else 0
    ins, rest = refs[:n_in], refs[n_in:]
    c_in, rest = rest[:nc], rest[nc:]
    outs, rest = rest[:n_out], rest[n_out:]
    c_out, rest = rest[:nc], rest[nc:]
    scratch, c_sems = rest[:n_scratch], rest[n_scratch:]
    return ins, outs, scratch, ((c_in, c_out, c_sems) if comm else None)


def _with_comm(comm, in_specs, out_specs, out_shape, scratch):
    if comm is None:
        return in_specs, out_specs, out_shape, scratch, []
    return (in_specs + _any_specs(comm.n), out_specs + _any_specs(comm.n), out_shape + comm.out_shape,
            scratch + comm.scratch, comm.arrays)


def _mm_nt(name, a_list, w_list, pairs, epilogue, out_dtypes, tt, tn, comm=None, tiles=()):
    t = a_list[0].shape[0]
    n = w_list[0].shape[0]
    na, nw, ntile = len(a_list), len(w_list), len(tiles)
    ni, nj = t // tt, n // tn

    def body(*refs):
        ins, o_refs, _, c_refs = _split_refs(refs, na + nw + ntile, len(out_dtypes), 0, comm)
        a_refs, w_refs, t_refs = ins[:na], ins[na:na + nw], ins[na + nw:]
        if comm:
            i = pl.program_id(0)
            comm.attach(c_refs, i == 0, i == ni - 1)
        a_vals = [a_ref[...] for a_ref in a_refs]
        for j in range(nj):
            cols = pl.ds(j * tn, tn)
            accs = [_dot_nt(a_vals[ai], w_refs[wi][cols, :]) for ai, wi in pairs]
            for o_ref, o in zip(o_refs, epilogue(accs, [t_ref[:, cols] for t_ref in t_refs])):
                o_ref[:, cols] = o.astype(o_ref.dtype)

    in_specs = ([pl.BlockSpec((tt, a.shape[1]), lambda i: (i, 0)) for a in a_list]
                + [pl.BlockSpec(w.shape, lambda i: (0, 0), pipeline_mode=pl.Buffered(1)) for w in w_list]
                + [pl.BlockSpec((tt, n), lambda i: (i, 0)) for _ in tiles])
    in_specs, out_specs, out_shape, scratch, extra = _with_comm(
        comm, in_specs, [pl.BlockSpec((tt, n), lambda i: (i, 0)) for _ in out_dtypes],
        [jax.ShapeDtypeStruct((t, n), dt) for dt in out_dtypes], [])
    return pl.pallas_call(
        body, name=name, grid=(ni,), in_specs=in_specs, out_specs=out_specs, out_shape=out_shape,
        scratch_shapes=scratch,
        compiler_params=_cparams("arbitrary" if comm else "parallel"),
    )(*a_list, *w_list, *tiles, *extra)


def _mm_nn(name, pairs, rows, fulls, epilogue, out_kinds, tt, tk, comm=None):
    t, k_total = pairs[0][0].shape
    n = pairs[0][1].shape[1]
    nk = k_total // tk
    nt = t // tt
    npair, nrow, nfull = len(pairs), len(rows), len(fulls)

    def body(*refs):
        ins, o_refs, scratch, c_refs = _split_refs(refs, 2 * npair + nrow + nfull, len(out_kinds), min(nk - 1, 1), comm)
        a_refs, w_refs = ins[:npair], ins[npair:2 * npair]
        r_refs, f_refs = ins[2 * npair:2 * npair + nrow], ins[2 * npair + nrow:]
        i, k = pl.program_id(0), pl.program_id(1)
        if comm:
            comm.attach(c_refs, jnp.logical_and(i == 0, k == 0), jnp.logical_and(i == nt - 1, k == nk - 1))
        s = _dot_nn(a_refs[0][...], w_refs[0][...])
        for a_ref, w_ref in zip(a_refs[1:], w_refs[1:]):
            s = s + _dot_nn(a_ref[...], w_ref[...])

        def finish(acc):
            outs = epilogue(acc, [r[...] for r in r_refs], [f[...] for f in f_refs])
            for o_ref, o in zip(o_refs, outs):
                o_ref[...] = o.astype(o_ref.dtype)

        if nk == 1:
            finish(s)
        else:
            acc_ref = scratch[0]

            @pl.when(k == 0)
            def _():
                acc_ref[...] = s

            @pl.when(k > 0)
            def _():
                acc_ref[...] += s

            @pl.when(k == nk - 1)
            def _():
                finish(acc_ref[...])

    once = dict(pipeline_mode=pl.Buffered(1)) if nk == 1 else {}
    in_specs = ([pl.BlockSpec((tt, tk), lambda i, k: (i, k)) for _ in pairs]
                + [pl.BlockSpec((tk, n), (lambda i, k, off=off: (k + off, 0)), **once) for _, _, off in pairs]
                + [pl.BlockSpec((tt, n), lambda i, k: (i, 0)) for _ in rows]
                + [pl.BlockSpec((1, n), lambda i, k: (0, 0)) for _ in fulls])
    out_specs, out_shape = [], []
    for kind, dt in out_kinds:
        if kind == "tile":
            out_specs.append(pl.BlockSpec((tt, n), lambda i, k: (i, 0)))
            out_shape.append(jax.ShapeDtypeStruct((t, n), dt))
        else:
            out_specs.append(pl.BlockSpec((SUBLANES, n), lambda i, k: (i, 0)))
            out_shape.append(jax.ShapeDtypeStruct((nt * SUBLANES, n), dt))
    in_specs, out_specs, out_shape, scratch, extra = _with_comm(
        comm, in_specs, out_specs, out_shape, [] if nk == 1 else [pltpu.VMEM((tt, n), F32)])
    return pl.pallas_call(
        body, name=name, grid=(nt, nk), in_specs=in_specs, out_specs=out_specs, out_shape=out_shape,
        scratch_shapes=scratch,
        compiler_params=_cparams("arbitrary" if comm else "parallel", "arbitrary"),
    )(*[a for a, _, _ in pairs], *[w for _, w, _ in pairs], *rows, *fulls, *extra)


def _mm_tn(name, a, b, scale, tm, tt, comm=None):
    t, m = a.shape
    n = b.shape[1]
    nt = t // tt
    nm = m // tm

    def body(*refs):
        (a_ref, b_ref), (o_ref,), (acc_ref,), c_refs = _split_refs(refs, 2, 1, 1, comm)
        k = pl.program_id(1)
        if comm:
            i = pl.program_id(0)
            comm.attach(c_refs, jnp.logical_and(i == 0, k == 0), jnp.logical_and(i == nm - 1, k == nt - 1))
        s = _dot_tn(a_ref[...], b_ref[...])

        @pl.when(k == 0)
        def _():
            acc_ref[...] = s

        @pl.when(k > 0)
        def _():
            acc_ref[...] += s

        @pl.when(k == nt - 1)
        def _():
            o_ref[...] = (acc_ref[...] * scale).astype(o_ref.dtype)

    in_specs, out_specs, out_shape, scratch, extra = _with_comm(
        comm, [pl.BlockSpec((tt, tm), lambda i, k: (k, i)), pl.BlockSpec((tt, n), lambda i, k: (k, 0))],
        [pl.BlockSpec((tm, n), lambda i, k: (i, 0))], [jax.ShapeDtypeStruct((m, n), BF16)],
        [pltpu.VMEM((tm, n), F32)])
    out = pl.pallas_call(
        body, name=name, grid=(nm, nt), in_specs=in_specs, out_specs=out_specs, out_shape=out_shape,
        scratch_shapes=scratch,
        compiler_params=_cparams("arbitrary" if comm else "parallel", "arbitrary"),
    )(a, b, *extra)
    return out if comm else out[0]


def _mm_tn_slabs(name, a_list, b, tt, comm=None):
    t, m = a_list[0].shape
    n = b.shape[1]
    na = len(a_list)
    nt = t // tt

    def body(*refs):
        ins, (o_ref,), (acc_ref,), c_refs = _split_refs(refs, na + 1, 1, 1, comm)
        k = pl.program_id(0)
        if comm:
            comm.attach(c_refs, k == 0, k == nt - 1)
        bv = ins[na][...]
        parts = [_dot_tn(a_ref[...], bv) for a_ref in ins[:na]]

        @pl.when(k == 0)
        def _():
            for j, part in enumerate(parts):
                acc_ref[pl.ds(j * m, m), :] = part

        @pl.when(k > 0)
        def _():
            for j, part in enumerate(parts):
                acc_ref[pl.ds(j * m, m), :] += part

        @pl.when(k == nt - 1)
        def _():
            o_ref[...] = acc_ref[...].astype(o_ref.dtype)

    in_specs, out_specs, out_shape, scratch, extra = _with_comm(
        comm, [pl.BlockSpec((tt, m), lambda k: (k, 0))] * na + [pl.BlockSpec((tt, n), lambda k: (k, 0))],
        [pl.BlockSpec((na * m, n), lambda k: (0, 0))], [jax.ShapeDtypeStruct((na * m, n), BF16)],
        [pltpu.VMEM((na * m, n), F32)])
    out = pl.pallas_call(
        body, name=name, grid=(nt,), in_specs=in_specs, out_specs=out_specs, out_shape=out_shape,
        scratch_shapes=scratch, compiler_params=_cparams("arbitrary"),
    )(*a_list, b, *extra)
    return out if comm else out[0]


def _group_sum(v, bd):
    hi, lo = _split2(v)
    return _dot_nn(hi, bd) + _dot_nn(lo, bd)


def _qknorm_fwd(proj, qg, kg, bd, tt):
    t = proj.shape[0]

    def body(q_ref, k_ref, v_ref, qg_ref, kg_ref, bd_ref, qn_ref, kn_ref, vb_ref):
        bdv = bd_ref[...]
        for x_ref, g_ref, o_ref in ((q_ref, qg_ref, qn_ref), (k_ref, kg_ref, kn_ref)):
            xv = x_ref[...]
            r = lax.rsqrt(_group_sum(xv * xv, bdv) * (1.0 / HEAD_DIM) + EPS)
            o_ref[...] = ((xv * r) * g_ref[...]).astype(BF16)
        vb_ref[...] = v_ref[...].astype(BF16)

    slab = lambda s: pl.BlockSpec((tt, SLAB), lambda i, s=s: (i, s))
    full = lambda shape: pl.BlockSpec(shape, lambda i: (0, 0))
    out = pl.BlockSpec((tt, SLAB), lambda i: (i, 0))
    return pl.pallas_call(
        body, name="qknorm_fwd", grid=(t // tt,),
        in_specs=[slab(3), slab(4), slab(5), full((1, SLAB)), full((1, SLAB)), full((SLAB, SLAB))],
        out_specs=[out, out, out],
        out_shape=[jax.ShapeDtypeStruct((t, SLAB), BF16)] * 3,
        compiler_params=_cparams("parallel"),
    )(proj, proj, proj, qg, kg, bd)


def _qknorm_bwd(proj, dqn, dkn, dv, qg, kg, bd, tt):
    t = proj.shape[0]

    def body(q_ref, k_ref, dqn_ref, dkn_ref, dv_ref, qg_ref, kg_ref, bd_ref, dq_ref, dk_ref, dvb_ref, part_ref):
        bdv = bd_ref[...]
        parts = []
        for x_ref, d_ref, g_ref, o_ref in ((q_ref, dqn_ref, qg_ref, dq_ref), (k_ref, dkn_ref, kg_ref, dk_ref)):
            xv, dn = x_ref[...], d_ref[...]
            r = lax.rsqrt(_group_sum(xv * xv, bdv) * (1.0 / HEAD_DIM) + EPS)
            u = dn * g_ref[...]
            dx = r * u - xv * (r * r * r) * (_group_sum(u * xv, bdv) * (1.0 / HEAD_DIM))
            o_ref[...] = dx.astype(BF16)
            parts.append(_fold8(dn * xv * r))
        dvb_ref[...] = dv_ref[...].astype(BF16)
        part_ref[...] = jnp.concatenate(parts, axis=1)

    slab = lambda s: pl.BlockSpec((tt, SLAB), lambda i, s=s: (i, s))
    tile = pl.BlockSpec((tt, SLAB), lambda i: (i, 0))
    full = lambda shape: pl.BlockSpec(shape, lambda i: (0, 0))
    return pl.pallas_call(
        body, name="qknorm_bwd", grid=(t // tt,),
        in_specs=[slab(3), slab(4), tile, tile, tile, full((1, SLAB)), full((1, SLAB)), full((SLAB, SLAB))],
        out_specs=[tile, tile, tile, pl.BlockSpec((SUBLANES, 2 * SLAB), lambda i: (i, 0))],
        out_shape=[jax.ShapeDtypeStruct((t, SLAB), BF16)] * 3
        + [jax.ShapeDtypeStruct((t // tt * SUBLANES, 2 * SLAB), F32)],
        compiler_params=_cparams("parallel"),
    )(proj, proj, dqn, dkn, dv, qg, kg, bd)


def _conv_taps(z, z_prev, row):
    zm1 = jnp.where(row == 0, z_prev[7:8], pltpu.roll(z, 1, 0))
    zm2 = jnp.where(row == 0, z_prev[6:7], jnp.where(row == 1, z_prev[7:8], pltpu.roll(z, 2, 0)))
    return zm1, zm2


def _conv_fwd(proj, cw, cb, tt):
    t = proj.shape[0]
    tb = tt // SUBLANES

    def body(b_ref, c_ref, u_ref, cp_ref, up_ref, cw_ref, cb_ref, o_ref):
        i = pl.program_id(0)
        z = c_ref[...] * u_ref[...]
        z_prev = jnp.where(i > 0, cp_ref[...] * up_ref[...], 0.0)
        row = lax.broadcasted_iota(jnp.int32, (tt, 1), 0)
        zm1, zm2 = _conv_taps(z, z_prev, row)
        y = cw_ref[0:1] * zm2 + cw_ref[1:2] * zm1 + cw_ref[2:3] * z + cb_ref[...]
        o_ref[...] = (b_ref[...] * y).astype(BF16)

    slab = lambda s: pl.BlockSpec((tt, SLAB), lambda i, s=s: (i, s))
    prev = lambda s: pl.BlockSpec((SUBLANES, SLAB), lambda i, s=s: (jnp.maximum(i * tb - 1, 0), s))
    return pl.pallas_call(
        body, name="conv_fwd", grid=(t // tt,),
        in_specs=[slab(0), slab(1), slab(2), prev(1), prev(2),
                  pl.BlockSpec((SUBLANES, SLAB), lambda i: (0, 0)), pl.BlockSpec((1, SLAB), lambda i: (0, 0))],
        out_specs=pl.BlockSpec((tt, SLAB), lambda i: (i, 0)),
        out_shape=jax.ShapeDtypeStruct((t, SLAB), BF16),
        compiler_params=_cparams("parallel"),
    )(proj, proj, proj, proj, proj, cw, cb)


def _conv_bwd(proj, dycat, cw, cb, tt):
    t = proj.shape[0]
    tb = tt // SUBLANES
    nblk = t // SUBLANES

    def body(b_ref, c_ref, u_ref, cp_ref, up_ref, bn_ref, dy_ref, dyn_ref, cw_ref, cb_ref,
             db_ref, dc_ref, du_ref, part_ref):
        i = pl.program_id(0)
        c, u, b, dyc = c_ref[...], u_ref[...], b_ref[...], dy_ref[...]
        z = c * u
        z_prev = jnp.where(i > 0, cp_ref[...] * up_ref[...], 0.0)
        row = lax.broadcasted_iota(jnp.int32, (tt, 1), 0)
        zm1, zm2 = _conv_taps(z, z_prev, row)
        w0, w1, w2 = cw_ref[0:1], cw_ref[1:2], cw_ref[2:3]
        y = w0 * zm2 + w1 * zm1 + w2 * z + cb_ref[...]
        db_ref[...] = (dyc * y).astype(BF16)
        g = dyc * b
        g_next = jnp.where(i < pl.num_programs(0) - 1, dyn_ref[...] * bn_ref[...], 0.0)
        gp1 = jnp.where(row == tt - 1, g_next[0:1], pltpu.roll(g, tt - 1, 0))
        gp2 = jnp.where(row == tt - 2, g_next[0:1], jnp.where(row == tt - 1, g_next[1:2], pltpu.roll(g, tt - 2, 0)))
        dz = w2 * g + w1 * gp1 + w0 * gp2
        dc_ref[...] = (dz * u).astype(BF16)
        du_ref[...] = (dz * c).astype(BF16)
        part_ref[...] = jnp.concatenate([_fold8(g * zm2), _fold8(g * zm1), _fold8(g * z), _fold8(g)], axis=1)

    slab = lambda s: pl.BlockSpec((tt, SLAB), lambda i, s=s: (i, s))
    prev = lambda s: pl.BlockSpec((SUBLANES, SLAB), lambda i, s=s: (jnp.maximum(i * tb - 1, 0), s))
    nxt = lambda s: pl.BlockSpec((SUBLANES, SLAB), lambda i, s=s: (jnp.minimum((i + 1) * tb, nblk - 1), s))
    tile = pl.BlockSpec((tt, SLAB), lambda i: (i, 0))
    return pl.pallas_call(
        body, name="conv_bwd", grid=(t // tt,),
        in_specs=[slab(0), slab(1), slab(2), prev(1), prev(2), nxt(0), slab(0), nxt(0),
                  pl.BlockSpec((SUBLANES, SLAB), lambda i: (0, 0)), pl.BlockSpec((1, SLAB), lambda i: (0, 0))],
        out_specs=[tile, tile, tile, pl.BlockSpec((SUBLANES, 4 * SLAB), lambda i: (i, 0))],
        out_shape=[jax.ShapeDtypeStruct((t, SLAB), BF16)] * 3
        + [jax.ShapeDtypeStruct((t // tt * SUBLANES, 4 * SLAB), F32)],
        compiler_params=_cparams("parallel"),
    )(proj, proj, proj, proj, proj, proj, dycat, dycat, cw, cb)


def _tri_masks(n):
    r = lax.broadcasted_iota(jnp.int32, (n, n), 0)
    c = lax.broadcasted_iota(jnp.int32, (n, n), 1)
    return (r > c).astype(BF16), (r >= c).astype(BF16)


def _sb_logits(z, causal):
    softplus = jnp.maximum(z, 0.0) + jnp.log(1.0 + jnp.exp(-jnp.abs(z)))
    lk = -softplus
    if causal is not None:
        lk = jnp.where(causal, lk, 0.0)
    return (z, lk, *_split2(lk))


def _sb_finish(z, lk, hi, lo, r_run, tri, causal, later):
    later = later + r_run
    ls = z + lk
    arg = ls + later
    if causal is not None:
        arg = jnp.where(causal, arg, -1e30)
    return lk, ls, jnp.exp(arg), later[:, 0:1] + lk[:, 0:1]


SB_DEAD_LOG = -111.0
CHAINS = ((0, 0), (0, 1), (1, 0), (1, 1))


def _all_dead(r_runs):
    m = r_runs[0]
    for r in r_runs[1:]:
        m = jnp.maximum(m, r)
    return (jnp.max(m) < SB_DEAD_LOG).astype(jnp.int32)


def _attn_fwd(qn, kn, vb, tri, sb, comm=None):
    t = qn.shape[0]
    bq = 2 * sb
    scale = HEAD_DIM ** -0.5

    def body(*refs):
        (q_ref, k_ref, v_ref, tri_ref), (o_ref, ob_ref), (acc_ref,), c_refs = _split_refs(refs, 4, 2, 1, comm)
        qi = pl.program_id(1)
        if comm:
            hp = pl.program_id(0)
            comm.attach(c_refs, jnp.logical_and(hp == 0, qi == 0),
                        jnp.logical_and(hp == pl.num_programs(0) - 1, qi == pl.num_programs(1) - 1))
        lane = lax.broadcasted_iota(jnp.int32, (1, LANES), 1)
        hmasks = (lane < HEAD_DIM, lane >= HEAD_DIM)
        diag = lax.broadcasted_iota(jnp.int32, (sb, sb), 1) < lax.broadcasted_iota(jnp.int32, (sb, sb), 0)
        triv = tri_ref[...]
        qs = [jnp.where(hmasks[hh], q_ref[pl.ds(s * sb, sb), :], 0) * scale for s, hh in CHAINS]
        acc_ref[...] = jnp.zeros_like(acc_ref)

        def load_kv(kb):
            ks = kb * sb if isinstance(kb, int) else pl.multiple_of(kb * sb, sb)
            vraw = v_ref[pl.ds(ks, sb), :]
            return k_ref[pl.ds(ks, sb), :], [jnp.where(hm, vraw, 0) for hm in hmasks]

        def run_tiles(tiles, r_in):
            zs = [_dot_nt(qs[c], kv[0]) for c, kv, _, _ in tiles]
            mids = [_sb_logits(z, causal) for z, (_, _, causal, _) in zip(zs, tiles)]
            laters = [_dot_nn(m[2], triv) + _dot_nn(m[3], triv) for m in mids]
            outs = []
            for m, later, (c, _, causal, dep) in zip(mids, laters, tiles):
                outs.append(_sb_finish(*m, r_in[c] if dep is None else outs[dep][3], triv, causal, later))
            for o, (c, kv, _, _) in zip(outs, tiles):
                acc_ref[c] += _dot_nn(o[2].astype(BF16), kv[1][c % 2])
            return [o[3] for o in outs]

        zero = jnp.zeros((sb, 1), F32)
        kv_diag = [load_kv(2 * qi), load_kv(2 * qi + 1)]
        kv_prev = [load_kv(jnp.maximum(2 * qi - 1, 0)), kv_diag[0]]
        has_prev = lax.broadcasted_iota(jnp.int32, (sb, sb), 0) >= jnp.where(qi > 0, 0, sb)
        r_runs = run_tiles([(c, kv_diag[c // 2], diag, None) for c in range(4)]
                           + [(c, kv_prev[c // 2], has_prev if c < 2 else None, c) for c in range(4)], [zero] * 4)[4:]

        def step(carry):
            i, _, *rs = carry
            kvs = [load_kv(2 * qi - 2 - i), load_kv(2 * qi - 1 - i)]
            rs = run_tiles([(c, kvs[c // 2], None, None) for c in range(4)], rs)
            return (i + 1, _all_dead(rs), *rs)

        i_end, _, *rs = lax.while_loop(lambda c: jnp.logical_and(c[0] < 2 * qi - 1, c[1] == 0), step,
                                       (jnp.int32(0), _all_dead(r_runs), *r_runs))

        @pl.when(jnp.logical_and(i_end == 2 * qi - 1, _all_dead(rs[2:]) == 0))
        def _():
            kv_last = load_kv(0)
            run_tiles([(c, kv_last, None, None) for c in (2, 3)], rs)
        for s in range(2):
            out = acc_ref[2 * s] + acc_ref[2 * s + 1]
            o_ref[pl.ds(s * sb, sb), :] = out
            ob_ref[pl.ds(s * sb, sb), :] = out.astype(BF16)

    qspec = pl.BlockSpec((bq, LANES), lambda h, i: (i, h))
    kspec = pl.BlockSpec((t, LANES), lambda h, i: (0, h))
    in_specs, out_specs, out_shape, scratch, extra = _with_comm(
        comm, [qspec, kspec, kspec, pl.BlockSpec((sb, sb), lambda h, i: (0, 0))], [qspec, qspec],
        [jax.ShapeDtypeStruct((t, SB_DIM), F32), jax.ShapeDtypeStruct((t, SB_DIM), BF16)],
        [pltpu.VMEM((4, sb, LANES), F32)])
    return pl.pallas_call(
        body, name="attn_fwd", grid=(SB_DIM // LANES, t // bq),
        in_specs=in_specs, out_specs=out_specs, out_shape=out_shape, scratch_shapes=scratch,
        compiler_params=_cparams("arbitrary" if comm else "parallel", "arbitrary"),
    )(qn, kn, vb, tri, *extra)


def _attn_bwd(qn, kn, vb, o, dycat, tri, tri_inc, sb, comm=None):
    t = qn.shape[0]
    bq = 2 * sb
    scale = HEAD_DIM ** -0.5

    def body(*refs):
        ins, (dq_ref, dk_ref, dv_ref), (dq_acc,), c_refs = _split_refs(refs, 7, 3, 1, comm)
        q_ref, k_ref, v_ref, o_ref, do_ref, tri_ref, tinc_ref = ins
        qi = pl.program_id(1)
        if comm:
            hp = pl.program_id(0)
            comm.attach(c_refs, jnp.logical_and(hp == 0, qi == 0),
                        jnp.logical_and(hp == pl.num_programs(0) - 1, qi == pl.num_programs(1) - 1))

        @pl.when(qi == 0)
        def _():
            dk_ref[...] = jnp.zeros_like(dk_ref)
            dv_ref[...] = jnp.zeros_like(dv_ref)

        lane = lax.broadcasted_iota(jnp.int32, (1, LANES), 1)
        hmasks = (lane < HEAD_DIM, lane >= HEAD_DIM)
        diag = lax.broadcasted_iota(jnp.int32, (sb, sb), 1) < lax.broadcasted_iota(jnp.int32, (sb, sb), 0)
        triv, tincv = tri_ref[...], tinc_ref[...]
        qs, dobs, d_rows = [], [], []
        for s, hh in CHAINS:
            rows = pl.ds(s * sb, sb)
            qs.append(jnp.where(hmasks[hh], q_ref[rows, :], 0) * scale)
            dobs.append(jnp.where(hmasks[hh], do_ref[rows, :], 0.0).astype(BF16))
            d_rows.append(jnp.sum(dobs[-1].astype(F32) * o_ref[rows, :], axis=1, keepdims=True))
        dq_acc[...] = jnp.zeros_like(dq_acc)

        def load_kv(kb):
            ks = kb * sb if isinstance(kb, int) else pl.multiple_of(kb * sb, sb)
            return k_ref[pl.ds(ks, sb), :], v_ref[pl.ds(ks, sb), :], ks

        def run_tiles(tiles, r_in, g_in):
            zs = [_dot_nt(qs[c], kv[0]) for c, kv, _, _ in tiles]
            das = [_dot_nt(dobs[c], kv[1]) for c, kv, _, _ in tiles]
            mids = [_sb_logits(z, causal) for z, (_, _, causal, _) in zip(zs, tiles)]
            laters = [_dot_nn(m[2], triv) + _dot_nn(m[3], triv) for m in mids]
            fins, abs_, es = [], [], []
            for m, later, da, (c, _, causal, dep) in zip(mids, laters, das, tiles):
                fins.append(_sb_finish(*m, r_in[c] if dep is None else fins[dep][3], triv, causal, later))
                abs_.append(fins[-1][2].astype(BF16))
                es.append(da * abs_[-1].astype(F32))
            splits = [_split2(e) for e in es]
            e_sums = [_dot_nn(hi, tincv) + _dot_nn(lo, tincv) for hi, lo in splits]
            e_froms, dzbs = [], []
            for e, e_sum, fin, (c, _, causal, dep) in zip(es, e_sums, fins, tiles):
                e_froms.append(e_sum + (g_in[c] if dep is None else e_froms[dep][:, 0:1]))
                dz = e - jnp.exp(fin[1]) * (e + (d_rows[c] - e_froms[-1]))
                if causal is not None:
                    dz = jnp.where(causal, dz, 0.0)
                dzbs.append(dz.astype(BF16))
            for dzb, (c, kv, _, _) in zip(dzbs, tiles):
                dq_acc[c] += _dot_nn(dzb, kv[0])
            by_rows = {}
            for dzb, ab, (c, kv, _, _) in zip(dzbs, abs_, tiles):
                by_rows.setdefault(id(kv), (kv[2], []))[1].append((_dot_tn(dzb, qs[c]), _dot_tn(ab, dobs[c])))
            for first_row, parts in by_rows.values():
                rows = pl.ds(first_row, sb)
                dk_ref[rows, :] += sum(p[0] for p in parts[1:]) + parts[0][0]
                dv_ref[rows, :] += sum(p[1] for p in parts[1:]) + parts[0][1]
            return [f[3] for f in fins], [ef[:, 0:1] for ef in e_froms]

        zero = jnp.zeros((sb, 1), F32)
        kv_diag = [load_kv(2 * qi), load_kv(2 * qi + 1)]
        kv_prev = [load_kv(jnp.maximum(2 * qi - 1, 0)), kv_diag[0]]
        has_prev = lax.broadcasted_iota(jnp.int32, (sb, sb), 0) >= jnp.where(qi > 0, 0, sb)
        r_first, g_first = run_tiles(
            [(c, kv_diag[c // 2], diag, None) for c in range(4)]
            + [(c, kv_prev[c // 2], has_prev if c < 2 else None, c) for c in range(4)], [zero] * 4, [zero] * 4)
        r_runs, g_runs = r_first[4:], g_first[4:]

        def step(carry):
            i, _, *rg = carry
            kvs = [load_kv(2 * qi - 2 - i), load_kv(2 * qi - 1 - i)]
            rs, gs = run_tiles([(c, kvs[c // 2], None, None) for c in range(4)], rg[:4], rg[4:])
            return (i + 1, _all_dead(rs), *rs, *gs)

        i_end, _, *rg = lax.while_loop(lambda c: jnp.logical_and(c[0] < 2 * qi - 1, c[1] == 0), step,
                                       (jnp.int32(0), _all_dead(r_runs), *r_runs, *g_runs))

        @pl.when(jnp.logical_and(i_end == 2 * qi - 1, _all_dead(rg[2:4]) == 0))
        def _():
            kv_last = load_kv(0)
            run_tiles([(c, kv_last, None, None) for c in (2, 3)], rg[:4], rg[4:])
        for s in range(2):
            dq_ref[pl.ds(s * sb, sb), :] = jnp.where(hmasks[0], dq_acc[2 * s], dq_acc[2 * s + 1]) * scale

    qspec = pl.BlockSpec((bq, LANES), lambda h, i: (i, h))
    dospec = pl.BlockSpec((bq, LANES), lambda h, i: (i, h + CONV_DIM // LANES))
    kspec = pl.BlockSpec((t, LANES), lambda h, i: (0, h))
    full = pl.BlockSpec((sb, sb), lambda h, i: (0, 0))
    in_specs, out_specs, out_shape, scratch, extra = _with_comm(
        comm, [qspec, kspec, kspec, qspec, dospec, full, full], [qspec, kspec, kspec],
        [jax.ShapeDtypeStruct((t, SB_DIM), F32)] * 3, [pltpu.VMEM((4, sb, LANES), F32)])
    return pl.pallas_call(
        body, name="attn_bwd", grid=(SB_DIM // LANES, t // bq),
        in_specs=in_specs, out_specs=out_specs, out_shape=out_shape, scratch_shapes=scratch,
        compiler_params=_cparams("arbitrary" if comm else "parallel", "arbitrary"),
    )(qn, kn, vb, o, dycat, tri, tri_inc, *extra)


def _ple_loss(x3, p2, tgt, gain, wpg, wppt, tt):
    t, d = x3.shape
    pdim = p2.shape[1]
    nt = t // tt

    def body(x_ref, p_ref, t_ref, g_ref, wg_ref, wp_ref,
             dx_ref, dxb_ref, dwg_ref, dwp_ref, gpart_ref, lpart_ref, accg_ref, accp_ref):
        i = pl.program_id(0)
        xv, gain_v = x_ref[...], g_ref[...]
        hb = ((xv * _rms_stats(xv)) * gain_v).astype(BF16)
        gate = jax.nn.sigmoid(_dot_nn(hb, wg_ref[...]))
        pb = p_ref[...].astype(BF16)
        pe = _dot_nt(pb, wp_ref[...])
        diff = xv + gate * pe - t_ref[...]
        lsum = jnp.sum(_fold8(diff * diff), axis=1, keepdims=True) * (0.5 / d)
        lpart_ref[...] = jnp.broadcast_to(lsum, (SUBLANES, LANES))
        dy = diff * (1.0 / d)
        dgz = ((dy * pe) * gate * (1.0 - gate)).astype(BF16)
        dpe = (dy * gate).astype(BF16)
        dx_n, grow = _rms_bwd(_dot_nt(dgz, wg_ref[...]), xv, gain_v)
        dx = dy + dx_n
        dx_ref[...] = dx
        dxb_ref[...] = dx.astype(BF16)
        gpart_ref[...] = _fold8(grow)
        sg = _dot_tn(hb, dgz)
        sp = _dot_tn(dpe, pb)

        @pl.when(i == 0)
        def _():
            accg_ref[...] = sg
            accp_ref[...] = sp

        @pl.when(i > 0)
        def _():
            accg_ref[...] += sg
            accp_ref[...] += sp

        @pl.when(i == nt - 1)
        def _():
            dwg_ref[...] = accg_ref[...].astype(BF16)
            dwp_ref[...] = accp_ref[...].astype(BF16)

    tile = lambda w: pl.BlockSpec((tt, w), lambda i: (i, 0))
    full = lambda shape: pl.BlockSpec(shape, lambda i: (0, 0))
    return pl.pallas_call(
        body, name="ple_loss", grid=(nt,),
        in_specs=[tile(d), tile(pdim), tile(d), full((1, d)),
                  pl.BlockSpec((d, d), lambda i: (0, 0), pipeline_mode=pl.Buffered(1)),
                  pl.BlockSpec((d, pdim), lambda i: (0, 0), pipeline_mode=pl.Buffered(1))],
        out_specs=[tile(d), tile(d), full((d, d)), full((d, pdim)),
                   pl.BlockSpec((SUBLANES, d), lambda i: (i, 0)), pl.BlockSpec((SUBLANES, LANES), lambda i: (i, 0))],
        out_shape=[jax.ShapeDtypeStruct((t, d), F32), jax.ShapeDtypeStruct((t, d), BF16),
                   jax.ShapeDtypeStruct((d, d), BF16), jax.ShapeDtypeStruct((d, pdim), BF16),
                   jax.ShapeDtypeStruct((nt * SUBLANES, d), F32), jax.ShapeDtypeStruct((nt * SUBLANES, LANES), F32)],
        scratch_shapes=[pltpu.VMEM((d, d), F32), pltpu.VMEM((d, pdim), F32)],
        compiler_params=_cparams("arbitrary"),
    )(x3, p2, tgt, gain, wpg, wppt)


def _pack_small(parts_gain, conv_part, qk_part):
    d = parts_gain[0].shape[1]
    ng = len(parts_gain)

    def body(*refs):
        g_refs, conv_ref, qk_ref, o_ref = refs[:ng], refs[ng], refs[ng + 1], refs[ng + 2]
        rows = [jnp.sum(r[...], axis=0, keepdims=True) for r in g_refs]
        cs = jnp.sum(conv_ref[...], axis=0, keepdims=True)
        qs = jnp.sum(qk_ref[...], axis=0, keepdims=True)
        rows.append(jnp.concatenate([cs[:, 3 * SLAB:], cs[:, :SLAB]], axis=1))
        rows.append(cs[:, SLAB:3 * SLAB])
        rows.append(qs)
        rid = lax.broadcasted_iota(jnp.int32, (2 * SUBLANES, 1), 0)
        out = jnp.zeros((2 * SUBLANES, d), F32)
        for idx, r in enumerate(rows):
            out = jnp.where(rid == idx, r, out)
        o_ref[...] = out

    return pl.pallas_call(
        body, name="pack_small", out_shape=jax.ShapeDtypeStruct((2 * SUBLANES, d), F32),
    )(*parts_gain, conv_part, qk_part)


def _sum_slots(name, slots, out_dtype=F32):
    _, r, c = slots.shape

    def body(s_ref, o_ref):
        acc = s_ref[0].astype(F32)
        for d in range(1, N_DEV):
            acc = acc + s_ref[d].astype(F32)
        o_ref[...] = acc.astype(o_ref.dtype)

    return pl.pallas_call(body, name=name, out_shape=jax.ShapeDtypeStruct((r, c), out_dtype),
                          compiler_params=pltpu.CompilerParams(vmem_limit_bytes=VMEM_LIMIT_BYTES))(slots)


def _adamw(name, w, g, m, v):
    c1 = 1.0 - ADAM_B1 ** ADAM_STEP
    c2 = 1.0 - ADAM_B2 ** ADAM_STEP

    def body(w_ref, g_ref, m_ref, v_ref, d_ref, nm_ref, nv_ref):
        gv = g_ref[...]
        nm = ADAM_B1 * m_ref[...] + (1.0 - ADAM_B1) * gv
        nv = ADAM_B2 * v_ref[...] + (1.0 - ADAM_B2) * (gv * gv)
        d_ref[...] = -ADAM_LR * ((nm / c1) / (jnp.sqrt(nv / c2) + ADAM_EPS) + ADAM_WD * w_ref[...])
        nm_ref[...] = nm
        nv_ref[...] = nv

    return pl.pallas_call(body, name=name, out_shape=[jax.ShapeDtypeStruct(w.shape, F32)] * 3,
                          compiler_params=pltpu.CompilerParams(vmem_limit_bytes=VMEM_LIMIT_BYTES))(w, g, m, v)


def _any_specs(n):
    return [pl.BlockSpec(memory_space=pl.ANY)] * n


def _all_gather(name, shards):
    n = len(shards)

    def body(*refs):
        ins, outs = refs[:n], refs[n:2 * n]
        send_sems, recv_sems, local_sems = refs[2 * n:]
        x, y, c = (lax.axis_index(a) for a in MESH_AXES)
        me, sibling = (x, y, c), (x, y, 1 - c)
        chips = [(1 - x, y), (x, 1 - y), (1 - x, 1 - y)]

        def rows(a, px, py, pc):
            r = ins[a].shape[0]
            return outs[a].at[pl.ds((4 * px + 2 * py + pc) * r, r), :]

        def copy(a, k, block, to, src=None):
            return pltpu.make_async_remote_copy(
                src_ref=rows(a, *block) if src is None else src, dst_ref=rows(a, *block),
                send_sem=send_sems.at[7 * a + k], recv_sem=recv_sems.at[7 * a + k],
                device_id=to, device_id_type=MESH)

        mine = [pltpu.make_async_copy(ins[a], rows(a, *me), local_sems.at[a]) for a in range(n)]
        for cp in mine:
            cp.start()
        first = []
        for a in range(n):
            first.append(copy(a, 0, me, sibling, src=ins[a]))
            first += [copy(a, 1 + j, me, (*chip, c), src=ins[a]) for j, chip in enumerate(chips)]
        for cp in first:
            cp.start()
        passed = []
        for j, chip in enumerate(chips):
            for a in range(n):
                copy(a, 1 + j, (*chip, c), me).wait_recv()
                fwd = copy(a, 4 + j, (*chip, c), sibling)
                fwd.start()
                passed.append(fwd)
        for a in range(n):
            copy(a, 0, sibling, me).wait_recv()
            for j, chip in enumerate(chips):
                copy(a, 4 + j, (*chip, 1 - c), me).wait_recv()
        for cp in first + passed:
            cp.wait_send()
        for cp in mine:
            cp.wait()

    return pl.pallas_call(
        body, name=name, in_specs=_any_specs(n), out_specs=_any_specs(n),
        out_shape=[jax.ShapeDtypeStruct((N_DEV * s.shape[0], s.shape[1]), s.dtype) for s in shards],
        scratch_shapes=[pltpu.SemaphoreType.DMA((7 * n,)), pltpu.SemaphoreType.DMA((7 * n,)),
                        pltpu.SemaphoreType.DMA((n,))],
    )(*shards)


def _residual_and_norm(res_scale):
    def epilogue(acc, rows, fulls):
        out = rows[0] + res_scale * acc
        return [out] + [(out * _rms_stats(out)) * gain for gain in fulls]
    return epilogue


def _ffn_fwd(tag, x, h, wgt, wut, wd, next_gain, tt_nt, tt_nn, comm_gate=None, comm_down=None):
    f = wgt.shape[0]

    def gate_up(accs, _):
        return [accs[0], accs[1], jax.nn.silu(accs[0]) * accs[1]]

    g, u, a, *got_gate = _mm_nt(f"{tag}_gate_up", [h], [wgt, wut], [(0, 0), (0, 1)], gate_up, [BF16] * 3,
                                tt_nn, _pick(f, 256), comm_gate)
    if wd is None:
        wd = got_gate[0]
    gains = [] if next_gain is None else [next_gain]
    out, *rest = _mm_nn(f"{tag}_down", [(a, wd, 0)], [x], gains, _residual_and_norm(FFN_RES),
                        [("tile", F32)] + [("tile", BF16)] * len(gains), tt_nt, f, comm_down)
    h_next = rest.pop(0) if gains else None
    return out, h_next, (g, u, a), got_gate, rest


def _norm_bwd_epilogue(acc, rows, fulls):
    x_in, dy = rows
    dx_n, grow = _rms_bwd(acc, x_in, fulls[0])
    dx = dy + dx_n
    return [dx, dx, _fold8(grow)]


_NORM_BWD_OUTS = [("tile", F32), ("tile", BF16), ("part", F32)]


def _ffn_bwd(tag, x_in, h, hidden, dy, dyb, gain, wgt, wut, wd, tt_nt, tt_nn, riders=(None, None),
             exchange_own=False):
    g, u, a = hidden
    f = wgt.shape[0]
    tt_tn = _pick(h.shape[0], 2 * tt_nt)
    own = (lambda arr: _Exchange([arr], gather=False)) if exchange_own else (lambda arr: None)

    def carried(result, rider):
        return (result[0], result[1:]) if rider else (result, [])

    def hidden_grads(accs, tiles):
        da = FFN_RES * accs[0]
        gv, uv = tiles[0].astype(F32), tiles[1].astype(F32)
        sg = jax.nn.sigmoid(gv)
        s = gv * sg
        return [da * uv * (sg * (1.0 + gv * (1.0 - sg))), da * s]

    dwd, got_dwd = carried(_mm_tn(f"{tag}_dwd", a, dyb, FFN_RES, f // 2, tt_tn, riders[0]), riders[0])
    dg, du, *x_dwd = _mm_nt(f"{tag}_bwd_hidden", [dyb], [wd], [(0, 0)], hidden_grads, [BF16, BF16],
                            tt_nn, _pick(f, 256), own(dwd), tiles=[g, u])
    dwg, got_dwg = carried(_mm_tn(f"{tag}_dwg", dg, h, 1.0, f // 2, tt_tn, riders[1]), riders[1])
    dwu, x_dwg = carried(_mm_tn(f"{tag}_dwu", du, h, 1.0, f // 2, tt_tn, own(dwg)), exchange_own)
    dx, dxb, gpart, *x_dwu = _mm_nn(f"{tag}_bwd_dx", [(dg, wgt, 0), (du, wut, 0)], [x_in, dy], [gain],
                                    _norm_bwd_epilogue, _NORM_BWD_OUTS, tt_nn, f, own(dwu))
    grads = [*x_dwg, *x_dwu, *x_dwd] if exchange_own else [dwg, dwu, dwd]
    return dx, dxb, gpart, grads, [*got_dwd, *got_dwg]


def kernel(x, p, ffn1_norm, ffn1_w_gate, ffn1_w_up, ffn1_w_down, mix_norm, w_in, conv_w, conv_b, q_norm, k_norm, w_out, ffn2_norm, ffn2_w_gate, ffn2_w_up, ffn2_w_down, ple_norm, ple_w_gate, ple_w_proj, loss_target, m_ffn1_norm, m_ffn1_w_gate, m_ffn1_w_up, m_ffn1_w_down, m_mix_norm, m_w_in, m_conv_w, m_conv_b, m_q_norm, m_k_norm, m_w_out, m_ffn2_norm, m_ffn2_w_gate, m_ffn2_w_up, m_ffn2_w_down, m_ple_norm, m_ple_w_gate, m_ple_w_proj, v_ffn1_norm, v_ffn1_w_gate, v_ffn1_w_up, v_ffn1_w_down, v_mix_norm, v_w_in, v_conv_w, v_conv_b, v_q_norm, v_k_norm, v_w_out, v_ffn2_norm, v_ffn2_w_gate, v_ffn2_w_up, v_ffn2_w_down, v_ple_norm, v_ple_w_gate, v_ple_w_proj):
    x0, p2, tgt = x[0], p[0, 0], loss_target[0]
    t, d = x0.shape
    tt_nt = _pick(t, 1024)
    tt_nn = _pick(t, 512)
    tt_ew = _pick(t, 512)
    tt_ple = _pick(t, 512)
    sb = _pick(t // 2, 256)

    t_bf = lambda w: w[0].T.astype(BF16)
    n_bf = lambda w: w[0].astype(BF16)
    cw_tile = jnp.zeros((SUBLANES, LANES), F32).at[:conv_w.shape[1], :conv_w.shape[2]].set(conv_w[0])
    wg1t, wu1t = _all_gather("gather_ffn1_weights", [t_bf(ffn1_w_gate), t_bf(ffn1_w_up)])
    gather_down = _Exchange([n_bf(ffn1_w_down), n_bf(w_out), cw_tile], gather=True)
    gather_in = _Exchange([t_bf(w_in)], gather=True)
    gather_late = _Exchange([t_bf(ffn2_w_gate), t_bf(ffn2_w_up), n_bf(ffn2_w_down), n_bf(ple_w_gate),
                             t_bf(ple_w_proj)], gather=True)
    ncs = conv_w.shape[2]

    qg = jnp.tile(q_norm, (1, SB_DIM // HEAD_DIM))
    kg = jnp.tile(k_norm, (1, SB_DIM // HEAD_DIM))
    gi = lax.broadcasted_iota(jnp.int32, (SLAB, SLAB), 0) // HEAD_DIM
    gj = lax.broadcasted_iota(jnp.int32, (SLAB, SLAB), 1) // HEAD_DIM
    bd = (gi == gj).astype(BF16)
    tri, tri_inc = _tri_masks(sb)

    h1 = _rmsnorm("ffn1_norm", x0, ffn1_norm, tt_ew)
    x1, h2, hidden1, (wd1, wout, cw_all), (wint,) = _ffn_fwd(
        "ffn1", x0, h1, wg1t, wu1t, None, mix_norm, tt_nt, tt_nn, gather_down, gather_in)
    cw_full = cw_all.reshape(N_DEV, SUBLANES, LANES)[:, :, :ncs].transpose(1, 0, 2).reshape(SUBLANES, N_DEV * ncs)
    (proj,) = _mm_nt("in_proj", [h2], [wint], [(0, 0)], lambda accs, _: accs, [F32], tt_nn, SLAB)
    y_conv = _conv_fwd(proj, cw_full, conv_b, tt_ew)
    qn, kn, vb = _qknorm_fwd(proj, qg, kg, bd, tt_ew)
    o, ob, wg2t, wu2t, wd2, wpg, wppt = _attn_fwd(qn, kn, vb, tri, sb, gather_late)
    x2, h3 = _mm_nn("out_proj", [(y_conv, wout, 0), (ob, wout, 1)], [x1], [ffn2_norm], _residual_and_norm(1.0),
                    [("tile", F32), ("tile", BF16)], tt_nn, SLAB)
    x3, _, hidden2, _, _ = _ffn_fwd("ffn2", x2, h3, wg2t, wu2t, wd2, None, tt_nt, tt_nn)

    dx3, dx3b, dwpg, dwppt, gp_ple, lpart = _ple_loss(x3, p2, tgt, ple_norm, wpg, wppt, tt_ple)
    loss = lax.psum(jnp.sum(lpart[:, 0]), MESH_AXES)
    dx2, dx2b, gp_ffn2, (dwg2, dwu2, dwd2), _ = _ffn_bwd("ffn2", x2, h3, hidden2, dx3, dx3b, ffn2_norm,
                                                         wg2t, wu2t, wd2, tt_nt, tt_nn)
    (dycat,) = _mm_nt("out_proj_bwd", [dx2b], [wout], [(0, 0)], lambda accs, _: accs, [F32], tt_nn, SLAB)
    dwout = _mm_tn_slabs("dwout", [y_conv, ob], dx2b, tt_nt)
    dqn, dkn, dv, *slots_late = _attn_bwd(qn, kn, vb, o, dycat, tri, tri_inc, sb,
                                          _Exchange([dwg2, dwu2, dwd2, dwpg, dwppt], gather=False))
    dq, dk, dvb, qk_part = _qknorm_bwd(proj, dqn, dkn, dv, qg, kg, bd, tt_ew)
    db, dc, du, conv_part = _conv_bwd(proj, dycat, cw_full, conv_b, tt_ew)
    dproj = [db, dc, du, dq, dk, dvb]
    dwin, slot_wout = _mm_tn_slabs("dwin", dproj, h2, tt_nt, _Exchange([dwout], gather=False))
    dx1, dx1b, gp_mix, slot_win = _mm_nn(
        "in_proj_bwd", [(dp, wint, s) for s, dp in enumerate(dproj)], [x1, dx2], [mix_norm],
        _norm_bwd_epilogue, _NORM_BWD_OUTS, tt_nn, SLAB, _Exchange([dwin], gather=False))
    dx0, _, gp_ffn1, slots_ffn1, _ = _ffn_bwd(
        "ffn1", x0, h1, hidden1, dx1, dx1b, ffn1_norm, wg1t, wu1t, wd1, tt_nt, tt_nn, exchange_own=True)

    slots = [*slots_ffn1, slot_win, slot_wout, *slots_late]
    sums = [_sum_slots(f"sum_grads_{i}", s.reshape(N_DEV, s.shape[0] // N_DEV, s.shape[1]))
            for i, s in enumerate(slots)]
    g_wg1, g_wu1, g_wd1, g_win, g_wout, g_wg2, g_wu2, g_wd2, g_wpg, g_wpp = sums
    small = _pack_small([gp_ffn1, gp_mix, gp_ffn2, gp_ple], conv_part, qk_part)
    (small_all,) = _all_gather("gather_small_grads", [small])
    sm = _sum_slots("sum_small_grads", small_all.reshape(N_DEV, 2 * SUBLANES, d))
    fold = lambda r: r.reshape(SB_DIM // HEAD_DIM, HEAD_DIM).sum(axis=0)[None]
    me_idx = 4 * lax.axis_index("x") + 2 * lax.axis_index("y") + lax.axis_index("c")
    cw_grad = jnp.stack([sm[4, SLAB:], sm[5, :SLAB], sm[5, SLAB:]])
    grads = {
        "ffn1_norm": sm[0:1], "ffn1_w_gate": g_wg1.T, "ffn1_w_up": g_wu1.T, "ffn1_w_down": g_wd1,
        "mix_norm": sm[1:2], "w_in": g_win.T, "conv_w": lax.dynamic_slice_in_dim(cw_grad, me_idx * ncs, ncs, axis=1),
        "conv_b": sm[4:5, :SLAB], "q_norm": fold(sm[6, :SLAB]), "k_norm": fold(sm[6, SLAB:]),
        "w_out": g_wout, "ffn2_norm": sm[2:3], "ffn2_w_gate": g_wg2.T, "ffn2_w_up": g_wu2.T, "ffn2_w_down": g_wd2,
        "ple_norm": sm[3:4], "ple_w_gate": g_wpg, "ple_w_proj": g_wpp.T,
    }

    weights = dict(ffn1_norm=ffn1_norm, ffn1_w_gate=ffn1_w_gate, ffn1_w_up=ffn1_w_up, ffn1_w_down=ffn1_w_down,
                   mix_norm=mix_norm, w_in=w_in, conv_w=conv_w, conv_b=conv_b, q_norm=q_norm, k_norm=k_norm,
                   w_out=w_out, ffn2_norm=ffn2_norm, ffn2_w_gate=ffn2_w_gate, ffn2_w_up=ffn2_w_up,
                   ffn2_w_down=ffn2_w_down, ple_norm=ple_norm, ple_w_gate=ple_w_gate, ple_w_proj=ple_w_proj)
    m_in = dict(ffn1_norm=m_ffn1_norm, ffn1_w_gate=m_ffn1_w_gate, ffn1_w_up=m_ffn1_w_up, ffn1_w_down=m_ffn1_w_down,
                mix_norm=m_mix_norm, w_in=m_w_in, conv_w=m_conv_w, conv_b=m_conv_b, q_norm=m_q_norm,
                k_norm=m_k_norm, w_out=m_w_out, ffn2_norm=m_ffn2_norm, ffn2_w_gate=m_ffn2_w_gate,
                ffn2_w_up=m_ffn2_w_up, ffn2_w_down=m_ffn2_w_down, ple_norm=m_ple_norm, ple_w_gate=m_ple_w_gate,
                ple_w_proj=m_ple_w_proj)
    v_in = dict(ffn1_norm=v_ffn1_norm, ffn1_w_gate=v_ffn1_w_gate, ffn1_w_up=v_ffn1_w_up, ffn1_w_down=v_ffn1_w_down,
                mix_norm=v_mix_norm, w_in=v_w_in, conv_w=v_conv_w, conv_b=v_conv_b, q_norm=v_q_norm,
                k_norm=v_k_norm, w_out=v_w_out, ffn2_norm=v_ffn2_norm, ffn2_w_gate=v_ffn2_w_gate,
                ffn2_w_up=v_ffn2_w_up, ffn2_w_down=v_ffn2_w_down, ple_norm=v_ple_norm, ple_w_gate=v_ple_w_gate,
                ple_w_proj=v_ple_w_proj)
    g_out, d_out, m_out, v_out = [], [], [], []
    for name, w in weights.items():
        w2 = w.reshape(w.shape[-2:])
        g2 = grads[name].reshape(w2.shape)
        dlt, nm, nv = _adamw(f"adamw_{name}", w2, g2, m_in[name].reshape(w2.shape), v_in[name].reshape(w2.shape))
        g_out.append(g2.reshape(w.shape))
        d_out.append(dlt.reshape(w.shape))
        m_out.append(nm.reshape(w.shape))
        v_out.append(nv.reshape(w.shape))
    return (loss, dx0[None], *g_out, *d_out, *m_out, *v_out)
```

```python
import jax
import jax.numpy as jnp
from jax import lax
from jax.experimental import pallas as pl
from jax.experimental.pallas import tpu as pltpu

F32 = jnp.float32
BF16 = jnp.bfloat16

EPS = 1e-6
FFN_RES = 0.5
HEAD_DIM = 64
CONV_DIM = 512
SB_DIM = 512
SLAB = 512
N_DEV = 8
MESH_AXES = ("x", "y", "c")
MESH = pl.DeviceIdType.MESH

ADAM_LR = 0.001
ADAM_B1 = 0.9
ADAM_B2 = 0.999
ADAM_EPS = 1e-08
ADAM_WD = 0.01
ADAM_STEP = 10

VMEM_LIMIT_BYTES = 56 * 1024 * 1024
SUBLANES = 8
LANES = 128


def _cparams(*semantics):
    return pltpu.CompilerParams(dimension_semantics=semantics, vmem_limit_bytes=VMEM_LIMIT_BYTES)


def _dot_nn(a, b):
    return jnp.dot(a, b, preferred_element_type=F32)


def _dot_nt(a, b):
    return lax.dot_general(a, b, (((1,), (1,)), ((), ())), preferred_element_type=F32)


def _dot_tn(a, b):
    return lax.dot_general(a, b, (((0,), (0,)), ((), ())), preferred_element_type=F32)


def _fold8(v):
    rows, cols = v.shape
    return jnp.sum(v.reshape(rows // SUBLANES, SUBLANES, cols), axis=0)


def _split2(v):
    hi = v.astype(BF16)
    lo = (v - hi.astype(F32)).astype(BF16)
    return hi, lo


def _rms_stats(x):
    return lax.rsqrt(jnp.mean(x * x, axis=-1, keepdims=True) + EPS)


def _rms_bwd(dh, x, gain):
    r = _rms_stats(x)
    u = dh * gain
    dx = r * u - x * (r * r * r) * jnp.mean(u * x, axis=-1, keepdims=True)
    return dx, dh * x * r


def _pick(n, pref):
    return pref if n % pref == 0 else n


def _rmsnorm(name, x, gain, tt):
    t, d = x.shape

    def body(x_ref, g_ref, o_ref):
        xv = x_ref[...]
        o_ref[...] = ((xv * _rms_stats(xv)) * g_ref[...]).astype(BF16)

    return pl.pallas_call(
        body, name=name, grid=(t // tt,),
        in_specs=[pl.BlockSpec((tt, d), lambda i: (i, 0)), pl.BlockSpec((1, d), lambda i: (0, 0))],
        out_specs=pl.BlockSpec((tt, d), lambda i: (i, 0)),
        out_shape=jax.ShapeDtypeStruct((t, d), BF16),
        compiler_params=_cparams("parallel"),
    )(x, gain)


class _Exchange:
    FLIPS = [(fx, fy, fc) for fx in (0, 1) for fy in (0, 1) for fc in (0, 1)][1:]

    def __init__(self, arrays, gather):
        self.arrays = list(arrays)
        self.gather = gather
        self.n = len(self.arrays)
        self.rows = [a.shape[0] if gather else a.shape[0] // N_DEV for a in self.arrays]
        self.out_shape = [jax.ShapeDtypeStruct((N_DEV * r, a.shape[1]), a.dtype)
                          for r, a in zip(self.rows, self.arrays)]
        self.scratch = [pltpu.SemaphoreType.DMA((7 * self.n,)), pltpu.SemaphoreType.DMA((7 * self.n,)),
                        pltpu.SemaphoreType.DMA((self.n,))]

    def _copies(self, ins, outs, sems, arrivals):
        send_sems, recv_sems, local_sems = sems
        x, y, c = (lax.axis_index(a) for a in MESH_AXES)
        me_idx = 4 * x + 2 * y + c
        local, send, recv = [], [], []
        for a in range(self.n):
            r = self.rows[a]

            def blk(ref, idx, r=r):
                return ref.at[pl.ds(idx * r, r), :]

            def src(idx, a=a, blk=blk):
                return ins[a] if self.gather else blk(ins[a], idx)

            local.append(pltpu.make_async_copy(src(me_idx), blk(outs[a], me_idx), local_sems.at[a]))
            for k, flip in enumerate(self.FLIPS):
                px, py, pc = (1 - v if f else v for v, f in zip((x, y, c), flip))
                p_idx = 4 * px + 2 * py + pc
                for dst_idx, group in ((me_idx, send), (p_idx, recv))[:2 if arrivals else 1]:
                    group.append(pltpu.make_async_remote_copy(
                        src_ref=src(p_idx), dst_ref=blk(outs[a], dst_idx),
                        send_sem=send_sems.at[7 * a + k], recv_sem=recv_sems.at[7 * a + k],
                        device_id=(px, py, pc), device_id_type=MESH))
        return local, send, recv

    def start(self, ins, outs, sems):
        local, send, _ = self._copies(ins, outs, sems, arrivals=False)
        for cp in local + send:
            cp.start()

    def wait(self, ins, outs, sems):
        local, send, recv = self._copies(ins, outs, sems, arrivals=True)
        for s, r in zip(send, recv):
            r.wait_recv()
            s.wait_send()
        for cp in local:
            cp.wait()

    def attach(self, refs, first, last):
        pl.when(first)(lambda: self.start(*refs))
        pl.when(last)(lambda: self.wait(*refs))


def _split_refs(refs, n_in, n_out, n_scratch, comm):
    nc = comm.n if comm else 0
    ins, rest = refs[:n_in], refs[n_in:]
    c_in, rest = rest[:nc], rest[nc:]
    outs, rest = rest[:n_out], rest[n_out:]
    c_out, rest = rest[:nc], rest[nc:]
    scratch, c_sems = rest[:n_scratch], rest[n_scratch:]
    return ins, outs, scratch, ((c_in, c_out, c_sems) if comm else None)


def _with_comm(comm, in_specs, out_specs, out_shape, scratch):
    if comm is None:
        return in_specs, out_specs, out_shape, scratch, []
    return (in_specs + _any_specs(comm.n), out_specs + _any_specs(comm.n), out_shape + comm.out_shape,
            scratch + comm.scratch, comm.arrays)


def _mm_nt(name, a_list, w_list, pairs, epilogue, out_dtypes, tt, tn, comm=None, tiles=()):
    t = a_list[0].shape[0]
    n = w_list[0].shape[0]
    na, nw, ntile = len(a_list), len(w_list), len(tiles)
    ni, nj = t // tt, n // tn

    def body(*refs):
        ins, o_refs, _, c_refs = _split_refs(refs, na + nw + ntile, len(out_dtypes), 0, comm)
        a_refs, w_refs, t_refs = ins[:na], ins[na:na + nw], ins[na + nw:]
        if comm:
            i = pl.program_id(0)
            comm.attach(c_refs, i == 0, i == ni - 1)
        a_vals = [a_ref[...] for a_ref in a_refs]
        for j in range(nj):
            cols = pl.ds(j * tn, tn)
            accs = [_dot_nt(a_vals[ai], w_refs[wi][cols, :]) for ai, wi in pairs]
            for o_ref, o in zip(o_refs, epilogue(accs, [t_ref[:, cols] for t_ref in t_refs])):
                o_ref[:, cols] = o.astype(o_ref.dtype)

    in_specs = ([pl.BlockSpec((tt, a.shape[1]), lambda i: (i, 0)) for a in a_list]
                + [pl.BlockSpec(w.shape, lambda i: (0, 0), pipeline_mode=pl.Buffered(1)) for w in w_list]
                + [pl.BlockSpec((tt, n), lambda i: (i, 0)) for _ in tiles])
    in_specs, out_specs, out_shape, scratch, extra = _with_comm(
        comm, in_specs, [pl.BlockSpec((tt, n), lambda i: (i, 0)) for _ in out_dtypes],
        [jax.ShapeDtypeStruct((t, n), dt) for dt in out_dtypes], [])
    return pl.pallas_call(
        body, name=name, grid=(ni,), in_specs=in_specs, out_specs=out_specs, out_shape=out_shape,
        scratch_shapes=scratch,
        compiler_params=_cparams("arbitrary" if comm else "parallel"),
    )(*a_list, *w_list, *tiles, *extra)


def _mm_nn(name, pairs, rows, fulls, epilogue, out_kinds, tt, tk, comm=None):
    t, k_total = pairs[0][0].shape
    n = pairs[0][1].shape[1]
    nk = k_total // tk
    nt = t // tt
    npair, nrow, nfull = len(pairs), len(rows), len(fulls)

    def body(*refs):
        ins, o_refs, scratch, c_refs = _split_refs(refs, 2 * npair + nrow + nfull, len(out_kinds), min(nk - 1, 1), comm)
        a_refs, w_refs = ins[:npair], ins[npair:2 * npair]
        r_refs, f_refs = ins[2 * npair:2 * npair + nrow], ins[2 * npair + nrow:]
        i, k = pl.program_id(0), pl.program_id(1)
        if comm:
            comm.attach(c_refs, jnp.logical_and(i == 0, k == 0), jnp.logical_and(i == nt - 1, k == nk - 1))
        s = _dot_nn(a_refs[0][...], w_refs[0][...])
        for a_ref, w_ref in zip(a_refs[1:], w_refs[1:]):
            s = s + _dot_nn(a_ref[...], w_ref[...])

        def finish(acc):
            outs = epilogue(acc, [r[...] for r in r_refs], [f[...] for f in f_refs])
            for o_ref, o in zip(o_refs, outs):
                o_ref[...] = o.astype(o_ref.dtype)

        if nk == 1:
            finish(s)
        else:
            acc_ref = scratch[0]

            @pl.when(k == 0)
            def _():
                acc_ref[...] = s

            @pl.when(k > 0)
            def _():
                acc_ref[...] += s

            @pl.when(k == nk - 1)
            def _():
                finish(acc_ref[...])

    once = dict(pipeline_mode=pl.Buffered(1)) if nk == 1 else {}
    in_specs = ([pl.BlockSpec((tt, tk), lambda i, k: (i, k)) for _ in pairs]
                + [pl.BlockSpec((tk, n), (lambda i, k, off=off: (k + off, 0)), **once) for _, _, off in pairs]
                + [pl.BlockSpec((tt, n), lambda i, k: (i, 0)) for _ in rows]
                + [pl.BlockSpec((1, n), lambda i, k: (0, 0)) for _ in fulls])
    out_specs, out_shape = [], []
    for kind, dt in out_kinds:
        if kind == "tile":
            out_specs.append(pl.BlockSpec((tt, n), lambda i, k: (i, 0)))
            out_shape.append(jax.ShapeDtypeStruct((t, n), dt))
        else:
            out_specs.append(pl.BlockSpec((SUBLANES, n), lambda i, k: (i, 0)))
            out_shape.append(jax.ShapeDtypeStruct((nt * SUBLANES, n), dt))
    in_specs, out_specs, out_shape, scratch, extra = _with_comm(
        comm, in_specs, out_specs, out_shape, [] if nk == 1 else [pltpu.VMEM((tt, n), F32)])
    return pl.pallas_call(
        body, name=name, grid=(nt, nk), in_specs=in_specs, out_specs=out_specs, out_shape=out_shape,
        scratch_shapes=scratch,
        compiler_params=_cparams("arbitrary" if comm else "parallel", "arbitrary"),
    )(*[a for a, _, _ in pairs], *[w for _, w, _ in pairs], *rows, *fulls, *extra)


def _mm_tn(name, a, b, scale, tm, tt, comm=None):
    t, m = a.shape
    n = b.shape[1]
    nt = t // tt
    nm = m // tm

    def body(*refs):
        (a_ref, b_ref), (o_ref,), (acc_ref,), c_refs = _split_refs(refs, 2, 1, 1, comm)
        k = pl.program_id(1)
        if comm:
            i = pl.program_id(0)
            comm.attach(c_refs, jnp.logical_and(i == 0, k == 0), jnp.logical_and(i == nm - 1, k == nt - 1))
        s = _dot_tn(a_ref[...], b_ref[...])

        @pl.when(k == 0)
        def _():
            acc_ref[...] = s

        @pl.when(k > 0)
        def _():
            acc_ref[...] += s

        @pl.when(k == nt - 1)
        def _():
            o_ref[...] = (acc_ref[...] * scale).astype(o_ref.dtype)

    in_specs, out_specs, out_shape, scratch, extra = _with_comm(
        comm, [pl.BlockSpec((tt, tm), lambda i, k: (k, i)), pl.BlockSpec((tt, n), lambda i, k: (k, 0))],
        [pl.BlockSpec((tm, n), lambda i, k: (i, 0))], [jax.ShapeDtypeStruct((m, n), BF16)],
        [pltpu.VMEM((tm, n), F32)])
    out = pl.pallas_call(
        body, name=name, grid=(nm, nt), in_specs=in_specs, out_specs=out_specs, out_shape=out_shape,
        scratch_shapes=scratch,
        compiler_params=_cparams("arbitrary" if comm else "parallel", "arbitrary"),
    )(a, b, *extra)
    return out if comm else out[0]


def _mm_tn_slabs(name, a_list, b, tt, comm=None):
    t, m = a_list[0].shape
    n = b.shape[1]
    na = len(a_list)
    nt = t // tt

    def body(*refs):
        ins, (o_ref,), (acc_ref,), c_refs = _split_refs(refs, na + 1, 1, 1, comm)
        k = pl.program_id(0)
        if comm:
            comm.attach(c_refs, k == 0, k == nt - 1)
        bv = ins[na][...]
        parts = [_dot_tn(a_ref[...], bv) for a_ref in ins[:na]]

        @pl.when(k == 0)
        def _():
            for j, part in enumerate(parts):
                acc_ref[pl.ds(j * m, m), :] = part

        @pl.when(k > 0)
        def _():
            for j, part in enumerate(parts):
                acc_ref[pl.ds(j * m, m), :] += part

        @pl.when(k == nt - 1)
        def _():
            o_ref[...] = acc_ref[...].astype(o_ref.dtype)

    in_specs, out_specs, out_shape, scratch, extra = _with_comm(
        comm, [pl.BlockSpec((tt, m), lambda k: (k, 0))] * na + [pl.BlockSpec((tt, n), lambda k: (k, 0))],
        [pl.BlockSpec((na * m, n), lambda k: (0, 0))], [jax.ShapeDtypeStruct((na * m, n), BF16)],
        [pltpu.VMEM((na * m, n), F32)])
    out = pl.pallas_call(
        body, name=name, grid=(nt,), in_specs=in_specs, out_specs=out_specs, out_shape=out_shape,
        scratch_shapes=scratch, compiler_params=_cparams("arbitrary"),
    )(*a_list, b, *extra)
    return out if comm else out[0]


def _group_sum(v, bd):
    hi, lo = _split2(v)
    return _dot_nn(hi, bd) + _dot_nn(lo, bd)


def _qknorm_fwd(proj, qg, kg, bd, tt):
    t = proj.shape[0]

    def body(q_ref, k_ref, v_ref, qg_ref, kg_ref, bd_ref, qn_ref, kn_ref, vb_ref):
        bdv = bd_ref[...]
        for x_ref, g_ref, o_ref in ((q_ref, qg_ref, qn_ref), (k_ref, kg_ref, kn_ref)):
            xv = x_ref[...]
            r = lax.rsqrt(_group_sum(xv * xv, bdv) * (1.0 / HEAD_DIM) + EPS)
            o_ref[...] = ((xv * r) * g_ref[...]).astype(BF16)
        vb_ref[...] = v_ref[...].astype(BF16)

    slab = lambda s: pl.BlockSpec((tt, SLAB), lambda i, s=s: (i, s))
    full = lambda shape: pl.BlockSpec(shape, lambda i: (0, 0))
    out = pl.BlockSpec((tt, SLAB), lambda i: (i, 0))
    return pl.pallas_call(
        body, name="qknorm_fwd", grid=(t // tt,),
        in_specs=[slab(3), slab(4), slab(5), full((1, SLAB)), full((1, SLAB)), full((SLAB, SLAB))],
        out_specs=[out, out, out],
        out_shape=[jax.ShapeDtypeStruct((t, SLAB), BF16)] * 3,
        compiler_params=_cparams("parallel"),
    )(proj, proj, proj, qg, kg, bd)


def _qknorm_bwd(proj, dqn, dkn, dv, qg, kg, bd, tt):
    t = proj.shape[0]

    def body(q_ref, k_ref, dqn_ref, dkn_ref, dv_ref, qg_ref, kg_ref, bd_ref, dq_ref, dk_ref, dvb_ref, part_ref):
        bdv = bd_ref[...]
        parts = []
        for x_ref, d_ref, g_ref, o_ref in ((q_ref, dqn_ref, qg_ref, dq_ref), (k_ref, dkn_ref, kg_ref, dk_ref)):
            xv, dn = x_ref[...], d_ref[...]
            r = lax.rsqrt(_group_sum(xv * xv, bdv) * (1.0 / HEAD_DIM) + EPS)
            u = dn * g_ref[...]
            dx = r * u - xv * (r * r * r) * (_group_sum(u * xv, bdv) * (1.0 / HEAD_DIM))
            o_ref[...] = dx.astype(BF16)
            parts.append(_fold8(dn * xv * r))
        dvb_ref[...] = dv_ref[...].astype(BF16)
        part_ref[...] = jnp.concatenate(parts, axis=1)

    slab = lambda s: pl.BlockSpec((tt, SLAB), lambda i, s=s: (i, s))
    tile = pl.BlockSpec((tt, SLAB), lambda i: (i, 0))
    full = lambda shape: pl.BlockSpec(shape, lambda i: (0, 0))
    return pl.pallas_call(
        body, name="qknorm_bwd", grid=(t // tt,),
        in_specs=[slab(3), slab(4), tile, tile, tile, full((1, SLAB)), full((1, SLAB)), full((SLAB, SLAB))],
        out_specs=[tile, tile, tile, pl.BlockSpec((SUBLANES, 2 * SLAB), lambda i: (i, 0))],
        out_shape=[jax.ShapeDtypeStruct((t, SLAB), BF16)] * 3
        + [jax.ShapeDtypeStruct((t // tt * SUBLANES, 2 * SLAB), F32)],
        compiler_params=_cparams("parallel"),
    )(proj, proj, dqn, dkn, dv, qg, kg, bd)


def _conv_taps(z, z_prev, row):
    zm1 = jnp.where(row == 0, z_prev[7:8], pltpu.roll(z, 1, 0))
    zm2 = jnp.where(row == 0, z_prev[6:7], jnp.where(row == 1, z_prev[7:8], pltpu.roll(z, 2, 0)))
    return zm1, zm2


def _conv_fwd(proj, cw, cb, tt):
    t = proj.shape[0]
    tb = tt // SUBLANES

    def body(b_ref, c_ref, u_ref, cp_ref, up_ref, cw_ref, cb_ref, o_ref):
        i = pl.program_id(0)
        z = c_ref[...] * u_ref[...]
        z_prev = jnp.where(i > 0, cp_ref[...] * up_ref[...], 0.0)
        row = lax.broadcasted_iota(jnp.int32, (tt, 1), 0)
        zm1, zm2 = _conv_taps(z, z_prev, row)
        y = cw_ref[0:1] * zm2 + cw_ref[1:2] * zm1 + cw_ref[2:3] * z + cb_ref[...]
        o_ref[...] = (b_ref[...] * y).astype(BF16)

    slab = lambda s: pl.BlockSpec((tt, SLAB), lambda i, s=s: (i, s))
    prev = lambda s: pl.BlockSpec((SUBLANES, SLAB), lambda i, s=s: (jnp.maximum(i * tb - 1, 0), s))
    return pl.pallas_call(
        body, name="conv_fwd", grid=(t // tt,),
        in_specs=[slab(0), slab(1), slab(2), prev(1), prev(2),
                  pl.BlockSpec((SUBLANES, SLAB), lambda i: (0, 0)), pl.BlockSpec((1, SLAB), lambda i: (0, 0))],
        out_specs=pl.BlockSpec((tt, SLAB), lambda i: (i, 0)),
        out_shape=jax.ShapeDtypeStruct((t, SLAB), BF16),
        compiler_params=_cparams("parallel"),
    )(proj, proj, proj, proj, proj, cw, cb)


def _conv_bwd(proj, dycat, cw, cb, tt):
    t = proj.shape[0]
    tb = tt // SUBLANES
    nblk = t // SUBLANES

    def body(b_ref, c_ref, u_ref, cp_ref, up_ref, bn_ref, dy_ref, dyn_ref, cw_ref, cb_ref,
             db_ref, dc_ref, du_ref, part_ref):
        i = pl.program_id(0)
        c, u, b, dyc = c_ref[...], u_ref[...], b_ref[...], dy_ref[...]
        z = c * u
        z_prev = jnp.where(i > 0, cp_ref[...] * up_ref[...], 0.0)
        row = lax.broadcasted_iota(jnp.int32, (tt, 1), 0)
        zm1, zm2 = _conv_taps(z, z_prev, row)
        w0, w1, w2 = cw_ref[0:1], cw_ref[1:2], cw_ref[2:3]
        y = w0 * zm2 + w1 * zm1 + w2 * z + cb_ref[...]
        db_ref[...] = (dyc * y).astype(BF16)
        g = dyc * b
        g_next = jnp.where(i < pl.num_programs(0) - 1, dyn_ref[...] * bn_ref[...], 0.0)
        gp1 = jnp.where(row == tt - 1, g_next[0:1], pltpu.roll(g, tt - 1, 0))
        gp2 = jnp.where(row == tt - 2, g_next[0:1], jnp.where(row == tt - 1, g_next[1:2], pltpu.roll(g, tt - 2, 0)))
        dz = w2 * g + w1 * gp1 + w0 * gp2
        dc_ref[...] = (dz * u).astype(BF16)
        du_ref[...] = (dz * c).astype(BF16)
        part_ref[...] = jnp.concatenate([_fold8(g * zm2), _fold8(g * zm1), _fold8(g * z), _fold8(g)], axis=1)

    slab = lambda s: pl.BlockSpec((tt, SLAB), lambda i, s=s: (i, s))
    prev = lambda s: pl.BlockSpec((SUBLANES, SLAB), lambda i, s=s: (jnp.maximum(i * tb - 1, 0), s))
    nxt = lambda s: pl.BlockSpec((SUBLANES, SLAB), lambda i, s=s: (jnp.minimum((i + 1) * tb, nblk - 1), s))
    tile = pl.BlockSpec((tt, SLAB), lambda i: (i, 0))
    return pl.pallas_call(
        body, name="conv_bwd", grid=(t // tt,),
        in_specs=[slab(0), slab(1), slab(2), prev(1), prev(2), nxt(0), slab(0), nxt(0),
                  pl.BlockSpec((SUBLANES, SLAB), lambda i: (0, 0)), pl.BlockSpec((1, SLAB), lambda i: (0, 0))],
        out_specs=[tile, tile, tile, pl.BlockSpec((SUBLANES, 4 * SLAB), lambda i: (i, 0))],
        out_shape=[jax.ShapeDtypeStruct((t, SLAB), BF16)] * 3
        + [jax.ShapeDtypeStruct((t // tt * SUBLANES, 4 * SLAB), F32)],
        compiler_params=_cparams("parallel"),
    )(proj, proj, proj, proj, proj, proj, dycat, dycat, cw, cb)


def _tri_masks(n):
    r = lax.broadcasted_iota(jnp.int32, (n, n), 0)
    c = lax.broadcasted_iota(jnp.int32, (n, n), 1)
    return (r > c).astype(BF16), (r >= c).astype(BF16)


def _sb_logits(z, causal):
    softplus = jnp.maximum(z, 0.0) + jnp.log(1.0 + jnp.exp(-jnp.abs(z)))
    lk = -softplus
    if causal is not None:
        lk = jnp.where(causal, lk, 0.0)
    return (z, lk, *_split2(lk))


def _sb_finish(z, lk, hi, lo, r_run, tri, causal, later):
    later = later + r_run
    ls = z + lk
    arg = ls + later
    if causal is not None:
        arg = jnp.where(causal, arg, -1e30)
    return lk, ls, jnp.exp(arg), later[:, 0:1] + lk[:, 0:1]


SB_DEAD_LOG = -111.0
CHAINS = ((0, 0), (0, 1), (1, 0), (1, 1))


def _all_dead(r_runs):
    m = r_runs[0]
    for r in r_runs[1:]:
        m = jnp.maximum(m, r)
    return (jnp.max(m) < SB_DEAD_LOG).astype(jnp.int32)


def _attn_fwd(qn, kn, vb, tri, sb, comm=None):
    t = qn.shape[0]
    bq = 2 * sb
    scale = HEAD_DIM ** -0.5

    def body(*refs):
        (q_ref, k_ref, v_ref, tri_ref), (o_ref, ob_ref), (acc_ref,), c_refs = _split_refs(refs, 4, 2, 1, comm)
        qi = pl.program_id(1)
        if comm:
            hp = pl.program_id(0)
            comm.attach(c_refs, jnp.logical_and(hp == 0, qi == 0),
                        jnp.logical_and(hp == pl.num_programs(0) - 1, qi == pl.num_programs(1) - 1))
        lane = lax.broadcasted_iota(jnp.int32, (1, LANES), 1)
        hmasks = (lane < HEAD_DIM, lane >= HEAD_DIM)
        diag = lax.broadcasted_iota(jnp.int32, (sb, sb), 1) < lax.broadcasted_iota(jnp.int32, (sb, sb), 0)
        triv = tri_ref[...]
        qs = [jnp.where(hmasks[hh], q_ref[pl.ds(s * sb, sb), :], 0) * scale for s, hh in CHAINS]
        acc_ref[...] = jnp.zeros_like(acc_ref)

        def load_kv(kb):
            ks = kb * sb if isinstance(kb, int) else pl.multiple_of(kb * sb, sb)
            vraw = v_ref[pl.ds(ks, sb), :]
            return k_ref[pl.ds(ks, sb), :], [jnp.where(hm, vraw, 0) for hm in hmasks]

        def run_tiles(tiles, r_in):
            zs = [_dot_nt(qs[c], kv[0]) for c, kv, _, _ in tiles]
            mids = [_sb_logits(z, causal) for z, (_, _, causal, _) in zip(zs, tiles)]
            laters = [_dot_nn(m[2], triv) + _dot_nn(m[3], triv) for m in mids]
            outs = []
            for m, later, (c, _, causal, dep) in zip(mids, laters, tiles):
                outs.append(_sb_finish(*m, r_in[c] if dep is None else outs[dep][3], triv, causal, later))
            for o, (c, kv, _, _) in zip(outs, tiles):
                acc_ref[c] += _dot_nn(o[2].astype(BF16), kv[1][c % 2])
            return [o[3] for o in outs]

        zero = jnp.zeros((sb, 1), F32)
        kv_diag = [load_kv(2 * qi), load_kv(2 * qi + 1)]
        kv_prev = [load_kv(jnp.maximum(2 * qi - 1, 0)), kv_diag[0]]
        has_prev = lax.broadcasted_iota(jnp.int32, (sb, sb), 0) >= jnp.where(qi > 0, 0, sb)
        r_runs = run_tiles([(c, kv_diag[c // 2], diag, None) for c in range(4)]
                           + [(c, kv_prev[c // 2], has_prev if c < 2 else None, c) for c in range(4)], [zero] * 4)[4:]

        def step(carry):
            i, _, *rs = carry
            kvs = [load_kv(2 * qi - 2 - i), load_kv(2 * qi - 1 - i)]
            rs = run_tiles([(c, kvs[c // 2], None, None) for c in range(4)], rs)
            return (i + 1, _all_dead(rs), *rs)

        i_end, _, *rs = lax.while_loop(lambda c: jnp.logical_and(c[0] < 2 * qi - 1, c[1] == 0), step,
                                       (jnp.int32(0), _all_dead(r_runs), *r_runs))

        @pl.when(jnp.logical_and(i_end == 2 * qi - 1, _all_dead(rs[2:]) == 0))
        def _():
            kv_last = load_kv(0)
            run_tiles([(c, kv_last, None, None) for c in (2, 3)], rs)
        for s in range(2):
            out = acc_ref[2 * s] + acc_ref[2 * s + 1]
            o_ref[pl.ds(s * sb, sb), :] = out
            ob_ref[pl.ds(s * sb, sb), :] = out.astype(BF16)

    qspec = pl.BlockSpec((bq, LANES), lambda h, i: (i, h))
    kspec = pl.BlockSpec((t, LANES), lambda h, i: (0, h))
    in_specs, out_specs, out_shape, scratch, extra = _with_comm(
        comm, [qspec, kspec, kspec, pl.BlockSpec((sb, sb), lambda h, i: (0, 0))], [qspec, qspec],
        [jax.ShapeDtypeStruct((t, SB_DIM), F32), jax.ShapeDtypeStruct((t, SB_DIM), BF16)],
        [pltpu.VMEM((4, sb, LANES), F32)])
    return pl.pallas_call(
        body, name="attn_fwd", grid=(SB_DIM // LANES, t // bq),
        in_specs=in_specs, out_specs=out_specs, out_shape=out_shape, scratch_shapes=scratch,
        compiler_params=_cparams("arbitrary" if comm else "parallel", "arbitrary"),
    )(qn, kn, vb, tri, *extra)


def _attn_bwd(qn, kn, vb, o, dycat, tri, tri_inc, sb, comm=None):
    t = qn.shape[0]
    bq = 2 * sb
    scale = HEAD_DIM ** -0.5

    def body(*refs):
        ins, (dq_ref, dk_ref, dv_ref), (dq_acc,), c_refs = _split_refs(refs, 7, 3, 1, comm)
        q_ref, k_ref, v_ref, o_ref, do_ref, tri_ref, tinc_ref = ins
        qi = pl.program_id(1)
        if comm:
            hp = pl.program_id(0)
            comm.attach(c_refs, jnp.logical_and(hp == 0, qi == 0),
                        jnp.logical_and(hp == pl.num_programs(0) - 1, qi == pl.num_programs(1) - 1))

        @pl.when(qi == 0)
        def _():
            dk_ref[...] = jnp.zeros_like(dk_ref)
            dv_ref[...] = jnp.zeros_like(dv_ref)

        lane = lax.broadcasted_iota(jnp.int32, (1, LANES), 1)
        hmasks = (lane < HEAD_DIM, lane >= HEAD_DIM)
        diag = lax.broadcasted_iota(jnp.int32, (sb, sb), 1) < lax.broadcasted_iota(jnp.int32, (sb, sb), 0)
        triv, tincv = tri_ref[...], tinc_ref[...]
        qs, dobs, d_rows = [], [], []
        for s, hh in CHAINS:
            rows = pl.ds(s * sb, sb)
            qs.append(jnp.where(hmasks[hh], q_ref[rows, :], 0) * scale)
            dobs.append(jnp.where(hmasks[hh], do_ref[rows, :], 0.0).astype(BF16))
            d_rows.append(jnp.sum(dobs[-1].astype(F32) * o_ref[rows, :], axis=1, keepdims=True))
        dq_acc[...] = jnp.zeros_like(dq_acc)

        def load_kv(kb):
            ks = kb * sb if isinstance(kb, int) else pl.multiple_of(kb * sb, sb)
            return k_ref[pl.ds(ks, sb), :], v_ref[pl.ds(ks, sb), :], ks

        def run_tiles(tiles, r_in, g_in):
            zs = [_dot_nt(qs[c], kv[0]) for c, kv, _, _ in tiles]
            das = [_dot_nt(dobs[c], kv[1]) for c, kv, _, _ in tiles]
            mids = [_sb_logits(z, causal) for z, (_, _, causal, _) in zip(zs, tiles)]
            laters = [_dot_nn(m[2], triv) + _dot_nn(m[3], triv) for m in mids]
            fins, abs_, es = [], [], []
            for m, later, da, (c, _, causal, dep) in zip(mids, laters, das, tiles):
                fins.append(_sb_finish(*m, r_in[c] if dep is None else fins[dep][3], triv, causal, later))
                abs_.append(fins[-1][2].astype(BF16))
                es.append(da * abs_[-1].astype(F32))
            splits = [_split2(e) for e in es]
            e_sums = [_dot_nn(hi, tincv) + _dot_nn(lo, tincv) for hi, lo in splits]
            e_froms, dzbs = [], []
            for e, e_sum, fin, (c, _, causal, dep) in zip(es, e_sums, fins, tiles):
                e_froms.append(e_sum + (g_in[c] if dep is None else e_froms[dep][:, 0:1]))
                dz = e - jnp.exp(fin[1]) * (e + (d_rows[c] - e_froms[-1]))
                if causal is not None:
                    dz = jnp.where(causal, dz, 0.0)
                dzbs.append(dz.astype(BF16))
            for dzb, (c, kv, _, _) in zip(dzbs, tiles):
                dq_acc[c] += _dot_nn(dzb, kv[0])
            by_rows = {}
            for dzb, ab, (c, kv, _, _) in zip(dzbs, abs_, tiles):
                by_rows.setdefault(id(kv), (kv[2], []))[1].append((_dot_tn(dzb, qs[c]), _dot_tn(ab, dobs[c])))
            for first_row, parts in by_rows.values():
                rows = pl.ds(first_row, sb)
                dk_ref[rows, :] += sum(p[0] for p in parts[1:]) + parts[0][0]
                dv_ref[rows, :] += sum(p[1] for p in parts[1:]) + parts[0][1]
            return [f[3] for f in fins], [ef[:, 0:1] for ef in e_froms]

        zero = jnp.zeros((sb, 1), F32)
        kv_diag = [load_kv(2 * qi), load_kv(2 * qi + 1)]
        kv_prev = [load_kv(jnp.maximum(2 * qi - 1, 0)), kv_diag[0]]
        has_prev = lax.broadcasted_iota(jnp.int32, (sb, sb), 0) >= jnp.where(qi > 0, 0, sb)
        r_first, g_first = run_tiles(
            [(c, kv_diag[c // 2], diag, None) for c in range(4)]
            + [(c, kv_prev[c // 2], has_prev if c < 2 else None, c) for c in range(4)], [zero] * 4, [zero] * 4)
        r_runs, g_runs = r_first[4:], g_first[4:]

        def step(carry):
            i, _, *rg = carry
            kvs = [load_kv(2 * qi - 2 - i), load_kv(2 * qi - 1 - i)]
            rs, gs = run_tiles([(c, kvs[c // 2], None, None) for c in range(4)], rg[:4], rg[4:])
            return (i + 1, _all_dead(rs), *rs, *gs)

        i_end, _, *rg = lax.while_loop(lambda c: jnp.logical_and(c[0] < 2 * qi - 1, c[1] == 0), step,
                                       (jnp.int32(0), _all_dead(r_runs), *r_runs, *g_runs))

        @pl.when(jnp.logical_and(i_end == 2 * qi - 1, _all_dead(rg[2:4]) == 0))
        def _():
            kv_last = load_kv(0)
            run_tiles([(c, kv_last, None, None) for c in (2, 3)], rg[:4], rg[4:])
        for s in range(2):
            dq_ref[pl.ds(s * sb, sb), :] = jnp.where(hmasks[0], dq_acc[2 * s], dq_acc[2 * s + 1]) * scale

    qspec = pl.BlockSpec((bq, LANES), lambda h, i: (i, h))
    dospec = pl.BlockSpec((bq, LANES), lambda h, i: (i, h + CONV_DIM // LANES))
    kspec = pl.BlockSpec((t, LANES), lambda h, i: (0, h))
    full = pl.BlockSpec((sb, sb), lambda h, i: (0, 0))
    in_specs, out_specs, out_shape, scratch, extra = _with_comm(
        comm, [qspec, kspec, kspec, qspec, dospec, full, full], [qspec, kspec, kspec],
        [jax.ShapeDtypeStruct((t, SB_DIM), F32)] * 3, [pltpu.VMEM((4, sb, LANES), F32)])
    return pl.pallas_call(
        body, name="attn_bwd", grid=(SB_DIM // LANES, t // bq),
        in_specs=in_specs, out_specs=out_specs, out_shape=out_shape, scratch_shapes=scratch,
        compiler_params=_cparams("arbitrary" if comm else "parallel", "arbitrary"),
    )(qn, kn, vb, o, dycat, tri, tri_inc, *extra)


def _ple_loss(x3, p2, tgt, gain, wpg, wppt, tt):
    t, d = x3.shape
    pdim = p2.shape[1]
    nt = t // tt

    def body(x_ref, p_ref, t_ref, g_ref, wg_ref, wp_ref,
             dx_ref, dxb_ref, dwg_ref, dwp_ref, gpart_ref, lpart_ref, accg_ref, accp_ref):
        i = pl.program_id(0)
        xv, gain_v = x_ref[...], g_ref[...]
        hb = ((xv * _rms_stats(xv)) * gain_v).astype(BF16)
        gate = jax.nn.sigmoid(_dot_nn(hb, wg_ref[...]))
        pb = p_ref[...].astype(BF16)
        pe = _dot_nt(pb, wp_ref[...])
        diff = xv + gate * pe - t_ref[...]
        lsum = jnp.sum(_fold8(diff * diff), axis=1, keepdims=True) * (0.5 / d)
        lpart_ref[...] = jnp.broadcast_to(lsum, (SUBLANES, LANES))
        dy = diff * (1.0 / d)
        dgz = ((dy * pe) * gate * (1.0 - gate)).astype(BF16)
        dpe = (dy * gate).astype(BF16)
        dx_n, grow = _rms_bwd(_dot_nt(dgz, wg_ref[...]), xv, gain_v)
        dx = dy + dx_n
        dx_ref[...] = dx
        dxb_ref[...] = dx.astype(BF16)
        gpart_ref[...] = _fold8(grow)
        sg = _dot_tn(hb, dgz)
        sp = _dot_tn(dpe, pb)

        @pl.when(i == 0)
        def _():
            accg_ref[...] = sg
            accp_ref[...] = sp

        @pl.when(i > 0)
        def _():
            accg_ref[...] += sg
            accp_ref[...] += sp

        @pl.when(i == nt - 1)
        def _():
            dwg_ref[...] = accg_ref[...].astype(BF16)
            dwp_ref[...] = accp_ref[...].astype(BF16)

    tile = lambda w: pl.BlockSpec((tt, w), lambda i: (i, 0))
    full = lambda shape: pl.BlockSpec(shape, lambda i: (0, 0))
    return pl.pallas_call(
        body, name="ple_loss", grid=(nt,),
        in_specs=[tile(d), tile(pdim), tile(d), full((1, d)),
                  pl.BlockSpec((d, d), lambda i: (0, 0), pipeline_mode=pl.Buffered(1)),
                  pl.BlockSpec((d, pdim), lambda i: (0, 0), pipeline_mode=pl.Buffered(1))],
        out_specs=[tile(d), tile(d), full((d, d)), full((d, pdim)),
                   pl.BlockSpec((SUBLANES, d), lambda i: (i, 0)), pl.BlockSpec((SUBLANES, LANES), lambda i: (i, 0))],
        out_shape=[jax.ShapeDtypeStruct((t, d), F32), jax.ShapeDtypeStruct((t, d), BF16),
                   jax.ShapeDtypeStruct((d, d), BF16), jax.ShapeDtypeStruct((d, pdim), BF16),
                   jax.ShapeDtypeStruct((nt * SUBLANES, d), F32), jax.ShapeDtypeStruct((nt * SUBLANES, LANES), F32)],
        scratch_shapes=[pltpu.VMEM((d, d), F32), pltpu.VMEM((d, pdim), F32)],
        compiler_params=_cparams("arbitrary"),
    )(x3, p2, tgt, gain, wpg, wppt)


def _pack_small(parts_gain, conv_part, qk_part):
    d = parts_gain[0].shape[1]
    ng = len(parts_gain)

    def body(*refs):
        g_refs, conv_ref, qk_ref, o_ref = refs[:ng], refs[ng], refs[ng + 1], refs[ng + 2]
        rows = [jnp.sum(r[...], axis=0, keepdims=True) for r in g_refs]
        cs = jnp.sum(conv_ref[...], axis=0, keepdims=True)
        qs = jnp.sum(qk_ref[...], axis=0, keepdims=True)
        rows.append(jnp.concatenate([cs[:, 3 * SLAB:], cs[:, :SLAB]], axis=1))
        rows.append(cs[:, SLAB:3 * SLAB])
        rows.append(qs)
        rid = lax.broadcasted_iota(jnp.int32, (2 * SUBLANES, 1), 0)
        out = jnp.zeros((2 * SUBLANES, d), F32)
        for idx, r in enumerate(rows):
            out = jnp.where(rid == idx, r, out)
        o_ref[...] = out

    return pl.pallas_call(
        body, name="pack_small", out_shape=jax.ShapeDtypeStruct((2 * SUBLANES, d), F32),
    )(*parts_gain, conv_part, qk_part)


def _sum_slots(name, slots, out_dtype=F32):
    _, r, c = slots.shape

    def body(s_ref, o_ref):
        acc = s_ref[0].astype(F32)
        for d in range(1, N_DEV):
            acc = acc + s_ref[d].astype(F32)
        o_ref[...] = acc.astype(o_ref.dtype)

    return pl.pallas_call(body, name=name, out_shape=jax.ShapeDtypeStruct((r, c), out_dtype),
                          compiler_params=pltpu.CompilerParams(vmem_limit_bytes=VMEM_LIMIT_BYTES))(slots)


def _adamw(name, w, g, m, v):
    c1 = 1.0 - ADAM_B1 ** ADAM_STEP
    c2 = 1.0 - ADAM_B2 ** ADAM_STEP

    def body(w_ref, g_ref, m_ref, v_ref, go_ref, d_ref, nm_ref, nv_ref):
        if g.ndim == 3:
            gv = g_ref[0].astype(F32)
            for dev in range(1, N_DEV):
                gv = gv + g_ref[dev].astype(F32)
        else:
            gv = g_ref[...]
        go_ref[...] = gv
        nm = ADAM_B1 * m_ref[...] + (1.0 - ADAM_B1) * gv
        nv = ADAM_B2 * v_ref[...] + (1.0 - ADAM_B2) * (gv * gv)
        d_ref[...] = -ADAM_LR * ((nm / c1) / (jnp.sqrt(nv / c2) + ADAM_EPS) + ADAM_WD * w_ref[...])
        nm_ref[...] = nm
        nv_ref[...] = nv

    return pl.pallas_call(body, name=name, out_shape=[jax.ShapeDtypeStruct(w.shape, F32)] * 4,
                          compiler_params=pltpu.CompilerParams(vmem_limit_bytes=VMEM_LIMIT_BYTES))(w, g, m, v)


def _any_specs(n):
    return [pl.BlockSpec(memory_space=pl.ANY)] * n


def _all_gather(name, shards):
    n = len(shards)

    def body(*refs):
        ins, outs = refs[:n], refs[n:2 * n]
        send_sems, recv_sems, local_sems = refs[2 * n:]
        x, y, c = (lax.axis_index(a) for a in MESH_AXES)
        me, sibling = (x, y, c), (x, y, 1 - c)
        chips = [(1 - x, y), (x, 1 - y), (1 - x, 1 - y)]

        def rows(a, px, py, pc):
            r = ins[a].shape[0]
            return outs[a].at[pl.ds((4 * px + 2 * py + pc) * r, r), :]

        def copy(a, k, block, to, src=None):
            return pltpu.make_async_remote_copy(
                src_ref=rows(a, *block) if src is None else src, dst_ref=rows(a, *block),
                send_sem=send_sems.at[7 * a + k], recv_sem=recv_sems.at[7 * a + k],
                device_id=to, device_id_type=MESH)

        mine = [pltpu.make_async_copy(ins[a], rows(a, *me), local_sems.at[a]) for a in range(n)]
        for cp in mine:
            cp.start()
        first = []
        for a in range(n):
            first.append(copy(a, 0, me, sibling, src=ins[a]))
            first += [copy(a, 1 + j, me, (*chip, c), src=ins[a]) for j, chip in enumerate(chips)]
        for cp in first:
            cp.start()
        passed = []
        for j, chip in enumerate(chips):
            for a in range(n):
                copy(a, 1 + j, (*chip, c), me).wait_recv()
                fwd = copy(a, 4 + j, (*chip, c), sibling)
                fwd.start()
                passed.append(fwd)
        for a in range(n):
            copy(a, 0, sibling, me).wait_recv()
            for j, chip in enumerate(chips):
                copy(a, 4 + j, (*chip, 1 - c), me).wait_recv()
        for cp in first + passed:
            cp.wait_send()
        for cp in mine:
            cp.wait()

    return pl.pallas_call(
        body, name=name, in_specs=_any_specs(n), out_specs=_any_specs(n),
        out_shape=[jax.ShapeDtypeStruct((N_DEV * s.shape[0], s.shape[1]), s.dtype) for s in shards],
        scratch_shapes=[pltpu.SemaphoreType.DMA((7 * n,)), pltpu.SemaphoreType.DMA((7 * n,)),
                        pltpu.SemaphoreType.DMA((n,))],
    )(*shards)


def _residual_and_norm(res_scale):
    def epilogue(acc, rows, fulls):
        out = rows[0] + res_scale * acc
        return [out] + [(out * _rms_stats(out)) * gain for gain in fulls]
    return epilogue


def _ffn_fwd(tag, x, h, wgt, wut, wd, next_gain, tt_nt, tt_nn, comm_gate=None, comm_down=None):
    f = wgt.shape[0]

    def gate_up(accs, _):
        return [accs[0], accs[1], jax.nn.silu(accs[0]) * accs[1]]

    g, u, a, *got_gate = _mm_nt(f"{tag}_gate_up", [h], [wgt, wut], [(0, 0), (0, 1)], gate_up, [BF16] * 3,
                                tt_nn, _pick(f, 256), comm_gate)
    if wd is None:
        wd = got_gate[0]
    gains = [] if next_gain is None else [next_gain]
    out, *rest = _mm_nn(f"{tag}_down", [(a, wd, 0)], [x], gains, _residual_and_norm(FFN_RES),
                        [("tile", F32)] + [("tile", BF16)] * len(gains), tt_nt, f, comm_down)
    h_next = rest.pop(0) if gains else None
    return out, h_next, (g, u, a), got_gate, rest


def _norm_bwd_epilogue(acc, rows, fulls):
    x_in, dy = rows
    dx_n, grow = _rms_bwd(acc, x_in, fulls[0])
    dx = dy + dx_n
    return [dx, dx, _fold8(grow)]


_NORM_BWD_OUTS = [("tile", F32), ("tile", BF16), ("part", F32)]


def _ffn_bwd(tag, x_in, h, hidden, dy, dyb, gain, wgt, wut, wd, tt_nt, tt_nn, riders=(None, None),
             exchange_own=False):
    g, u, a = hidden
    f = wgt.shape[0]
    tt_tn = _pick(h.shape[0], 2 * tt_nt)
    own = (lambda arr: _Exchange([arr], gather=False)) if exchange_own else (lambda arr: None)

    def carried(result, rider):
        return (result[0], result[1:]) if rider else (result, [])

    def hidden_grads(accs, tiles):
        da = FFN_RES * accs[0]
        gv, uv = tiles[0].astype(F32), tiles[1].astype(F32)
        sg = jax.nn.sigmoid(gv)
        s = gv * sg
        return [da * uv * (sg * (1.0 + gv * (1.0 - sg))), da * s]

    dwd, got_dwd = carried(_mm_tn(f"{tag}_dwd", a, dyb, FFN_RES, f // 2, tt_tn, riders[0]), riders[0])
    dg, du, *x_dwd = _mm_nt(f"{tag}_bwd_hidden", [dyb], [wd], [(0, 0)], hidden_grads, [BF16, BF16],
                            tt_nn, _pick(f, 256), own(dwd), tiles=[g, u])
    dwg, got_dwg = carried(_mm_tn(f"{tag}_dwg", dg, h, 1.0, f // 2, tt_tn, riders[1]), riders[1])
    dwu, x_dwg = carried(_mm_tn(f"{tag}_dwu", du, h, 1.0, f // 2, tt_tn, own(dwg)), exchange_own)
    dx, dxb, gpart, *x_dwu = _mm_nn(f"{tag}_bwd_dx", [(dg, wgt, 0), (du, wut, 0)], [x_in, dy], [gain],
                                    _norm_bwd_epilogue, _NORM_BWD_OUTS, tt_nn, f, own(dwu))
    grads = [*x_dwg, *x_dwu, *x_dwd] if exchange_own else [dwg, dwu, dwd]
    return dx, dxb, gpart, grads, [*got_dwd, *got_dwg]


def kernel(x, p, ffn1_norm, ffn1_w_gate, ffn1_w_up, ffn1_w_down, mix_norm, w_in, conv_w, conv_b, q_norm, k_norm, w_out, ffn2_norm, ffn2_w_gate, ffn2_w_up, ffn2_w_down, ple_norm, ple_w_gate, ple_w_proj, loss_target, m_ffn1_norm, m_ffn1_w_gate, m_ffn1_w_up, m_ffn1_w_down, m_mix_norm, m_w_in, m_conv_w, m_conv_b, m_q_norm, m_k_norm, m_w_out, m_ffn2_norm, m_ffn2_w_gate, m_ffn2_w_up, m_ffn2_w_down, m_ple_norm, m_ple_w_gate, m_ple_w_proj, v_ffn1_norm, v_ffn1_w_gate, v_ffn1_w_up, v_ffn1_w_down, v_mix_norm, v_w_in, v_conv_w, v_conv_b, v_q_norm, v_k_norm, v_w_out, v_ffn2_norm, v_ffn2_w_gate, v_ffn2_w_up, v_ffn2_w_down, v_ple_norm, v_ple_w_gate, v_ple_w_proj):
    x0, p2, tgt = x[0], p[0, 0], loss_target[0]
    t, d = x0.shape
    tt_nt = _pick(t, 1024)
    tt_nn = _pick(t, 512)
    tt_ew = _pick(t, 512)
    tt_ple = _pick(t, 512)
    sb = _pick(t // 2, 256)

    t_bf = lambda w: w[0].T.astype(BF16)
    n_bf = lambda w: w[0].astype(BF16)
    cw_tile = jnp.zeros((SUBLANES, LANES), F32).at[:conv_w.shape[1], :conv_w.shape[2]].set(conv_w[0])
    wg1t, wu1t = _all_gather("gather_ffn1_weights", [t_bf(ffn1_w_gate), t_bf(ffn1_w_up)])
    gather_down = _Exchange([n_bf(ffn1_w_down), n_bf(w_out), cw_tile], gather=True)
    gather_in = _Exchange([t_bf(w_in)], gather=True)
    gather_late = _Exchange([t_bf(ffn2_w_gate), t_bf(ffn2_w_up), n_bf(ffn2_w_down), n_bf(ple_w_gate),
                             t_bf(ple_w_proj)], gather=True)
    ncs = conv_w.shape[2]

    qg = jnp.tile(q_norm, (1, SB_DIM // HEAD_DIM))
    kg = jnp.tile(k_norm, (1, SB_DIM // HEAD_DIM))
    gi = lax.broadcasted_iota(jnp.int32, (SLAB, SLAB), 0) // HEAD_DIM
    gj = lax.broadcasted_iota(jnp.int32, (SLAB, SLAB), 1) // HEAD_DIM
    bd = (gi == gj).astype(BF16)
    tri, tri_inc = _tri_masks(sb)

    h1 = _rmsnorm("ffn1_norm", x0, ffn1_norm, tt_ew)
    x1, h2, hidden1, (wd1, wout, cw_all), (wint,) = _ffn_fwd(
        "ffn1", x0, h1, wg1t, wu1t, None, mix_norm, tt_nt, tt_nn, gather_down, gather_in)
    cw_full = cw_all.reshape(N_DEV, SUBLANES, LANES)[:, :, :ncs].transpose(1, 0, 2).reshape(SUBLANES, N_DEV * ncs)
    (proj,) = _mm_nt("in_proj", [h2], [wint], [(0, 0)], lambda accs, _: accs, [F32], tt_nn, SLAB)
    y_conv = _conv_fwd(proj, cw_full, conv_b, tt_ew)
    qn, kn, vb = _qknorm_fwd(proj, qg, kg, bd, tt_ew)
    o, ob, wg2t, wu2t, wd2, wpg, wppt = _attn_fwd(qn, kn, vb, tri, sb, gather_late)
    x2, h3 = _mm_nn("out_proj", [(y_conv, wout, 0), (ob, wout, 1)], [x1], [ffn2_norm], _residual_and_norm(1.0),
                    [("tile", F32), ("tile", BF16)], tt_nn, SLAB)
    x3, _, hidden2, _, _ = _ffn_fwd("ffn2", x2, h3, wg2t, wu2t, wd2, None, tt_nt, tt_nn)

    dx3, dx3b, dwpg, dwppt, gp_ple, lpart = _ple_loss(x3, p2, tgt, ple_norm, wpg, wppt, tt_ple)
    loss = lax.psum(jnp.sum(lpart[:, 0]), MESH_AXES)
    dx2, dx2b, gp_ffn2, (dwg2, dwu2, dwd2), _ = _ffn_bwd("ffn2", x2, h3, hidden2, dx3, dx3b, ffn2_norm,
                                                         wg2t, wu2t, wd2, tt_nt, tt_nn)
    (dycat,) = _mm_nt("out_proj_bwd", [dx2b], [wout], [(0, 0)], lambda accs, _: accs, [F32], tt_nn, SLAB)
    dwout = _mm_tn_slabs("dwout", [y_conv, ob], dx2b, tt_nt)
    dqn, dkn, dv, *slots_late = _attn_bwd(qn, kn, vb, o, dycat, tri, tri_inc, sb,
                                          _Exchange([dwg2, dwu2, dwd2, dwpg, dwppt], gather=False))
    dq, dk, dvb, qk_part = _qknorm_bwd(proj, dqn, dkn, dv, qg, kg, bd, tt_ew)
    db, dc, du, conv_part = _conv_bwd(proj, dycat, cw_full, conv_b, tt_ew)
    dproj = [db, dc, du, dq, dk, dvb]
    dwin, slot_wout = _mm_tn_slabs("dwin", dproj, h2, tt_nt, _Exchange([dwout], gather=False))
    dx1, dx1b, gp_mix, slot_win = _mm_nn(
        "in_proj_bwd", [(dp, wint, s) for s, dp in enumerate(dproj)], [x1, dx2], [mix_norm],
        _norm_bwd_epilogue, _NORM_BWD_OUTS, tt_nn, SLAB, _Exchange([dwin], gather=False))
    dx0, _, gp_ffn1, slots_ffn1, _ = _ffn_bwd(
        "ffn1", x0, h1, hidden1, dx1, dx1b, ffn1_norm, wg1t, wu1t, wd1, tt_nt, tt_nn, exchange_own=True)

    slots = [*slots_ffn1, slot_win, slot_wout, *slots_late]
    per_dev = [s.reshape(N_DEV, s.shape[0] // N_DEV, s.shape[1]) for s in slots]
    c_wg1, c_wu1, c_wd1, c_win, c_wout, c_wg2, c_wu2, c_wd2, c_wpg, c_wpp = per_dev
    g_win, g_wpp = _sum_slots("sum_grads_w_in", c_win), _sum_slots("sum_grads_ple_w_proj", c_wpp)
    small = _pack_small([gp_ffn1, gp_mix, gp_ffn2, gp_ple], conv_part, qk_part)
    (small_all,) = _all_gather("gather_small_grads", [small])
    sm = _sum_slots("sum_small_grads", small_all.reshape(N_DEV, 2 * SUBLANES, d))
    fold = lambda r: r.reshape(SB_DIM // HEAD_DIM, HEAD_DIM).sum(axis=0)[None]
    me_idx = 4 * lax.axis_index("x") + 2 * lax.axis_index("y") + lax.axis_index("c")
    cw_grad = jnp.stack([sm[4, SLAB:], sm[5, :SLAB], sm[5, SLAB:]])
    grads = {
        "ffn1_norm": sm[0:1], "ffn1_w_down": c_wd1,
        "mix_norm": sm[1:2], "w_in": g_win.T, "conv_w": lax.dynamic_slice_in_dim(cw_grad, me_idx * ncs, ncs, axis=1),
        "conv_b": sm[4:5, :SLAB], "q_norm": fold(sm[6, :SLAB]), "k_norm": fold(sm[6, SLAB:]),
        "w_out": c_wout, "ffn2_norm": sm[2:3], "ffn2_w_down": c_wd2,
        "ple_norm": sm[3:4], "ple_w_gate": c_wpg, "ple_w_proj": g_wpp.T,
    }
    grads_t = {"ffn1_w_gate": c_wg1, "ffn1_w_up": c_wu1, "ffn2_w_gate": c_wg2, "ffn2_w_up": c_wu2}

    weights = dict(ffn1_norm=ffn1_norm, ffn1_w_gate=ffn1_w_gate, ffn1_w_up=ffn1_w_up, ffn1_w_down=ffn1_w_down,
                   mix_norm=mix_norm, w_in=w_in, conv_w=conv_w, conv_b=conv_b, q_norm=q_norm, k_norm=k_norm,
                   w_out=w_out, ffn2_norm=ffn2_norm, ffn2_w_gate=ffn2_w_gate, ffn2_w_up=ffn2_w_up,
                   ffn2_w_down=ffn2_w_down, ple_norm=ple_norm, ple_w_gate=ple_w_gate, ple_w_proj=ple_w_proj)
    m_in = dict(ffn1_norm=m_ffn1_norm, ffn1_w_gate=m_ffn1_w_gate, ffn1_w_up=m_ffn1_w_up, ffn1_w_down=m_ffn1_w_down,
                mix_norm=m_mix_norm, w_in=m_w_in, conv_w=m_conv_w, conv_b=m_conv_b, q_norm=m_q_norm,
                k_norm=m_k_norm, w_out=m_w_out, ffn2_norm=m_ffn2_norm, ffn2_w_gate=m_ffn2_w_gate,
                ffn2_w_up=m_ffn2_w_up, ffn2_w_down=m_ffn2_w_down, ple_norm=m_ple_norm, ple_w_gate=m_ple_w_gate,
                ple_w_proj=m_ple_w_proj)
    v_in = dict(ffn1_norm=v_ffn1_norm, ffn1_w_gate=v_ffn1_w_gate, ffn1_w_up=v_ffn1_w_up, ffn1_w_down=v_ffn1_w_down,
                mix_norm=v_mix_norm, w_in=v_w_in, conv_w=v_conv_w, conv_b=v_conv_b, q_norm=v_q_norm,
                k_norm=v_k_norm, w_out=v_w_out, ffn2_norm=v_ffn2_norm, ffn2_w_gate=v_ffn2_w_gate,
                ffn2_w_up=v_ffn2_w_up, ffn2_w_down=v_ffn2_w_down, ple_norm=v_ple_norm, ple_w_gate=v_ple_w_gate,
                ple_w_proj=v_ple_w_proj)
    g_out, d_out, m_out, v_out = [], [], [], []
    for name, w in weights.items():
        if name in grads_t:
            view, back = (lambda a: a[0].T), (lambda a: a.T[None])
            g_in = grads_t[name]
        else:
            view, back = (lambda a, w=w: a.reshape(w.shape[-2:])), (lambda a, w=w: a.reshape(w.shape))
            g_in = grads[name] if grads[name].ndim == 3 else view(grads[name])
        g2, dlt, nm, nv = _adamw(f"adamw_{name}", view(w), g_in, view(m_in[name]), view(v_in[name]))
        g_out.append(back(g2))
        d_out.append(back(dlt))
        m_out.append(back(nm))
        v_out.append(back(nv))
    return (loss, dx0[None], *g_out, *d_out, *m_out, *v_out)
```

```python
import jax
import jax.numpy as jnp
from jax import lax
from jax.experimental import pallas as pl
from jax.experimental.pallas import tpu as pltpu

F32 = jnp.float32
BF16 = jnp.bfloat16

EPS = 1e-6
FFN_RES = 0.5
HEAD_DIM = 64
CONV_DIM = 512
SB_DIM = 512
SLAB = 512
N_DEV = 8
MESH_AXES = ("x", "y", "c")
MESH = pl.DeviceIdType.MESH

ADAM_LR = 0.001
ADAM_B1 = 0.9
ADAM_B2 = 0.999
ADAM_EPS = 1e-08
ADAM_WD = 0.01
ADAM_STEP = 10

VMEM_LIMIT_BYTES = 56 * 1024 * 1024
SUBLANES = 8
LANES = 128


def _cparams(*semantics):
    return pltpu.CompilerParams(dimension_semantics=semantics, vmem_limit_bytes=VMEM_LIMIT_BYTES)


def _dot_nn(a, b):
    return jnp.dot(a, b, preferred_element_type=F32)


def _dot_nt(a, b):
    return lax.dot_general(a, b, (((1,), (1,)), ((), ())), preferred_element_type=F32)


def _dot_tn(a, b):
    return lax.dot_general(a, b, (((0,), (0,)), ((), ())), preferred_element_type=F32)


def _fold8(v):
    rows, cols = v.shape
    return jnp.sum(v.reshape(rows // SUBLANES, SUBLANES, cols), axis=0)


def _split2(v):
    hi = v.astype(BF16)
    lo = (v - hi.astype(F32)).astype(BF16)
    return hi, lo


def _rms_stats(x):
    return lax.rsqrt(jnp.mean(x * x, axis=-1, keepdims=True) + EPS)


def _rms_bwd(dh, x, gain):
    r = _rms_stats(x)
    u = dh * gain
    dx = r * u - x * (r * r * r) * jnp.mean(u * x, axis=-1, keepdims=True)
    return dx, dh * x * r


def _pick(n, pref):
    return pref if n % pref == 0 else n


def _rmsnorm(name, x, gain, tt, comm=None):
    t, d = x.shape
    nt = t // tt

    def body(*refs):
        (x_ref, g_ref), (o_ref,), _, c_refs = _split_refs(refs, 2, 1, 0, comm)
        if comm:
            comm.attach(c_refs, pl.program_id(0), nt)
        xv = x_ref[...]
        o_ref[...] = ((xv * _rms_stats(xv)) * g_ref[...]).astype(BF16)

    in_specs, out_specs, out_shape, scratch, extra = _with_comm(
        comm, [pl.BlockSpec((tt, d), lambda i: (i, 0)), pl.BlockSpec((1, d), lambda i: (0, 0))],
        [pl.BlockSpec((tt, d), lambda i: (i, 0))], [jax.ShapeDtypeStruct((t, d), BF16)], [])
    return pl.pallas_call(
        body, name=name, grid=(nt,), in_specs=in_specs, out_specs=out_specs, out_shape=out_shape,
        scratch_shapes=scratch, compiler_params=_cparams("arbitrary" if comm else "parallel"),
    )(x, gain, *extra)


class _Exchange:
    FLIPS = [(fx, fy, fc) for fx in (0, 1) for fy in (0, 1) for fc in (0, 1)][1:]

    def __init__(self, arrays, gather):
        self.arrays = list(arrays)
        self.gather = gather
        self.n = len(self.arrays)
        self.rows = [a.shape[0] if gather else a.shape[0] // N_DEV for a in self.arrays]
        self.out_shape = [jax.ShapeDtypeStruct((N_DEV * r, a.shape[1]), a.dtype)
                          for r, a in zip(self.rows, self.arrays)]
        self.scratch = [pltpu.SemaphoreType.DMA((7 * self.n,)), pltpu.SemaphoreType.DMA((7 * self.n,)),
                        pltpu.SemaphoreType.DMA((self.n,))]

    def _copies(self, ins, outs, sems, arrivals):
        send_sems, recv_sems, local_sems = sems
        x, y, c = (lax.axis_index(a) for a in MESH_AXES)
        me_idx = 4 * x + 2 * y + c
        local, send, recv = [], [], []
        for a in range(self.n):
            r = self.rows[a]

            def blk(ref, idx, r=r):
                return ref.at[pl.ds(idx * r, r), :]

            def src(idx, a=a, blk=blk):
                return ins[a] if self.gather else blk(ins[a], idx)

            local.append(pltpu.make_async_copy(src(me_idx), blk(outs[a], me_idx), local_sems.at[a]))
            for k, flip in enumerate(self.FLIPS):
                px, py, pc = (1 - v if f else v for v, f in zip((x, y, c), flip))
                p_idx = 4 * px + 2 * py + pc
                for dst_idx, group in ((me_idx, send), (p_idx, recv))[:2 if arrivals else 1]:
                    group.append(pltpu.make_async_remote_copy(
                        src_ref=src(p_idx), dst_ref=blk(outs[a], dst_idx),
                        send_sem=send_sems.at[7 * a + k], recv_sem=recv_sems.at[7 * a + k],
                        device_id=(px, py, pc), device_id_type=MESH))
        return local, send, recv

    def start(self, ins, outs, sems):
        local, send, _ = self._copies(ins, outs, sems, arrivals=False)
        for cp in local + send:
            cp.start()

    def wait(self, ins, outs, sems):
        local, send, recv = self._copies(ins, outs, sems, arrivals=True)
        for s, r in zip(send, recv):
            r.wait_recv()
            s.wait_send()
        for cp in local:
            cp.wait()

    def attach(self, refs, step, n_steps):
        pl.when(step == 0)(lambda: self.start(*refs))
        pl.when(step == n_steps - 1)(lambda: self.wait(*refs))


class _GatherTwoLevel:
    def __init__(self, arrays):
        self.arrays = list(arrays)
        self.n = len(self.arrays)
        self.out_shape = [jax.ShapeDtypeStruct((N_DEV * a.shape[0], a.shape[1]), a.dtype) for a in self.arrays]
        self.scratch = [pltpu.SemaphoreType.DMA((7 * self.n,)), pltpu.SemaphoreType.DMA((7 * self.n,)),
                        pltpu.SemaphoreType.DMA((self.n,))]

    def _phase(self, refs, phase):
        ins, outs, (send_sems, recv_sems, local_sems) = refs
        x, y, c = (lax.axis_index(a) for a in MESH_AXES)
        me, sibling = (x, y, c), (x, y, 1 - c)
        chips = [(1 - x, y), (x, 1 - y), (1 - x, 1 - y)]

        def rows(a, px, py, pc):
            r = ins[a].shape[0]
            return outs[a].at[pl.ds((4 * px + 2 * py + pc) * r, r), :]

        def copy(a, k, block, to, src=None):
            return pltpu.make_async_remote_copy(
                src_ref=rows(a, *block) if src is None else src, dst_ref=rows(a, *block),
                send_sem=send_sems.at[7 * a + k], recv_sem=recv_sems.at[7 * a + k],
                device_id=to, device_id_type=MESH)

        for a in range(self.n):
            if phase == "start":
                pltpu.make_async_copy(ins[a], rows(a, *me), local_sems.at[a]).start()
                copy(a, 0, me, sibling, src=ins[a]).start()
                for j, chip in enumerate(chips):
                    copy(a, 1 + j, me, (*chip, c), src=ins[a]).start()
            elif phase == "forward":
                for j, chip in enumerate(chips):
                    copy(a, 1 + j, (*chip, c), me).wait_recv()
                    copy(a, 4 + j, (*chip, c), sibling).start()
            else:
                copy(a, 0, sibling, me).wait_recv()
                copy(a, 0, me, sibling, src=ins[a]).wait_send()
                for j, chip in enumerate(chips):
                    copy(a, 4 + j, (*chip, 1 - c), me).wait_recv()
                    copy(a, 1 + j, me, (*chip, c), src=ins[a]).wait_send()
                    copy(a, 4 + j, (*chip, c), sibling).wait_send()
                pltpu.make_async_copy(ins[a], rows(a, *me), local_sems.at[a]).wait()

    def attach(self, refs, step, n_steps):
        for when, phase in ((0, "start"), (n_steps // 2, "forward"), (n_steps - 1, "wait")):
            pl.when(step == when)(lambda phase=phase: self._phase(refs, phase))


def _split_refs(refs, n_in, n_out, n_scratch, comm):
    nc = comm.n if comm else 0
    ins, rest = refs[:n_in], refs[n_in:]
    c_in, rest = rest[:nc], rest[nc:]
    outs, rest = rest[:n_out], rest[n_out:]
    c_out, rest = rest[:nc], rest[nc:]
    scratch, c_sems = rest[:n_scratch], rest[n_scratch:]
    return ins, outs, scratch, ((c_in, c_out, c_sems) if comm else None)


def _with_comm(comm, in_specs, out_specs, out_shape, scratch):
    if comm is None:
        return in_specs, out_specs, out_shape, scratch, []
    return (in_specs + _any_specs(comm.n), out_specs + _any_specs(comm.n), out_shape + comm.out_shape,
            scratch + comm.scratch, comm.arrays)


def _mm_nt(name, a_list, w_list, pairs, epilogue, out_dtypes, tt, tn, comm=None, tiles=()):
    t = a_list[0].shape[0]
    n = w_list[0].shape[0]
    na, nw, ntile = len(a_list), len(w_list), len(tiles)
    ni, nj = t // tt, n // tn

    def body(*refs):
        ins, o_refs, _, c_refs = _split_refs(refs, na + nw + ntile, len(out_dtypes), 0, comm)
        a_refs, w_refs, t_refs = ins[:na], ins[na:na + nw], ins[na + nw:]
        if comm:
            i = pl.program_id(0)
            comm.attach(c_refs, i, ni)
        a_vals = [a_ref[...] for a_ref in a_refs]
        for j in range(nj):
            cols = pl.ds(j * tn, tn)
            accs = [_dot_nt(a_vals[ai], w_refs[wi][cols, :]) for ai, wi in pairs]
            for o_ref, o in zip(o_refs, epilogue(accs, [t_ref[:, cols] for t_ref in t_refs])):
                o_ref[:, cols] = o.astype(o_ref.dtype)

    in_specs = ([pl.BlockSpec((tt, a.shape[1]), lambda i: (i, 0)) for a in a_list]
                + [pl.BlockSpec(w.shape, lambda i: (0, 0), pipeline_mode=pl.Buffered(1)) for w in w_list]
                + [pl.BlockSpec((tt, n), lambda i: (i, 0)) for _ in tiles])
    in_specs, out_specs, out_shape, scratch, extra = _with_comm(
        comm, in_specs, [pl.BlockSpec((tt, n), lambda i: (i, 0)) for _ in out_dtypes],
        [jax.ShapeDtypeStruct((t, n), dt) for dt in out_dtypes], [])
    return pl.pallas_call(
        body, name=name, grid=(ni,), in_specs=in_specs, out_specs=out_specs, out_shape=out_shape,
        scratch_shapes=scratch,
        compiler_params=_cparams("arbitrary" if comm else "parallel"),
    )(*a_list, *w_list, *tiles, *extra)


def _mm_nn(name, pairs, rows, fulls, epilogue, out_kinds, tt, tk, comm=None):
    t, k_total = pairs[0][0].shape
    n = pairs[0][1].shape[1]
    nk = k_total // tk
    nt = t // tt
    npair, nrow, nfull = len(pairs), len(rows), len(fulls)

    def body(*refs):
        ins, o_refs, scratch, c_refs = _split_refs(refs, 2 * npair + nrow + nfull, len(out_kinds), min(nk - 1, 1), comm)
        a_refs, w_refs = ins[:npair], ins[npair:2 * npair]
        r_refs, f_refs = ins[2 * npair:2 * npair + nrow], ins[2 * npair + nrow:]
        i, k = pl.program_id(0), pl.program_id(1)
        if comm:
            comm.attach(c_refs, i * nk + k, nt * nk)
        s = _dot_nn(a_refs[0][...], w_refs[0][...])
        for a_ref, w_ref in zip(a_refs[1:], w_refs[1:]):
            s = s + _dot_nn(a_ref[...], w_ref[...])

        def finish(acc):
            outs = epilogue(acc, [r[...] for r in r_refs], [f[...] for f in f_refs])
            for o_ref, o in zip(o_refs, outs):
                o_ref[...] = o.astype(o_ref.dtype)

        if nk == 1:
            finish(s)
        else:
            acc_ref = scratch[0]

            @pl.when(k == 0)
            def _():
                acc_ref[...] = s

            @pl.when(k > 0)
            def _():
                acc_ref[...] += s

            @pl.when(k == nk - 1)
            def _():
                finish(acc_ref[...])

    once = dict(pipeline_mode=pl.Buffered(1)) if nk == 1 else {}
    in_specs = ([pl.BlockSpec((tt, tk), lambda i, k: (i, k)) for _ in pairs]
                + [pl.BlockSpec((tk, n), (lambda i, k, off=off: (k + off, 0)), **once) for _, _, off in pairs]
                + [pl.BlockSpec((tt, n), lambda i, k: (i, 0)) for _ in rows]
                + [pl.BlockSpec((1, n), lambda i, k: (0, 0)) for _ in fulls])
    out_specs, out_shape = [], []
    for kind, dt in out_kinds:
        if kind == "tile":
            out_specs.append(pl.BlockSpec((tt, n), lambda i, k: (i, 0)))
            out_shape.append(jax.ShapeDtypeStruct((t, n), dt))
        else:
            out_specs.append(pl.BlockSpec((SUBLANES, n), lambda i, k: (i, 0)))
            out_shape.append(jax.ShapeDtypeStruct((nt * SUBLANES, n), dt))
    in_specs, out_specs, out_shape, scratch, extra = _with_comm(
        comm, in_specs, out_specs, out_shape, [] if nk == 1 else [pltpu.VMEM((tt, n), F32)])
    return pl.pallas_call(
        body, name=name, grid=(nt, nk), in_specs=in_specs, out_specs=out_specs, out_shape=out_shape,
        scratch_shapes=scratch,
        compiler_params=_cparams("arbitrary" if comm else "parallel", "arbitrary"),
    )(*[a for a, _, _ in pairs], *[w for _, w, _ in pairs], *rows, *fulls, *extra)


def _mm_tn(name, a, b, scale, tm, tt, comm=None):
    t, m = a.shape
    n = b.shape[1]
    nt = t // tt
    nm = m // tm

    def body(*refs):
        (a_ref, b_ref), (o_ref,), (acc_ref,), c_refs = _split_refs(refs, 2, 1, 1, comm)
        k = pl.program_id(1)
        if comm:
            i = pl.program_id(0)
            comm.attach(c_refs, i * nt + k, nm * nt)
        s = _dot_tn(a_ref[...], b_ref[...])

        @pl.when(k == 0)
        def _():
            acc_ref[...] = s

        @pl.when(k > 0)
        def _():
            acc_ref[...] += s

        @pl.when(k == nt - 1)
        def _():
            o_ref[...] = (acc_ref[...] * scale).astype(o_ref.dtype)

    in_specs, out_specs, out_shape, scratch, extra = _with_comm(
        comm, [pl.BlockSpec((tt, tm), lambda i, k: (k, i)), pl.BlockSpec((tt, n), lambda i, k: (k, 0))],
        [pl.BlockSpec((tm, n), lambda i, k: (i, 0))], [jax.ShapeDtypeStruct((m, n), BF16)],
        [pltpu.VMEM((tm, n), F32)])
    out = pl.pallas_call(
        body, name=name, grid=(nm, nt), in_specs=in_specs, out_specs=out_specs, out_shape=out_shape,
        scratch_shapes=scratch,
        compiler_params=_cparams("arbitrary" if comm else "parallel", "arbitrary"),
    )(a, b, *extra)
    return out if comm else out[0]


def _mm_tn_slabs(name, a_list, b, tt, comm=None):
    t, m = a_list[0].shape
    n = b.shape[1]
    na = len(a_list)
    nt = t // tt

    def body(*refs):
        ins, (o_ref,), (acc_ref,), c_refs = _split_refs(refs, na + 1, 1, 1, comm)
        k = pl.program_id(0)
        if comm:
            comm.attach(c_refs, k, nt)
        bv = ins[na][...]
        parts = [_dot_tn(a_ref[...], bv) for a_ref in ins[:na]]

        @pl.when(k == 0)
        def _():
            for j, part in enumerate(parts):
                acc_ref[pl.ds(j * m, m), :] = part

        @pl.when(k > 0)
        def _():
            for j, part in enumerate(parts):
                acc_ref[pl.ds(j * m, m), :] += part

        @pl.when(k == nt - 1)
        def _():
            o_ref[...] = acc_ref[...].astype(o_ref.dtype)

    in_specs, out_specs, out_shape, scratch, extra = _with_comm(
        comm, [pl.BlockSpec((tt, m), lambda k: (k, 0))] * na + [pl.BlockSpec((tt, n), lambda k: (k, 0))],
        [pl.BlockSpec((na * m, n), lambda k: (0, 0))], [jax.ShapeDtypeStruct((na * m, n), BF16)],
        [pltpu.VMEM((na * m, n), F32)])
    out = pl.pallas_call(
        body, name=name, grid=(nt,), in_specs=in_specs, out_specs=out_specs, out_shape=out_shape,
        scratch_shapes=scratch, compiler_params=_cparams("arbitrary"),
    )(*a_list, b, *extra)
    return out if comm else out[0]


def _group_sum(v, bd):
    hi, lo = _split2(v)
    return _dot_nn(hi, bd) + _dot_nn(lo, bd)


def _qknorm_fwd(proj, qg, kg, bd, tt):
    t = proj.shape[0]

    def body(q_ref, k_ref, v_ref, qg_ref, kg_ref, bd_ref, qn_ref, kn_ref, vb_ref):
        bdv = bd_ref[...]
        for x_ref, g_ref, o_ref in ((q_ref, qg_ref, qn_ref), (k_ref, kg_ref, kn_ref)):
            xv = x_ref[...]
            r = lax.rsqrt(_group_sum(xv * xv, bdv) * (1.0 / HEAD_DIM) + EPS)
            o_ref[...] = ((xv * r) * g_ref[...]).astype(BF16)
        vb_ref[...] = v_ref[...].astype(BF16)

    slab = lambda s: pl.BlockSpec((tt, SLAB), lambda i, s=s: (i, s))
    full = lambda shape: pl.BlockSpec(shape, lambda i: (0, 0))
    out = pl.BlockSpec((tt, SLAB), lambda i: (i, 0))
    return pl.pallas_call(
        body, name="qknorm_fwd", grid=(t // tt,),
        in_specs=[slab(3), slab(4), slab(5), full((1, SLAB)), full((1, SLAB)), full((SLAB, SLAB))],
        out_specs=[out, out, out],
        out_shape=[jax.ShapeDtypeStruct((t, SLAB), BF16)] * 3,
        compiler_params=_cparams("parallel"),
    )(proj, proj, proj, qg, kg, bd)


def _qknorm_bwd(proj, dqn, dkn, dv, qg, kg, bd, tt):
    t = proj.shape[0]

    def body(q_ref, k_ref, dqn_ref, dkn_ref, dv_ref, qg_ref, kg_ref, bd_ref, dq_ref, dk_ref, dvb_ref, part_ref):
        bdv = bd_ref[...]
        parts = []
        for x_ref, d_ref, g_ref, o_ref in ((q_ref, dqn_ref, qg_ref, dq_ref), (k_ref, dkn_ref, kg_ref, dk_ref)):
            xv, dn = x_ref[...], d_ref[...]
            r = lax.rsqrt(_group_sum(xv * xv, bdv) * (1.0 / HEAD_DIM) + EPS)
            u = dn * g_ref[...]
            dx = r * u - xv * (r * r * r) * (_group_sum(u * xv, bdv) * (1.0 / HEAD_DIM))
            o_ref[...] = dx.astype(BF16)
            parts.append(_fold8(dn * xv * r))
        dvb_ref[...] = dv_ref[...].astype(BF16)
        part_ref[...] = jnp.concatenate(parts, axis=1)

    slab = lambda s: pl.BlockSpec((tt, SLAB), lambda i, s=s: (i, s))
    tile = pl.BlockSpec((tt, SLAB), lambda i: (i, 0))
    full = lambda shape: pl.BlockSpec(shape, lambda i: (0, 0))
    return pl.pallas_call(
        body, name="qknorm_bwd", grid=(t // tt,),
        in_specs=[slab(3), slab(4), tile, tile, tile, full((1, SLAB)), full((1, SLAB)), full((SLAB, SLAB))],
        out_specs=[tile, tile, tile, pl.BlockSpec((SUBLANES, 2 * SLAB), lambda i: (i, 0))],
        out_shape=[jax.ShapeDtypeStruct((t, SLAB), BF16)] * 3
        + [jax.ShapeDtypeStruct((t // tt * SUBLANES, 2 * SLAB), F32)],
        compiler_params=_cparams("parallel"),
    )(proj, proj, dqn, dkn, dv, qg, kg, bd)


def _conv_taps(z, z_prev, row):
    zm1 = jnp.where(row == 0, z_prev[7:8], pltpu.roll(z, 1, 0))
    zm2 = jnp.where(row == 0, z_prev[6:7], jnp.where(row == 1, z_prev[7:8], pltpu.roll(z, 2, 0)))
    return zm1, zm2


def _conv_fwd(proj, cw, cb, tt):
    t = proj.shape[0]
    tb = tt // SUBLANES

    def body(b_ref, c_ref, u_ref, cp_ref, up_ref, cw_ref, cb_ref, o_ref):
        i = pl.program_id(0)
        z = c_ref[...] * u_ref[...]
        z_prev = jnp.where(i > 0, cp_ref[...] * up_ref[...], 0.0)
        row = lax.broadcasted_iota(jnp.int32, (tt, 1), 0)
        zm1, zm2 = _conv_taps(z, z_prev, row)
        y = cw_ref[0:1] * zm2 + cw_ref[1:2] * zm1 + cw_ref[2:3] * z + cb_ref[...]
        o_ref[...] = (b_ref[...] * y).astype(BF16)

    slab = lambda s: pl.BlockSpec((tt, SLAB), lambda i, s=s: (i, s))
    prev = lambda s: pl.BlockSpec((SUBLANES, SLAB), lambda i, s=s: (jnp.maximum(i * tb - 1, 0), s))
    return pl.pallas_call(
        body, name="conv_fwd", grid=(t // tt,),
        in_specs=[slab(0), slab(1), slab(2), prev(1), prev(2),
                  pl.BlockSpec((SUBLANES, SLAB), lambda i: (0, 0)), pl.BlockSpec((1, SLAB), lambda i: (0, 0))],
        out_specs=pl.BlockSpec((tt, SLAB), lambda i: (i, 0)),
        out_shape=jax.ShapeDtypeStruct((t, SLAB), BF16),
        compiler_params=_cparams("parallel"),
    )(proj, proj, proj, proj, proj, cw, cb)


def _conv_bwd(proj, dycat, cw, cb, tt):
    t = proj.shape[0]
    tb = tt // SUBLANES
    nblk = t // SUBLANES

    def body(b_ref, c_ref, u_ref, cp_ref, up_ref, bn_ref, dy_ref, dyn_ref, cw_ref, cb_ref,
             db_ref, dc_ref, du_ref, part_ref):
        i = pl.program_id(0)
        c, u, b, dyc = c_ref[...], u_ref[...], b_ref[...], dy_ref[...]
        z = c * u
        z_prev = jnp.where(i > 0, cp_ref[...] * up_ref[...], 0.0)
        row = lax.broadcasted_iota(jnp.int32, (tt, 1), 0)
        zm1, zm2 = _conv_taps(z, z_prev, row)
        w0, w1, w2 = cw_ref[0:1], cw_ref[1:2], cw_ref[2:3]
        y = w0 * zm2 + w1 * zm1 + w2 * z + cb_ref[...]
        db_ref[...] = (dyc * y).astype(BF16)
        g = dyc * b
        g_next = jnp.where(i < pl.num_programs(0) - 1, dyn_ref[...] * bn_ref[...], 0.0)
        gp1 = jnp.where(row == tt - 1, g_next[0:1], pltpu.roll(g, tt - 1, 0))
        gp2 = jnp.where(row == tt - 2, g_next[0:1], jnp.where(row == tt - 1, g_next[1:2], pltpu.roll(g, tt - 2, 0)))
        dz = w2 * g + w1 * gp1 + w0 * gp2
        dc_ref[...] = (dz * u).astype(BF16)
        du_ref[...] = (dz * c).astype(BF16)
        part_ref[...] = jnp.concatenate([_fold8(g * zm2), _fold8(g * zm1), _fold8(g * z), _fold8(g)], axis=1)

    slab = lambda s: pl.BlockSpec((tt, SLAB), lambda i, s=s: (i, s))
    prev = lambda s: pl.BlockSpec((SUBLANES, SLAB), lambda i, s=s: (jnp.maximum(i * tb - 1, 0), s))
    nxt = lambda s: pl.BlockSpec((SUBLANES, SLAB), lambda i, s=s: (jnp.minimum((i + 1) * tb, nblk - 1), s))
    tile = pl.BlockSpec((tt, SLAB), lambda i: (i, 0))
    return pl.pallas_call(
        body, name="conv_bwd", grid=(t // tt,),
        in_specs=[slab(0), slab(1), slab(2), prev(1), prev(2), nxt(0), slab(0), nxt(0),
                  pl.BlockSpec((SUBLANES, SLAB), lambda i: (0, 0)), pl.BlockSpec((1, SLAB), lambda i: (0, 0))],
        out_specs=[tile, tile, tile, pl.BlockSpec((SUBLANES, 4 * SLAB), lambda i: (i, 0))],
        out_shape=[jax.ShapeDtypeStruct((t, SLAB), BF16)] * 3
        + [jax.ShapeDtypeStruct((t // tt * SUBLANES, 4 * SLAB), F32)],
        compiler_params=_cparams("parallel"),
    )(proj, proj, proj, proj, proj, proj, dycat, dycat, cw, cb)


def _tri_masks(n):
    r = lax.broadcasted_iota(jnp.int32, (n, n), 0)
    c = lax.broadcasted_iota(jnp.int32, (n, n), 1)
    return (r > c).astype(BF16), (r >= c).astype(BF16)


def _sb_logits(z, causal):
    softplus = jnp.maximum(z, 0.0) + jnp.log(1.0 + jnp.exp(-jnp.abs(z)))
    lk = -softplus
    if causal is not None:
        lk = jnp.where(causal, lk, 0.0)
    return (z, lk, *_split2(lk))


def _sb_finish(z, lk, hi, lo, r_run, tri, causal, later):
    later = later + r_run
    ls = z + lk
    arg = ls + later
    if causal is not None:
        arg = jnp.where(causal, arg, -1e30)
    return lk, ls, jnp.exp(arg), later[:, 0:1] + lk[:, 0:1]


SB_DEAD_LOG = -111.0
CHAINS = ((0, 0), (0, 1), (1, 0), (1, 1))


def _all_dead(r_runs):
    m = r_runs[0]
    for r in r_runs[1:]:
        m = jnp.maximum(m, r)
    return (jnp.max(m) < SB_DEAD_LOG).astype(jnp.int32)


def _attn_fwd(qn, kn, vb, tri, sb, comm=None):
    t = qn.shape[0]
    bq = 2 * sb
    scale = HEAD_DIM ** -0.5

    def body(*refs):
        (q_ref, k_ref, v_ref, tri_ref), (o_ref, ob_ref), (acc_ref,), c_refs = _split_refs(refs, 4, 2, 1, comm)
        qi = pl.program_id(1)
        if comm:
            hp = pl.program_id(0)
            comm.attach(c_refs, hp * (t // bq) + qi, (SB_DIM // LANES) * (t // bq))
        lane = lax.broadcasted_iota(jnp.int32, (1, LANES), 1)
        hmasks = (lane < HEAD_DIM, lane >= HEAD_DIM)
        diag = lax.broadcasted_iota(jnp.int32, (sb, sb), 1) < lax.broadcasted_iota(jnp.int32, (sb, sb), 0)
        triv = tri_ref[...]
        qs = [jnp.where(hmasks[hh], q_ref[pl.ds(s * sb, sb), :], 0) * scale for s, hh in CHAINS]
        acc_ref[...] = jnp.zeros_like(acc_ref)

        def load_kv(kb):
            ks = kb * sb if isinstance(kb, int) else pl.multiple_of(kb * sb, sb)
            vraw = v_ref[pl.ds(ks, sb), :]
            return k_ref[pl.ds(ks, sb), :], [jnp.where(hm, vraw, 0) for hm in hmasks]

        def run_tiles(tiles, r_in):
            zs = [_dot_nt(qs[c], kv[0]) for c, kv, _, _ in tiles]
            mids = [_sb_logits(z, causal) for z, (_, _, causal, _) in zip(zs, tiles)]
            laters = [_dot_nn(m[2], triv) + _dot_nn(m[3], triv) for m in mids]
            outs = []
            for m, later, (c, _, causal, dep) in zip(mids, laters, tiles):
                outs.append(_sb_finish(*m, r_in[c] if dep is None else outs[dep][3], triv, causal, later))
            for o, (c, kv, _, _) in zip(outs, tiles):
                acc_ref[c] += _dot_nn(o[2].astype(BF16), kv[1][c % 2])
            return [o[3] for o in outs]

        zero = jnp.zeros((sb, 1), F32)
        kv_diag = [load_kv(2 * qi), load_kv(2 * qi + 1)]
        kv_prev = [load_kv(jnp.maximum(2 * qi - 1, 0)), kv_diag[0]]
        has_prev = lax.broadcasted_iota(jnp.int32, (sb, sb), 0) >= jnp.where(qi > 0, 0, sb)
        r_runs = run_tiles([(c, kv_diag[c // 2], diag, None) for c in range(4)]
                           + [(c, kv_prev[c // 2], has_prev if c < 2 else None, c) for c in range(4)], [zero] * 4)[4:]

        def step(carry):
            i, _, *rs = carry
            kvs = [load_kv(2 * qi - 2 - i), load_kv(2 * qi - 1 - i)]
            rs = run_tiles([(c, kvs[c // 2], None, None) for c in range(4)], rs)
            return (i + 1, _all_dead(rs), *rs)

        i_end, _, *rs = lax.while_loop(lambda c: jnp.logical_and(c[0] < 2 * qi - 1, c[1] == 0), step,
                                       (jnp.int32(0), _all_dead(r_runs), *r_runs))

        @pl.when(jnp.logical_and(i_end == 2 * qi - 1, _all_dead(rs[2:]) == 0))
        def _():
            kv_last = load_kv(0)
            run_tiles([(c, kv_last, None, None) for c in (2, 3)], rs)
        for s in range(2):
            out = acc_ref[2 * s] + acc_ref[2 * s + 1]
            o_ref[pl.ds(s * sb, sb), :] = out
            ob_ref[pl.ds(s * sb, sb), :] = out.astype(BF16)

    qspec = pl.BlockSpec((bq, LANES), lambda h, i: (i, h))
    kspec = pl.BlockSpec((t, LANES), lambda h, i: (0, h))
    in_specs, out_specs, out_shape, scratch, extra = _with_comm(
        comm, [qspec, kspec, kspec, pl.BlockSpec((sb, sb), lambda h, i: (0, 0))], [qspec, qspec],
        [jax.ShapeDtypeStruct((t, SB_DIM), F32), jax.ShapeDtypeStruct((t, SB_DIM), BF16)],
        [pltpu.VMEM((4, sb, LANES), F32)])
    return pl.pallas_call(
        body, name="attn_fwd", grid=(SB_DIM // LANES, t // bq),
        in_specs=in_specs, out_specs=out_specs, out_shape=out_shape, scratch_shapes=scratch,
        compiler_params=_cparams("arbitrary" if comm else "parallel", "arbitrary"),
    )(qn, kn, vb, tri, *extra)


def _attn_bwd(qn, kn, vb, o, dycat, tri, tri_inc, sb, comm=None):
    t = qn.shape[0]
    bq = 2 * sb
    scale = HEAD_DIM ** -0.5

    def body(*refs):
        ins, (dq_ref, dk_ref, dv_ref), (dq_acc,), c_refs = _split_refs(refs, 7, 3, 1, comm)
        q_ref, k_ref, v_ref, o_ref, do_ref, tri_ref, tinc_ref = ins
        qi = pl.program_id(1)
        if comm:
            hp = pl.program_id(0)
            comm.attach(c_refs, hp * (t // bq) + qi, (SB_DIM // LANES) * (t // bq))

        @pl.when(qi == 0)
        def _():
            dk_ref[...] = jnp.zeros_like(dk_ref)
            dv_ref[...] = jnp.zeros_like(dv_ref)

        lane = lax.broadcasted_iota(jnp.int32, (1, LANES), 1)
        hmasks = (lane < HEAD_DIM, lane >= HEAD_DIM)
        diag = lax.broadcasted_iota(jnp.int32, (sb, sb), 1) < lax.broadcasted_iota(jnp.int32, (sb, sb), 0)
        triv, tincv = tri_ref[...], tinc_ref[...]
        qs, dobs, d_rows = [], [], []
        for s, hh in CHAINS:
            rows = pl.ds(s * sb, sb)
            qs.append(jnp.where(hmasks[hh], q_ref[rows, :], 0) * scale)
            dobs.append(jnp.where(hmasks[hh], do_ref[rows, :], 0.0).astype(BF16))
            d_rows.append(jnp.sum(dobs[-1].astype(F32) * o_ref[rows, :], axis=1, keepdims=True))
        dq_acc[...] = jnp.zeros_like(dq_acc)

        def load_kv(kb):
            ks = kb * sb if isinstance(kb, int) else pl.multiple_of(kb * sb, sb)
            return k_ref[pl.ds(ks, sb), :], v_ref[pl.ds(ks, sb), :], ks

        def run_tiles(tiles, r_in, g_in):
            zs = [_dot_nt(qs[c], kv[0]) for c, kv, _, _ in tiles]
            das = [_dot_nt(dobs[c], kv[1]) for c, kv, _, _ in tiles]
            mids = [_sb_logits(z, causal) for z, (_, _, causal, _) in zip(zs, tiles)]
            laters = [_dot_nn(m[2], triv) + _dot_nn(m[3], triv) for m in mids]
            fins, abs_, es = [], [], []
            for m, later, da, (c, _, causal, dep) in zip(mids, laters, das, tiles):
                fins.append(_sb_finish(*m, r_in[c] if dep is None else fins[dep][3], triv, causal, later))
                abs_.append(fins[-1][2].astype(BF16))
                es.append(da * abs_[-1].astype(F32))
            splits = [_split2(e) for e in es]
            e_sums = [_dot_nn(hi, tincv) + _dot_nn(lo, tincv) for hi, lo in splits]
            e_froms, dzbs = [], []
            for e, e_sum, fin, (c, _, causal, dep) in zip(es, e_sums, fins, tiles):
                e_froms.append(e_sum + (g_in[c] if dep is None else e_froms[dep][:, 0:1]))
                dz = e - jnp.exp(fin[1]) * (e + (d_rows[c] - e_froms[-1]))
                if causal is not None:
                    dz = jnp.where(causal, dz, 0.0)
                dzbs.append(dz.astype(BF16))
            for dzb, (c, kv, _, _) in zip(dzbs, tiles):
                dq_acc[c] += _dot_nn(dzb, kv[0])
            by_rows = {}
            for dzb, ab, (c, kv, _, _) in zip(dzbs, abs_, tiles):
                by_rows.setdefault(id(kv), (kv[2], []))[1].append((_dot_tn(dzb, qs[c]), _dot_tn(ab, dobs[c])))
            for first_row, parts in by_rows.values():
                rows = pl.ds(first_row, sb)
                dk_ref[rows, :] += sum(p[0] for p in parts[1:]) + parts[0][0]
                dv_ref[rows, :] += sum(p[1] for p in parts[1:]) + parts[0][1]
            return [f[3] for f in fins], [ef[:, 0:1] for ef in e_froms]

        zero = jnp.zeros((sb, 1), F32)
        kv_diag = [load_kv(2 * qi), load_kv(2 * qi + 1)]
        kv_prev = [load_kv(jnp.maximum(2 * qi - 1, 0)), kv_diag[0]]
        has_prev = lax.broadcasted_iota(jnp.int32, (sb, sb), 0) >= jnp.where(qi > 0, 0, sb)
        r_first, g_first = run_tiles(
            [(c, kv_diag[c // 2], diag, None) for c in range(4)]
            + [(c, kv_prev[c // 2], has_prev if c < 2 else None, c) for c in range(4)], [zero] * 4, [zero] * 4)
        r_runs, g_runs = r_first[4:], g_first[4:]

        def step(carry):
            i, _, *rg = carry
            kvs = [load_kv(2 * qi - 2 - i), load_kv(2 * qi - 1 - i)]
            rs, gs = run_tiles([(c, kvs[c // 2], None, None) for c in range(4)], rg[:4], rg[4:])
            return (i + 1, _all_dead(rs), *rs, *gs)

        i_end, _, *rg = lax.while_loop(lambda c: jnp.logical_and(c[0] < 2 * qi - 1, c[1] == 0), step,
                                       (jnp.int32(0), _all_dead(r_runs), *r_runs, *g_runs))

        @pl.when(jnp.logical_and(i_end == 2 * qi - 1, _all_dead(rg[2:4]) == 0))
        def _():
            kv_last = load_kv(0)
            run_tiles([(c, kv_last, None, None) for c in (2, 3)], rg[:4], rg[4:])
        for s in range(2):
            dq_ref[pl.ds(s * sb, sb), :] = jnp.where(hmasks[0], dq_acc[2 * s], dq_acc[2 * s + 1]) * scale

    qspec = pl.BlockSpec((bq, LANES), lambda h, i: (i, h))
    dospec = pl.BlockSpec((bq, LANES), lambda h, i: (i, h + CONV_DIM // LANES))
    kspec = pl.BlockSpec((t, LANES), lambda h, i: (0, h))
    full = pl.BlockSpec((sb, sb), lambda h, i: (0, 0))
    in_specs, out_specs, out_shape, scratch, extra = _with_comm(
        comm, [qspec, kspec, kspec, qspec, dospec, full, full], [qspec, kspec, kspec],
        [jax.ShapeDtypeStruct((t, SB_DIM), F32)] * 3, [pltpu.VMEM((4, sb, LANES), F32)])
    return pl.pallas_call(
        body, name="attn_bwd", grid=(SB_DIM // LANES, t // bq),
        in_specs=in_specs, out_specs=out_specs, out_shape=out_shape, scratch_shapes=scratch,
        compiler_params=_cparams("arbitrary" if comm else "parallel", "arbitrary"),
    )(qn, kn, vb, o, dycat, tri, tri_inc, *extra)


def _ple_loss(x3, p2, tgt, gain, wpg, wppt, tt):
    t, d = x3.shape
    pdim = p2.shape[1]
    nt = t // tt

    def body(x_ref, p_ref, t_ref, g_ref, wg_ref, wp_ref,
             dx_ref, dxb_ref, dwg_ref, dwp_ref, gpart_ref, lpart_ref, accg_ref, accp_ref):
        i = pl.program_id(0)
        xv, gain_v = x_ref[...], g_ref[...]
        hb = ((xv * _rms_stats(xv)) * gain_v).astype(BF16)
        gate = jax.nn.sigmoid(_dot_nn(hb, wg_ref[...]))
        pb = p_ref[...].astype(BF16)
        pe = _dot_nt(pb, wp_ref[...])
        diff = xv + gate * pe - t_ref[...]
        lsum = jnp.sum(_fold8(diff * diff), axis=1, keepdims=True) * (0.5 / d)
        lpart_ref[...] = jnp.broadcast_to(lsum, (SUBLANES, LANES))
        dy = diff * (1.0 / d)
        dgz = ((dy * pe) * gate * (1.0 - gate)).astype(BF16)
        dpe = (dy * gate).astype(BF16)
        dx_n, grow = _rms_bwd(_dot_nt(dgz, wg_ref[...]), xv, gain_v)
        dx = dy + dx_n
        dx_ref[...] = dx
        dxb_ref[...] = dx.astype(BF16)
        gpart_ref[...] = _fold8(grow)
        sg = _dot_tn(hb, dgz)
        sp = _dot_tn(dpe, pb)

        @pl.when(i == 0)
        def _():
            accg_ref[...] = sg
            accp_ref[...] = sp

        @pl.when(i > 0)
        def _():
            accg_ref[...] += sg
            accp_ref[...] += sp

        @pl.when(i == nt - 1)
        def _():
            dwg_ref[...] = accg_ref[...].astype(BF16)
            dwp_ref[...] = accp_ref[...].astype(BF16)

    tile = lambda w: pl.BlockSpec((tt, w), lambda i: (i, 0))
    full = lambda shape: pl.BlockSpec(shape, lambda i: (0, 0))
    return pl.pallas_call(
        body, name="ple_loss", grid=(nt,),
        in_specs=[tile(d), tile(pdim), tile(d), full((1, d)),
                  pl.BlockSpec((d, d), lambda i: (0, 0), pipeline_mode=pl.Buffered(1)),
                  pl.BlockSpec((d, pdim), lambda i: (0, 0), pipeline_mode=pl.Buffered(1))],
        out_specs=[tile(d), tile(d), full((d, d)), full((d, pdim)),
                   pl.BlockSpec((SUBLANES, d), lambda i: (i, 0)), pl.BlockSpec((SUBLANES, LANES), lambda i: (i, 0))],
        out_shape=[jax.ShapeDtypeStruct((t, d), F32), jax.ShapeDtypeStruct((t, d), BF16),
                   jax.ShapeDtypeStruct((d, d), BF16), jax.ShapeDtypeStruct((d, pdim), BF16),
                   jax.ShapeDtypeStruct((nt * SUBLANES, d), F32), jax.ShapeDtypeStruct((nt * SUBLANES, LANES), F32)],
        scratch_shapes=[pltpu.VMEM((d, d), F32), pltpu.VMEM((d, pdim), F32)],
        compiler_params=_cparams("arbitrary"),
    )(x3, p2, tgt, gain, wpg, wppt)


def _pack_small(parts_gain, conv_part, qk_part):
    d = parts_gain[0].shape[1]
    ng = len(parts_gain)

    def body(*refs):
        g_refs, conv_ref, qk_ref, o_ref = refs[:ng], refs[ng], refs[ng + 1], refs[ng + 2]
        rows = [jnp.sum(r[...], axis=0, keepdims=True) for r in g_refs]
        cs = jnp.sum(conv_ref[...], axis=0, keepdims=True)
        qs = jnp.sum(qk_ref[...], axis=0, keepdims=True)
        rows.append(jnp.concatenate([cs[:, 3 * SLAB:], cs[:, :SLAB]], axis=1))
        rows.append(cs[:, SLAB:3 * SLAB])
        rows.append(qs)
        rid = lax.broadcasted_iota(jnp.int32, (2 * SUBLANES, 1), 0)
        out = jnp.zeros((2 * SUBLANES, d), F32)
        for idx, r in enumerate(rows):
            out = jnp.where(rid == idx, r, out)
        o_ref[...] = out

    return pl.pallas_call(
        body, name="pack_small", out_shape=jax.ShapeDtypeStruct((2 * SUBLANES, d), F32),
    )(*parts_gain, conv_part, qk_part)


def _sum_slots(name, slots, out_dtype=F32):
    _, r, c = slots.shape

    def body(s_ref, o_ref):
        acc = s_ref[0].astype(F32)
        for d in range(1, N_DEV):
            acc = acc + s_ref[d].astype(F32)
        o_ref[...] = acc.astype(o_ref.dtype)

    return pl.pallas_call(body, name=name, out_shape=jax.ShapeDtypeStruct((r, c), out_dtype),
                          compiler_params=pltpu.CompilerParams(vmem_limit_bytes=VMEM_LIMIT_BYTES))(slots)


def _adamw(name, w, g, m, v):
    c1 = 1.0 - ADAM_B1 ** ADAM_STEP
    c2 = 1.0 - ADAM_B2 ** ADAM_STEP

    def body(w_ref, g_ref, m_ref, v_ref, go_ref, d_ref, nm_ref, nv_ref):
        if g.ndim == 3:
            gv = g_ref[0].astype(F32)
            for dev in range(1, N_DEV):
                gv = gv + g_ref[dev].astype(F32)
        else:
            gv = g_ref[...]
        go_ref[...] = gv
        nm = ADAM_B1 * m_ref[...] + (1.0 - ADAM_B1) * gv
        nv = ADAM_B2 * v_ref[...] + (1.0 - ADAM_B2) * (gv * gv)
        d_ref[...] = -ADAM_LR * ((nm / c1) / (jnp.sqrt(nv / c2) + ADAM_EPS) + ADAM_WD * w_ref[...])
        nm_ref[...] = nm
        nv_ref[...] = nv

    return pl.pallas_call(body, name=name, out_shape=[jax.ShapeDtypeStruct(w.shape, F32)] * 4,
                          compiler_params=pltpu.CompilerParams(vmem_limit_bytes=VMEM_LIMIT_BYTES))(w, g, m, v)


def _any_specs(n):
    return [pl.BlockSpec(memory_space=pl.ANY)] * n


def _all_gather(name, shards):
    n = len(shards)

    def body(*refs):
        ins, outs = refs[:n], refs[n:2 * n]
        send_sems, recv_sems, local_sems = refs[2 * n:]
        x, y, c = (lax.axis_index(a) for a in MESH_AXES)
        me, sibling = (x, y, c), (x, y, 1 - c)
        chips = [(1 - x, y), (x, 1 - y), (1 - x, 1 - y)]

        def rows(a, px, py, pc):
            r = ins[a].shape[0]
            return outs[a].at[pl.ds((4 * px + 2 * py + pc) * r, r), :]

        def copy(a, k, block, to, src=None):
            return pltpu.make_async_remote_copy(
                src_ref=rows(a, *block) if src is None else src, dst_ref=rows(a, *block),
                send_sem=send_sems.at[7 * a + k], recv_sem=recv_sems.at[7 * a + k],
                device_id=to, device_id_type=MESH)

        mine = [pltpu.make_async_copy(ins[a], rows(a, *me), local_sems.at[a]) for a in range(n)]
        for cp in mine:
            cp.start()
        first = []
        for a in range(n):
            first.append(copy(a, 0, me, sibling, src=ins[a]))
            first += [copy(a, 1 + j, me, (*chip, c), src=ins[a]) for j, chip in enumerate(chips)]
        for cp in first:
            cp.start()
        passed = []
        for j, chip in enumerate(chips):
            for a in range(n):
                copy(a, 1 + j, (*chip, c), me).wait_recv()
                fwd = copy(a, 4 + j, (*chip, c), sibling)
                fwd.start()
                passed.append(fwd)
        for a in range(n):
            copy(a, 0, sibling, me).wait_recv()
            for j, chip in enumerate(chips):
                copy(a, 4 + j, (*chip, 1 - c), me).wait_recv()
        for cp in first + passed:
            cp.wait_send()
        for cp in mine:
            cp.wait()

    return pl.pallas_call(
        body, name=name, in_specs=_any_specs(n), out_specs=_any_specs(n),
        out_shape=[jax.ShapeDtypeStruct((N_DEV * s.shape[0], s.shape[1]), s.dtype) for s in shards],
        scratch_shapes=[pltpu.SemaphoreType.DMA((7 * n,)), pltpu.SemaphoreType.DMA((7 * n,)),
                        pltpu.SemaphoreType.DMA((n,))],
    )(*shards)


def _residual_and_norm(res_scale):
    def epilogue(acc, rows, fulls):
        out = rows[0] + res_scale * acc
        return [out] + [(out * _rms_stats(out)) * gain for gain in fulls]
    return epilogue


def _ffn_fwd(tag, x, h, wgt, wut, wd, next_gain, tt_nt, tt_nn, comm_gate=None, comm_down=None):
    f = wgt.shape[0]

    def gate_up(accs, _):
        return [accs[0], accs[1], jax.nn.silu(accs[0]) * accs[1]]

    g, u, a, *got_gate = _mm_nt(f"{tag}_gate_up", [h], [wgt, wut], [(0, 0), (0, 1)], gate_up, [BF16] * 3,
                                tt_nn, _pick(f, 256), comm_gate)
    if wd is None:
        wd = got_gate[0]
    gains = [] if next_gain is None else [next_gain]
    out, *rest = _mm_nn(f"{tag}_down", [(a, wd, 0)], [x], gains, _residual_and_norm(FFN_RES),
                        [("tile", F32)] + [("tile", BF16)] * len(gains), tt_nt, f, comm_down)
    h_next = rest.pop(0) if gains else None
    return out, h_next, (g, u, a), got_gate, rest


def _norm_bwd_epilogue(acc, rows, fulls):
    x_in, dy = rows
    dx_n, grow = _rms_bwd(acc, x_in, fulls[0])
    dx = dy + dx_n
    return [dx, dx, _fold8(grow)]


_NORM_BWD_OUTS = [("tile", F32), ("tile", BF16), ("part", F32)]


def _ffn_bwd(tag, x_in, h, hidden, dy, dyb, gain, wgt, wut, wd, tt_nt, tt_nn, riders=(None, None),
             exchange_own=False):
    g, u, a = hidden
    f = wgt.shape[0]
    tt_tn = _pick(h.shape[0], 2 * tt_nt)
    own = (lambda arr: _Exchange([arr], gather=False)) if exchange_own else (lambda arr: None)

    def carried(result, rider):
        return (result[0], result[1:]) if rider else (result, [])

    def hidden_grads(accs, tiles):
        da = FFN_RES * accs[0]
        gv, uv = tiles[0].astype(F32), tiles[1].astype(F32)
        sg = jax.nn.sigmoid(gv)
        s = gv * sg
        return [da * uv * (sg * (1.0 + gv * (1.0 - sg))), da * s]

    dwd, got_dwd = carried(_mm_tn(f"{tag}_dwd", a, dyb, FFN_RES, f // 2, tt_tn, riders[0]), riders[0])
    dg, du, *x_dwd = _mm_nt(f"{tag}_bwd_hidden", [dyb], [wd], [(0, 0)], hidden_grads, [BF16, BF16],
                            tt_nn, _pick(f, 256), own(dwd), tiles=[g, u])
    dwg, got_dwg = carried(_mm_tn(f"{tag}_dwg", dg, h, 1.0, f // 2, tt_tn, riders[1]), riders[1])
    dwu, x_dwg = carried(_mm_tn(f"{tag}_dwu", du, h, 1.0, f // 2, tt_tn, own(dwg)), exchange_own)
    dx, dxb, gpart, *x_dwu = _mm_nn(f"{tag}_bwd_dx", [(dg, wgt, 0), (du, wut, 0)], [x_in, dy], [gain],
                                    _norm_bwd_epilogue, _NORM_BWD_OUTS, tt_nn, f, own(dwu))
    grads = [*x_dwg, *x_dwu, *x_dwd] if exchange_own else [dwg, dwu, dwd]
    return dx, dxb, gpart, grads, [*got_dwd, *got_dwg]


def kernel(x, p, ffn1_norm, ffn1_w_gate, ffn1_w_up, ffn1_w_down, mix_norm, w_in, conv_w, conv_b, q_norm, k_norm, w_out, ffn2_norm, ffn2_w_gate, ffn2_w_up, ffn2_w_down, ple_norm, ple_w_gate, ple_w_proj, loss_target, m_ffn1_norm, m_ffn1_w_gate, m_ffn1_w_up, m_ffn1_w_down, m_mix_norm, m_w_in, m_conv_w, m_conv_b, m_q_norm, m_k_norm, m_w_out, m_ffn2_norm, m_ffn2_w_gate, m_ffn2_w_up, m_ffn2_w_down, m_ple_norm, m_ple_w_gate, m_ple_w_proj, v_ffn1_norm, v_ffn1_w_gate, v_ffn1_w_up, v_ffn1_w_down, v_mix_norm, v_w_in, v_conv_w, v_conv_b, v_q_norm, v_k_norm, v_w_out, v_ffn2_norm, v_ffn2_w_gate, v_ffn2_w_up, v_ffn2_w_down, v_ple_norm, v_ple_w_gate, v_ple_w_proj):
    x0, p2, tgt = x[0], p[0, 0], loss_target[0]
    t, d = x0.shape
    tt_nt = _pick(t, 1024)
    tt_nn = _pick(t, 512)
    tt_ew = _pick(t, 512)
    tt_ple = _pick(t, 512)
    sb = _pick(t // 2, 256)

    t_bf = lambda w: w[0].T.astype(BF16)
    n_bf = lambda w: w[0].astype(BF16)
    cw_tile = jnp.zeros((SUBLANES, LANES), F32).at[:conv_w.shape[1], :conv_w.shape[2]].set(conv_w[0])
    gather_first = _GatherTwoLevel([t_bf(ffn1_w_gate), t_bf(ffn1_w_up)])
    gather_down = _GatherTwoLevel([n_bf(ffn1_w_down), n_bf(w_out), cw_tile])
    gather_in = _GatherTwoLevel([t_bf(w_in)])
    gather_late = _GatherTwoLevel([t_bf(ffn2_w_gate), t_bf(ffn2_w_up), n_bf(ffn2_w_down), n_bf(ple_w_gate),
                                   t_bf(ple_w_proj)])
    ncs = conv_w.shape[2]

    qg = jnp.tile(q_norm, (1, SB_DIM // HEAD_DIM))
    kg = jnp.tile(k_norm, (1, SB_DIM // HEAD_DIM))
    gi = lax.broadcasted_iota(jnp.int32, (SLAB, SLAB), 0) // HEAD_DIM
    gj = lax.broadcasted_iota(jnp.int32, (SLAB, SLAB), 1) // HEAD_DIM
    bd = (gi == gj).astype(BF16)
    tri, tri_inc = _tri_masks(sb)

    h1, wg1t, wu1t = _rmsnorm("ffn1_norm", x0, ffn1_norm, tt_ew, gather_first)
    x1, h2, hidden1, (wd1, wout, cw_all), (wint,) = _ffn_fwd(
        "ffn1", x0, h1, wg1t, wu1t, None, mix_norm, tt_nt, tt_nn, gather_down, gather_in)
    cw_full = cw_all.reshape(N_DEV, SUBLANES, LANES)[:, :, :ncs].transpose(1, 0, 2).reshape(SUBLANES, N_DEV * ncs)
    (proj,) = _mm_nt("in_proj", [h2], [wint], [(0, 0)], lambda accs, _: accs, [F32], tt_nn, SLAB)
    y_conv = _conv_fwd(proj, cw_full, conv_b, tt_ew)
    qn, kn, vb = _qknorm_fwd(proj, qg, kg, bd, tt_ew)
    o, ob, wg2t, wu2t, wd2, wpg, wppt = _attn_fwd(qn, kn, vb, tri, sb, gather_late)
    x2, h3 = _mm_nn("out_proj", [(y_conv, wout, 0), (ob, wout, 1)], [x1], [ffn2_norm], _residual_and_norm(1.0),
                    [("tile", F32), ("tile", BF16)], tt_nn, SLAB)
    x3, _, hidden2, _, _ = _ffn_fwd("ffn2", x2, h3, wg2t, wu2t, wd2, None, tt_nt, tt_nn)

    dx3, dx3b, dwpg, dwppt, gp_ple, lpart = _ple_loss(x3, p2, tgt, ple_norm, wpg, wppt, tt_ple)
    loss = lax.psum(jnp.sum(lpart[:, 0]), MESH_AXES)
    dx2, dx2b, gp_ffn2, (dwg2, dwu2, dwd2), _ = _ffn_bwd("ffn2", x2, h3, hidden2, dx3, dx3b, ffn2_norm,
                                                         wg2t, wu2t, wd2, tt_nt, tt_nn)
    (dycat,) = _mm_nt("out_proj_bwd", [dx2b], [wout], [(0, 0)], lambda accs, _: accs, [F32], tt_nn, SLAB)
    dwout = _mm_tn_slabs("dwout", [y_conv, ob], dx2b, tt_nt)
    dqn, dkn, dv, *slots_late = _attn_bwd(qn, kn, vb, o, dycat, tri, tri_inc, sb,
                                          _Exchange([dwg2, dwu2, dwd2, dwpg, dwppt], gather=False))
    dq, dk, dvb, qk_part = _qknorm_bwd(proj, dqn, dkn, dv, qg, kg, bd, tt_ew)
    db, dc, du, conv_part = _conv_bwd(proj, dycat, cw_full, conv_b, tt_ew)
    dproj = [db, dc, du, dq, dk, dvb]
    dwin, slot_wout = _mm_tn_slabs("dwin", dproj, h2, tt_nt, _Exchange([dwout], gather=False))
    dx1, dx1b, gp_mix, slot_win = _mm_nn(
        "in_proj_bwd", [(dp, wint, s) for s, dp in enumerate(dproj)], [x1, dx2], [mix_norm],
        _norm_bwd_epilogue, _NORM_BWD_OUTS, tt_nn, SLAB, _Exchange([dwin], gather=False))
    dx0, _, gp_ffn1, slots_ffn1, _ = _ffn_bwd(
        "ffn1", x0, h1, hidden1, dx1, dx1b, ffn1_norm, wg1t, wu1t, wd1, tt_nt, tt_nn, exchange_own=True)

    slots = [*slots_ffn1, slot_win, slot_wout, *slots_late]
    per_dev = [s.reshape(N_DEV, s.shape[0] // N_DEV, s.shape[1]) for s in slots]
    c_wg1, c_wu1, c_wd1, c_win, c_wout, c_wg2, c_wu2, c_wd2, c_wpg, c_wpp = per_dev
    g_win, g_wpp = _sum_slots("sum_grads_w_in", c_win), _sum_slots("sum_grads_ple_w_proj", c_wpp)
    small = _pack_small([gp_ffn1, gp_mix, gp_ffn2, gp_ple], conv_part, qk_part)
    (small_all,) = _all_gather("gather_small_grads", [small])
    sm = _sum_slots("sum_small_grads", small_all.reshape(N_DEV, 2 * SUBLANES, d))
    fold = lambda r: r.reshape(SB_DIM // HEAD_DIM, HEAD_DIM).sum(axis=0)[None]
    me_idx = 4 * lax.axis_index("x") + 2 * lax.axis_index("y") + lax.axis_index("c")
    cw_grad = jnp.stack([sm[4, SLAB:], sm[5, :SLAB], sm[5, SLAB:]])
    grads = {
        "ffn1_norm": sm[0:1], "ffn1_w_down": c_wd1,
        "mix_norm": sm[1:2], "w_in": g_win.T, "conv_w": lax.dynamic_slice_in_dim(cw_grad, me_idx * ncs, ncs, axis=1),
        "conv_b": sm[4:5, :SLAB], "q_norm": fold(sm[6, :SLAB]), "k_norm": fold(sm[6, SLAB:]),
        "w_out": c_wout, "ffn2_norm": sm[2:3], "ffn2_w_down": c_wd2,
        "ple_norm": sm[3:4], "ple_w_gate": c_wpg, "ple_w_proj": g_wpp.T,
    }
    grads_t = {"ffn1_w_gate": c_wg1, "ffn1_w_up": c_wu1, "ffn2_w_gate": c_wg2, "ffn2_w_up": c_wu2}

    weights = dict(ffn1_norm=ffn1_norm, ffn1_w_gate=ffn1_w_gate, ffn1_w_up=ffn1_w_up, ffn1_w_down=ffn1_w_down,
                   mix_norm=mix_norm, w_in=w_in, conv_w=conv_w, conv_b=conv_b, q_norm=q_norm, k_norm=k_norm,
                   w_out=w_out, ffn2_norm=ffn2_norm, ffn2_w_gate=ffn2_w_gate, ffn2_w_up=ffn2_w_up,
                   ffn2_w_down=ffn2_w_down, ple_norm=ple_norm, ple_w_gate=ple_w_gate, ple_w_proj=ple_w_proj)
    m_in = dict(ffn1_norm=m_ffn1_norm, ffn1_w_gate=m_ffn1_w_gate, ffn1_w_up=m_ffn1_w_up, ffn1_w_down=m_ffn1_w_down,
                mix_norm=m_mix_norm, w_in=m_w_in, conv_w=m_conv_w, conv_b=m_conv_b, q_norm=m_q_norm,
                k_norm=m_k_norm, w_out=m_w_out, ffn2_norm=m_ffn2_norm, ffn2_w_gate=m_ffn2_w_gate,
                ffn2_w_up=m_ffn2_w_up, ffn2_w_down=m_ffn2_w_down, ple_norm=m_ple_norm, ple_w_gate=m_ple_w_gate,
                ple_w_proj=m_ple_w_proj)
    v_in = dict(ffn1_norm=v_ffn1_norm, ffn1_w_gate=v_ffn1_w_gate, ffn1_w_up=v_ffn1_w_up, ffn1_w_down=v_ffn1_w_down,
                mix_norm=v_mix_norm, w_in=v_w_in, conv_w=v_conv_w, conv_b=v_conv_b, q_norm=v_q_norm,
                k_norm=v_k_norm, w_out=v_w_out, ffn2_norm=v_ffn2_norm, ffn2_w_gate=v_ffn2_w_gate,
                ffn2_w_up=v_ffn2_w_up, ffn2_w_down=v_ffn2_w_down, ple_norm=v_ple_norm, ple_w_gate=v_ple_w_gate,
                ple_w_proj=v_ple_w_proj)
    g_out, d_out, m_out, v_out = [], [], [], []
    for name, w in weights.items():
        if name in grads_t:
            view, back = (lambda a: a[0].T), (lambda a: a.T[None])
            g_in = grads_t[name]
        else:
            view, back = (lambda a, w=w: a.reshape(w.shape[-2:])), (lambda a, w=w: a.reshape(w.shape))
            g_in = grads[name] if grads[name].ndim == 3 else view(grads[name])
        g2, dlt, nm, nv = _adamw(f"adamw_{name}", view(w), g_in, view(m_in[name]), view(v_in[name]))
        g_out.append(back(g2))
        d_out.append(back(dlt))
        m_out.append(back(nm))
        v_out.append(back(nv))
    return (loss, dx0[None], *g_out, *d_out, *m_out, *v_out)
```

```python
import jax
import jax.numpy as jnp
from jax import lax
from jax.experimental import pallas as pl
from jax.experimental.pallas import tpu as pltpu

F32 = jnp.float32
BF16 = jnp.bfloat16

EPS = 1e-6
FFN_RES = 0.5
HEAD_DIM = 64
CONV_DIM = 512
SB_DIM = 512
SLAB = 512
N_DEV = 8
MESH_AXES = ("x", "y", "c")
MESH = pl.DeviceIdType.MESH

ADAM_LR = 0.001
ADAM_B1 = 0.9
ADAM_B2 = 0.999
ADAM_EPS = 1e-08
ADAM_WD = 0.01
ADAM_STEP = 10

VMEM_LIMIT_BYTES = 56 * 1024 * 1024
SUBLANES = 8
LANES = 128


def _cparams(*semantics):
    return pltpu.CompilerParams(dimension_semantics=semantics, vmem_limit_bytes=VMEM_LIMIT_BYTES)


def _dot_nn(a, b):
    return jnp.dot(a, b, preferred_element_type=F32)


def _dot_nt(a, b):
    return lax.dot_general(a, b, (((1,), (1,)), ((), ())), preferred_element_type=F32)


def _dot_tn(a, b):
    return lax.dot_general(a, b, (((0,), (0,)), ((), ())), preferred_element_type=F32)


def _fold8(v):
    rows, cols = v.shape
    return jnp.sum(v.reshape(rows // SUBLANES, SUBLANES, cols), axis=0)


def _split2(v):
    hi = v.astype(BF16)
    lo = (v - hi.astype(F32)).astype(BF16)
    return hi, lo


def _rms_stats(x):
    return lax.rsqrt(jnp.mean(x * x, axis=-1, keepdims=True) + EPS)


def _rms_bwd(dh, x, gain):
    r = _rms_stats(x)
    u = dh * gain
    dx = r * u - x * (r * r * r) * jnp.mean(u * x, axis=-1, keepdims=True)
    return dx, dh * x * r


def _pick(n, pref):
    return pref if n % pref == 0 else n


def _rmsnorm(name, x, gain, tt, comm=None):
    t, d = x.shape
    nt = t // tt

    def body(*refs):
        (x_ref, g_ref), (o_ref,), _, c_refs = _split_refs(refs, 2, 1, 0, comm)
        if comm:
            comm.attach(c_refs, pl.program_id(0), nt)
        xv = x_ref[...]
        o_ref[...] = ((xv * _rms_stats(xv)) * g_ref[...]).astype(BF16)

    in_specs, out_specs, out_shape, scratch, extra = _with_comm(
        comm, [pl.BlockSpec((tt, d), lambda i: (i, 0)), pl.BlockSpec((1, d), lambda i: (0, 0))],
        [pl.BlockSpec((tt, d), lambda i: (i, 0))], [jax.ShapeDtypeStruct((t, d), BF16)], [])
    return pl.pallas_call(
        body, name=name, grid=(nt,), in_specs=in_specs, out_specs=out_specs, out_shape=out_shape,
        scratch_shapes=scratch, compiler_params=_cparams("arbitrary" if comm else "parallel"),
    )(x, gain, *extra)


class _Exchange:
    FLIPS = [(fx, fy, fc) for fx in (0, 1) for fy in (0, 1) for fc in (0, 1)][1:]

    def __init__(self, arrays):
        self.arrays = list(arrays)
        self.n = len(self.arrays)
        self.rows = [a.shape[0] // N_DEV for a in self.arrays]
        self.out_shape = [jax.ShapeDtypeStruct(a.shape, a.dtype) for a in self.arrays]
        self.scratch = [pltpu.SemaphoreType.DMA((7 * self.n,)), pltpu.SemaphoreType.DMA((7 * self.n,)),
                        pltpu.SemaphoreType.DMA((self.n,))]

    def _copies(self, ins, outs, sems, arrivals):
        send_sems, recv_sems, local_sems = sems
        x, y, c = (lax.axis_index(a) for a in MESH_AXES)
        me_idx = 4 * x + 2 * y + c
        local, send, recv = [], [], []
        for a in range(self.n):
            r = self.rows[a]

            def blk(ref, idx, r=r):
                return ref.at[pl.ds(idx * r, r), :]

            def src(idx, a=a, blk=blk):
                return blk(ins[a], idx)

            local.append(pltpu.make_async_copy(src(me_idx), blk(outs[a], me_idx), local_sems.at[a]))
            for k, flip in enumerate(self.FLIPS):
                px, py, pc = (1 - v if f else v for v, f in zip((x, y, c), flip))
                p_idx = 4 * px + 2 * py + pc
                for dst_idx, group in ((me_idx, send), (p_idx, recv))[:2 if arrivals else 1]:
                    group.append(pltpu.make_async_remote_copy(
                        src_ref=src(p_idx), dst_ref=blk(outs[a], dst_idx),
                        send_sem=send_sems.at[7 * a + k], recv_sem=recv_sems.at[7 * a + k],
                        device_id=(px, py, pc), device_id_type=MESH))
        return local, send, recv

    def start(self, ins, outs, sems):
        local, send, _ = self._copies(ins, outs, sems, arrivals=False)
        for cp in local + send:
            cp.start()

    def wait(self, ins, outs, sems):
        local, send, recv = self._copies(ins, outs, sems, arrivals=True)
        for s, r in zip(send, recv):
            r.wait_recv()
            s.wait_send()
        for cp in local:
            cp.wait()

    def attach(self, refs, step, n_steps):
        pl.when(step == 0)(lambda: self.start(*refs))
        pl.when(step == n_steps - 1)(lambda: self.wait(*refs))


class _GatherTwoLevel:
    def __init__(self, arrays):
        self.arrays = list(arrays)
        self.n = len(self.arrays)
        self.out_shape = [jax.ShapeDtypeStruct((N_DEV * a.shape[0], a.shape[1]), a.dtype) for a in self.arrays]
        self.scratch = [pltpu.SemaphoreType.DMA((7 * self.n,)), pltpu.SemaphoreType.DMA((7 * self.n,)),
                        pltpu.SemaphoreType.DMA((self.n,))]

    def _phase(self, refs, phase):
        ins, outs, (send_sems, recv_sems, local_sems) = refs
        x, y, c = (lax.axis_index(a) for a in MESH_AXES)
        me, sibling = (x, y, c), (x, y, 1 - c)
        chips = [(1 - x, y), (x, 1 - y), (1 - x, 1 - y)]

        def rows(a, px, py, pc):
            r = ins[a].shape[0]
            return outs[a].at[pl.ds((4 * px + 2 * py + pc) * r, r), :]

        def copy(a, k, block, to, src=None):
            return pltpu.make_async_remote_copy(
                src_ref=rows(a, *block) if src is None else src, dst_ref=rows(a, *block),
                send_sem=send_sems.at[7 * a + k], recv_sem=recv_sems.at[7 * a + k],
                device_id=to, device_id_type=MESH)

        for a in range(self.n):
            if phase == "start":
                pltpu.make_async_copy(ins[a], rows(a, *me), local_sems.at[a]).start()
                copy(a, 0, me, sibling, src=ins[a]).start()
                for j, chip in enumerate(chips):
                    copy(a, 1 + j, me, (*chip, c), src=ins[a]).start()
            elif phase == "forward":
                for j, chip in enumerate(chips):
                    copy(a, 1 + j, (*chip, c), me).wait_recv()
                    copy(a, 4 + j, (*chip, c), sibling).start()
            else:
                copy(a, 0, sibling, me).wait_recv()
                copy(a, 0, me, sibling, src=ins[a]).wait_send()
                for j, chip in enumerate(chips):
                    copy(a, 4 + j, (*chip, 1 - c), me).wait_recv()
                    copy(a, 1 + j, me, (*chip, c), src=ins[a]).wait_send()
                    copy(a, 4 + j, (*chip, c), sibling).wait_send()
                pltpu.make_async_copy(ins[a], rows(a, *me), local_sems.at[a]).wait()

    def attach(self, refs, step, n_steps):
        for when, phase in ((0, "start"), (3 * n_steps // 4, "forward"), (n_steps - 1, "wait")):
            pl.when(step == when)(lambda phase=phase: self._phase(refs, phase))


def _split_refs(refs, n_in, n_out, n_scratch, comm):
    nc = comm.n if comm else 0
    ins, rest = refs[:n_in], refs[n_in:]
    c_in, rest = rest[:nc], rest[nc:]
    outs, rest = rest[:n_out], rest[n_out:]
    c_out, rest = rest[:nc], rest[nc:]
    scratch, c_sems = rest[:n_scratch], rest[n_scratch:]
    return ins, outs, scratch, ((c_in, c_out, c_sems) if comm else None)


def _with_comm(comm, in_specs, out_specs, out_shape, scratch):
    if comm is None:
        return in_specs, out_specs, out_shape, scratch, []
    return (in_specs + _any_specs(comm.n), out_specs + _any_specs(comm.n), out_shape + comm.out_shape,
            scratch + comm.scratch, comm.arrays)


def _mm_nt(name, a_list, w_list, pairs, epilogue, out_dtypes, tt, tn, comm=None, tiles=()):
    t = a_list[0].shape[0]
    n = w_list[0].shape[0]
    na, nw, ntile = len(a_list), len(w_list), len(tiles)
    ni, nj = t // tt, n // tn

    def body(*refs):
        ins, o_refs, _, c_refs = _split_refs(refs, na + nw + ntile, len(out_dtypes), 0, comm)
        a_refs, w_refs, t_refs = ins[:na], ins[na:na + nw], ins[na + nw:]
        if comm:
            i = pl.program_id(0)
            comm.attach(c_refs, i, ni)
        a_vals = [a_ref[...] for a_ref in a_refs]
        for j in range(nj):
            cols = pl.ds(j * tn, tn)
            accs = [_dot_nt(a_vals[ai], w_refs[wi][cols, :]) for ai, wi in pairs]
            for o_ref, o in zip(o_refs, epilogue(accs, [t_ref[:, cols] for t_ref in t_refs])):
                o_ref[:, cols] = o.astype(o_ref.dtype)

    in_specs = ([pl.BlockSpec((tt, a.shape[1]), lambda i: (i, 0)) for a in a_list]
                + [pl.BlockSpec(w.shape, lambda i: (0, 0), pipeline_mode=pl.Buffered(1)) for w in w_list]
                + [pl.BlockSpec((tt, n), lambda i: (i, 0)) for _ in tiles])
    in_specs, out_specs, out_shape, scratch, extra = _with_comm(
        comm, in_specs, [pl.BlockSpec((tt, n), lambda i: (i, 0)) for _ in out_dtypes],
        [jax.ShapeDtypeStruct((t, n), dt) for dt in out_dtypes], [])
    return pl.pallas_call(
        body, name=name, grid=(ni,), in_specs=in_specs, out_specs=out_specs, out_shape=out_shape,
        scratch_shapes=scratch,
        compiler_params=_cparams("arbitrary" if comm else "parallel"),
    )(*a_list, *w_list, *tiles, *extra)


def _mm_nn(name, pairs, rows, fulls, epilogue, out_kinds, tt, tk, comm=None):
    t, k_total = pairs[0][0].shape
    n = pairs[0][1].shape[1]
    nk = k_total // tk
    nt = t // tt
    npair, nrow, nfull = len(pairs), len(rows), len(fulls)

    def body(*refs):
        ins, o_refs, scratch, c_refs = _split_refs(refs, 2 * npair + nrow + nfull, len(out_kinds), min(nk - 1, 1), comm)
        a_refs, w_refs = ins[:npair], ins[npair:2 * npair]
        r_refs, f_refs = ins[2 * npair:2 * npair + nrow], ins[2 * npair + nrow:]
        i, k = pl.program_id(0), pl.program_id(1)
        if comm:
            comm.attach(c_refs, i * nk + k, nt * nk)
        s = _dot_nn(a_refs[0][...], w_refs[0][...])
        for a_ref, w_ref in zip(a_refs[1:], w_refs[1:]):
            s = s + _dot_nn(a_ref[...], w_ref[...])

        def finish(acc):
            outs = epilogue(acc, [r[...] for r in r_refs], [f[...] for f in f_refs])
            for o_ref, o in zip(o_refs, outs):
                o_ref[...] = o.astype(o_ref.dtype)

        if nk == 1:
            finish(s)
        else:
            acc_ref = scratch[0]

            @pl.when(k == 0)
            def _():
                acc_ref[...] = s

            @pl.when(k > 0)
            def _():
                acc_ref[...] += s

            @pl.when(k == nk - 1)
            def _():
                finish(acc_ref[...])

    once = dict(pipeline_mode=pl.Buffered(1)) if nk == 1 else {}
    in_specs = ([pl.BlockSpec((tt, tk), lambda i, k: (i, k)) for _ in pairs]
                + [pl.BlockSpec((tk, n), (lambda i, k, off=off: (k + off, 0)), **once) for _, _, off in pairs]
                + [pl.BlockSpec((tt, n), lambda i, k: (i, 0)) for _ in rows]
                + [pl.BlockSpec((1, n), lambda i, k: (0, 0)) for _ in fulls])
    out_specs, out_shape = [], []
    for kind, dt in out_kinds:
        if kind == "tile":
            out_specs.append(pl.BlockSpec((tt, n), lambda i, k: (i, 0)))
            out_shape.append(jax.ShapeDtypeStruct((t, n), dt))
        else:
            out_specs.append(pl.BlockSpec((SUBLANES, n), lambda i, k: (i, 0)))
            out_shape.append(jax.ShapeDtypeStruct((nt * SUBLANES, n), dt))
    in_specs, out_specs, out_shape, scratch, extra = _with_comm(
        comm, in_specs, out_specs, out_shape, [] if nk == 1 else [pltpu.VMEM((tt, n), F32)])
    return pl.pallas_call(
        body, name=name, grid=(nt, nk), in_specs=in_specs, out_specs=out_specs, out_shape=out_shape,
        scratch_shapes=scratch,
        compiler_params=_cparams("arbitrary" if comm else "parallel", "arbitrary"),
    )(*[a for a, _, _ in pairs], *[w for _, w, _ in pairs], *rows, *fulls, *extra)


def _mm_tn(name, a, b, scale, tm, tt, comm=None):
    t, m = a.shape
    n = b.shape[1]
    nt = t // tt
    nm = m // tm

    def body(*refs):
        (a_ref, b_ref), (o_ref,), (acc_ref,), c_refs = _split_refs(refs, 2, 1, 1, comm)
        k = pl.program_id(1)
        if comm:
            i = pl.program_id(0)
            comm.attach(c_refs, i * nt + k, nm * nt)
        s = _dot_tn(a_ref[...], b_ref[...])

        @pl.when(k == 0)
        def _():
            acc_ref[...] = s

        @pl.when(k > 0)
        def _():
            acc_ref[...] += s

        @pl.when(k == nt - 1)
        def _():
            o_ref[...] = (acc_ref[...] * scale).astype(o_ref.dtype)

    in_specs, out_specs, out_shape, scratch, extra = _with_comm(
        comm, [pl.BlockSpec((tt, tm), lambda i, k: (k, i)), pl.BlockSpec((tt, n), lambda i, k: (k, 0))],
        [pl.BlockSpec((tm, n), lambda i, k: (i, 0))], [jax.ShapeDtypeStruct((m, n), BF16)],
        [pltpu.VMEM((tm, n), F32)])
    out = pl.pallas_call(
        body, name=name, grid=(nm, nt), in_specs=in_specs, out_specs=out_specs, out_shape=out_shape,
        scratch_shapes=scratch,
        compiler_params=_cparams("arbitrary" if comm else "parallel", "arbitrary"),
    )(a, b, *extra)
    return out if comm else out[0]


def _mm_tn_slabs(name, a_list, b, tt, comm=None):
    t, m = a_list[0].shape
    n = b.shape[1]
    na = len(a_list)
    nt = t // tt

    def body(*refs):
        ins, (o_ref,), (acc_ref,), c_refs = _split_refs(refs, na + 1, 1, 1, comm)
        k = pl.program_id(0)
        if comm:
            comm.attach(c_refs, k, nt)
        bv = ins[na][...]
        parts = [_dot_tn(a_ref[...], bv) for a_ref in ins[:na]]

        @pl.when(k == 0)
        def _():
            for j, part in enumerate(parts):
                acc_ref[pl.ds(j * m, m), :] = part

        @pl.when(k > 0)
        def _():
            for j, part in enumerate(parts):
                acc_ref[pl.ds(j * m, m), :] += part

        @pl.when(k == nt - 1)
        def _():
            o_ref[...] = acc_ref[...].astype(o_ref.dtype)

    in_specs, out_specs, out_shape, scratch, extra = _with_comm(
        comm, [pl.BlockSpec((tt, m), lambda k: (k, 0))] * na + [pl.BlockSpec((tt, n), lambda k: (k, 0))],
        [pl.BlockSpec((na * m, n), lambda k: (0, 0))], [jax.ShapeDtypeStruct((na * m, n), BF16)],
        [pltpu.VMEM((na * m, n), F32)])
    out = pl.pallas_call(
        body, name=name, grid=(nt,), in_specs=in_specs, out_specs=out_specs, out_shape=out_shape,
        scratch_shapes=scratch, compiler_params=_cparams("arbitrary"),
    )(*a_list, b, *extra)
    return out if comm else out[0]


def _group_sum(v, bd):
    hi, lo = _split2(v)
    return _dot_nn(hi, bd) + _dot_nn(lo, bd)


def _qknorm_fwd(proj, qg, kg, bd, tt):
    t = proj.shape[0]

    def body(q_ref, k_ref, v_ref, qg_ref, kg_ref, bd_ref, qn_ref, kn_ref, vb_ref):
        bdv = bd_ref[...]
        for x_ref, g_ref, o_ref in ((q_ref, qg_ref, qn_ref), (k_ref, kg_ref, kn_ref)):
            xv = x_ref[...]
            r = lax.rsqrt(_group_sum(xv * xv, bdv) * (1.0 / HEAD_DIM) + EPS)
            o_ref[...] = ((xv * r) * g_ref[...]).astype(BF16)
        vb_ref[...] = v_ref[...].astype(BF16)

    slab = lambda s: pl.BlockSpec((tt, SLAB), lambda i, s=s: (i, s))
    full = lambda shape: pl.BlockSpec(shape, lambda i: (0, 0))
    out = pl.BlockSpec((tt, SLAB), lambda i: (i, 0))
    return pl.pallas_call(
        body, name="qknorm_fwd", grid=(t // tt,),
        in_specs=[slab(3), slab(4), slab(5), full((1, SLAB)), full((1, SLAB)), full((SLAB, SLAB))],
        out_specs=[out, out, out],
        out_shape=[jax.ShapeDtypeStruct((t, SLAB), BF16)] * 3,
        compiler_params=_cparams("parallel"),
    )(proj, proj, proj, qg, kg, bd)


def _qknorm_bwd(proj, dqn, dkn, dv, qg, kg, bd, tt):
    t = proj.shape[0]

    def body(q_ref, k_ref, dqn_ref, dkn_ref, dv_ref, qg_ref, kg_ref, bd_ref, dq_ref, dk_ref, dvb_ref, part_ref):
        bdv = bd_ref[...]
        parts = []
        for x_ref, d_ref, g_ref, o_ref in ((q_ref, dqn_ref, qg_ref, dq_ref), (k_ref, dkn_ref, kg_ref, dk_ref)):
            xv, dn = x_ref[...], d_ref[...]
            r = lax.rsqrt(_group_sum(xv * xv, bdv) * (1.0 / HEAD_DIM) + EPS)
            u = dn * g_ref[...]
            dx = r * u - xv * (r * r * r) * (_group_sum(u * xv, bdv) * (1.0 / HEAD_DIM))
            o_ref[...] = dx.astype(BF16)
            parts.append(_fold8(dn * xv * r))
        dvb_ref[...] = dv_ref[...].astype(BF16)
        part_ref[...] = jnp.concatenate(parts, axis=1)

    slab = lambda s: pl.BlockSpec((tt, SLAB), lambda i, s=s: (i, s))
    tile = pl.BlockSpec((tt, SLAB), lambda i: (i, 0))
    full = lambda shape: pl.BlockSpec(shape, lambda i: (0, 0))
    return pl.pallas_call(
        body, name="qknorm_bwd", grid=(t // tt,),
        in_specs=[slab(3), slab(4), tile, tile, tile, full((1, SLAB)), full((1, SLAB)), full((SLAB, SLAB))],
        out_specs=[tile, tile, tile, pl.BlockSpec((SUBLANES, 2 * SLAB), lambda i: (i, 0))],
        out_shape=[jax.ShapeDtypeStruct((t, SLAB), BF16)] * 3
        + [jax.ShapeDtypeStruct((t // tt * SUBLANES, 2 * SLAB), F32)],
        compiler_params=_cparams("parallel"),
    )(proj, proj, dqn, dkn, dv, qg, kg, bd)


def _conv_taps(z, z_prev, row):
    zm1 = jnp.where(row == 0, z_prev[7:8], pltpu.roll(z, 1, 0))
    zm2 = jnp.where(row == 0, z_prev[6:7], jnp.where(row == 1, z_prev[7:8], pltpu.roll(z, 2, 0)))
    return zm1, zm2


def _conv_fwd(proj, cw, cb, tt):
    t = proj.shape[0]
    tb = tt // SUBLANES

    def body(b_ref, c_ref, u_ref, cp_ref, up_ref, cw_ref, cb_ref, o_ref):
        i = pl.program_id(0)
        z = c_ref[...] * u_ref[...]
        z_prev = jnp.where(i > 0, cp_ref[...] * up_ref[...], 0.0)
        row = lax.broadcasted_iota(jnp.int32, (tt, 1), 0)
        zm1, zm2 = _conv_taps(z, z_prev, row)
        y = cw_ref[0:1] * zm2 + cw_ref[1:2] * zm1 + cw_ref[2:3] * z + cb_ref[...]
        o_ref[...] = (b_ref[...] * y).astype(BF16)

    slab = lambda s: pl.BlockSpec((tt, SLAB), lambda i, s=s: (i, s))
    prev = lambda s: pl.BlockSpec((SUBLANES, SLAB), lambda i, s=s: (jnp.maximum(i * tb - 1, 0), s))
    return pl.pallas_call(
        body, name="conv_fwd", grid=(t // tt,),
        in_specs=[slab(0), slab(1), slab(2), prev(1), prev(2),
                  pl.BlockSpec((SUBLANES, SLAB), lambda i: (0, 0)), pl.BlockSpec((1, SLAB), lambda i: (0, 0))],
        out_specs=pl.BlockSpec((tt, SLAB), lambda i: (i, 0)),
        out_shape=jax.ShapeDtypeStruct((t, SLAB), BF16),
        compiler_params=_cparams("parallel"),
    )(proj, proj, proj, proj, proj, cw, cb)


def _conv_bwd(proj, dycat, cw, cb, tt):
    t = proj.shape[0]
    tb = tt // SUBLANES
    nblk = t // SUBLANES

    def body(b_ref, c_ref, u_ref, cp_ref, up_ref, bn_ref, dy_ref, dyn_ref, cw_ref, cb_ref,
             db_ref, dc_ref, du_ref, part_ref):
        i = pl.program_id(0)
        c, u, b, dyc = c_ref[...], u_ref[...], b_ref[...], dy_ref[...]
        z = c * u
        z_prev = jnp.where(i > 0, cp_ref[...] * up_ref[...], 0.0)
        row = lax.broadcasted_iota(jnp.int32, (tt, 1), 0)
        zm1, zm2 = _conv_taps(z, z_prev, row)
        w0, w1, w2 = cw_ref[0:1], cw_ref[1:2], cw_ref[2:3]
        y = w0 * zm2 + w1 * zm1 + w2 * z + cb_ref[...]
        db_ref[...] = (dyc * y).astype(BF16)
        g = dyc * b
        g_next = jnp.where(i < pl.num_programs(0) - 1, dyn_ref[...] * bn_ref[...], 0.0)
        gp1 = jnp.where(row == tt - 1, g_next[0:1], pltpu.roll(g, tt - 1, 0))
        gp2 = jnp.where(row == tt - 2, g_next[0:1], jnp.where(row == tt - 1, g_next[1:2], pltpu.roll(g, tt - 2, 0)))
        dz = w2 * g + w1 * gp1 + w0 * gp2
        dc_ref[...] = (dz * u).astype(BF16)
        du_ref[...] = (dz * c).astype(BF16)
        part_ref[...] = jnp.concatenate([_fold8(g * zm2), _fold8(g * zm1), _fold8(g * z), _fold8(g)], axis=1)

    slab = lambda s: pl.BlockSpec((tt, SLAB), lambda i, s=s: (i, s))
    prev = lambda s: pl.BlockSpec((SUBLANES, SLAB), lambda i, s=s: (jnp.maximum(i * tb - 1, 0), s))
    nxt = lambda s: pl.BlockSpec((SUBLANES, SLAB), lambda i, s=s: (jnp.minimum((i + 1) * tb, nblk - 1), s))
    tile = pl.BlockSpec((tt, SLAB), lambda i: (i, 0))
    return pl.pallas_call(
        body, name="conv_bwd", grid=(t // tt,),
        in_specs=[slab(0), slab(1), slab(2), prev(1), prev(2), nxt(0), slab(0), nxt(0),
                  pl.BlockSpec((SUBLANES, SLAB), lambda i: (0, 0)), pl.BlockSpec((1, SLAB), lambda i: (0, 0))],
        out_specs=[tile, tile, tile, pl.BlockSpec((SUBLANES, 4 * SLAB), lambda i: (i, 0))],
        out_shape=[jax.ShapeDtypeStruct((t, SLAB), BF16)] * 3
        + [jax.ShapeDtypeStruct((t // tt * SUBLANES, 4 * SLAB), F32)],
        compiler_params=_cparams("parallel"),
    )(proj, proj, proj, proj, proj, proj, dycat, dycat, cw, cb)


def _tri_masks(n):
    r = lax.broadcasted_iota(jnp.int32, (n, n), 0)
    c = lax.broadcasted_iota(jnp.int32, (n, n), 1)
    return (r > c).astype(BF16), (r >= c).astype(BF16)


def _sb_logits(z, causal):
    softplus = jnp.maximum(z, 0.0) + jnp.log(1.0 + jnp.exp(-jnp.abs(z)))
    lk = -softplus
    if causal is not None:
        lk = jnp.where(causal, lk, 0.0)
    return (z, lk, *_split2(lk))


def _sb_finish(z, lk, hi, lo, r_run, tri, causal, later):
    later = later + r_run
    ls = z + lk
    arg = ls + later
    if causal is not None:
        arg = jnp.where(causal, arg, -1e30)
    return lk, ls, jnp.exp(arg), later[:, 0:1] + lk[:, 0:1]


SB_DEAD_LOG = -111.0
CHAINS = ((0, 0), (0, 1), (1, 0), (1, 1))


def _all_dead(r_runs):
    m = r_runs[0]
    for r in r_runs[1:]:
        m = jnp.maximum(m, r)
    return (jnp.max(m) < SB_DEAD_LOG).astype(jnp.int32)


def _attn_fwd(qn, kn, vb, tri, sb, comm=None):
    t = qn.shape[0]
    bq = 2 * sb
    scale = HEAD_DIM ** -0.5

    def body(*refs):
        (q_ref, k_ref, v_ref, tri_ref), (o_ref, ob_ref), (acc_ref,), c_refs = _split_refs(refs, 4, 2, 1, comm)
        qi = pl.program_id(1)
        if comm:
            hp = pl.program_id(0)
            comm.attach(c_refs, hp * (t // bq) + qi, (SB_DIM // LANES) * (t // bq))
        lane = lax.broadcasted_iota(jnp.int32, (1, LANES), 1)
        hmasks = (lane < HEAD_DIM, lane >= HEAD_DIM)
        diag = lax.broadcasted_iota(jnp.int32, (sb, sb), 1) < lax.broadcasted_iota(jnp.int32, (sb, sb), 0)
        triv = tri_ref[...]
        qs = [jnp.where(hmasks[hh], q_ref[pl.ds(s * sb, sb), :], 0) * scale for s, hh in CHAINS]
        acc_ref[...] = jnp.zeros_like(acc_ref)

        def load_kv(kb):
            ks = kb * sb if isinstance(kb, int) else pl.multiple_of(kb * sb, sb)
            vraw = v_ref[pl.ds(ks, sb), :]
            return k_ref[pl.ds(ks, sb), :], [jnp.where(hm, vraw, 0) for hm in hmasks]

        def run_tiles(tiles, r_in):
            zs = [_dot_nt(qs[c], kv[0]) for c, kv, _, _ in tiles]
            mids = [_sb_logits(z, causal) for z, (_, _, causal, _) in zip(zs, tiles)]
            laters = [_dot_nn(m[2], triv) + _dot_nn(m[3], triv) for m in mids]
            outs = []
            for m, later, (c, _, causal, dep) in zip(mids, laters, tiles):
                outs.append(_sb_finish(*m, r_in[c] if dep is None else outs[dep][3], triv, causal, later))
            for o, (c, kv, _, _) in zip(outs, tiles):
                acc_ref[c] += _dot_nn(o[2].astype(BF16), kv[1][c % 2])
            return [o[3] for o in outs]

        zero = jnp.zeros((sb, 1), F32)
        kv_diag = [load_kv(2 * qi), load_kv(2 * qi + 1)]
        kv_prev = [load_kv(jnp.maximum(2 * qi - 1, 0)), kv_diag[0]]
        has_prev = lax.broadcasted_iota(jnp.int32, (sb, sb), 0) >= jnp.where(qi > 0, 0, sb)
        r_runs = run_tiles([(c, kv_diag[c // 2], diag, None) for c in range(4)]
                           + [(c, kv_prev[c // 2], has_prev if c < 2 else None, c) for c in range(4)], [zero] * 4)[4:]

        def step(carry):
            i, _, *rs = carry
            kvs = [load_kv(2 * qi - 2 - i), load_kv(2 * qi - 1 - i)]
            rs = run_tiles([(c, kvs[c // 2], None, None) for c in range(4)], rs)
            return (i + 1, _all_dead(rs), *rs)

        i_end, _, *rs = lax.while_loop(lambda c: jnp.logical_and(c[0] < 2 * qi - 1, c[1] == 0), step,
                                       (jnp.int32(0), _all_dead(r_runs), *r_runs))

        @pl.when(jnp.logical_and(i_end == 2 * qi - 1, _all_dead(rs[2:]) == 0))
        def _():
            kv_last = load_kv(0)
            run_tiles([(c, kv_last, None, None) for c in (2, 3)], rs)
        for s in range(2):
            out = acc_ref[2 * s] + acc_ref[2 * s + 1]
            o_ref[pl.ds(s * sb, sb), :] = out
            ob_ref[pl.ds(s * sb, sb), :] = out.astype(BF16)

    qspec = pl.BlockSpec((bq, LANES), lambda h, i: (i, h))
    kspec = pl.BlockSpec((t, LANES), lambda h, i: (0, h))
    in_specs, out_specs, out_shape, scratch, extra = _with_comm(
        comm, [qspec, kspec, kspec, pl.BlockSpec((sb, sb), lambda h, i: (0, 0))], [qspec, qspec],
        [jax.ShapeDtypeStruct((t, SB_DIM), F32), jax.ShapeDtypeStruct((t, SB_DIM), BF16)],
        [pltpu.VMEM((4, sb, LANES), F32)])
    return pl.pallas_call(
        body, name="attn_fwd", grid=(SB_DIM // LANES, t // bq),
        in_specs=in_specs, out_specs=out_specs, out_shape=out_shape, scratch_shapes=scratch,
        compiler_params=_cparams("arbitrary" if comm else "parallel", "arbitrary"),
    )(qn, kn, vb, tri, *extra)


def _attn_bwd(qn, kn, vb, o, dycat, tri, tri_inc, sb, comm=None):
    t = qn.shape[0]
    bq = 2 * sb
    scale = HEAD_DIM ** -0.5

    def body(*refs):
        ins, (dq_ref, dk_ref, dv_ref), (dq_acc,), c_refs = _split_refs(refs, 7, 3, 1, comm)
        q_ref, k_ref, v_ref, o_ref, do_ref, tri_ref, tinc_ref = ins
        qi = pl.program_id(1)
        if comm:
            hp = pl.program_id(0)
            comm.attach(c_refs, hp * (t // bq) + qi, (SB_DIM // LANES) * (t // bq))

        @pl.when(qi == 0)
        def _():
            dk_ref[...] = jnp.zeros_like(dk_ref)
            dv_ref[...] = jnp.zeros_like(dv_ref)

        lane = lax.broadcasted_iota(jnp.int32, (1, LANES), 1)
        hmasks = (lane < HEAD_DIM, lane >= HEAD_DIM)
        diag = lax.broadcasted_iota(jnp.int32, (sb, sb), 1) < lax.broadcasted_iota(jnp.int32, (sb, sb), 0)
        triv, tincv = tri_ref[...], tinc_ref[...]
        qs, dobs, d_rows = [], [], []
        for s, hh in CHAINS:
            rows = pl.ds(s * sb, sb)
            qs.append(jnp.where(hmasks[hh], q_ref[rows, :], 0) * scale)
            dobs.append(jnp.where(hmasks[hh], do_ref[rows, :], 0.0).astype(BF16))
            d_rows.append(jnp.sum(dobs[-1].astype(F32) * o_ref[rows, :], axis=1, keepdims=True))
        dq_acc[...] = jnp.zeros_like(dq_acc)

        def load_kv(kb):
            ks = kb * sb if isinstance(kb, int) else pl.multiple_of(kb * sb, sb)
            return k_ref[pl.ds(ks, sb), :], v_ref[pl.ds(ks, sb), :], ks

        def run_tiles(tiles, r_in, g_in):
            zs = [_dot_nt(qs[c], kv[0]) for c, kv, _, _ in tiles]
            das = [_dot_nt(dobs[c], kv[1]) for c, kv, _, _ in tiles]
            mids = [_sb_logits(z, causal) for z, (_, _, causal, _) in zip(zs, tiles)]
            laters = [_dot_nn(m[2], triv) + _dot_nn(m[3], triv) for m in mids]
            fins, abs_, es = [], [], []
            for m, later, da, (c, _, causal, dep) in zip(mids, laters, das, tiles):
                fins.append(_sb_finish(*m, r_in[c] if dep is None else fins[dep][3], triv, causal, later))
                abs_.append(fins[-1][2].astype(BF16))
                es.append(da * abs_[-1].astype(F32))
            splits = [_split2(e) for e in es]
            e_sums = [_dot_nn(hi, tincv) + _dot_nn(lo, tincv) for hi, lo in splits]
            e_froms, dzbs = [], []
            for e, e_sum, fin, (c, _, causal, dep) in zip(es, e_sums, fins, tiles):
                e_froms.append(e_sum + (g_in[c] if dep is None else e_froms[dep][:, 0:1]))
                dz = e - jnp.exp(fin[1]) * (e + (d_rows[c] - e_froms[-1]))
                if causal is not None:
                    dz = jnp.where(causal, dz, 0.0)
                dzbs.append(dz.astype(BF16))
            for dzb, (c, kv, _, _) in zip(dzbs, tiles):
                dq_acc[c] += _dot_nn(dzb, kv[0])
            by_rows = {}
            for dzb, ab, (c, kv, _, _) in zip(dzbs, abs_, tiles):
                by_rows.setdefault(id(kv), (kv[2], []))[1].append((_dot_tn(dzb, qs[c]), _dot_tn(ab, dobs[c])))
            for first_row, parts in by_rows.values():
                rows = pl.ds(first_row, sb)
                dk_ref[rows, :] += sum(p[0] for p in parts[1:]) + parts[0][0]
                dv_ref[rows, :] += sum(p[1] for p in parts[1:]) + parts[0][1]
            return [f[3] for f in fins], [ef[:, 0:1] for ef in e_froms]

        zero = jnp.zeros((sb, 1), F32)
        kv_diag = [load_kv(2 * qi), load_kv(2 * qi + 1)]
        kv_prev = [load_kv(jnp.maximum(2 * qi - 1, 0)), kv_diag[0]]
        has_prev = lax.broadcasted_iota(jnp.int32, (sb, sb), 0) >= jnp.where(qi > 0, 0, sb)
        r_first, g_first = run_tiles(
            [(c, kv_diag[c // 2], diag, None) for c in range(4)]
            + [(c, kv_prev[c // 2], has_prev if c < 2 else None, c) for c in range(4)], [zero] * 4, [zero] * 4)
        r_runs, g_runs = r_first[4:], g_first[4:]

        def step(carry):
            i, _, *rg = carry
            kvs = [load_kv(2 * qi - 2 - i), load_kv(2 * qi - 1 - i)]
            rs, gs = run_tiles([(c, kvs[c // 2], None, None) for c in range(4)], rg[:4], rg[4:])
            return (i + 1, _all_dead(rs), *rs, *gs)

        i_end, _, *rg = lax.while_loop(lambda c: jnp.logical_and(c[0] < 2 * qi - 1, c[1] == 0), step,
                                       (jnp.int32(0), _all_dead(r_runs), *r_runs, *g_runs))

        @pl.when(jnp.logical_and(i_end == 2 * qi - 1, _all_dead(rg[2:4]) == 0))
        def _():
            kv_last = load_kv(0)
            run_tiles([(c, kv_last, None, None) for c in (2, 3)], rg[:4], rg[4:])
        for s in range(2):
            dq_ref[pl.ds(s * sb, sb), :] = jnp.where(hmasks[0], dq_acc[2 * s], dq_acc[2 * s + 1]) * scale

    qspec = pl.BlockSpec((bq, LANES), lambda h, i: (i, h))
    dospec = pl.BlockSpec((bq, LANES), lambda h, i: (i, h + CONV_DIM // LANES))
    kspec = pl.BlockSpec((t, LANES), lambda h, i: (0, h))
    full = pl.BlockSpec((sb, sb), lambda h, i: (0, 0))
    in_specs, out_specs, out_shape, scratch, extra = _with_comm(
        comm, [qspec, kspec, kspec, qspec, dospec, full, full], [qspec, kspec, kspec],
        [jax.ShapeDtypeStruct((t, SB_DIM), F32)] * 3, [pltpu.VMEM((4, sb, LANES), F32)])
    return pl.pallas_call(
        body, name="attn_bwd", grid=(SB_DIM // LANES, t // bq),
        in_specs=in_specs, out_specs=out_specs, out_shape=out_shape, scratch_shapes=scratch,
        compiler_params=_cparams("arbitrary" if comm else "parallel", "arbitrary"),
    )(qn, kn, vb, o, dycat, tri, tri_inc, *extra)


def _ple_loss(x3, p2, tgt, gain, wpg, wppt, tt):
    t, d = x3.shape
    pdim = p2.shape[1]
    nt = t // tt

    def body(x_ref, p_ref, t_ref, g_ref, wg_ref, wp_ref,
             dx_ref, dxb_ref, dwg_ref, dwp_ref, gpart_ref, lpart_ref, accg_ref, accp_ref):
        i = pl.program_id(0)
        xv, gain_v = x_ref[...], g_ref[...]
        hb = ((xv * _rms_stats(xv)) * gain_v).astype(BF16)
        gate = jax.nn.sigmoid(_dot_nn(hb, wg_ref[...]))
        pb = p_ref[...].astype(BF16)
        pe = _dot_nt(pb, wp_ref[...])
        diff = xv + gate * pe - t_ref[...]
        lsum = jnp.sum(_fold8(diff * diff), axis=1, keepdims=True) * (0.5 / d)
        lpart_ref[...] = jnp.broadcast_to(lsum, (SUBLANES, LANES))
        dy = diff * (1.0 / d)
        dgz = ((dy * pe) * gate * (1.0 - gate)).astype(BF16)
        dpe = (dy * gate).astype(BF16)
        dx_n, grow = _rms_bwd(_dot_nt(dgz, wg_ref[...]), xv, gain_v)
        dx = dy + dx_n
        dx_ref[...] = dx
        dxb_ref[...] = dx.astype(BF16)
        gpart_ref[...] = _fold8(grow)
        sg = _dot_tn(hb, dgz)
        sp = _dot_tn(dpe, pb)

        @pl.when(i == 0)
        def _():
            accg_ref[...] = sg
            accp_ref[...] = sp

        @pl.when(i > 0)
        def _():
            accg_ref[...] += sg
            accp_ref[...] += sp

        @pl.when(i == nt - 1)
        def _():
            dwg_ref[...] = accg_ref[...].astype(BF16)
            dwp_ref[...] = accp_ref[...].astype(BF16)

    tile = lambda w: pl.BlockSpec((tt, w), lambda i: (i, 0))
    full = lambda shape: pl.BlockSpec(shape, lambda i: (0, 0))
    return pl.pallas_call(
        body, name="ple_loss", grid=(nt,),
        in_specs=[tile(d), tile(pdim), tile(d), full((1, d)),
                  pl.BlockSpec((d, d), lambda i: (0, 0), pipeline_mode=pl.Buffered(1)),
                  pl.BlockSpec((d, pdim), lambda i: (0, 0), pipeline_mode=pl.Buffered(1))],
        out_specs=[tile(d), tile(d), full((d, d)), full((d, pdim)),
                   pl.BlockSpec((SUBLANES, d), lambda i: (i, 0)), pl.BlockSpec((SUBLANES, LANES), lambda i: (i, 0))],
        out_shape=[jax.ShapeDtypeStruct((t, d), F32), jax.ShapeDtypeStruct((t, d), BF16),
                   jax.ShapeDtypeStruct((d, d), BF16), jax.ShapeDtypeStruct((d, pdim), BF16),
                   jax.ShapeDtypeStruct((nt * SUBLANES, d), F32), jax.ShapeDtypeStruct((nt * SUBLANES, LANES), F32)],
        scratch_shapes=[pltpu.VMEM((d, d), F32), pltpu.VMEM((d, pdim), F32)],
        compiler_params=_cparams("arbitrary"),
    )(x3, p2, tgt, gain, wpg, wppt)


def _pack_small(parts_gain, conv_part, qk_part):
    d = parts_gain[0].shape[1]
    ng = len(parts_gain)

    def body(*refs):
        g_refs, conv_ref, qk_ref, o_ref = refs[:ng], refs[ng], refs[ng + 1], refs[ng + 2]
        rows = [jnp.sum(r[...], axis=0, keepdims=True) for r in g_refs]
        cs = jnp.sum(conv_ref[...], axis=0, keepdims=True)
        qs = jnp.sum(qk_ref[...], axis=0, keepdims=True)
        rows.append(jnp.concatenate([cs[:, 3 * SLAB:], cs[:, :SLAB]], axis=1))
        rows.append(cs[:, SLAB:3 * SLAB])
        rows.append(qs)
        rid = lax.broadcasted_iota(jnp.int32, (2 * SUBLANES, 1), 0)
        out = jnp.zeros((2 * SUBLANES, d), F32)
        for idx, r in enumerate(rows):
            out = jnp.where(rid == idx, r, out)
        o_ref[...] = out

    return pl.pallas_call(
        body, name="pack_small", out_shape=jax.ShapeDtypeStruct((2 * SUBLANES, d), F32),
    )(*parts_gain, conv_part, qk_part)


def _sum_slots(name, slots, out_dtype=F32):
    _, r, c = slots.shape

    def body(s_ref, o_ref):
        acc = s_ref[0].astype(F32)
        for d in range(1, N_DEV):
            acc = acc + s_ref[d].astype(F32)
        o_ref[...] = acc.astype(o_ref.dtype)

    return pl.pallas_call(body, name=name, out_shape=jax.ShapeDtypeStruct((r, c), out_dtype),
                          compiler_params=pltpu.CompilerParams(vmem_limit_bytes=VMEM_LIMIT_BYTES))(slots)


def _adamw(name, w, g, m, v):
    c1 = 1.0 - ADAM_B1 ** ADAM_STEP
    c2 = 1.0 - ADAM_B2 ** ADAM_STEP

    def body(w_ref, g_ref, m_ref, v_ref, go_ref, d_ref, nm_ref, nv_ref):
        if g.ndim == 3:
            gv = g_ref[0].astype(F32)
            for dev in range(1, N_DEV):
                gv = gv + g_ref[dev].astype(F32)
        else:
            gv = g_ref[...]
        go_ref[...] = gv
        nm = ADAM_B1 * m_ref[...] + (1.0 - ADAM_B1) * gv
        nv = ADAM_B2 * v_ref[...] + (1.0 - ADAM_B2) * (gv * gv)
        d_ref[...] = -ADAM_LR * ((nm / c1) / (jnp.sqrt(nv / c2) + ADAM_EPS) + ADAM_WD * w_ref[...])
        nm_ref[...] = nm
        nv_ref[...] = nv

    return pl.pallas_call(body, name=name, out_shape=[jax.ShapeDtypeStruct(w.shape, F32)] * 4,
                          compiler_params=pltpu.CompilerParams(vmem_limit_bytes=VMEM_LIMIT_BYTES))(w, g, m, v)


def _any_specs(n):
    return [pl.BlockSpec(memory_space=pl.ANY)] * n


def _all_gather(name, shards):
    n = len(shards)

    def body(*refs):
        ins, outs = refs[:n], refs[n:2 * n]
        send_sems, recv_sems, local_sems = refs[2 * n:]
        x, y, c = (lax.axis_index(a) for a in MESH_AXES)
        me, sibling = (x, y, c), (x, y, 1 - c)
        chips = [(1 - x, y), (x, 1 - y), (1 - x, 1 - y)]

        def rows(a, px, py, pc):
            r = ins[a].shape[0]
            return outs[a].at[pl.ds((4 * px + 2 * py + pc) * r, r), :]

        def copy(a, k, block, to, src=None):
            return pltpu.make_async_remote_copy(
                src_ref=rows(a, *block) if src is None else src, dst_ref=rows(a, *block),
                send_sem=send_sems.at[7 * a + k], recv_sem=recv_sems.at[7 * a + k],
                device_id=to, device_id_type=MESH)

        mine = [pltpu.make_async_copy(ins[a], rows(a, *me), local_sems.at[a]) for a in range(n)]
        for cp in mine:
            cp.start()
        first = []
        for a in range(n):
            first.append(copy(a, 0, me, sibling, src=ins[a]))
            first += [copy(a, 1 + j, me, (*chip, c), src=ins[a]) for j, chip in enumerate(chips)]
        for cp in first:
            cp.start()
        passed = []
        for j, chip in enumerate(chips):
            for a in range(n):
                copy(a, 1 + j, (*chip, c), me).wait_recv()
                fwd = copy(a, 4 + j, (*chip, c), sibling)
                fwd.start()
                passed.append(fwd)
        for a in range(n):
            copy(a, 0, sibling, me).wait_recv()
            for j, chip in enumerate(chips):
                copy(a, 4 + j, (*chip, 1 - c), me).wait_recv()
        for cp in first + passed:
            cp.wait_send()
        for cp in mine:
            cp.wait()

    return pl.pallas_call(
        body, name=name, in_specs=_any_specs(n), out_specs=_any_specs(n),
        out_shape=[jax.ShapeDtypeStruct((N_DEV * s.shape[0], s.shape[1]), s.dtype) for s in shards],
        scratch_shapes=[pltpu.SemaphoreType.DMA((7 * n,)), pltpu.SemaphoreType.DMA((7 * n,)),
                        pltpu.SemaphoreType.DMA((n,))],
    )(*shards)


def _residual_and_norm(res_scale):
    def epilogue(acc, rows, fulls):
        out = rows[0] + res_scale * acc
        return [out] + [(out * _rms_stats(out)) * gain for gain in fulls]
    return epilogue


def _ffn_fwd(tag, x, h, wgt, wut, wd, next_gain, tt_nt, tt_nn, comm_gate=None, comm_down=None):
    f = wgt.shape[0]

    def gate_up(accs, _):
        return [accs[0], accs[1], jax.nn.silu(accs[0]) * accs[1]]

    g, u, a, *got_gate = _mm_nt(f"{tag}_gate_up", [h], [wgt, wut], [(0, 0), (0, 1)], gate_up, [BF16] * 3,
                                tt_nn, _pick(f, 256), comm_gate)
    if wd is None:
        wd = got_gate[0]
    gains = [] if next_gain is None else [next_gain]
    out, *rest = _mm_nn(f"{tag}_down", [(a, wd, 0)], [x], gains, _residual_and_norm(FFN_RES),
                        [("tile", F32)] + [("tile", BF16)] * len(gains), tt_nt, f, comm_down)
    h_next = rest.pop(0) if gains else None
    return out, h_next, (g, u, a), got_gate, rest


def _norm_bwd_epilogue(acc, rows, fulls):
    x_in, dy = rows
    dx_n, grow = _rms_bwd(acc, x_in, fulls[0])
    dx = dy + dx_n
    return [dx, dx, _fold8(grow)]


_NORM_BWD_OUTS = [("tile", F32), ("tile", BF16), ("part", F32)]


def _ffn_bwd(tag, x_in, h, hidden, dy, dyb, gain, wgt, wut, wd, tt_nt, tt_nn, riders=(None, None),
             exchange_own=False):
    g, u, a = hidden
    f = wgt.shape[0]
    tt_tn = _pick(h.shape[0], 2 * tt_nt)
    own = (lambda arr: _Exchange([arr])) if exchange_own else (lambda arr: None)

    def carried(result, rider):
        return (result[0], result[1:]) if rider else (result, [])

    def hidden_grads(accs, tiles):
        da = FFN_RES * accs[0]
        gv, uv = tiles[0].astype(F32), tiles[1].astype(F32)
        sg = jax.nn.sigmoid(gv)
        s = gv * sg
        return [da * uv * (sg * (1.0 + gv * (1.0 - sg))), da * s]

    dwd, got_dwd = carried(_mm_tn(f"{tag}_dwd", a, dyb, FFN_RES, f // 2, tt_tn, riders[0]), riders[0])
    dg, du, *x_dwd = _mm_nt(f"{tag}_bwd_hidden", [dyb], [wd], [(0, 0)], hidden_grads, [BF16, BF16],
                            tt_nn, _pick(f, 256), own(dwd), tiles=[g, u])
    dwg, got_dwg = carried(_mm_tn(f"{tag}_dwg", dg, h, 1.0, f // 2, tt_tn, riders[1]), riders[1])
    dwu, x_dwg = carried(_mm_tn(f"{tag}_dwu", du, h, 1.0, f // 2, tt_tn, own(dwg)), exchange_own)
    dx, dxb, gpart, *x_dwu = _mm_nn(f"{tag}_bwd_dx", [(dg, wgt, 0), (du, wut, 0)], [x_in, dy], [gain],
                                    _norm_bwd_epilogue, _NORM_BWD_OUTS, tt_nn, f, own(dwu))
    grads = [*x_dwg, *x_dwu, *x_dwd] if exchange_own else [dwg, dwu, dwd]
    return dx, dxb, gpart, grads, [*got_dwd, *got_dwg]


def kernel(x, p, ffn1_norm, ffn1_w_gate, ffn1_w_up, ffn1_w_down, mix_norm, w_in, conv_w, conv_b, q_norm, k_norm, w_out, ffn2_norm, ffn2_w_gate, ffn2_w_up, ffn2_w_down, ple_norm, ple_w_gate, ple_w_proj, loss_target, m_ffn1_norm, m_ffn1_w_gate, m_ffn1_w_up, m_ffn1_w_down, m_mix_norm, m_w_in, m_conv_w, m_conv_b, m_q_norm, m_k_norm, m_w_out, m_ffn2_norm, m_ffn2_w_gate, m_ffn2_w_up, m_ffn2_w_down, m_ple_norm, m_ple_w_gate, m_ple_w_proj, v_ffn1_norm, v_ffn1_w_gate, v_ffn1_w_up, v_ffn1_w_down, v_mix_norm, v_w_in, v_conv_w, v_conv_b, v_q_norm, v_k_norm, v_w_out, v_ffn2_norm, v_ffn2_w_gate, v_ffn2_w_up, v_ffn2_w_down, v_ple_norm, v_ple_w_gate, v_ple_w_proj):
    x0, p2, tgt = x[0], p[0, 0], loss_target[0]
    t, d = x0.shape
    tt_nt = _pick(t, 1024)
    tt_nn = _pick(t, 512)
    tt_ew = _pick(t, 512)
    tt_ple = _pick(t, 512)
    sb = _pick(t // 2, 256)

    t_bf = lambda w: w[0].T.astype(BF16)
    n_bf = lambda w: w[0].astype(BF16)
    cw_tile = jnp.zeros((SUBLANES, LANES), F32).at[:conv_w.shape[1], :conv_w.shape[2]].set(conv_w[0])
    gather_first = _GatherTwoLevel([t_bf(ffn1_w_gate), t_bf(ffn1_w_up)])
    gather_down = _GatherTwoLevel([n_bf(ffn1_w_down), n_bf(w_out), cw_tile])
    gather_in = _GatherTwoLevel([t_bf(w_in)])
    gather_late = _GatherTwoLevel([t_bf(ffn2_w_gate), t_bf(ffn2_w_up), n_bf(ffn2_w_down), n_bf(ple_w_gate),
                                   t_bf(ple_w_proj)])
    ncs = conv_w.shape[2]

    qg = jnp.tile(q_norm, (1, SB_DIM // HEAD_DIM))
    kg = jnp.tile(k_norm, (1, SB_DIM // HEAD_DIM))
    gi = lax.broadcasted_iota(jnp.int32, (SLAB, SLAB), 0) // HEAD_DIM
    gj = lax.broadcasted_iota(jnp.int32, (SLAB, SLAB), 1) // HEAD_DIM
    bd = (gi == gj).astype(BF16)
    tri, tri_inc = _tri_masks(sb)

    h1, wg1t, wu1t = _rmsnorm("ffn1_norm", x0, ffn1_norm, tt_ew, gather_first)
    x1, h2, hidden1, (wd1, wout, cw_all), (wint,) = _ffn_fwd(
        "ffn1", x0, h1, wg1t, wu1t, None, mix_norm, tt_nt, tt_nn, gather_down, gather_in)
    cw_full = cw_all.reshape(N_DEV, SUBLANES, LANES)[:, :, :ncs].transpose(1, 0, 2).reshape(SUBLANES, N_DEV * ncs)
    (proj,) = _mm_nt("in_proj", [h2], [wint], [(0, 0)], lambda accs, _: accs, [F32], tt_nn, SLAB)
    y_conv = _conv_fwd(proj, cw_full, conv_b, tt_ew)
    qn, kn, vb = _qknorm_fwd(proj, qg, kg, bd, tt_ew)
    o, ob, wg2t, wu2t, wd2, wpg, wppt = _attn_fwd(qn, kn, vb, tri, sb, gather_late)
    x2, h3 = _mm_nn("out_proj", [(y_conv, wout, 0), (ob, wout, 1)], [x1], [ffn2_norm], _residual_and_norm(1.0),
                    [("tile", F32), ("tile", BF16)], tt_nn, SLAB)
    x3, _, hidden2, _, _ = _ffn_fwd("ffn2", x2, h3, wg2t, wu2t, wd2, None, tt_nt, tt_nn)

    dx3, dx3b, dwpg, dwppt, gp_ple, lpart = _ple_loss(x3, p2, tgt, ple_norm, wpg, wppt, tt_ple)
    loss = lax.psum(jnp.sum(lpart[:, 0]), MESH_AXES)
    dx2, dx2b, gp_ffn2, (dwg2, dwu2, dwd2), _ = _ffn_bwd("ffn2", x2, h3, hidden2, dx3, dx3b, ffn2_norm,
                                                         wg2t, wu2t, wd2, tt_nt, tt_nn)
    (dycat,) = _mm_nt("out_proj_bwd", [dx2b], [wout], [(0, 0)], lambda accs, _: accs, [F32], tt_nn, SLAB)
    dwout = _mm_tn_slabs("dwout", [y_conv, ob], dx2b, tt_nt)
    dqn, dkn, dv, *slots_late = _attn_bwd(qn, kn, vb, o, dycat, tri, tri_inc, sb,
                                          _Exchange([dwg2, dwu2, dwd2, dwpg, dwppt]))
    dq, dk, dvb, qk_part = _qknorm_bwd(proj, dqn, dkn, dv, qg, kg, bd, tt_ew)
    db, dc, du, conv_part = _conv_bwd(proj, dycat, cw_full, conv_b, tt_ew)
    dproj = [db, dc, du, dq, dk, dvb]
    dwin, slot_wout = _mm_tn_slabs("dwin", dproj, h2, tt_nt, _Exchange([dwout]))
    dx1, dx1b, gp_mix, slot_win = _mm_nn(
        "in_proj_bwd", [(dp, wint, s) for s, dp in enumerate(dproj)], [x1, dx2], [mix_norm],
        _norm_bwd_epilogue, _NORM_BWD_OUTS, tt_nn, SLAB, _Exchange([dwin]))
    dx0, _, gp_ffn1, slots_ffn1, _ = _ffn_bwd(
        "ffn1", x0, h1, hidden1, dx1, dx1b, ffn1_norm, wg1t, wu1t, wd1, tt_nt, tt_nn, exchange_own=True)

    slots = [*slots_ffn1, slot_win, slot_wout, *slots_late]
    per_dev = [s.reshape(N_DEV, s.shape[0] // N_DEV, s.shape[1]) for s in slots]
    c_wg1, c_wu1, c_wd1, c_win, c_wout, c_wg2, c_wu2, c_wd2, c_wpg, c_wpp = per_dev
    g_win, g_wpp = _sum_slots("sum_grads_w_in", c_win), _sum_slots("sum_grads_ple_w_proj", c_wpp)
    small = _pack_small([gp_ffn1, gp_mix, gp_ffn2, gp_ple], conv_part, qk_part)
    (small_all,) = _all_gather("gather_small_grads", [small])
    sm = _sum_slots("sum_small_grads", small_all.reshape(N_DEV, 2 * SUBLANES, d))
    fold = lambda r: r.reshape(SB_DIM // HEAD_DIM, HEAD_DIM).sum(axis=0)[None]
    me_idx = 4 * lax.axis_index("x") + 2 * lax.axis_index("y") + lax.axis_index("c")
    cw_grad = jnp.stack([sm[4, SLAB:], sm[5, :SLAB], sm[5, SLAB:]])
    grads = {
        "ffn1_norm": sm[0:1], "ffn1_w_down": c_wd1,
        "mix_norm": sm[1:2], "w_in": g_win.T, "conv_w": lax.dynamic_slice_in_dim(cw_grad, me_idx * ncs, ncs, axis=1),
        "conv_b": sm[4:5, :SLAB], "q_norm": fold(sm[6, :SLAB]), "k_norm": fold(sm[6, SLAB:]),
        "w_out": c_wout, "ffn2_norm": sm[2:3], "ffn2_w_down": c_wd2,
        "ple_norm": sm[3:4], "ple_w_gate": c_wpg, "ple_w_proj": g_wpp.T,
    }
    grads_t = {"ffn1_w_gate": c_wg1, "ffn1_w_up": c_wu1, "ffn2_w_gate": c_wg2, "ffn2_w_up": c_wu2}

    weights = dict(ffn1_norm=ffn1_norm, ffn1_w_gate=ffn1_w_gate, ffn1_w_up=ffn1_w_up, ffn1_w_down=ffn1_w_down,
                   mix_norm=mix_norm, w_in=w_in, conv_w=conv_w, conv_b=conv_b, q_norm=q_norm, k_norm=k_norm,
                   w_out=w_out, ffn2_norm=ffn2_norm, ffn2_w_gate=ffn2_w_gate, ffn2_w_up=ffn2_w_up,
                   ffn2_w_down=ffn2_w_down, ple_norm=ple_norm, ple_w_gate=ple_w_gate, ple_w_proj=ple_w_proj)
    m_in = dict(ffn1_norm=m_ffn1_norm, ffn1_w_gate=m_ffn1_w_gate, ffn1_w_up=m_ffn1_w_up, ffn1_w_down=m_ffn1_w_down,
                mix_norm=m_mix_norm, w_in=m_w_in, conv_w=m_conv_w, conv_b=m_conv_b, q_norm=m_q_norm,
                k_norm=m_k_norm, w_out=m_w_out, ffn2_norm=m_ffn2_norm, ffn2_w_gate=m_ffn2_w_gate,
                ffn2_w_up=m_ffn2_w_up, ffn2_w_down=m_ffn2_w_down, ple_norm=m_ple_norm, ple_w_gate=m_ple_w_gate,
                ple_w_proj=m_ple_w_proj)
    v_in = dict(ffn1_norm=v_ffn1_norm, ffn1_w_gate=v_ffn1_w_gate, ffn1_w_up=v_ffn1_w_up, ffn1_w_down=v_ffn1_w_down,
                mix_norm=v_mix_norm, w_in=v_w_in, conv_w=v_conv_w, conv_b=v_conv_b, q_norm=v_q_norm,
                k_norm=v_k_norm, w_out=v_w_out, ffn2_norm=v_ffn2_norm, ffn2_w_gate=v_ffn2_w_gate,
                ffn2_w_up=v_ffn2_w_up, ffn2_w_down=v_ffn2_w_down, ple_norm=v_ple_norm, ple_w_gate=v_ple_w_gate,
                ple_w_proj=v_ple_w_proj)
    g_out, d_out, m_out, v_out = [], [], [], []
    for name, w in weights.items():
        if name in grads_t:
            view, back = (lambda a: a[0].T), (lambda a: a.T[None])
            g_in = grads_t[name]
        else:
            view, back = (lambda a, w=w: a.reshape(w.shape[-2:])), (lambda a, w=w: a.reshape(w.shape))
            g_in = grads[name] if grads[name].ndim == 3 else view(grads[name])
        g2, dlt, nm, nv = _adamw(f"adamw_{name}", view(w), g_in, view(m_in[name]), view(v_in[name]))
        g_out.append(back(g2))
        d_out.append(back(dlt))
        m_out.append(back(nm))
        v_out.append(back(nv))
    return (loss, dx0[None], *g_out, *d_out, *m_out, *v_out)
```

```python
import jax
import jax.numpy as jnp
from jax import lax
from jax.experimental import pallas as pl
from jax.experimental.pallas import tpu as pltpu

F32 = jnp.float32
BF16 = jnp.bfloat16

EPS = 1e-6
FFN_RES = 0.5
HEAD_DIM = 64
CONV_DIM = 512
SB_DIM = 512
SLAB = 512
N_DEV = 8
MESH_AXES = ("x", "y", "c")
MESH = pl.DeviceIdType.MESH

ADAM_LR = 0.001
ADAM_B1 = 0.9
ADAM_B2 = 0.999
ADAM_EPS = 1e-08
ADAM_WD = 0.01
ADAM_STEP = 10

VMEM_LIMIT_BYTES = 56 * 1024 * 1024
SUBLANES = 8
LANES = 128


def _cparams(*semantics):
    return pltpu.CompilerParams(dimension_semantics=semantics, vmem_limit_bytes=VMEM_LIMIT_BYTES)


def _dot_nn(a, b):
    return jnp.dot(a, b, preferred_element_type=F32)


def _dot_nt(a, b):
    return lax.dot_general(a, b, (((1,), (1,)), ((), ())), preferred_element_type=F32)


def _dot_tn(a, b):
    return lax.dot_general(a, b, (((0,), (0,)), ((), ())), preferred_element_type=F32)


def _fold8(v):
    rows, cols = v.shape
    return jnp.sum(v.reshape(rows // SUBLANES, SUBLANES, cols), axis=0)


def _split2(v):
    hi = v.astype(BF16)
    lo = (v - hi.astype(F32)).astype(BF16)
    return hi, lo


def _rms_stats(x):
    return lax.rsqrt(jnp.mean(x * x, axis=-1, keepdims=True) + EPS)


def _rms_bwd(dh, x, gain):
    r = _rms_stats(x)
    u = dh * gain
    dx = r * u - x * (r * r * r) * jnp.mean(u * x, axis=-1, keepdims=True)
    return dx, dh * x * r


def _pick(n, pref):
    return pref if n % pref == 0 else n


def _rmsnorm(name, x, gain, tt, comm=None):
    t, d = x.shape
    nt = t // tt

    def body(*refs):
        (x_ref, g_ref), (o_ref,), _, c_refs = _split_refs(refs, 2, 1, 0, comm)
        if comm:
            comm.attach(c_refs, pl.program_id(0), nt)
        xv = x_ref[...]
        o_ref[...] = ((xv * _rms_stats(xv)) * g_ref[...]).astype(BF16)

    in_specs, out_specs, out_shape, scratch, extra = _with_comm(
        comm, [pl.BlockSpec((tt, d), lambda i: (i, 0)), pl.BlockSpec((1, d), lambda i: (0, 0))],
        [pl.BlockSpec((tt, d), lambda i: (i, 0))], [jax.ShapeDtypeStruct((t, d), BF16)], [])
    return pl.pallas_call(
        body, name=name, grid=(nt,), in_specs=in_specs, out_specs=out_specs, out_shape=out_shape,
        scratch_shapes=scratch, compiler_params=_cparams("arbitrary" if comm else "parallel"),
    )(x, gain, *extra)


class _Exchange:
    FLIPS = [(fx, fy, fc) for fx in (0, 1) for fy in (0, 1) for fc in (0, 1)][1:]

    def __init__(self, arrays):
        self.arrays = list(arrays)
        self.n = len(self.arrays)
        self.rows = [a.shape[0] // N_DEV for a in self.arrays]
        self.out_shape = [jax.ShapeDtypeStruct(a.shape, a.dtype) for a in self.arrays]
        self.scratch = [pltpu.SemaphoreType.DMA((7 * self.n,)), pltpu.SemaphoreType.DMA((7 * self.n,)),
                        pltpu.SemaphoreType.DMA((self.n,))]

    def _copies(self, ins, outs, sems, arrivals):
        send_sems, recv_sems, local_sems = sems
        x, y, c = (lax.axis_index(a) for a in MESH_AXES)
        me_idx = 4 * x + 2 * y + c
        local, send, recv = [], [], []
        for a in range(self.n):
            r = self.rows[a]

            def blk(ref, idx, r=r):
                return ref.at[pl.ds(idx * r, r), :]

            def src(idx, a=a, blk=blk):
                return blk(ins[a], idx)

            local.append(pltpu.make_async_copy(src(me_idx), blk(outs[a], me_idx), local_sems.at[a]))
            for k, flip in enumerate(self.FLIPS):
                px, py, pc = (1 - v if f else v for v, f in zip((x, y, c), flip))
                p_idx = 4 * px + 2 * py + pc
                for dst_idx, group in ((me_idx, send), (p_idx, recv))[:2 if arrivals else 1]:
                    group.append(pltpu.make_async_remote_copy(
                        src_ref=src(p_idx), dst_ref=blk(outs[a], dst_idx),
                        send_sem=send_sems.at[7 * a + k], recv_sem=recv_sems.at[7 * a + k],
                        device_id=(px, py, pc), device_id_type=MESH))
        return local, send, recv

    def start(self, ins, outs, sems):
        local, send, _ = self._copies(ins, outs, sems, arrivals=False)
        for cp in local + send:
            cp.start()

    def wait(self, ins, outs, sems):
        local, send, recv = self._copies(ins, outs, sems, arrivals=True)
        for s, r in zip(send, recv):
            r.wait_recv()
            s.wait_send()
        for cp in local:
            cp.wait()

    def attach(self, refs, step, n_steps):
        pl.when(step == 0)(lambda: self.start(*refs))
        pl.when(step == n_steps - 1)(lambda: self.wait(*refs))


class _GatherTwoLevel:
    def __init__(self, arrays):
        self.arrays = list(arrays)
        self.n = len(self.arrays)
        self.out_shape = [jax.ShapeDtypeStruct((N_DEV * a.shape[0], a.shape[1]), a.dtype) for a in self.arrays]
        self.scratch = [pltpu.SemaphoreType.DMA((7 * self.n,)), pltpu.SemaphoreType.DMA((7 * self.n,)),
                        pltpu.SemaphoreType.DMA((self.n,))]

    def _phase(self, refs, phase):
        ins, outs, (send_sems, recv_sems, local_sems) = refs
        x, y, c = (lax.axis_index(a) for a in MESH_AXES)
        me, sibling = (x, y, c), (x, y, 1 - c)
        chips = [(1 - x, y), (x, 1 - y), (1 - x, 1 - y)]

        def rows(a, px, py, pc):
            r = ins[a].shape[0]
            return outs[a].at[pl.ds((4 * px + 2 * py + pc) * r, r), :]

        def copy(a, k, block, to, src=None):
            return pltpu.make_async_remote_copy(
                src_ref=rows(a, *block) if src is None else src, dst_ref=rows(a, *block),
                send_sem=send_sems.at[7 * a + k], recv_sem=recv_sems.at[7 * a + k],
                device_id=to, device_id_type=MESH)

        for a in range(self.n):
            if phase == "start":
                pltpu.make_async_copy(ins[a], rows(a, *me), local_sems.at[a]).start()
                copy(a, 0, me, sibling, src=ins[a]).start()
                for j, chip in enumerate(chips):
                    copy(a, 1 + j, me, (*chip, c), src=ins[a]).start()
            elif phase == "forward":
                for j, chip in enumerate(chips):
                    copy(a, 1 + j, (*chip, c), me).wait_recv()
                    copy(a, 4 + j, (*chip, c), sibling).start()
            else:
                copy(a, 0, sibling, me).wait_recv()
                copy(a, 0, me, sibling, src=ins[a]).wait_send()
                for j, chip in enumerate(chips):
                    copy(a, 4 + j, (*chip, 1 - c), me).wait_recv()
                    copy(a, 1 + j, me, (*chip, c), src=ins[a]).wait_send()
                    copy(a, 4 + j, (*chip, c), sibling).wait_send()
                pltpu.make_async_copy(ins[a], rows(a, *me), local_sems.at[a]).wait()

    def attach(self, refs, step, n_steps):
        for when, phase in ((0, "start"), (7 * n_steps // 8, "forward"), (n_steps - 1, "wait")):
            pl.when(step == when)(lambda phase=phase: self._phase(refs, phase))


def _split_refs(refs, n_in, n_out, n_scratch, comm):
    nc = comm.n if comm else 0
    ins, rest = refs[:n_in], refs[n_in:]
    c_in, rest = rest[:nc], rest[nc:]
    outs, rest = rest[:n_out], rest[n_out:]
    c_out, rest = rest[:nc], rest[nc:]
    scratch, c_sems = rest[:n_scratch], rest[n_scratch:]
    return ins, outs, scratch, ((c_in, c_out, c_sems) if comm else None)


def _with_comm(comm, in_specs, out_specs, out_shape, scratch):
    if comm is None:
        return in_specs, out_specs, out_shape, scratch, []
    return (in_specs + _any_specs(comm.n), out_specs + _any_specs(comm.n), out_shape + comm.out_shape,
            scratch + comm.scratch, comm.arrays)


def _mm_nt(name, a_list, w_list, pairs, epilogue, out_dtypes, tt, tn, comm=None, tiles=()):
    t = a_list[0].shape[0]
    n = w_list[0].shape[0]
    na, nw, ntile = len(a_list), len(w_list), len(tiles)
    ni, nj = t // tt, n // tn

    def body(*refs):
        ins, o_refs, _, c_refs = _split_refs(refs, na + nw + ntile, len(out_dtypes), 0, comm)
        a_refs, w_refs, t_refs = ins[:na], ins[na:na + nw], ins[na + nw:]
        if comm:
            i = pl.program_id(0)
            comm.attach(c_refs, i, ni)
        a_vals = [a_ref[...] for a_ref in a_refs]
        for j in range(nj):
            cols = pl.ds(j * tn, tn)
            accs = [_dot_nt(a_vals[ai], w_refs[wi][cols, :]) for ai, wi in pairs]
            for o_ref, o in zip(o_refs, epilogue(accs, [t_ref[:, cols] for t_ref in t_refs])):
                o_ref[:, cols] = o.astype(o_ref.dtype)

    in_specs = ([pl.BlockSpec((tt, a.shape[1]), lambda i: (i, 0)) for a in a_list]
                + [pl.BlockSpec(w.shape, lambda i: (0, 0), pipeline_mode=pl.Buffered(1)) for w in w_list]
                + [pl.BlockSpec((tt, n), lambda i: (i, 0)) for _ in tiles])
    in_specs, out_specs, out_shape, scratch, extra = _with_comm(
        comm, in_specs, [pl.BlockSpec((tt, n), lambda i: (i, 0)) for _ in out_dtypes],
        [jax.ShapeDtypeStruct((t, n), dt) for dt in out_dtypes], [])
    return pl.pallas_call(
        body, name=name, grid=(ni,), in_specs=in_specs, out_specs=out_specs, out_shape=out_shape,
        scratch_shapes=scratch,
        compiler_params=_cparams("arbitrary" if comm else "parallel"),
    )(*a_list, *w_list, *tiles, *extra)


def _mm_nn(name, pairs, rows, fulls, epilogue, out_kinds, tt, tk, comm=None):
    t, k_total = pairs[0][0].shape
    n = pairs[0][1].shape[1]
    nk = k_total // tk
    nt = t // tt
    npair, nrow, nfull = len(pairs), len(rows), len(fulls)

    def body(*refs):
        ins, o_refs, scratch, c_refs = _split_refs(refs, 2 * npair + nrow + nfull, len(out_kinds), min(nk - 1, 1), comm)
        a_refs, w_refs = ins[:npair], ins[npair:2 * npair]
        r_refs, f_refs = ins[2 * npair:2 * npair + nrow], ins[2 * npair + nrow:]
        i, k = pl.program_id(0), pl.program_id(1)
        if comm:
            comm.attach(c_refs, i * nk + k, nt * nk)
        s = _dot_nn(a_refs[0][...], w_refs[0][...])
        for a_ref, w_ref in zip(a_refs[1:], w_refs[1:]):
            s = s + _dot_nn(a_ref[...], w_ref[...])

        def finish(acc):
            outs = epilogue(acc, [r[...] for r in r_refs], [f[...] for f in f_refs])
            for o_ref, o in zip(o_refs, outs):
                o_ref[...] = o.astype(o_ref.dtype)

        if nk == 1:
            finish(s)
        else:
            acc_ref = scratch[0]

            @pl.when(k == 0)
            def _():
                acc_ref[...] = s

            @pl.when(k > 0)
            def _():
                acc_ref[...] += s

            @pl.when(k == nk - 1)
            def _():
                finish(acc_ref[...])

    once = dict(pipeline_mode=pl.Buffered(1)) if nk == 1 else {}
    in_specs = ([pl.BlockSpec((tt, tk), lambda i, k: (i, k)) for _ in pairs]
                + [pl.BlockSpec((tk, n), (lambda i, k, off=off: (k + off, 0)), **once) for _, _, off in pairs]
                + [pl.BlockSpec((tt, n), lambda i, k: (i, 0)) for _ in rows]
                + [pl.BlockSpec((1, n), lambda i, k: (0, 0)) for _ in fulls])
    out_specs, out_shape = [], []
    for kind, dt in out_kinds:
        if kind == "tile":
            out_specs.append(pl.BlockSpec((tt, n), lambda i, k: (i, 0)))
            out_shape.append(jax.ShapeDtypeStruct((t, n), dt))
        else:
            out_specs.append(pl.BlockSpec((SUBLANES, n), lambda i, k: (i, 0)))
            out_shape.append(jax.ShapeDtypeStruct((nt * SUBLANES, n), dt))
    in_specs, out_specs, out_shape, scratch, extra = _with_comm(
        comm, in_specs, out_specs, out_shape, [] if nk == 1 else [pltpu.VMEM((tt, n), F32)])
    return pl.pallas_call(
        body, name=name, grid=(nt, nk), in_specs=in_specs, out_specs=out_specs, out_shape=out_shape,
        scratch_shapes=scratch,
        compiler_params=_cparams("arbitrary" if comm else "parallel", "arbitrary"),
    )(*[a for a, _, _ in pairs], *[w for _, w, _ in pairs], *rows, *fulls, *extra)


def _mm_tn(name, a, b, scale, tm, tt, comm=None):
    t, m = a.shape
    n = b.shape[1]
    nt = t // tt
    nm = m // tm

    def body(*refs):
        (a_ref, b_ref), (o_ref,), (acc_ref,), c_refs = _split_refs(refs, 2, 1, 1, comm)
        k = pl.program_id(1)
        if comm:
            i = pl.program_id(0)
            comm.attach(c_refs, i * nt + k, nm * nt)
        s = _dot_tn(a_ref[...], b_ref[...])

        @pl.when(k == 0)
        def _():
            acc_ref[...] = s

        @pl.when(k > 0)
        def _():
            acc_ref[...] += s

        @pl.when(k == nt - 1)
        def _():
            o_ref[...] = (acc_ref[...] * scale).astype(o_ref.dtype)

    in_specs, out_specs, out_shape, scratch, extra = _with_comm(
        comm, [pl.BlockSpec((tt, tm), lambda i, k: (k, i)), pl.BlockSpec((tt, n), lambda i, k: (k, 0))],
        [pl.BlockSpec((tm, n), lambda i, k: (i, 0))], [jax.ShapeDtypeStruct((m, n), BF16)],
        [pltpu.VMEM((tm, n), F32)])
    out = pl.pallas_call(
        body, name=name, grid=(nm, nt), in_specs=in_specs, out_specs=out_specs, out_shape=out_shape,
        scratch_shapes=scratch,
        compiler_params=_cparams("arbitrary" if comm else "parallel", "arbitrary"),
    )(a, b, *extra)
    return out if comm else out[0]


def _mm_tn_slabs(name, a_list, b, tt, comm=None):
    t, m = a_list[0].shape
    n = b.shape[1]
    na = len(a_list)
    nt = t // tt

    def body(*refs):
        ins, (o_ref,), (acc_ref,), c_refs = _split_refs(refs, na + 1, 1, 1, comm)
        k = pl.program_id(0)
        if comm:
            comm.attach(c_refs, k, nt)
        bv = ins[na][...]
        parts = [_dot_tn(a_ref[...], bv) for a_ref in ins[:na]]

        @pl.when(k == 0)
        def _():
            for j, part in enumerate(parts):
                acc_ref[pl.ds(j * m, m), :] = part

        @pl.when(k > 0)
        def _():
            for j, part in enumerate(parts):
                acc_ref[pl.ds(j * m, m), :] += part

        @pl.when(k == nt - 1)
        def _():
            o_ref[...] = acc_ref[...].astype(o_ref.dtype)

    in_specs, out_specs, out_shape, scratch, extra = _with_comm(
        comm, [pl.BlockSpec((tt, m), lambda k: (k, 0))] * na + [pl.BlockSpec((tt, n), lambda k: (k, 0))],
        [pl.BlockSpec((na * m, n), lambda k: (0, 0))], [jax.ShapeDtypeStruct((na * m, n), BF16)],
        [pltpu.VMEM((na * m, n), F32)])
    out = pl.pallas_call(
        body, name=name, grid=(nt,), in_specs=in_specs, out_specs=out_specs, out_shape=out_shape,
        scratch_shapes=scratch, compiler_params=_cparams("arbitrary"),
    )(*a_list, b, *extra)
    return out if comm else out[0]


def _group_sum(v, bd):
    hi, lo = _split2(v)
    return _dot_nn(hi, bd) + _dot_nn(lo, bd)


def _qknorm_fwd(proj, qg, kg, bd, tt):
    t = proj.shape[0]

    def body(q_ref, k_ref, v_ref, qg_ref, kg_ref, bd_ref, qn_ref, kn_ref, vb_ref):
        bdv = bd_ref[...]
        for x_ref, g_ref, o_ref in ((q_ref, qg_ref, qn_ref), (k_ref, kg_ref, kn_ref)):
            xv = x_ref[...]
            r = lax.rsqrt(_group_sum(xv * xv, bdv) * (1.0 / HEAD_DIM) + EPS)
            o_ref[...] = ((xv * r) * g_ref[...]).astype(BF16)
        vb_ref[...] = v_ref[...].astype(BF16)

    slab = lambda s: pl.BlockSpec((tt, SLAB), lambda i, s=s: (i, s))
    full = lambda shape: pl.BlockSpec(shape, lambda i: (0, 0))
    out = pl.BlockSpec((tt, SLAB), lambda i: (i, 0))
    return pl.pallas_call(
        body, name="qknorm_fwd", grid=(t // tt,),
        in_specs=[slab(3), slab(4), slab(5), full((1, SLAB)), full((1, SLAB)), full((SLAB, SLAB))],
        out_specs=[out, out, out],
        out_shape=[jax.ShapeDtypeStruct((t, SLAB), BF16)] * 3,
        compiler_params=_cparams("parallel"),
    )(proj, proj, proj, qg, kg, bd)


def _qknorm_bwd(proj, dqn, dkn, dv, qg, kg, bd, tt):
    t = proj.shape[0]

    def body(q_ref, k_ref, dqn_ref, dkn_ref, dv_ref, qg_ref, kg_ref, bd_ref, dq_ref, dk_ref, dvb_ref, part_ref):
        bdv = bd_ref[...]
        parts = []
        for x_ref, d_ref, g_ref, o_ref in ((q_ref, dqn_ref, qg_ref, dq_ref), (k_ref, dkn_ref, kg_ref, dk_ref)):
            xv, dn = x_ref[...], d_ref[...]
            r = lax.rsqrt(_group_sum(xv * xv, bdv) * (1.0 / HEAD_DIM) + EPS)
            u = dn * g_ref[...]
            dx = r * u - xv * (r * r * r) * (_group_sum(u * xv, bdv) * (1.0 / HEAD_DIM))
            o_ref[...] = dx.astype(BF16)
            parts.append(_fold8(dn * xv * r))
        dvb_ref[...] = dv_ref[...].astype(BF16)
        part_ref[...] = jnp.concatenate(parts, axis=1)

    slab = lambda s: pl.BlockSpec((tt, SLAB), lambda i, s=s: (i, s))
    tile = pl.BlockSpec((tt, SLAB), lambda i: (i, 0))
    full = lambda shape: pl.BlockSpec(shape, lambda i: (0, 0))
    return pl.pallas_call(
        body, name="qknorm_bwd", grid=(t // tt,),
        in_specs=[slab(3), slab(4), tile, tile, tile, full((1, SLAB)), full((1, SLAB)), full((SLAB, SLAB))],
        out_specs=[tile, tile, tile, pl.BlockSpec((SUBLANES, 2 * SLAB), lambda i: (i, 0))],
        out_shape=[jax.ShapeDtypeStruct((t, SLAB), BF16)] * 3
        + [jax.ShapeDtypeStruct((t // tt * SUBLANES, 2 * SLAB), F32)],
        compiler_params=_cparams("parallel"),
    )(proj, proj, dqn, dkn, dv, qg, kg, bd)


def _conv_taps(z, z_prev, row):
    zm1 = jnp.where(row == 0, z_prev[7:8], pltpu.roll(z, 1, 0))
    zm2 = jnp.where(row == 0, z_prev[6:7], jnp.where(row == 1, z_prev[7:8], pltpu.roll(z, 2, 0)))
    return zm1, zm2


def _conv_fwd(proj, cw, cb, tt):
    t = proj.shape[0]
    tb = tt // SUBLANES

    def body(b_ref, c_ref, u_ref, cp_ref, up_ref, cw_ref, cb_ref, o_ref):
        i = pl.program_id(0)
        z = c_ref[...] * u_ref[...]
        z_prev = jnp.where(i > 0, cp_ref[...] * up_ref[...], 0.0)
        row = lax.broadcasted_iota(jnp.int32, (tt, 1), 0)
        zm1, zm2 = _conv_taps(z, z_prev, row)
        y = cw_ref[0:1] * zm2 + cw_ref[1:2] * zm1 + cw_ref[2:3] * z + cb_ref[...]
        o_ref[...] = (b_ref[...] * y).astype(BF16)

    slab = lambda s: pl.BlockSpec((tt, SLAB), lambda i, s=s: (i, s))
    prev = lambda s: pl.BlockSpec((SUBLANES, SLAB), lambda i, s=s: (jnp.maximum(i * tb - 1, 0), s))
    return pl.pallas_call(
        body, name="conv_fwd", grid=(t // tt,),
        in_specs=[slab(0), slab(1), slab(2), prev(1), prev(2),
                  pl.BlockSpec((SUBLANES, SLAB), lambda i: (0, 0)), pl.BlockSpec((1, SLAB), lambda i: (0, 0))],
        out_specs=pl.BlockSpec((tt, SLAB), lambda i: (i, 0)),
        out_shape=jax.ShapeDtypeStruct((t, SLAB), BF16),
        compiler_params=_cparams("parallel"),
    )(proj, proj, proj, proj, proj, cw, cb)


def _conv_bwd(proj, dycat, cw, cb, tt):
    t = proj.shape[0]
    tb = tt // SUBLANES
    nblk = t // SUBLANES

    def body(b_ref, c_ref, u_ref, cp_ref, up_ref, bn_ref, dy_ref, dyn_ref, cw_ref, cb_ref,
             db_ref, dc_ref, du_ref, part_ref):
        i = pl.program_id(0)
        c, u, b, dyc = c_ref[...], u_ref[...], b_ref[...], dy_ref[...]
        z = c * u
        z_prev = jnp.where(i > 0, cp_ref[...] * up_ref[...], 0.0)
        row = lax.broadcasted_iota(jnp.int32, (tt, 1), 0)
        zm1, zm2 = _conv_taps(z, z_prev, row)
        w0, w1, w2 = cw_ref[0:1], cw_ref[1:2], cw_ref[2:3]
        y = w0 * zm2 + w1 * zm1 + w2 * z + cb_ref[...]
        db_ref[...] = (dyc * y).astype(BF16)
        g = dyc * b
        g_next = jnp.where(i < pl.num_programs(0) - 1, dyn_ref[...] * bn_ref[...], 0.0)
        gp1 = jnp.where(row == tt - 1, g_next[0:1], pltpu.roll(g, tt - 1, 0))
        gp2 = jnp.where(row == tt - 2, g_next[0:1], jnp.where(row == tt - 1, g_next[1:2], pltpu.roll(g, tt - 2, 0)))
        dz = w2 * g + w1 * gp1 + w0 * gp2
        dc_ref[...] = (dz * u).astype(BF16)
        du_ref[...] = (dz * c).astype(BF16)
        part_ref[...] = jnp.concatenate([_fold8(g * zm2), _fold8(g * zm1), _fold8(g * z), _fold8(g)], axis=1)

    slab = lambda s: pl.BlockSpec((tt, SLAB), lambda i, s=s: (i, s))
    prev = lambda s: pl.BlockSpec((SUBLANES, SLAB), lambda i, s=s: (jnp.maximum(i * tb - 1, 0), s))
    nxt = lambda s: pl.BlockSpec((SUBLANES, SLAB), lambda i, s=s: (jnp.minimum((i + 1) * tb, nblk - 1), s))
    tile = pl.BlockSpec((tt, SLAB), lambda i: (i, 0))
    return pl.pallas_call(
        body, name="conv_bwd", grid=(t // tt,),
        in_specs=[slab(0), slab(1), slab(2), prev(1), prev(2), nxt(0), slab(0), nxt(0),
                  pl.BlockSpec((SUBLANES, SLAB), lambda i: (0, 0)), pl.BlockSpec((1, SLAB), lambda i: (0, 0))],
        out_specs=[tile, tile, tile, pl.BlockSpec((SUBLANES, 4 * SLAB), lambda i: (i, 0))],
        out_shape=[jax.ShapeDtypeStruct((t, SLAB), BF16)] * 3
        + [jax.ShapeDtypeStruct((t // tt * SUBLANES, 4 * SLAB), F32)],
        compiler_params=_cparams("parallel"),
    )(proj, proj, proj, proj, proj, proj, dycat, dycat, cw, cb)


def _tri_masks(n):
    r = lax.broadcasted_iota(jnp.int32, (n, n), 0)
    c = lax.broadcasted_iota(jnp.int32, (n, n), 1)
    return (r > c).astype(BF16), (r >= c).astype(BF16)


def _sb_logits(z, causal):
    softplus = jnp.maximum(z, 0.0) + jnp.log(1.0 + jnp.exp(-jnp.abs(z)))
    lk = -softplus
    if causal is not None:
        lk = jnp.where(causal, lk, 0.0)
    return (z, lk, *_split2(lk))


def _sb_finish(z, lk, hi, lo, r_run, tri, causal, later):
    later = later + r_run
    ls = z + lk
    arg = ls + later
    if causal is not None:
        arg = jnp.where(causal, arg, -1e30)
    return lk, ls, jnp.exp(arg), later[:, 0:1] + lk[:, 0:1]


SB_DEAD_LOG = -111.0
CHAINS = ((0, 0), (0, 1), (1, 0), (1, 1))


def _all_dead(r_runs):
    m = r_runs[0]
    for r in r_runs[1:]:
        m = jnp.maximum(m, r)
    return (jnp.max(m) < SB_DEAD_LOG).astype(jnp.int32)


def _attn_fwd(qn, kn, vb, tri, sb, comm=None):
    t = qn.shape[0]
    bq = 2 * sb
    scale = HEAD_DIM ** -0.5

    def body(*refs):
        (q_ref, k_ref, v_ref, tri_ref), (o_ref, ob_ref), (acc_ref,), c_refs = _split_refs(refs, 4, 2, 1, comm)
        qi = pl.program_id(1)
        if comm:
            hp = pl.program_id(0)
            comm.attach(c_refs, hp * (t // bq) + qi, (SB_DIM // LANES) * (t // bq))
        lane = lax.broadcasted_iota(jnp.int32, (1, LANES), 1)
        hmasks = (lane < HEAD_DIM, lane >= HEAD_DIM)
        diag = lax.broadcasted_iota(jnp.int32, (sb, sb), 1) < lax.broadcasted_iota(jnp.int32, (sb, sb), 0)
        triv = tri_ref[...]
        qs = [jnp.where(hmasks[hh], q_ref[pl.ds(s * sb, sb), :], 0) * scale for s, hh in CHAINS]
        acc_ref[...] = jnp.zeros_like(acc_ref)

        def load_kv(kb):
            ks = kb * sb if isinstance(kb, int) else pl.multiple_of(kb * sb, sb)
            vraw = v_ref[pl.ds(ks, sb), :]
            return k_ref[pl.ds(ks, sb), :], [jnp.where(hm, vraw, 0) for hm in hmasks]

        def run_tiles(tiles, r_in):
            zs = [_dot_nt(qs[c], kv[0]) for c, kv, _, _ in tiles]
            mids = [_sb_logits(z, causal) for z, (_, _, causal, _) in zip(zs, tiles)]
            laters = [_dot_nn(m[2], triv) + _dot_nn(m[3], triv) for m in mids]
            outs = []
            for m, later, (c, _, causal, dep) in zip(mids, laters, tiles):
                outs.append(_sb_finish(*m, r_in[c] if dep is None else outs[dep][3], triv, causal, later))
            for o, (c, kv, _, _) in zip(outs, tiles):
                acc_ref[c] += _dot_nn(o[2].astype(BF16), kv[1][c % 2])
            return [o[3] for o in outs]

        zero = jnp.zeros((sb, 1), F32)
        kv_diag = [load_kv(2 * qi), load_kv(2 * qi + 1)]
        kv_prev = [load_kv(jnp.maximum(2 * qi - 1, 0)), kv_diag[0]]
        has_prev = lax.broadcasted_iota(jnp.int32, (sb, sb), 0) >= jnp.where(qi > 0, 0, sb)
        r_runs = run_tiles([(c, kv_diag[c // 2], diag, None) for c in range(4)]
                           + [(c, kv_prev[c // 2], has_prev if c < 2 else None, c) for c in range(4)], [zero] * 4)[4:]

        def step(carry):
            i, _, *rs = carry
            kvs = [load_kv(2 * qi - 2 - i), load_kv(2 * qi - 1 - i)]
            rs = run_tiles([(c, kvs[c // 2], None, None) for c in range(4)], rs)
            return (i + 1, _all_dead(rs), *rs)

        i_end, _, *rs = lax.while_loop(lambda c: jnp.logical_and(c[0] < 2 * qi - 1, c[1] == 0), step,
                                       (jnp.int32(0), _all_dead(r_runs), *r_runs))

        @pl.when(jnp.logical_and(i_end == 2 * qi - 1, _all_dead(rs[2:]) == 0))
        def _():
            kv_last = load_kv(0)
            run_tiles([(c, kv_last, None, None) for c in (2, 3)], rs)
        for s in range(2):
            out = acc_ref[2 * s] + acc_ref[2 * s + 1]
            o_ref[pl.ds(s * sb, sb), :] = out
            ob_ref[pl.ds(s * sb, sb), :] = out.astype(BF16)

    qspec = pl.BlockSpec((bq, LANES), lambda h, i: (i, h))
    kspec = pl.BlockSpec((t, LANES), lambda h, i: (0, h))
    in_specs, out_specs, out_shape, scratch, extra = _with_comm(
        comm, [qspec, kspec, kspec, pl.BlockSpec((sb, sb), lambda h, i: (0, 0))], [qspec, qspec],
        [jax.ShapeDtypeStruct((t, SB_DIM), F32), jax.ShapeDtypeStruct((t, SB_DIM), BF16)],
        [pltpu.VMEM((4, sb, LANES), F32)])
    return pl.pallas_call(
        body, name="attn_fwd", grid=(SB_DIM // LANES, t // bq),
        in_specs=in_specs, out_specs=out_specs, out_shape=out_shape, scratch_shapes=scratch,
        compiler_params=_cparams("arbitrary" if comm else "parallel", "arbitrary"),
    )(qn, kn, vb, tri, *extra)


def _attn_bwd(qn, kn, vb, o, dycat, tri, tri_inc, sb, comm=None):
    t = qn.shape[0]
    bq = 2 * sb
    scale = HEAD_DIM ** -0.5

    def body(*refs):
        ins, (dq_ref, dk_ref, dv_ref), (dq_acc,), c_refs = _split_refs(refs, 7, 3, 1, comm)
        q_ref, k_ref, v_ref, o_ref, do_ref, tri_ref, tinc_ref = ins
        qi = pl.program_id(1)
        if comm:
            hp = pl.program_id(0)
            comm.attach(c_refs, hp * (t // bq) + qi, (SB_DIM // LANES) * (t // bq))

        @pl.when(qi == 0)
        def _():
            dk_ref[...] = jnp.zeros_like(dk_ref)
            dv_ref[...] = jnp.zeros_like(dv_ref)

        lane = lax.broadcasted_iota(jnp.int32, (1, LANES), 1)
        hmasks = (lane < HEAD_DIM, lane >= HEAD_DIM)
        diag = lax.broadcasted_iota(jnp.int32, (sb, sb), 1) < lax.broadcasted_iota(jnp.int32, (sb, sb), 0)
        triv, tincv = tri_ref[...], tinc_ref[...]
        qs, dobs, d_rows = [], [], []
        for s, hh in CHAINS:
            rows = pl.ds(s * sb, sb)
            qs.append(jnp.where(hmasks[hh], q_ref[rows, :], 0) * scale)
            dobs.append(jnp.where(hmasks[hh], do_ref[rows, :], 0.0).astype(BF16))
            d_rows.append(jnp.sum(dobs[-1].astype(F32) * o_ref[rows, :], axis=1, keepdims=True))
        dq_acc[...] = jnp.zeros_like(dq_acc)

        def load_kv(kb):
            ks = kb * sb if isinstance(kb, int) else pl.multiple_of(kb * sb, sb)
            return k_ref[pl.ds(ks, sb), :], v_ref[pl.ds(ks, sb), :], ks

        def run_tiles(tiles, r_in, g_in):
            zs = [_dot_nt(qs[c], kv[0]) for c, kv, _, _ in tiles]
            das = [_dot_nt(dobs[c], kv[1]) for c, kv, _, _ in tiles]
            mids = [_sb_logits(z, causal) for z, (_, _, causal, _) in zip(zs, tiles)]
            laters = [_dot_nn(m[2], triv) + _dot_nn(m[3], triv) for m in mids]
            fins, abs_, es = [], [], []
            for m, later, da, (c, _, causal, dep) in zip(mids, laters, das, tiles):
                fins.append(_sb_finish(*m, r_in[c] if dep is None else fins[dep][3], triv, causal, later))
                abs_.append(fins[-1][2].astype(BF16))
                es.append(da * abs_[-1].astype(F32))
            splits = [_split2(e) for e in es]
            e_sums = [_dot_nn(hi, tincv) + _dot_nn(lo, tincv) for hi, lo in splits]
            e_froms, dzbs = [], []
            for e, e_sum, fin, (c, _, causal, dep) in zip(es, e_sums, fins, tiles):
                e_froms.append(e_sum + (g_in[c] if dep is None else e_froms[dep][:, 0:1]))
                dz = e - jnp.exp(fin[1]) * (e + (d_rows[c] - e_froms[-1]))
                if causal is not None:
                    dz = jnp.where(causal, dz, 0.0)
                dzbs.append(dz.astype(BF16))
            for dzb, (c, kv, _, _) in zip(dzbs, tiles):
                dq_acc[c] += _dot_nn(dzb, kv[0])
            by_rows = {}
            for dzb, ab, (c, kv, _, _) in zip(dzbs, abs_, tiles):
                by_rows.setdefault(id(kv), (kv[2], []))[1].append((_dot_tn(dzb, qs[c]), _dot_tn(ab, dobs[c])))
            for first_row, parts in by_rows.values():
                rows = pl.ds(first_row, sb)
                dk_ref[rows, :] += sum(p[0] for p in parts[1:]) + parts[0][0]
                dv_ref[rows, :] += sum(p[1] for p in parts[1:]) + parts[0][1]
            return [f[3] for f in fins], [ef[:, 0:1] for ef in e_froms]

        zero = jnp.zeros((sb, 1), F32)
        kv_diag = [load_kv(2 * qi), load_kv(2 * qi + 1)]
        kv_prev = [load_kv(jnp.maximum(2 * qi - 1, 0)), kv_diag[0]]
        has_prev = lax.broadcasted_iota(jnp.int32, (sb, sb), 0) >= jnp.where(qi > 0, 0, sb)
        r_first, g_first = run_tiles(
            [(c, kv_diag[c // 2], diag, None) for c in range(4)]
            + [(c, kv_prev[c // 2], has_prev if c < 2 else None, c) for c in range(4)], [zero] * 4, [zero] * 4)
        r_runs, g_runs = r_first[4:], g_first[4:]

        def step(carry):
            i, _, *rg = carry
            kvs = [load_kv(2 * qi - 2 - i), load_kv(2 * qi - 1 - i)]
            rs, gs = run_tiles([(c, kvs[c // 2], None, None) for c in range(4)], rg[:4], rg[4:])
            return (i + 1, _all_dead(rs), *rs, *gs)

        i_end, _, *rg = lax.while_loop(lambda c: jnp.logical_and(c[0] < 2 * qi - 1, c[1] == 0), step,
                                       (jnp.int32(0), _all_dead(r_runs), *r_runs, *g_runs))

        @pl.when(jnp.logical_and(i_end == 2 * qi - 1, _all_dead(rg[2:4]) == 0))
        def _():
            kv_last = load_kv(0)
            run_tiles([(c, kv_last, None, None) for c in (2, 3)], rg[:4], rg[4:])
        for s in range(2):
            dq_ref[pl.ds(s * sb, sb), :] = jnp.where(hmasks[0], dq_acc[2 * s], dq_acc[2 * s + 1]) * scale

    qspec = pl.BlockSpec((bq, LANES), lambda h, i: (i, h))
    dospec = pl.BlockSpec((bq, LANES), lambda h, i: (i, h + CONV_DIM // LANES))
    kspec = pl.BlockSpec((t, LANES), lambda h, i: (0, h))
    full = pl.BlockSpec((sb, sb), lambda h, i: (0, 0))
    in_specs, out_specs, out_shape, scratch, extra = _with_comm(
        comm, [qspec, kspec, kspec, qspec, dospec, full, full], [qspec, kspec, kspec],
        [jax.ShapeDtypeStruct((t, SB_DIM), F32)] * 3, [pltpu.VMEM((4, sb, LANES), F32)])
    return pl.pallas_call(
        body, name="attn_bwd", grid=(SB_DIM // LANES, t // bq),
        in_specs=in_specs, out_specs=out_specs, out_shape=out_shape, scratch_shapes=scratch,
        compiler_params=_cparams("arbitrary" if comm else "parallel", "arbitrary"),
    )(qn, kn, vb, o, dycat, tri, tri_inc, *extra)


def _ple_loss(x3, p2, tgt, gain, wpg, wppt, tt):
    t, d = x3.shape
    pdim = p2.shape[1]
    nt = t // tt

    def body(x_ref, p_ref, t_ref, g_ref, wg_ref, wp_ref,
             dx_ref, dxb_ref, dwg_ref, dwp_ref, gpart_ref, lpart_ref, accg_ref, accp_ref):
        i = pl.program_id(0)
        xv, gain_v = x_ref[...], g_ref[...]
        hb = ((xv * _rms_stats(xv)) * gain_v).astype(BF16)
        gate = jax.nn.sigmoid(_dot_nn(hb, wg_ref[...]))
        pb = p_ref[...].astype(BF16)
        pe = _dot_nt(pb, wp_ref[...])
        diff = xv + gate * pe - t_ref[...]
        lsum = jnp.sum(_fold8(diff * diff), axis=1, keepdims=True) * (0.5 / d)
        lpart_ref[...] = jnp.broadcast_to(lsum, (SUBLANES, LANES))
        dy = diff * (1.0 / d)
        dgz = ((dy * pe) * gate * (1.0 - gate)).astype(BF16)
        dpe = (dy * gate).astype(BF16)
        dx_n, grow = _rms_bwd(_dot_nt(dgz, wg_ref[...]), xv, gain_v)
        dx = dy + dx_n
        dx_ref[...] = dx
        dxb_ref[...] = dx.astype(BF16)
        gpart_ref[...] = _fold8(grow)
        sg = _dot_tn(hb, dgz)
        sp = _dot_tn(dpe, pb)

        @pl.when(i == 0)
        def _():
            accg_ref[...] = sg
            accp_ref[...] = sp

        @pl.when(i > 0)
        def _():
            accg_ref[...] += sg
            accp_ref[...] += sp

        @pl.when(i == nt - 1)
        def _():
            dwg_ref[...] = accg_ref[...].astype(BF16)
            dwp_ref[...] = accp_ref[...].astype(BF16)

    tile = lambda w: pl.BlockSpec((tt, w), lambda i: (i, 0))
    full = lambda shape: pl.BlockSpec(shape, lambda i: (0, 0))
    return pl.pallas_call(
        body, name="ple_loss", grid=(nt,),
        in_specs=[tile(d), tile(pdim), tile(d), full((1, d)),
                  pl.BlockSpec((d, d), lambda i: (0, 0), pipeline_mode=pl.Buffered(1)),
                  pl.BlockSpec((d, pdim), lambda i: (0, 0), pipeline_mode=pl.Buffered(1))],
        out_specs=[tile(d), tile(d), full((d, d)), full((d, pdim)),
                   pl.BlockSpec((SUBLANES, d), lambda i: (i, 0)), pl.BlockSpec((SUBLANES, LANES), lambda i: (i, 0))],
        out_shape=[jax.ShapeDtypeStruct((t, d), F32), jax.ShapeDtypeStruct((t, d), BF16),
                   jax.ShapeDtypeStruct((d, d), BF16), jax.ShapeDtypeStruct((d, pdim), BF16),
                   jax.ShapeDtypeStruct((nt * SUBLANES, d), F32), jax.ShapeDtypeStruct((nt * SUBLANES, LANES), F32)],
        scratch_shapes=[pltpu.VMEM((d, d), F32), pltpu.VMEM((d, pdim), F32)],
        compiler_params=_cparams("arbitrary"),
    )(x3, p2, tgt, gain, wpg, wppt)


def _pack_small(parts_gain, conv_part, qk_part):
    d = parts_gain[0].shape[1]
    ng = len(parts_gain)

    def body(*refs):
        g_refs, conv_ref, qk_ref, o_ref = refs[:ng], refs[ng], refs[ng + 1], refs[ng + 2]
        rows = [jnp.sum(r[...], axis=0, keepdims=True) for r in g_refs]
        cs = jnp.sum(conv_ref[...], axis=0, keepdims=True)
        qs = jnp.sum(qk_ref[...], axis=0, keepdims=True)
        rows.append(jnp.concatenate([cs[:, 3 * SLAB:], cs[:, :SLAB]], axis=1))
        rows.append(cs[:, SLAB:3 * SLAB])
        rows.append(qs)
        rid = lax.broadcasted_iota(jnp.int32, (2 * SUBLANES, 1), 0)
        out = jnp.zeros((2 * SUBLANES, d), F32)
        for idx, r in enumerate(rows):
            out = jnp.where(rid == idx, r, out)
        o_ref[...] = out

    return pl.pallas_call(
        body, name="pack_small", out_shape=jax.ShapeDtypeStruct((2 * SUBLANES, d), F32),
    )(*parts_gain, conv_part, qk_part)


def _sum_slots(name, slots, out_dtype=F32):
    _, r, c = slots.shape

    def body(s_ref, o_ref):
        acc = s_ref[0].astype(F32)
        for d in range(1, N_DEV):
            acc = acc + s_ref[d].astype(F32)
        o_ref[...] = acc.astype(o_ref.dtype)

    return pl.pallas_call(body, name=name, out_shape=jax.ShapeDtypeStruct((r, c), out_dtype),
                          compiler_params=pltpu.CompilerParams(vmem_limit_bytes=VMEM_LIMIT_BYTES))(slots)


def _adamw(name, w, g, m, v):
    c1 = 1.0 - ADAM_B1 ** ADAM_STEP
    c2 = 1.0 - ADAM_B2 ** ADAM_STEP

    def body(w_ref, g_ref, m_ref, v_ref, go_ref, d_ref, nm_ref, nv_ref):
        if g.ndim == 3:
            gv = g_ref[0].astype(F32)
            for dev in range(1, N_DEV):
                gv = gv + g_ref[dev].astype(F32)
        else:
            gv = g_ref[...]
        go_ref[...] = gv
        nm = ADAM_B1 * m_ref[...] + (1.0 - ADAM_B1) * gv
        nv = ADAM_B2 * v_ref[...] + (1.0 - ADAM_B2) * (gv * gv)
        d_ref[...] = -ADAM_LR * ((nm / c1) / (jnp.sqrt(nv / c2) + ADAM_EPS) + ADAM_WD * w_ref[...])
        nm_ref[...] = nm
        nv_ref[...] = nv

    return pl.pallas_call(body, name=name, out_shape=[jax.ShapeDtypeStruct(w.shape, F32)] * 4,
                          compiler_params=pltpu.CompilerParams(vmem_limit_bytes=VMEM_LIMIT_BYTES))(w, g, m, v)


def _any_specs(n):
    return [pl.BlockSpec(memory_space=pl.ANY)] * n


def _all_gather(name, shards):
    n = len(shards)

    def body(*refs):
        ins, outs = refs[:n], refs[n:2 * n]
        send_sems, recv_sems, local_sems = refs[2 * n:]
        x, y, c = (lax.axis_index(a) for a in MESH_AXES)
        me, sibling = (x, y, c), (x, y, 1 - c)
        chips = [(1 - x, y), (x, 1 - y), (1 - x, 1 - y)]

        def rows(a, px, py, pc):
            r = ins[a].shape[0]
            return outs[a].at[pl.ds((4 * px + 2 * py + pc) * r, r), :]

        def copy(a, k, block, to, src=None):
            return pltpu.make_async_remote_copy(
                src_ref=rows(a, *block) if src is None else src, dst_ref=rows(a, *block),
                send_sem=send_sems.at[7 * a + k], recv_sem=recv_sems.at[7 * a + k],
                device_id=to, device_id_type=MESH)

        mine = [pltpu.make_async_copy(ins[a], rows(a, *me), local_sems.at[a]) for a in range(n)]
        for cp in mine:
            cp.start()
        first = []
        for a in range(n):
            first.append(copy(a, 0, me, sibling, src=ins[a]))
            first += [copy(a, 1 + j, me, (*chip, c), src=ins[a]) for j, chip in enumerate(chips)]
        for cp in first:
            cp.start()
        passed = []
        for j, chip in enumerate(chips):
            for a in range(n):
                copy(a, 1 + j, (*chip, c), me).wait_recv()
                fwd = copy(a, 4 + j, (*chip, c), sibling)
                fwd.start()
                passed.append(fwd)
        for a in range(n):
            copy(a, 0, sibling, me).wait_recv()
            for j, chip in enumerate(chips):
                copy(a, 4 + j, (*chip, 1 - c), me).wait_recv()
        for cp in first + passed:
            cp.wait_send()
        for cp in mine:
            cp.wait()

    return pl.pallas_call(
        body, name=name, in_specs=_any_specs(n), out_specs=_any_specs(n),
        out_shape=[jax.ShapeDtypeStruct((N_DEV * s.shape[0], s.shape[1]), s.dtype) for s in shards],
        scratch_shapes=[pltpu.SemaphoreType.DMA((7 * n,)), pltpu.SemaphoreType.DMA((7 * n,)),
                        pltpu.SemaphoreType.DMA((n,))],
    )(*shards)


def _residual_and_norm(res_scale):
    def epilogue(acc, rows, fulls):
        out = rows[0] + res_scale * acc
        return [out] + [(out * _rms_stats(out)) * gain for gain in fulls]
    return epilogue


def _ffn_fwd(tag, x, h, wgt, wut, wd, next_gain, tt_nt, tt_nn, comm_gate=None, comm_down=None):
    f = wgt.shape[0]

    def gate_up(accs, _):
        return [accs[0], accs[1], jax.nn.silu(accs[0]) * accs[1]]

    g, u, a, *got_gate = _mm_nt(f"{tag}_gate_up", [h], [wgt, wut], [(0, 0), (0, 1)], gate_up, [BF16] * 3,
                                tt_nn, _pick(f, 256), comm_gate)
    if wd is None:
        wd = got_gate[0]
    gains = [] if next_gain is None else [next_gain]
    out, *rest = _mm_nn(f"{tag}_down", [(a, wd, 0)], [x], gains, _residual_and_norm(FFN_RES),
                        [("tile", F32)] + [("tile", BF16)] * len(gains), tt_nt, f, comm_down)
    h_next = rest.pop(0) if gains else None
    return out, h_next, (g, u, a), got_gate, rest


def _norm_bwd_epilogue(acc, rows, fulls):
    x_in, dy = rows
    dx_n, grow = _rms_bwd(acc, x_in, fulls[0])
    dx = dy + dx_n
    return [dx, dx, _fold8(grow)]


_NORM_BWD_OUTS = [("tile", F32), ("tile", BF16), ("part", F32)]


def _ffn_bwd(tag, x_in, h, hidden, dy, dyb, gain, wgt, wut, wd, tt_nt, tt_nn, riders=(None, None),
             exchange_own=False):
    g, u, a = hidden
    f = wgt.shape[0]
    tt_tn = _pick(h.shape[0], 2 * tt_nt)
    own = (lambda arr: _Exchange([arr])) if exchange_own else (lambda arr: None)

    def carried(result, rider):
        return (result[0], result[1:]) if rider else (result, [])

    def hidden_grads(accs, tiles):
        da = FFN_RES * accs[0]
        gv, uv = tiles[0].astype(F32), tiles[1].astype(F32)
        sg = jax.nn.sigmoid(gv)
        s = gv * sg
        return [da * uv * (sg * (1.0 + gv * (1.0 - sg))), da * s]

    dwd, got_dwd = carried(_mm_tn(f"{tag}_dwd", a, dyb, FFN_RES, f // 2, tt_tn, riders[0]), riders[0])
    dg, du, *x_dwd = _mm_nt(f"{tag}_bwd_hidden", [dyb], [wd], [(0, 0)], hidden_grads, [BF16, BF16],
                            tt_nn, _pick(f, 256), own(dwd), tiles=[g, u])
    dwg, got_dwg = carried(_mm_tn(f"{tag}_dwg", dg, h, 1.0, f // 2, tt_tn, riders[1]), riders[1])
    dwu, x_dwg = carried(_mm_tn(f"{tag}_dwu", du, h, 1.0, f // 2, tt_tn, own(dwg)), exchange_own)
    dx, dxb, gpart, *x_dwu = _mm_nn(f"{tag}_bwd_dx", [(dg, wgt, 0), (du, wut, 0)], [x_in, dy], [gain],
                                    _norm_bwd_epilogue, _NORM_BWD_OUTS, tt_nn, f, own(dwu))
    grads = [*x_dwg, *x_dwu, *x_dwd] if exchange_own else [dwg, dwu, dwd]
    return dx, dxb, gpart, grads, [*got_dwd, *got_dwg]


def kernel(x, p, ffn1_norm, ffn1_w_gate, ffn1_w_up, ffn1_w_down, mix_norm, w_in, conv_w, conv_b, q_norm, k_norm, w_out, ffn2_norm, ffn2_w_gate, ffn2_w_up, ffn2_w_down, ple_norm, ple_w_gate, ple_w_proj, loss_target, m_ffn1_norm, m_ffn1_w_gate, m_ffn1_w_up, m_ffn1_w_down, m_mix_norm, m_w_in, m_conv_w, m_conv_b, m_q_norm, m_k_norm, m_w_out, m_ffn2_norm, m_ffn2_w_gate, m_ffn2_w_up, m_ffn2_w_down, m_ple_norm, m_ple_w_gate, m_ple_w_proj, v_ffn1_norm, v_ffn1_w_gate, v_ffn1_w_up, v_ffn1_w_down, v_mix_norm, v_w_in, v_conv_w, v_conv_b, v_q_norm, v_k_norm, v_w_out, v_ffn2_norm, v_ffn2_w_gate, v_ffn2_w_up, v_ffn2_w_down, v_ple_norm, v_ple_w_gate, v_ple_w_proj):
    x0, p2, tgt = x[0], p[0, 0], loss_target[0]
    t, d = x0.shape
    tt_nt = _pick(t, 1024)
    tt_nn = _pick(t, 512)
    tt_ew = _pick(t, 512)
    tt_ple = _pick(t, 512)
    sb = _pick(t // 2, 256)

    t_bf = lambda w: w[0].T.astype(BF16)
    n_bf = lambda w: w[0].astype(BF16)
    cw_tile = jnp.zeros((SUBLANES, LANES), F32).at[:conv_w.shape[1], :conv_w.shape[2]].set(conv_w[0])
    gather_first = _GatherTwoLevel([t_bf(ffn1_w_gate), t_bf(ffn1_w_up)])
    gather_down = _GatherTwoLevel([n_bf(ffn1_w_down), n_bf(w_out), cw_tile])
    gather_in = _GatherTwoLevel([t_bf(w_in)])
    gather_late = _GatherTwoLevel([t_bf(ffn2_w_gate), t_bf(ffn2_w_up), n_bf(ffn2_w_down), n_bf(ple_w_gate),
                                   t_bf(ple_w_proj)])
    ncs = conv_w.shape[2]

    qg = jnp.tile(q_norm, (1, SB_DIM // HEAD_DIM))
    kg = jnp.tile(k_norm, (1, SB_DIM // HEAD_DIM))
    gi = lax.broadcasted_iota(jnp.int32, (SLAB, SLAB), 0) // HEAD_DIM
    gj = lax.broadcasted_iota(jnp.int32, (SLAB, SLAB), 1) // HEAD_DIM
    bd = (gi == gj).astype(BF16)
    tri, tri_inc = _tri_masks(sb)

    h1, wg1t, wu1t = _rmsnorm("ffn1_norm", x0, ffn1_norm, tt_ew, gather_first)
    x1, h2, hidden1, (wd1, wout, cw_all), (wint,) = _ffn_fwd(
        "ffn1", x0, h1, wg1t, wu1t, None, mix_norm, tt_nt, tt_nn, gather_down, gather_in)
    cw_full = cw_all.reshape(N_DEV, SUBLANES, LANES)[:, :, :ncs].transpose(1, 0, 2).reshape(SUBLANES, N_DEV * ncs)
    (proj,) = _mm_nt("in_proj", [h2], [wint], [(0, 0)], lambda accs, _: accs, [F32], tt_nn, SLAB)
    y_conv = _conv_fwd(proj, cw_full, conv_b, tt_ew)
    qn, kn, vb = _qknorm_fwd(proj, qg, kg, bd, tt_ew)
    o, ob, wg2t, wu2t, wd2, wpg, wppt = _attn_fwd(qn, kn, vb, tri, sb, gather_late)
    x2, h3 = _mm_nn("out_proj", [(y_conv, wout, 0), (ob, wout, 1)], [x1], [ffn2_norm], _residual_and_norm(1.0),
                    [("tile", F32), ("tile", BF16)], tt_nn, SLAB)
    x3, _, hidden2, _, _ = _ffn_fwd("ffn2", x2, h3, wg2t, wu2t, wd2, None, tt_nt, tt_nn)

    dx3, dx3b, dwpg, dwppt, gp_ple, lpart = _ple_loss(x3, p2, tgt, ple_norm, wpg, wppt, tt_ple)
    loss = lax.psum(jnp.sum(lpart[:, 0]), MESH_AXES)
    dx2, dx2b, gp_ffn2, (dwg2, dwu2, dwd2), _ = _ffn_bwd("ffn2", x2, h3, hidden2, dx3, dx3b, ffn2_norm,
                                                         wg2t, wu2t, wd2, tt_nt, tt_nn)
    (dycat,) = _mm_nt("out_proj_bwd", [dx2b], [wout], [(0, 0)], lambda accs, _: accs, [F32], tt_nn, SLAB)
    dwout = _mm_tn_slabs("dwout", [y_conv, ob], dx2b, tt_nt)
    dqn, dkn, dv, *slots_late = _attn_bwd(qn, kn, vb, o, dycat, tri, tri_inc, sb,
                                          _Exchange([dwg2, dwu2, dwd2, dwpg, dwppt]))
    dq, dk, dvb, qk_part = _qknorm_bwd(proj, dqn, dkn, dv, qg, kg, bd, tt_ew)
    db, dc, du, conv_part = _conv_bwd(proj, dycat, cw_full, conv_b, tt_ew)
    dproj = [db, dc, du, dq, dk, dvb]
    dwin, slot_wout = _mm_tn_slabs("dwin", dproj, h2, tt_nt, _Exchange([dwout]))
    dx1, dx1b, gp_mix, slot_win = _mm_nn(
        "in_proj_bwd", [(dp, wint, s) for s, dp in enumerate(dproj)], [x1, dx2], [mix_norm],
        _norm_bwd_epilogue, _NORM_BWD_OUTS, tt_nn, SLAB, _Exchange([dwin]))
    dx0, _, gp_ffn1, slots_ffn1, _ = _ffn_bwd(
        "ffn1", x0, h1, hidden1, dx1, dx1b, ffn1_norm, wg1t, wu1t, wd1, tt_nt, tt_nn, exchange_own=True)

    slots = [*slots_ffn1, slot_win, slot_wout, *slots_late]
    per_dev = [s.reshape(N_DEV, s.shape[0] // N_DEV, s.shape[1]) for s in slots]
    c_wg1, c_wu1, c_wd1, c_win, c_wout, c_wg2, c_wu2, c_wd2, c_wpg, c_wpp = per_dev
    g_win, g_wpp = _sum_slots("sum_grads_w_in", c_win), _sum_slots("sum_grads_ple_w_proj", c_wpp)
    small = _pack_small([gp_ffn1, gp_mix, gp_ffn2, gp_ple], conv_part, qk_part)
    (small_all,) = _all_gather("gather_small_grads", [small])
    sm = _sum_slots("sum_small_grads", small_all.reshape(N_DEV, 2 * SUBLANES, d))
    fold = lambda r: r.reshape(SB_DIM // HEAD_DIM, HEAD_DIM).sum(axis=0)[None]
    me_idx = 4 * lax.axis_index("x") + 2 * lax.axis_index("y") + lax.axis_index("c")
    cw_grad = jnp.stack([sm[4, SLAB:], sm[5, :SLAB], sm[5, SLAB:]])
    grads = {
        "ffn1_norm": sm[0:1], "ffn1_w_down": c_wd1,
        "mix_norm": sm[1:2], "w_in": g_win.T, "conv_w": lax.dynamic_slice_in_dim(cw_grad, me_idx * ncs, ncs, axis=1),
        "conv_b": sm[4:5, :SLAB], "q_norm": fold(sm[6, :SLAB]), "k_norm": fold(sm[6, SLAB:]),
        "w_out": c_wout, "ffn2_norm": sm[2:3], "ffn2_w_down": c_wd2,
        "ple_norm": sm[3:4], "ple_w_gate": c_wpg, "ple_w_proj": g_wpp.T,
    }
    grads_t = {"ffn1_w_gate": c_wg1, "ffn1_w_up": c_wu1, "ffn2_w_gate": c_wg2, "ffn2_w_up": c_wu2}

    weights = dict(ffn1_norm=ffn1_norm, ffn1_w_gate=ffn1_w_gate, ffn1_w_up=ffn1_w_up, ffn1_w_down=ffn1_w_down,
                   mix_norm=mix_norm, w_in=w_in, conv_w=conv_w, conv_b=conv_b, q_norm=q_norm, k_norm=k_norm,
                   w_out=w_out, ffn2_norm=ffn2_norm, ffn2_w_gate=ffn2_w_gate, ffn2_w_up=ffn2_w_up,
                   ffn2_w_down=ffn2_w_down, ple_norm=ple_norm, ple_w_gate=ple_w_gate, ple_w_proj=ple_w_proj)
    m_in = dict(ffn1_norm=m_ffn1_norm, ffn1_w_gate=m_ffn1_w_gate, ffn1_w_up=m_ffn1_w_up, ffn1_w_down=m_ffn1_w_down,
                mix_norm=m_mix_norm, w_in=m_w_in, conv_w=m_conv_w, conv_b=m_conv_b, q_norm=m_q_norm,
                k_norm=m_k_norm, w_out=m_w_out, ffn2_norm=m_ffn2_norm, ffn2_w_gate=m_ffn2_w_gate,
                ffn2_w_up=m_ffn2_w_up, ffn2_w_down=m_ffn2_w_down, ple_norm=m_ple_norm, ple_w_gate=m_ple_w_gate,
                ple_w_proj=m_ple_w_proj)
    v_in = dict(ffn1_norm=v_ffn1_norm, ffn1_w_gate=v_ffn1_w_gate, ffn1_w_up=v_ffn1_w_up, ffn1_w_down=v_ffn1_w_down,
                mix_norm=v_mix_norm, w_in=v_w_in, conv_w=v_conv_w, conv_b=v_conv_b, q_norm=v_q_norm,
                k_norm=v_k_norm, w_out=v_w_out, ffn2_norm=v_ffn2_norm, ffn2_w_gate=v_ffn2_w_gate,
                ffn2_w_up=v_ffn2_w_up, ffn2_w_down=v_ffn2_w_down, ple_norm=v_ple_norm, ple_w_gate=v_ple_w_gate,
                ple_w_proj=v_ple_w_proj)
    g_out, d_out, m_out, v_out = [], [], [], []
    for name, w in weights.items():
        if name in grads_t:
            view, back = (lambda a: a[0].T), (lambda a: a.T[None])
            g_in = grads_t[name]
        else:
            view, back = (lambda a, w=w: a.reshape(w.shape[-2:])), (lambda a, w=w: a.reshape(w.shape))
            g_in = grads[name] if grads[name].ndim == 3 else view(grads[name])
        g2, dlt, nm, nv = _adamw(f"adamw_{name}", view(w), g_in, view(m_in[name]), view(v_in[name]))
        g_out.append(back(g2))
        d_out.append(back(dlt))
        m_out.append(back(nm))
        v_out.append(back(nv))
    return (loss, dx0[None], *g_out, *d_out, *m_out, *v_out)
```

```python
import jax
import jax.numpy as jnp
from jax import lax
from jax.experimental import pallas as pl
from jax.experimental.pallas import tpu as pltpu

F32 = jnp.float32
BF16 = jnp.bfloat16

EPS = 1e-6
FFN_RES = 0.5
HEAD_DIM = 64
CONV_DIM = 512
SB_DIM = 512
SLAB = 512
N_DEV = 8
MESH_AXES = ("x", "y", "c")
MESH = pl.DeviceIdType.MESH

ADAM_LR = 0.001
ADAM_B1 = 0.9
ADAM_B2 = 0.999
ADAM_EPS = 1e-08
ADAM_WD = 0.01
ADAM_STEP = 10

VMEM_LIMIT_BYTES = 56 * 1024 * 1024
SUBLANES = 8
LANES = 128


def _cparams(*semantics):
    return pltpu.CompilerParams(dimension_semantics=semantics, vmem_limit_bytes=VMEM_LIMIT_BYTES)


def _dot_nn(a, b):
    return jnp.dot(a, b, preferred_element_type=F32)


def _dot_nt(a, b):
    return lax.dot_general(a, b, (((1,), (1,)), ((), ())), preferred_element_type=F32)


def _dot_tn(a, b):
    return lax.dot_general(a, b, (((0,), (0,)), ((), ())), preferred_element_type=F32)


def _fold8(v):
    rows, cols = v.shape
    return jnp.sum(v.reshape(rows // SUBLANES, SUBLANES, cols), axis=0)


def _split2(v):
    hi = v.astype(BF16)
    lo = (v - hi.astype(F32)).astype(BF16)
    return hi, lo


def _rms_stats(x):
    return lax.rsqrt(jnp.mean(x * x, axis=-1, keepdims=True) + EPS)


def _rms_bwd(dh, x, gain):
    r = _rms_stats(x)
    u = dh * gain
    dx = r * u - x * (r * r * r) * jnp.mean(u * x, axis=-1, keepdims=True)
    return dx, dh * x * r


def _pick(n, pref):
    return pref if n % pref == 0 else n


def _rmsnorm(name, x, gain, tt, comm=None):
    t, d = x.shape
    nt = t // tt

    def body(*refs):
        (x_ref, g_ref), (o_ref,), _, c_refs = _split_refs(refs, 2, 1, 0, comm)
        if comm:
            comm.attach(c_refs, pl.program_id(0), nt)
        xv = x_ref[...]
        o_ref[...] = ((xv * _rms_stats(xv)) * g_ref[...]).astype(BF16)

    in_specs, out_specs, out_shape, scratch, extra = _with_comm(
        comm, [pl.BlockSpec((tt, d), lambda i: (i, 0)), pl.BlockSpec((1, d), lambda i: (0, 0))],
        [pl.BlockSpec((tt, d), lambda i: (i, 0))], [jax.ShapeDtypeStruct((t, d), BF16)], [])
    return pl.pallas_call(
        body, name=name, grid=(nt,), in_specs=in_specs, out_specs=out_specs, out_shape=out_shape,
        scratch_shapes=scratch, compiler_params=_cparams("arbitrary" if comm else "parallel"),
    )(x, gain, *extra)


class _Exchange:
    FLIPS = [(fx, fy, fc) for fx in (0, 1) for fy in (0, 1) for fc in (0, 1)][1:]

    def __init__(self, arrays):
        self.arrays = list(arrays)
        self.n = len(self.arrays)
        self.rows = [a.shape[0] // N_DEV for a in self.arrays]
        self.out_shape = [jax.ShapeDtypeStruct(a.shape, a.dtype) for a in self.arrays]
        self.scratch = [pltpu.SemaphoreType.DMA((7 * self.n,)), pltpu.SemaphoreType.DMA((7 * self.n,)),
                        pltpu.SemaphoreType.DMA((self.n,))]

    def _copies(self, ins, outs, sems, arrivals):
        send_sems, recv_sems, local_sems = sems
        x, y, c = (lax.axis_index(a) for a in MESH_AXES)
        me_idx = 4 * x + 2 * y + c
        local, send, recv = [], [], []
        for a in range(self.n):
            r = self.rows[a]

            def blk(ref, idx, r=r):
                return ref.at[pl.ds(idx * r, r), :]

            def src(idx, a=a, blk=blk):
                return blk(ins[a], idx)

            local.append(pltpu.make_async_copy(src(me_idx), blk(outs[a], me_idx), local_sems.at[a]))
            for k, flip in enumerate(self.FLIPS):
                px, py, pc = (1 - v if f else v for v, f in zip((x, y, c), flip))
                p_idx = 4 * px + 2 * py + pc
                for dst_idx, group in ((me_idx, send), (p_idx, recv))[:2 if arrivals else 1]:
                    group.append(pltpu.make_async_remote_copy(
                        src_ref=src(p_idx), dst_ref=blk(outs[a], dst_idx),
                        send_sem=send_sems.at[7 * a + k], recv_sem=recv_sems.at[7 * a + k],
                        device_id=(px, py, pc), device_id_type=MESH))
        return local, send, recv

    def start(self, ins, outs, sems):
        local, send, _ = self._copies(ins, outs, sems, arrivals=False)
        for cp in local + send:
            cp.start()

    def wait(self, ins, outs, sems):
        local, send, recv = self._copies(ins, outs, sems, arrivals=True)
        for s, r in zip(send, recv):
            r.wait_recv()
            s.wait_send()
        for cp in local:
            cp.wait()

    def attach(self, refs, step, n_steps):
        pl.when(step == 0)(lambda: self.start(*refs))
        pl.when(step == n_steps - 1)(lambda: self.wait(*refs))


class _GatherTwoLevel:
    def __init__(self, arrays):
        self.arrays = list(arrays)
        self.n = len(self.arrays)
        self.out_shape = [jax.ShapeDtypeStruct((N_DEV * a.shape[0], a.shape[1]), a.dtype) for a in self.arrays]
        self.scratch = [pltpu.SemaphoreType.DMA((7 * self.n,)), pltpu.SemaphoreType.DMA((7 * self.n,)),
                        pltpu.SemaphoreType.DMA((self.n,))]

    def _phase(self, refs, phase):
        ins, outs, (send_sems, recv_sems, local_sems) = refs
        x, y, c = (lax.axis_index(a) for a in MESH_AXES)
        me, sibling = (x, y, c), (x, y, 1 - c)
        chips = [(1 - x, y), (x, 1 - y), (1 - x, 1 - y)]

        def rows(a, px, py, pc):
            r = ins[a].shape[0]
            return outs[a].at[pl.ds((4 * px + 2 * py + pc) * r, r), :]

        def copy(a, k, block, to, src=None):
            return pltpu.make_async_remote_copy(
                src_ref=rows(a, *block) if src is None else src, dst_ref=rows(a, *block),
                send_sem=send_sems.at[7 * a + k], recv_sem=recv_sems.at[7 * a + k],
                device_id=to, device_id_type=MESH)

        for a in range(self.n):
            if phase == "start":
                pltpu.make_async_copy(ins[a], rows(a, *me), local_sems.at[a]).start()
                copy(a, 0, me, sibling, src=ins[a]).start()
                for j, chip in enumerate(chips):
                    copy(a, 1 + j, me, (*chip, c), src=ins[a]).start()
            elif phase == "forward":
                for j, chip in enumerate(chips):
                    copy(a, 1 + j, (*chip, c), me).wait_recv()
                    copy(a, 4 + j, (*chip, c), sibling).start()
            else:
                copy(a, 0, sibling, me).wait_recv()
                copy(a, 0, me, sibling, src=ins[a]).wait_send()
                for j, chip in enumerate(chips):
                    copy(a, 4 + j, (*chip, 1 - c), me).wait_recv()
                    copy(a, 1 + j, me, (*chip, c), src=ins[a]).wait_send()
                    copy(a, 4 + j, (*chip, c), sibling).wait_send()
                pltpu.make_async_copy(ins[a], rows(a, *me), local_sems.at[a]).wait()

    def attach(self, refs, step, n_steps):
        for when, phase in ((0, "start"), (7 * n_steps // 8, "forward"), (n_steps - 1, "wait")):
            pl.when(step == when)(lambda phase=phase: self._phase(refs, phase))


def _split_refs(refs, n_in, n_out, n_scratch, comm):
    nc = comm.n if comm else 0
    ins, rest = refs[:n_in], refs[n_in:]
    c_in, rest = rest[:nc], rest[nc:]
    outs, rest = rest[:n_out], rest[n_out:]
    c_out, rest = rest[:nc], rest[nc:]
    scratch, c_sems = rest[:n_scratch], rest[n_scratch:]
    return ins, outs, scratch, ((c_in, c_out, c_sems) if comm else None)


def _with_comm(comm, in_specs, out_specs, out_shape, scratch):
    if comm is None:
        return in_specs, out_specs, out_shape, scratch, []
    return (in_specs + _any_specs(comm.n), out_specs + _any_specs(comm.n), out_shape + comm.out_shape,
            scratch + comm.scratch, comm.arrays)


def _mm_nt(name, a_list, w_list, pairs, epilogue, out_dtypes, tt, tn, comm=None, tiles=()):
    t = a_list[0].shape[0]
    n = w_list[0].shape[0]
    na, nw, ntile = len(a_list), len(w_list), len(tiles)
    ni, nj = t // tt, n // tn

    def body(*refs):
        ins, o_refs, _, c_refs = _split_refs(refs, na + nw + ntile, len(out_dtypes), 0, comm)
        a_refs, w_refs, t_refs = ins[:na], ins[na:na + nw], ins[na + nw:]
        if comm:
            i = pl.program_id(0)
            comm.attach(c_refs, i, ni)
        a_vals = [a_ref[...] for a_ref in a_refs]
        for j in range(nj):
            cols = pl.ds(j * tn, tn)
            accs = [_dot_nt(a_vals[ai], w_refs[wi][cols, :]) for ai, wi in pairs]
            for o_ref, o in zip(o_refs, epilogue(accs, [t_ref[:, cols] for t_ref in t_refs])):
                o_ref[:, cols] = o.astype(o_ref.dtype)

    in_specs = ([pl.BlockSpec((tt, a.shape[1]), lambda i: (i, 0)) for a in a_list]
                + [pl.BlockSpec(w.shape, lambda i: (0, 0), pipeline_mode=pl.Buffered(1)) for w in w_list]
                + [pl.BlockSpec((tt, n), lambda i: (i, 0)) for _ in tiles])
    in_specs, out_specs, out_shape, scratch, extra = _with_comm(
        comm, in_specs, [pl.BlockSpec((tt, n), lambda i: (i, 0)) for _ in out_dtypes],
        [jax.ShapeDtypeStruct((t, n), dt) for dt in out_dtypes], [])
    return pl.pallas_call(
        body, name=name, grid=(ni,), in_specs=in_specs, out_specs=out_specs, out_shape=out_shape,
        scratch_shapes=scratch,
        compiler_params=_cparams("arbitrary" if comm else "parallel"),
    )(*a_list, *w_list, *tiles, *extra)


def _mm_nn(name, pairs, rows, fulls, epilogue, out_kinds, tt, tk, comm=None):
    t, k_total = pairs[0][0].shape
    n = pairs[0][1].shape[1]
    nk = k_total // tk
    nt = t // tt
    npair, nrow, nfull = len(pairs), len(rows), len(fulls)

    def body(*refs):
        ins, o_refs, scratch, c_refs = _split_refs(refs, 2 * npair + nrow + nfull, len(out_kinds), min(nk - 1, 1), comm)
        a_refs, w_refs = ins[:npair], ins[npair:2 * npair]
        r_refs, f_refs = ins[2 * npair:2 * npair + nrow], ins[2 * npair + nrow:]
        i, k = pl.program_id(0), pl.program_id(1)
        if comm:
            comm.attach(c_refs, i * nk + k, nt * nk)
        s = _dot_nn(a_refs[0][...], w_refs[0][...])
        for a_ref, w_ref in zip(a_refs[1:], w_refs[1:]):
            s = s + _dot_nn(a_ref[...], w_ref[...])

        def finish(acc):
            outs = epilogue(acc, [r[...] for r in r_refs], [f[...] for f in f_refs])
            for o_ref, o in zip(o_refs, outs):
                o_ref[...] = o.astype(o_ref.dtype)

        if nk == 1:
            finish(s)
        else:
            acc_ref = scratch[0]

            @pl.when(k == 0)
            def _():
                acc_ref[...] = s

            @pl.when(k > 0)
            def _():
                acc_ref[...] += s

            @pl.when(k == nk - 1)
            def _():
                finish(acc_ref[...])

    once = dict(pipeline_mode=pl.Buffered(1)) if nk == 1 else {}
    in_specs = ([pl.BlockSpec((tt, tk), lambda i, k: (i, k)) for _ in pairs]
                + [pl.BlockSpec((tk, n), (lambda i, k, off=off: (k + off, 0)), **once) for _, _, off in pairs]
                + [pl.BlockSpec((tt, n), lambda i, k: (i, 0)) for _ in rows]
                + [pl.BlockSpec((1, n), lambda i, k: (0, 0)) for _ in fulls])
    out_specs, out_shape = [], []
    for kind, dt in out_kinds:
        if kind == "tile":
            out_specs.append(pl.BlockSpec((tt, n), lambda i, k: (i, 0)))
            out_shape.append(jax.ShapeDtypeStruct((t, n), dt))
        else:
            out_specs.append(pl.BlockSpec((SUBLANES, n), lambda i, k: (i, 0)))
            out_shape.append(jax.ShapeDtypeStruct((nt * SUBLANES, n), dt))
    in_specs, out_specs, out_shape, scratch, extra = _with_comm(
        comm, in_specs, out_specs, out_shape, [] if nk == 1 else [pltpu.VMEM((tt, n), F32)])
    return pl.pallas_call(
        body, name=name, grid=(nt, nk), in_specs=in_specs, out_specs=out_specs, out_shape=out_shape,
        scratch_shapes=scratch,
        compiler_params=_cparams("arbitrary" if comm else "parallel", "arbitrary"),
    )(*[a for a, _, _ in pairs], *[w for _, w, _ in pairs], *rows, *fulls, *extra)


def _mm_tn(name, a, b, scale, tm, tt, comm=None):
    t, m = a.shape
    n = b.shape[1]
    nt = t // tt
    nm = m // tm

    def body(*refs):
        (a_ref, b_ref), (o_ref,), (acc_ref,), c_refs = _split_refs(refs, 2, 1, 1, comm)
        k = pl.program_id(1)
        if comm:
            i = pl.program_id(0)
            comm.attach(c_refs, i * nt + k, nm * nt)
        s = _dot_tn(a_ref[...], b_ref[...])

        @pl.when(k == 0)
        def _():
            acc_ref[...] = s

        @pl.when(k > 0)
        def _():
            acc_ref[...] += s

        @pl.when(k == nt - 1)
        def _():
            o_ref[...] = (acc_ref[...] * scale).astype(o_ref.dtype)

    in_specs, out_specs, out_shape, scratch, extra = _with_comm(
        comm, [pl.BlockSpec((tt, tm), lambda i, k: (k, i)), pl.BlockSpec((tt, n), lambda i, k: (k, 0))],
        [pl.BlockSpec((tm, n), lambda i, k: (i, 0))], [jax.ShapeDtypeStruct((m, n), BF16)],
        [pltpu.VMEM((tm, n), F32)])
    out = pl.pallas_call(
        body, name=name, grid=(nm, nt), in_specs=in_specs, out_specs=out_specs, out_shape=out_shape,
        scratch_shapes=scratch,
        compiler_params=_cparams("arbitrary" if comm else "parallel", "arbitrary"),
    )(a, b, *extra)
    return out if comm else out[0]


def _mm_tn_slabs(name, a_list, b, tt, comm=None):
    t, m = a_list[0].shape
    n = b.shape[1]
    na = len(a_list)
    nt = t // tt

    def body(*refs):
        ins, (o_ref,), (acc_ref,), c_refs = _split_refs(refs, na + 1, 1, 1, comm)
        k = pl.program_id(0)
        if comm:
            comm.attach(c_refs, k, nt)
        bv = ins[na][...]
        parts = [_dot_tn(a_ref[...], bv) for a_ref in ins[:na]]

        @pl.when(k == 0)
        def _():
            for j, part in enumerate(parts):
                acc_ref[pl.ds(j * m, m), :] = part

        @pl.when(k > 0)
        def _():
            for j, part in enumerate(parts):
                acc_ref[pl.ds(j * m, m), :] += part

        @pl.when(k == nt - 1)
        def _():
            o_ref[...] = acc_ref[...].astype(o_ref.dtype)

    in_specs, out_specs, out_shape, scratch, extra = _with_comm(
        comm, [pl.BlockSpec((tt, m), lambda k: (k, 0))] * na + [pl.BlockSpec((tt, n), lambda k: (k, 0))],
        [pl.BlockSpec((na * m, n), lambda k: (0, 0))], [jax.ShapeDtypeStruct((na * m, n), BF16)],
        [pltpu.VMEM((na * m, n), F32)])
    out = pl.pallas_call(
        body, name=name, grid=(nt,), in_specs=in_specs, out_specs=out_specs, out_shape=out_shape,
        scratch_shapes=scratch, compiler_params=_cparams("arbitrary"),
    )(*a_list, b, *extra)
    return out if comm else out[0]


def _group_sum(v, bd):
    hi, lo = _split2(v)
    return _dot_nn(hi, bd) + _dot_nn(lo, bd)


def _qknorm_fwd(proj, qg, kg, bd, tt):
    t = proj.shape[0]

    def body(q_ref, k_ref, v_ref, qg_ref, kg_ref, bd_ref, qn_ref, kn_ref, vb_ref):
        bdv = bd_ref[...]
        for x_ref, g_ref, o_ref in ((q_ref, qg_ref, qn_ref), (k_ref, kg_ref, kn_ref)):
            xv = x_ref[...]
            r = lax.rsqrt(_group_sum(xv * xv, bdv) * (1.0 / HEAD_DIM) + EPS)
            o_ref[...] = ((xv * r) * g_ref[...]).astype(BF16)
        vb_ref[...] = v_ref[...].astype(BF16)

    slab = lambda s: pl.BlockSpec((tt, SLAB), lambda i, s=s: (i, s))
    full = lambda shape: pl.BlockSpec(shape, lambda i: (0, 0))
    out = pl.BlockSpec((tt, SLAB), lambda i: (i, 0))
    return pl.pallas_call(
        body, name="qknorm_fwd", grid=(t // tt,),
        in_specs=[slab(3), slab(4), slab(5), full((1, SLAB)), full((1, SLAB)), full((SLAB, SLAB))],
        out_specs=[out, out, out],
        out_shape=[jax.ShapeDtypeStruct((t, SLAB), BF16)] * 3,
        compiler_params=_cparams("parallel"),
    )(proj, proj, proj, qg, kg, bd)


def _qknorm_bwd(proj, dqn, dkn, dv, qg, kg, bd, tt):
    t = proj.shape[0]

    def body(q_ref, k_ref, dqn_ref, dkn_ref, dv_ref, qg_ref, kg_ref, bd_ref, dq_ref, dk_ref, dvb_ref, part_ref):
        bdv = bd_ref[...]
        parts = []
        for x_ref, d_ref, g_ref, o_ref in ((q_ref, dqn_ref, qg_ref, dq_ref), (k_ref, dkn_ref, kg_ref, dk_ref)):
            xv, dn = x_ref[...], d_ref[...]
            r = lax.rsqrt(_group_sum(xv * xv, bdv) * (1.0 / HEAD_DIM) + EPS)
            u = dn * g_ref[...]
            dx = r * u - xv * (r * r * r) * (_group_sum(u * xv, bdv) * (1.0 / HEAD_DIM))
            o_ref[...] = dx.astype(BF16)
            parts.append(_fold8(dn * xv * r))
        dvb_ref[...] = dv_ref[...].astype(BF16)
        part_ref[...] = jnp.concatenate(parts, axis=1)

    slab = lambda s: pl.BlockSpec((tt, SLAB), lambda i, s=s: (i, s))
    tile = pl.BlockSpec((tt, SLAB), lambda i: (i, 0))
    full = lambda shape: pl.BlockSpec(shape, lambda i: (0, 0))
    return pl.pallas_call(
        body, name="qknorm_bwd", grid=(t // tt,),
        in_specs=[slab(3), slab(4), tile, tile, tile, full((1, SLAB)), full((1, SLAB)), full((SLAB, SLAB))],
        out_specs=[tile, tile, tile, pl.BlockSpec((SUBLANES, 2 * SLAB), lambda i: (i, 0))],
        out_shape=[jax.ShapeDtypeStruct((t, SLAB), BF16)] * 3
        + [jax.ShapeDtypeStruct((t // tt * SUBLANES, 2 * SLAB), F32)],
        compiler_params=_cparams("parallel"),
    )(proj, proj, dqn, dkn, dv, qg, kg, bd)


def _conv_taps(z, z_prev, row):
    zm1 = jnp.where(row == 0, z_prev[7:8], pltpu.roll(z, 1, 0))
    zm2 = jnp.where(row == 0, z_prev[6:7], jnp.where(row == 1, z_prev[7:8], pltpu.roll(z, 2, 0)))
    return zm1, zm2


def _conv_fwd(proj, cw, cb, tt):
    t = proj.shape[0]
    tb = tt // SUBLANES

    def body(b_ref, c_ref, u_ref, cp_ref, up_ref, cw_ref, cb_ref, o_ref):
        i = pl.program_id(0)
        z = c_ref[...] * u_ref[...]
        z_prev = jnp.where(i > 0, cp_ref[...] * up_ref[...], 0.0)
        row = lax.broadcasted_iota(jnp.int32, (tt, 1), 0)
        zm1, zm2 = _conv_taps(z, z_prev, row)
        y = cw_ref[0:1] * zm2 + cw_ref[1:2] * zm1 + cw_ref[2:3] * z + cb_ref[...]
        o_ref[...] = (b_ref[...] * y).astype(BF16)

    slab = lambda s: pl.BlockSpec((tt, SLAB), lambda i, s=s: (i, s))
    prev = lambda s: pl.BlockSpec((SUBLANES, SLAB), lambda i, s=s: (jnp.maximum(i * tb - 1, 0), s))
    return pl.pallas_call(
        body, name="conv_fwd", grid=(t // tt,),
        in_specs=[slab(0), slab(1), slab(2), prev(1), prev(2),
                  pl.BlockSpec((SUBLANES, SLAB), lambda i: (0, 0)), pl.BlockSpec((1, SLAB), lambda i: (0, 0))],
        out_specs=pl.BlockSpec((tt, SLAB), lambda i: (i, 0)),
        out_shape=jax.ShapeDtypeStruct((t, SLAB), BF16),
        compiler_params=_cparams("parallel"),
    )(proj, proj, proj, proj, proj, cw, cb)


def _conv_bwd(proj, dycat, cw, cb, tt):
    t = proj.shape[0]
    tb = tt // SUBLANES
    nblk = t // SUBLANES

    def body(b_ref, c_ref, u_ref, cp_ref, up_ref, bn_ref, dy_ref, dyn_ref, cw_ref, cb_ref,
             db_ref, dc_ref, du_ref, part_ref):
        i = pl.program_id(0)
        c, u, b, dyc = c_ref[...], u_ref[...], b_ref[...], dy_ref[...]
        z = c * u
        z_prev = jnp.where(i > 0, cp_ref[...] * up_ref[...], 0.0)
        row = lax.broadcasted_iota(jnp.int32, (tt, 1), 0)
        zm1, zm2 = _conv_taps(z, z_prev, row)
        w0, w1, w2 = cw_ref[0:1], cw_ref[1:2], cw_ref[2:3]
        y = w0 * zm2 + w1 * zm1 + w2 * z + cb_ref[...]
        db_ref[...] = (dyc * y).astype(BF16)
        g = dyc * b
        g_next = jnp.where(i < pl.num_programs(0) - 1, dyn_ref[...] * bn_ref[...], 0.0)
        gp1 = jnp.where(row == tt - 1, g_next[0:1], pltpu.roll(g, tt - 1, 0))
        gp2 = jnp.where(row == tt - 2, g_next[0:1], jnp.where(row == tt - 1, g_next[1:2], pltpu.roll(g, tt - 2, 0)))
        dz = w2 * g + w1 * gp1 + w0 * gp2
        dc_ref[...] = (dz * u).astype(BF16)
        du_ref[...] = (dz * c).astype(BF16)
        part_ref[...] = jnp.concatenate([_fold8(g * zm2), _fold8(g * zm1), _fold8(g * z), _fold8(g)], axis=1)

    slab = lambda s: pl.BlockSpec((tt, SLAB), lambda i, s=s: (i, s))
    prev = lambda s: pl.BlockSpec((SUBLANES, SLAB), lambda i, s=s: (jnp.maximum(i * tb - 1, 0), s))
    nxt = lambda s: pl.BlockSpec((SUBLANES, SLAB), lambda i, s=s: (jnp.minimum((i + 1) * tb, nblk - 1), s))
    tile = pl.BlockSpec((tt, SLAB), lambda i: (i, 0))
    return pl.pallas_call(
        body, name="conv_bwd", grid=(t // tt,),
        in_specs=[slab(0), slab(1), slab(2), prev(1), prev(2), nxt(0), slab(0), nxt(0),
                  pl.BlockSpec((SUBLANES, SLAB), lambda i: (0, 0)), pl.BlockSpec((1, SLAB), lambda i: (0, 0))],
        out_specs=[tile, tile, tile, pl.BlockSpec((SUBLANES, 4 * SLAB), lambda i: (i, 0))],
        out_shape=[jax.ShapeDtypeStruct((t, SLAB), BF16)] * 3
        + [jax.ShapeDtypeStruct((t // tt * SUBLANES, 4 * SLAB), F32)],
        compiler_params=_cparams("parallel"),
    )(proj, proj, proj, proj, proj, proj, dycat, dycat, cw, cb)


def _tri_masks(n):
    r = lax.broadcasted_iota(jnp.int32, (n, n), 0)
    c = lax.broadcasted_iota(jnp.int32, (n, n), 1)
    return (r > c).astype(BF16), (r >= c).astype(BF16)


def _sb_logits(z, causal):
    softplus = jnp.maximum(z, 0.0) + jnp.log(1.0 + jnp.exp(-jnp.abs(z)))
    lk = -softplus
    if causal is not None:
        lk = jnp.where(causal, lk, 0.0)
    return (z, lk, *_split2(lk))


def _sb_finish(z, lk, hi, lo, r_run, tri, causal, later):
    later = later + r_run
    ls = z + lk
    arg = ls + later
    if causal is not None:
        arg = jnp.where(causal, arg, -1e30)
    return lk, ls, jnp.exp(arg), later[:, 0:1] + lk[:, 0:1]


SB_DEAD_LOG = -111.0
CHAINS = ((0, 0), (0, 1), (1, 0), (1, 1))


def _all_dead(r_runs):
    m = r_runs[0]
    for r in r_runs[1:]:
        m = jnp.maximum(m, r)
    return (jnp.max(m) < SB_DEAD_LOG).astype(jnp.int32)


def _attn_fwd(qn, kn, vb, tri, sb, comm=None):
    t = qn.shape[0]
    bq = 2 * sb
    scale = HEAD_DIM ** -0.5

    def body(*refs):
        (q_ref, k_ref, v_ref, tri_ref), (o_ref, ob_ref), (acc_ref,), c_refs = _split_refs(refs, 4, 2, 1, comm)
        qi = pl.program_id(1)
        if comm:
            hp = pl.program_id(0)
            comm.attach(c_refs, hp * (t // bq) + qi, (SB_DIM // LANES) * (t // bq))
        lane = lax.broadcasted_iota(jnp.int32, (1, LANES), 1)
        hmasks = (lane < HEAD_DIM, lane >= HEAD_DIM)
        diag = lax.broadcasted_iota(jnp.int32, (sb, sb), 1) < lax.broadcasted_iota(jnp.int32, (sb, sb), 0)
        triv = tri_ref[...]
        tri2 = jnp.concatenate([triv, triv], axis=0)
        qs = [jnp.where(hmasks[hh], q_ref[pl.ds(s * sb, sb), :], 0) * scale for s, hh in CHAINS]
        acc_ref[...] = jnp.zeros_like(acc_ref)

        def load_kv(kb):
            ks = kb * sb if isinstance(kb, int) else pl.multiple_of(kb * sb, sb)
            vraw = v_ref[pl.ds(ks, sb), :]
            return k_ref[pl.ds(ks, sb), :], [jnp.where(hm, vraw, 0) for hm in hmasks]

        def run_tiles(tiles, r_in):
            zs = [_dot_nt(qs[c], kv[0]) for c, kv, _, _ in tiles]
            mids = [_sb_logits(z, causal) for z, (_, _, causal, _) in zip(zs, tiles)]
            laters = [_dot_nn(jnp.concatenate([m[2], m[3]], axis=1), tri2) for m in mids]
            outs = []
            for m, later, (c, _, causal, dep) in zip(mids, laters, tiles):
                outs.append(_sb_finish(*m, r_in[c] if dep is None else outs[dep][3], triv, causal, later))
            for o, (c, kv, _, _) in zip(outs, tiles):
                acc_ref[c] += _dot_nn(o[2].astype(BF16), kv[1][c % 2])
            return [o[3] for o in outs]

        zero = jnp.zeros((sb, 1), F32)
        kv_diag = [load_kv(2 * qi), load_kv(2 * qi + 1)]
        kv_prev = [load_kv(jnp.maximum(2 * qi - 1, 0)), kv_diag[0]]
        has_prev = lax.broadcasted_iota(jnp.int32, (sb, sb), 0) >= jnp.where(qi > 0, 0, sb)
        r_runs = run_tiles([(c, kv_diag[c // 2], diag, None) for c in range(4)]
                           + [(c, kv_prev[c // 2], has_prev if c < 2 else None, c) for c in range(4)], [zero] * 4)[4:]

        def step(carry):
            i, _, *rs = carry
            kvs = [load_kv(2 * qi - 2 - i), load_kv(2 * qi - 1 - i)]
            rs = run_tiles([(c, kvs[c // 2], None, None) for c in range(4)], rs)
            return (i + 1, _all_dead(rs), *rs)

        i_end, _, *rs = lax.while_loop(lambda c: jnp.logical_and(c[0] < 2 * qi - 1, c[1] == 0), step,
                                       (jnp.int32(0), _all_dead(r_runs), *r_runs))

        @pl.when(jnp.logical_and(i_end == 2 * qi - 1, _all_dead(rs[2:]) == 0))
        def _():
            kv_last = load_kv(0)
            run_tiles([(c, kv_last, None, None) for c in (2, 3)], rs)
        for s in range(2):
            out = acc_ref[2 * s] + acc_ref[2 * s + 1]
            o_ref[pl.ds(s * sb, sb), :] = out
            ob_ref[pl.ds(s * sb, sb), :] = out.astype(BF16)

    qspec = pl.BlockSpec((bq, LANES), lambda h, i: (i, h))
    kspec = pl.BlockSpec((t, LANES), lambda h, i: (0, h))
    in_specs, out_specs, out_shape, scratch, extra = _with_comm(
        comm, [qspec, kspec, kspec, pl.BlockSpec((sb, sb), lambda h, i: (0, 0))], [qspec, qspec],
        [jax.ShapeDtypeStruct((t, SB_DIM), F32), jax.ShapeDtypeStruct((t, SB_DIM), BF16)],
        [pltpu.VMEM((4, sb, LANES), F32)])
    return pl.pallas_call(
        body, name="attn_fwd", grid=(SB_DIM // LANES, t // bq),
        in_specs=in_specs, out_specs=out_specs, out_shape=out_shape, scratch_shapes=scratch,
        compiler_params=_cparams("arbitrary" if comm else "parallel", "arbitrary"),
    )(qn, kn, vb, tri, *extra)


def _attn_bwd(qn, kn, vb, o, dycat, tri, tri_inc, sb, comm=None):
    t = qn.shape[0]
    bq = 2 * sb
    scale = HEAD_DIM ** -0.5

    def body(*refs):
        ins, (dq_ref, dk_ref, dv_ref), (dq_acc,), c_refs = _split_refs(refs, 7, 3, 1, comm)
        q_ref, k_ref, v_ref, o_ref, do_ref, tri_ref, tinc_ref = ins
        qi = pl.program_id(1)
        if comm:
            hp = pl.program_id(0)
            comm.attach(c_refs, hp * (t // bq) + qi, (SB_DIM // LANES) * (t // bq))

        @pl.when(qi == 0)
        def _():
            dk_ref[...] = jnp.zeros_like(dk_ref)
            dv_ref[...] = jnp.zeros_like(dv_ref)

        lane = lax.broadcasted_iota(jnp.int32, (1, LANES), 1)
        hmasks = (lane < HEAD_DIM, lane >= HEAD_DIM)
        diag = lax.broadcasted_iota(jnp.int32, (sb, sb), 1) < lax.broadcasted_iota(jnp.int32, (sb, sb), 0)
        triv, tincv = tri_ref[...], tinc_ref[...]
        tri2 = jnp.concatenate([triv, triv], axis=0)
        tinc2 = jnp.concatenate([tincv, tincv], axis=0)
        qs, dobs, d_rows = [], [], []
        for s, hh in CHAINS:
            rows = pl.ds(s * sb, sb)
            qs.append(jnp.where(hmasks[hh], q_ref[rows, :], 0) * scale)
            dobs.append(jnp.where(hmasks[hh], do_ref[rows, :], 0.0).astype(BF16))
            d_rows.append(jnp.sum(dobs[-1].astype(F32) * o_ref[rows, :], axis=1, keepdims=True))
        dq_acc[...] = jnp.zeros_like(dq_acc)

        def load_kv(kb):
            ks = kb * sb if isinstance(kb, int) else pl.multiple_of(kb * sb, sb)
            return k_ref[pl.ds(ks, sb), :], v_ref[pl.ds(ks, sb), :], ks

        def run_tiles(tiles, r_in, g_in):
            zs = [_dot_nt(qs[c], kv[0]) for c, kv, _, _ in tiles]
            das = [_dot_nt(dobs[c], kv[1]) for c, kv, _, _ in tiles]
            mids = [_sb_logits(z, causal) for z, (_, _, causal, _) in zip(zs, tiles)]
            laters = [_dot_nn(jnp.concatenate([m[2], m[3]], axis=1), tri2) for m in mids]
            fins, abs_, es = [], [], []
            for m, later, da, (c, _, causal, dep) in zip(mids, laters, das, tiles):
                fins.append(_sb_finish(*m, r_in[c] if dep is None else fins[dep][3], triv, causal, later))
                abs_.append(fins[-1][2].astype(BF16))
                es.append(da * abs_[-1].astype(F32))
            splits = [_split2(e) for e in es]
            e_sums = [_dot_nn(jnp.concatenate([hi, lo], axis=1), tinc2) for hi, lo in splits]
            e_froms, dzbs = [], []
            for e, e_sum, fin, (c, _, causal, dep) in zip(es, e_sums, fins, tiles):
                e_froms.append(e_sum + (g_in[c] if dep is None else e_froms[dep][:, 0:1]))
                dz = e - jnp.exp(fin[1]) * (e + (d_rows[c] - e_froms[-1]))
                if causal is not None:
                    dz = jnp.where(causal, dz, 0.0)
                dzbs.append(dz.astype(BF16))
            for dzb, (c, kv, _, _) in zip(dzbs, tiles):
                dq_acc[c] += _dot_nn(dzb, kv[0])
            by_rows = {}
            for dzb, ab, (c, kv, _, _) in zip(dzbs, abs_, tiles):
                by_rows.setdefault(id(kv), (kv[2], []))[1].append((_dot_tn(dzb, qs[c]), _dot_tn(ab, dobs[c])))
            for first_row, parts in by_rows.values():
                rows = pl.ds(first_row, sb)
                dk_ref[rows, :] += sum(p[0] for p in parts[1:]) + parts[0][0]
                dv_ref[rows, :] += sum(p[1] for p in parts[1:]) + parts[0][1]
            return [f[3] for f in fins], [ef[:, 0:1] for ef in e_froms]

        zero = jnp.zeros((sb, 1), F32)
        kv_diag = [load_kv(2 * qi), load_kv(2 * qi + 1)]
        kv_prev = [load_kv(jnp.maximum(2 * qi - 1, 0)), kv_diag[0]]
        has_prev = lax.broadcasted_iota(jnp.int32, (sb, sb), 0) >= jnp.where(qi > 0, 0, sb)
        r_first, g_first = run_tiles(
            [(c, kv_diag[c // 2], diag, None) for c in range(4)]
            + [(c, kv_prev[c // 2], has_prev if c < 2 else None, c) for c in range(4)], [zero] * 4, [zero] * 4)
        r_runs, g_runs = r_first[4:], g_first[4:]

        def step(carry):
            i, _, *rg = carry
            kvs = [load_kv(2 * qi - 2 - i), load_kv(2 * qi - 1 - i)]
            rs, gs = run_tiles([(c, kvs[c // 2], None, None) for c in range(4)], rg[:4], rg[4:])
            return (i + 1, _all_dead(rs), *rs, *gs)

        i_end, _, *rg = lax.while_loop(lambda c: jnp.logical_and(c[0] < 2 * qi - 1, c[1] == 0), step,
                                       (jnp.int32(0), _all_dead(r_runs), *r_runs, *g_runs))

        @pl.when(jnp.logical_and(i_end == 2 * qi - 1, _all_dead(rg[2:4]) == 0))
        def _():
            kv_last = load_kv(0)
            run_tiles([(c, kv_last, None, None) for c in (2, 3)], rg[:4], rg[4:])
        for s in range(2):
            dq_ref[pl.ds(s * sb, sb), :] = jnp.where(hmasks[0], dq_acc[2 * s], dq_acc[2 * s + 1]) * scale

    qspec = pl.BlockSpec((bq, LANES), lambda h, i: (i, h))
    dospec = pl.BlockSpec((bq, LANES), lambda h, i: (i, h + CONV_DIM // LANES))
    kspec = pl.BlockSpec((t, LANES), lambda h, i: (0, h))
    full = pl.BlockSpec((sb, sb), lambda h, i: (0, 0))
    in_specs, out_specs, out_shape, scratch, extra = _with_comm(
        comm, [qspec, kspec, kspec, qspec, dospec, full, full], [qspec, kspec, kspec],
        [jax.ShapeDtypeStruct((t, SB_DIM), F32)] * 3, [pltpu.VMEM((4, sb, LANES), F32)])
    return pl.pallas_call(
        body, name="attn_bwd", grid=(SB_DIM // LANES, t // bq),
        in_specs=in_specs, out_specs=out_specs, out_shape=out_shape, scratch_shapes=scratch,
        compiler_params=_cparams("arbitrary" if comm else "parallel", "arbitrary"),
    )(qn, kn, vb, o, dycat, tri, tri_inc, *extra)


def _ple_loss(x3, p2, tgt, gain, wpg, wppt, tt):
    t, d = x3.shape
    pdim = p2.shape[1]
    nt = t // tt

    def body(x_ref, p_ref, t_ref, g_ref, wg_ref, wp_ref,
             dx_ref, dxb_ref, dwg_ref, dwp_ref, gpart_ref, lpart_ref, accg_ref, accp_ref):
        i = pl.program_id(0)
        xv, gain_v = x_ref[...], g_ref[...]
        hb = ((xv * _rms_stats(xv)) * gain_v).astype(BF16)
        gate = jax.nn.sigmoid(_dot_nn(hb, wg_ref[...]))
        pb = p_ref[...].astype(BF16)
        pe = _dot_nt(pb, wp_ref[...])
        diff = xv + gate * pe - t_ref[...]
        lsum = jnp.sum(_fold8(diff * diff), axis=1, keepdims=True) * (0.5 / d)
        lpart_ref[...] = jnp.broadcast_to(lsum, (SUBLANES, LANES))
        dy = diff * (1.0 / d)
        dgz = ((dy * pe) * gate * (1.0 - gate)).astype(BF16)
        dpe = (dy * gate).astype(BF16)
        dx_n, grow = _rms_bwd(_dot_nt(dgz, wg_ref[...]), xv, gain_v)
        dx = dy + dx_n
        dx_ref[...] = dx
        dxb_ref[...] = dx.astype(BF16)
        gpart_ref[...] = _fold8(grow)
        sg = _dot_tn(hb, dgz)
        sp = _dot_tn(dpe, pb)

        @pl.when(i == 0)
        def _():
            accg_ref[...] = sg
            accp_ref[...] = sp

        @pl.when(i > 0)
        def _():
            accg_ref[...] += sg
            accp_ref[...] += sp

        @pl.when(i == nt - 1)
        def _():
            dwg_ref[...] = accg_ref[...].astype(BF16)
            dwp_ref[...] = accp_ref[...].astype(BF16)

    tile = lambda w: pl.BlockSpec((tt, w), lambda i: (i, 0))
    full = lambda shape: pl.BlockSpec(shape, lambda i: (0, 0))
    return pl.pallas_call(
        body, name="ple_loss", grid=(nt,),
        in_specs=[tile(d), tile(pdim), tile(d), full((1, d)),
                  pl.BlockSpec((d, d), lambda i: (0, 0), pipeline_mode=pl.Buffered(1)),
                  pl.BlockSpec((d, pdim), lambda i: (0, 0), pipeline_mode=pl.Buffered(1))],
        out_specs=[tile(d), tile(d), full((d, d)), full((d, pdim)),
                   pl.BlockSpec((SUBLANES, d), lambda i: (i, 0)), pl.BlockSpec((SUBLANES, LANES), lambda i: (i, 0))],
        out_shape=[jax.ShapeDtypeStruct((t, d), F32), jax.ShapeDtypeStruct((t, d), BF16),
                   jax.ShapeDtypeStruct((d, d), BF16), jax.ShapeDtypeStruct((d, pdim), BF16),
                   jax.ShapeDtypeStruct((nt * SUBLANES, d), F32), jax.ShapeDtypeStruct((nt * SUBLANES, LANES), F32)],
        scratch_shapes=[pltpu.VMEM((d, d), F32), pltpu.VMEM((d, pdim), F32)],
        compiler_params=_cparams("arbitrary"),
    )(x3, p2, tgt, gain, wpg, wppt)


def _pack_small(parts_gain, conv_part, qk_part):
    d = parts_gain[0].shape[1]
    ng = len(parts_gain)

    def body(*refs):
        g_refs, conv_ref, qk_ref, o_ref = refs[:ng], refs[ng], refs[ng + 1], refs[ng + 2]
        rows = [jnp.sum(r[...], axis=0, keepdims=True) for r in g_refs]
        cs = jnp.sum(conv_ref[...], axis=0, keepdims=True)
        qs = jnp.sum(qk_ref[...], axis=0, keepdims=True)
        rows.append(jnp.concatenate([cs[:, 3 * SLAB:], cs[:, :SLAB]], axis=1))
        rows.append(cs[:, SLAB:3 * SLAB])
        rows.append(qs)
        rid = lax.broadcasted_iota(jnp.int32, (2 * SUBLANES, 1), 0)
        out = jnp.zeros((2 * SUBLANES, d), F32)
        for idx, r in enumerate(rows):
            out = jnp.where(rid == idx, r, out)
        o_ref[...] = out

    return pl.pallas_call(
        body, name="pack_small", out_shape=jax.ShapeDtypeStruct((2 * SUBLANES, d), F32),
    )(*parts_gain, conv_part, qk_part)


def _sum_slots(name, slots, out_dtype=F32):
    _, r, c = slots.shape

    def body(s_ref, o_ref):
        acc = s_ref[0].astype(F32)
        for d in range(1, N_DEV):
            acc = acc + s_ref[d].astype(F32)
        o_ref[...] = acc.astype(o_ref.dtype)

    return pl.pallas_call(body, name=name, out_shape=jax.ShapeDtypeStruct((r, c), out_dtype),
                          compiler_params=pltpu.CompilerParams(vmem_limit_bytes=VMEM_LIMIT_BYTES))(slots)


def _adamw(name, w, g, m, v):
    c1 = 1.0 - ADAM_B1 ** ADAM_STEP
    c2 = 1.0 - ADAM_B2 ** ADAM_STEP

    def body(w_ref, g_ref, m_ref, v_ref, go_ref, d_ref, nm_ref, nv_ref):
        if g.ndim == 3:
            gv = g_ref[0].astype(F32)
            for dev in range(1, N_DEV):
                gv = gv + g_ref[dev].astype(F32)
        else:
            gv = g_ref[...]
        go_ref[...] = gv
        nm = ADAM_B1 * m_ref[...] + (1.0 - ADAM_B1) * gv
        nv = ADAM_B2 * v_ref[...] + (1.0 - ADAM_B2) * (gv * gv)
        d_ref[...] = -ADAM_LR * ((nm / c1) / (jnp.sqrt(nv / c2) + ADAM_EPS) + ADAM_WD * w_ref[...])
        nm_ref[...] = nm
        nv_ref[...] = nv

    return pl.pallas_call(body, name=name, out_shape=[jax.ShapeDtypeStruct(w.shape, F32)] * 4,
                          compiler_params=pltpu.CompilerParams(vmem_limit_bytes=VMEM_LIMIT_BYTES))(w, g, m, v)


def _any_specs(n):
    return [pl.BlockSpec(memory_space=pl.ANY)] * n


def _all_gather(name, shards):
    n = len(shards)

    def body(*refs):
        ins, outs = refs[:n], refs[n:2 * n]
        send_sems, recv_sems, local_sems = refs[2 * n:]
        x, y, c = (lax.axis_index(a) for a in MESH_AXES)
        me, sibling = (x, y, c), (x, y, 1 - c)
        chips = [(1 - x, y), (x, 1 - y), (1 - x, 1 - y)]

        def rows(a, px, py, pc):
            r = ins[a].shape[0]
            return outs[a].at[pl.ds((4 * px + 2 * py + pc) * r, r), :]

        def copy(a, k, block, to, src=None):
            return pltpu.make_async_remote_copy(
                src_ref=rows(a, *block) if src is None else src, dst_ref=rows(a, *block),
                send_sem=send_sems.at[7 * a + k], recv_sem=recv_sems.at[7 * a + k],
                device_id=to, device_id_type=MESH)

        mine = [pltpu.make_async_copy(ins[a], rows(a, *me), local_sems.at[a]) for a in range(n)]
        for cp in mine:
            cp.start()
        first = []
        for a in range(n):
            first.append(copy(a, 0, me, sibling, src=ins[a]))
            first += [copy(a, 1 + j, me, (*chip, c), src=ins[a]) for j, chip in enumerate(chips)]
        for cp in first:
            cp.start()
        passed = []
        for j, chip in enumerate(chips):
            for a in range(n):
                copy(a, 1 + j, (*chip, c), me).wait_recv()
                fwd = copy(a, 4 + j, (*chip, c), sibling)
                fwd.start()
                passed.append(fwd)
        for a in range(n):
            copy(a, 0, sibling, me).wait_recv()
            for j, chip in enumerate(chips):
                copy(a, 4 + j, (*chip, 1 - c), me).wait_recv()
        for cp in first + passed:
            cp.wait_send()
        for cp in mine:
            cp.wait()

    return pl.pallas_call(
        body, name=name, in_specs=_any_specs(n), out_specs=_any_specs(n),
        out_shape=[jax.ShapeDtypeStruct((N_DEV * s.shape[0], s.shape[1]), s.dtype) for s in shards],
        scratch_shapes=[pltpu.SemaphoreType.DMA((7 * n,)), pltpu.SemaphoreType.DMA((7 * n,)),
                        pltpu.SemaphoreType.DMA((n,))],
    )(*shards)


def _residual_and_norm(res_scale):
    def epilogue(acc, rows, fulls):
        out = rows[0] + res_scale * acc
        return [out] + [(out * _rms_stats(out)) * gain for gain in fulls]
    return epilogue


def _ffn_fwd(tag, x, h, wgt, wut, wd, next_gain, tt_nt, tt_nn, comm_gate=None, comm_down=None):
    f = wgt.shape[0]

    def gate_up(accs, _):
        return [accs[0], accs[1], jax.nn.silu(accs[0]) * accs[1]]

    g, u, a, *got_gate = _mm_nt(f"{tag}_gate_up", [h], [wgt, wut], [(0, 0), (0, 1)], gate_up, [BF16] * 3,
                                tt_nn, _pick(f, 256), comm_gate)
    if wd is None:
        wd = got_gate[0]
    gains = [] if next_gain is None else [next_gain]
    out, *rest = _mm_nn(f"{tag}_down", [(a, wd, 0)], [x], gains, _residual_and_norm(FFN_RES),
                        [("tile", F32)] + [("tile", BF16)] * len(gains), tt_nt, f, comm_down)
    h_next = rest.pop(0) if gains else None
    return out, h_next, (g, u, a), got_gate, rest


def _norm_bwd_epilogue(acc, rows, fulls):
    x_in, dy = rows
    dx_n, grow = _rms_bwd(acc, x_in, fulls[0])
    dx = dy + dx_n
    return [dx, dx, _fold8(grow)]


_NORM_BWD_OUTS = [("tile", F32), ("tile", BF16), ("part", F32)]


def _ffn_bwd(tag, x_in, h, hidden, dy, dyb, gain, wgt, wut, wd, tt_nt, tt_nn, riders=(None, None),
             exchange_own=False):
    g, u, a = hidden
    f = wgt.shape[0]
    tt_tn = _pick(h.shape[0], 2 * tt_nt)
    own = (lambda arr: _Exchange([arr])) if exchange_own else (lambda arr: None)

    def carried(result, rider):
        return (result[0], result[1:]) if rider else (result, [])

    def hidden_grads(accs, tiles):
        da = FFN_RES * accs[0]
        gv, uv = tiles[0].astype(F32), tiles[1].astype(F32)
        sg = jax.nn.sigmoid(gv)
        s = gv * sg
        return [da * uv * (sg * (1.0 + gv * (1.0 - sg))), da * s]

    dwd, got_dwd = carried(_mm_tn(f"{tag}_dwd", a, dyb, FFN_RES, f // 2, tt_tn, riders[0]), riders[0])
    dg, du, *x_dwd = _mm_nt(f"{tag}_bwd_hidden", [dyb], [wd], [(0, 0)], hidden_grads, [BF16, BF16],
                            tt_nn, _pick(f, 256), own(dwd), tiles=[g, u])
    dwg, got_dwg = carried(_mm_tn(f"{tag}_dwg", dg, h, 1.0, f // 2, tt_tn, riders[1]), riders[1])
    dwu, x_dwg = carried(_mm_tn(f"{tag}_dwu", du, h, 1.0, f // 2, tt_tn, own(dwg)), exchange_own)
    dx, dxb, gpart, *x_dwu = _mm_nn(f"{tag}_bwd_dx", [(dg, wgt, 0), (du, wut, 0)], [x_in, dy], [gain],
                                    _norm_bwd_epilogue, _NORM_BWD_OUTS, tt_nn, f, own(dwu))
    grads = [*x_dwg, *x_dwu, *x_dwd] if exchange_own else [dwg, dwu, dwd]
    return dx, dxb, gpart, grads, [*got_dwd, *got_dwg]


def kernel(x, p, ffn1_norm, ffn1_w_gate, ffn1_w_up, ffn1_w_down, mix_norm, w_in, conv_w, conv_b, q_norm, k_norm, w_out, ffn2_norm, ffn2_w_gate, ffn2_w_up, ffn2_w_down, ple_norm, ple_w_gate, ple_w_proj, loss_target, m_ffn1_norm, m_ffn1_w_gate, m_ffn1_w_up, m_ffn1_w_down, m_mix_norm, m_w_in, m_conv_w, m_conv_b, m_q_norm, m_k_norm, m_w_out, m_ffn2_norm, m_ffn2_w_gate, m_ffn2_w_up, m_ffn2_w_down, m_ple_norm, m_ple_w_gate, m_ple_w_proj, v_ffn1_norm, v_ffn1_w_gate, v_ffn1_w_up, v_ffn1_w_down, v_mix_norm, v_w_in, v_conv_w, v_conv_b, v_q_norm, v_k_norm, v_w_out, v_ffn2_norm, v_ffn2_w_gate, v_ffn2_w_up, v_ffn2_w_down, v_ple_norm, v_ple_w_gate, v_ple_w_proj):
    x0, p2, tgt = x[0], p[0, 0], loss_target[0]
    t, d = x0.shape
    tt_nt = _pick(t, 1024)
    tt_nn = _pick(t, 512)
    tt_ew = _pick(t, 512)
    tt_ple = _pick(t, 512)
    sb = _pick(t // 2, 256)

    t_bf = lambda w: w[0].T.astype(BF16)
    n_bf = lambda w: w[0].astype(BF16)
    cw_tile = jnp.zeros((SUBLANES, LANES), F32).at[:conv_w.shape[1], :conv_w.shape[2]].set(conv_w[0])
    gather_first = _GatherTwoLevel([t_bf(ffn1_w_gate), t_bf(ffn1_w_up)])
    gather_down = _GatherTwoLevel([n_bf(ffn1_w_down), n_bf(w_out), cw_tile])
    gather_in = _GatherTwoLevel([t_bf(w_in)])
    gather_late = _GatherTwoLevel([t_bf(ffn2_w_gate), t_bf(ffn2_w_up), n_bf(ffn2_w_down), n_bf(ple_w_gate),
                                   t_bf(ple_w_proj)])
    ncs = conv_w.shape[2]

    qg = jnp.tile(q_norm, (1, SB_DIM // HEAD_DIM))
    kg = jnp.tile(k_norm, (1, SB_DIM // HEAD_DIM))
    gi = lax.broadcasted_iota(jnp.int32, (SLAB, SLAB), 0) // HEAD_DIM
    gj = lax.broadcasted_iota(jnp.int32, (SLAB, SLAB), 1) // HEAD_DIM
    bd = (gi == gj).astype(BF16)
    tri, tri_inc = _tri_masks(sb)

    h1, wg1t, wu1t = _rmsnorm("ffn1_norm", x0, ffn1_norm, tt_ew, gather_first)
    x1, h2, hidden1, (wd1, wout, cw_all), (wint,) = _ffn_fwd(
        "ffn1", x0, h1, wg1t, wu1t, None, mix_norm, tt_nt, tt_nn, gather_down, gather_in)
    cw_full = cw_all.reshape(N_DEV, SUBLANES, LANES)[:, :, :ncs].transpose(1, 0, 2).reshape(SUBLANES, N_DEV * ncs)
    (proj,) = _mm_nt("in_proj", [h2], [wint], [(0, 0)], lambda accs, _: accs, [F32], tt_nn, SLAB)
    y_conv = _conv_fwd(proj, cw_full, conv_b, tt_ew)
    qn, kn, vb = _qknorm_fwd(proj, qg, kg, bd, tt_ew)
    o, ob, wg2t, wu2t, wd2, wpg, wppt = _attn_fwd(qn, kn, vb, tri, sb, gather_late)
    x2, h3 = _mm_nn("out_proj", [(y_conv, wout, 0), (ob, wout, 1)], [x1], [ffn2_norm], _residual_and_norm(1.0),
                    [("tile", F32), ("tile", BF16)], tt_nn, SLAB)
    x3, _, hidden2, _, _ = _ffn_fwd("ffn2", x2, h3, wg2t, wu2t, wd2, None, tt_nt, tt_nn)

    dx3, dx3b, dwpg, dwppt, gp_ple, lpart = _ple_loss(x3, p2, tgt, ple_norm, wpg, wppt, tt_ple)
    loss = lax.psum(jnp.sum(lpart[:, 0]), MESH_AXES)
    dx2, dx2b, gp_ffn2, (dwg2, dwu2, dwd2), _ = _ffn_bwd("ffn2", x2, h3, hidden2, dx3, dx3b, ffn2_norm,
                                                         wg2t, wu2t, wd2, tt_nt, tt_nn)
    (dycat,) = _mm_nt("out_proj_bwd", [dx2b], [wout], [(0, 0)], lambda accs, _: accs, [F32], tt_nn, SLAB)
    dwout = _mm_tn_slabs("dwout", [y_conv, ob], dx2b, tt_nt)
    dqn, dkn, dv, *slots_late = _attn_bwd(qn, kn, vb, o, dycat, tri, tri_inc, sb,
                                          _Exchange([dwg2, dwu2, dwd2, dwpg, dwppt]))
    dq, dk, dvb, qk_part = _qknorm_bwd(proj, dqn, dkn, dv, qg, kg, bd, tt_ew)
    db, dc, du, conv_part = _conv_bwd(proj, dycat, cw_full, conv_b, tt_ew)
    dproj = [db, dc, du, dq, dk, dvb]
    dwin, slot_wout = _mm_tn_slabs("dwin", dproj, h2, tt_nt, _Exchange([dwout]))
    dx1, dx1b, gp_mix, slot_win = _mm_nn(
        "in_proj_bwd", [(dp, wint, s) for s, dp in enumerate(dproj)], [x1, dx2], [mix_norm],
        _norm_bwd_epilogue, _NORM_BWD_OUTS, tt_nn, SLAB, _Exchange([dwin]))
    dx0, _, gp_ffn1, slots_ffn1, _ = _ffn_bwd(
        "ffn1", x0, h1, hidden1, dx1, dx1b, ffn1_norm, wg1t, wu1t, wd1, tt_nt, tt_nn, exchange_own=True)

    slots = [*slots_ffn1, slot_win, slot_wout, *slots_late]
    per_dev = [s.reshape(N_DEV, s.shape[0] // N_DEV, s.shape[1]) for s in slots]
    c_wg1, c_wu1, c_wd1, c_win, c_wout, c_wg2, c_wu2, c_wd2, c_wpg, c_wpp = per_dev
    g_win, g_wpp = _sum_slots("sum_grads_w_in", c_win), _sum_slots("sum_grads_ple_w_proj", c_wpp)
    small = _pack_small([gp_ffn1, gp_mix, gp_ffn2, gp_ple], conv_part, qk_part)
    (small_all,) = _all_gather("gather_small_grads", [small])
    sm = _sum_slots("sum_small_grads", small_all.reshape(N_DEV, 2 * SUBLANES, d))
    fold = lambda r: r.reshape(SB_DIM // HEAD_DIM, HEAD_DIM).sum(axis=0)[None]
    me_idx = 4 * lax.axis_index("x") + 2 * lax.axis_index("y") + lax.axis_index("c")
    cw_grad = jnp.stack([sm[4, SLAB:], sm[5, :SLAB], sm[5, SLAB:]])
    grads = {
        "ffn1_norm": sm[0:1], "ffn1_w_down": c_wd1,
        "mix_norm": sm[1:2], "w_in": g_win.T, "conv_w": lax.dynamic_slice_in_dim(cw_grad, me_idx * ncs, ncs, axis=1),
        "conv_b": sm[4:5, :SLAB], "q_norm": fold(sm[6, :SLAB]), "k_norm": fold(sm[6, SLAB:]),
        "w_out": c_wout, "ffn2_norm": sm[2:3], "ffn2_w_down": c_wd2,
        "ple_norm": sm[3:4], "ple_w_gate": c_wpg, "ple_w_proj": g_wpp.T,
    }
    grads_t = {"ffn1_w_gate": c_wg1, "ffn1_w_up": c_wu1, "ffn2_w_gate": c_wg2, "ffn2_w_up": c_wu2}

    weights = dict(ffn1_norm=ffn1_norm, ffn1_w_gate=ffn1_w_gate, ffn1_w_up=ffn1_w_up, ffn1_w_down=ffn1_w_down,
                   mix_norm=mix_norm, w_in=w_in, conv_w=conv_w, conv_b=conv_b, q_norm=q_norm, k_norm=k_norm,
                   w_out=w_out, ffn2_norm=ffn2_norm, ffn2_w_gate=ffn2_w_gate, ffn2_w_up=ffn2_w_up,
                   ffn2_w_down=ffn2_w_down, ple_norm=ple_norm, ple_w_gate=ple_w_gate, ple_w_proj=ple_w_proj)
    m_in = dict(ffn1_norm=m_ffn1_norm, ffn1_w_gate=m_ffn1_w_gate, ffn1_w_up=m_ffn1_w_up, ffn1_w_down=m_ffn1_w_down,
                mix_norm=m_mix_norm, w_in=m_w_in, conv_w=m_conv_w, conv_b=m_conv_b, q_norm=m_q_norm,
                k_norm=m_k_norm, w_out=m_w_out, ffn2_norm=m_ffn2_norm, ffn2_w_gate=m_ffn2_w_gate,
                ffn2_w_up=m_ffn2_w_up, ffn2_w_down=m_ffn2_w_down, ple_norm=m_ple_norm, ple_w_gate=m_ple_w_gate,
                ple_w_proj=m_ple_w_proj)
    v_in = dict(ffn1_norm=v_ffn1_norm, ffn1_w_gate=v_ffn1_w_gate, ffn1_w_up=v_ffn1_w_up, ffn1_w_down=v_ffn1_w_down,
                mix_norm=v_mix_norm, w_in=v_w_in, conv_w=v_conv_w, conv_b=v_conv_b, q_norm=v_q_norm,
                k_norm=v_k_norm, w_out=v_w_out, ffn2_norm=v_ffn2_norm, ffn2_w_gate=v_ffn2_w_gate,
                ffn2_w_up=v_ffn2_w_up, ffn2_w_down=v_ffn2_w_down, ple_norm=v_ple_norm, ple_w_gate=v_ple_w_gate,
                ple_w_proj=v_ple_w_proj)
    g_out, d_out, m_out, v_out = [], [], [], []
    for name, w in weights.items():
        if name in grads_t:
            view, back = (lambda a: a[0].T), (lambda a: a.T[None])
            g_in = grads_t[name]
        else:
            view, back = (lambda a, w=w: a.reshape(w.shape[-2:])), (lambda a, w=w: a.reshape(w.shape))
            g_in = grads[name] if grads[name].ndim == 3 else view(grads[name])
        g2, dlt, nm, nv = _adamw(f"adamw_{name}", view(w), g_in, view(m_in[name]), view(v_in[name]))
        g_out.append(back(g2))
        d_out.append(back(dlt))
        m_out.append(back(nm))
        v_out.append(back(nv))
    return (loss, dx0[None], *g_out, *d_out, *m_out, *v_out)
```
